```python
import math
import jax
import jax.numpy as jnp
from jax import lax
import numpy as np

D_MODEL = 1024
BATCH = 16
SEQ = 256
DEPTH = 2
DEC_BATCH = 4
DEC_SEQ = 1024
PAST_LEN = 512

GRID_W = 64
N_AB = (DEPTH + 1) // 2
N_C = DEPTH // 2
D_A = D_MODEL // 2
A_HEADS = 4
A_DK = D_A // A_HEADS
A_DV = D_A // A_HEADS
CHUNK = 32
D_B = D_MODEL // 2
B_BLOCKS = 8
B_BLOCK = D_B // B_BLOCKS
CONV_W = 4
LRU_C = 8.0
D_IN_AB = 5 * D_A + 2 * D_B
C_HEAD = 64
C_HEADS = D_MODEL // C_HEAD
W_LORA = 64
A_LORA = 64
G_LORA = 128
W_DECAY_SCALE = math.exp(-0.5)
N_EXPERTS = 16
N_GROUPS = 4
TOP_K = 2
D_EXPERT = 256
RMS_EPS = 1e-6
GN_EPS = 64e-5
POS_BASE = 10000.0
F32 = jnp.float32

kernel_name = 'hybrid_hgrn2_rglru_rwkv7_moe_diffusion_step'


def rmsnorm(x, g):
    x32 = x.astype(F32)
    y = x32 * lax.rsqrt(jnp.mean(x32 * x32, axis=-1, keepdims=True) + RMS_EPS)
    return (y * g.astype(F32)).astype(x.dtype)


def ada_modulation(cvec, w, b):
    m = jnp.matmul(jax.nn.silu(cvec), w) + b
    return jnp.split(m[..., None, :], 6, axis=-1)


def grid_pos_embed(n_tok):
    rows = n_tok // GRID_W
    r, cl = jnp.meshgrid(jnp.arange(rows, dtype=F32), jnp.arange(GRID_W, dtype=F32), indexing='ij')
    quarter = D_MODEL // 4
    omega = 1.0 / (POS_BASE ** (jnp.arange(quarter, dtype=F32) / quarter))
    ang_r = r.reshape(-1, 1) * omega
    ang_c = cl.reshape(-1, 1) * omega
    return jnp.concatenate([jnp.sin(ang_r), jnp.cos(ang_r), jnp.sin(ang_c), jnp.cos(ang_c)], axis=-1)


def flip_t(z):
    return jnp.flip(z, axis=1)


def ident(z):
    return z


def gla_chunked(q, k, v, logf, s0):
    bsz, t, nh, _ = q.shape
    n = t // CHUNK

    def to_chunks(a):
        return jnp.moveaxis(a.reshape(bsz, n, CHUNK, *a.shape[2:]), 1, 0)

    causal = jnp.tril(jnp.ones((CHUNK, CHUNK), dtype=bool))[None, :, :, None, None]

    def step(s, inp):
        qc, kc, vc, gc = inp
        b = jnp.cumsum(gc, axis=1)
        diff = b[:, :, None] - b[:, None, :]
        decay = jnp.where(causal, jnp.exp(jnp.minimum(diff, 0.0)), 0.0)
        att = jnp.einsum('bthk,bshk,btshk->btsh', qc, kc, decay)
        o = jnp.einsum('btsh,bshv->bthv', att, vc) + jnp.einsum('bthk,bhkv->bthv', qc * jnp.exp(b), s)
        b_last = b[:, -1]
        s = s * jnp.exp(b_last)[..., None] + jnp.einsum('bshk,bshv->bhkv', kc * jnp.exp(b_last[:, None] - b), vc)
        return s, o

    s_fin, o = lax.scan(step, s0.astype(F32), (to_chunks(q), to_chunks(k), to_chunks(v), to_chunks(logf)))
    o = jnp.moveaxis(o, 0, 1).reshape(bsz, t, nh, v.shape[-1])
    return o, s_fin


def linear_scan(a, b, h0):
    def combine(lft, rgt):
        return lft[0] * rgt[0], rgt[0] * lft[1] + rgt[1]
    a_c, b_c = lax.associative_scan(combine, (a, b), axis=1)
    h = a_c * h0.astype(F32)[:, None] + b_c
    return h, h[:, -1]


def centred_dwconv(x, w, bias):
    t = x.shape[1]
    left = CONV_W // 2
    xp = jnp.pad(x, ((0, 0), (left, CONV_W - 1 - left), (0, 0)))
    out = bias
    for j in range(CONV_W):
        out = out + xp[:, j:j + t] * w[j]
    return out


def rwkv7_scan(r, w, k, v, kk, a, s0):
    def step(s, inp):
        r_t, w_t, k_t, v_t, kk_t, a_t = inp
        sa = jnp.einsum('bhvk,bhk->bhv', s, -kk_t)
        s = s * w_t[:, :, None, :] + sa[..., None] * (kk_t * a_t)[:, :, None, :] + v_t[..., None] * k_t[:, :, None, :]
        return s, jnp.einsum('bhvk,bhk->bhv', s, r_t)
    xs = tuple(jnp.moveaxis(z, 1, 0) for z in (r, w, k, v, kk, a))
    s_fin, o = lax.scan(step, s0.astype(F32), xs)
    return jnp.moveaxis(o, 0, 1), s_fin


def moe_ffn(h, router_w, router_b, w1, w3, w2):
    bsz, t, d = h.shape
    hf = h.reshape(-1, d)
    scores = jax.nn.sigmoid(jnp.matmul(hf, router_w).astype(F32))
    sel = scores + router_b.astype(F32)
    grp = sel.reshape(-1, N_GROUPS, N_EXPERTS // N_GROUPS)
    grp_score = jnp.sum(lax.top_k(grp, TOP_K)[0], axis=-1)
    best = jnp.argmax(grp_score, axis=-1)
    in_grp = (jnp.arange(N_EXPERTS) // (N_EXPERTS // N_GROUPS))[None, :] == best[:, None]
    _, idx = lax.top_k(jnp.where(in_grp, sel, -jnp.inf), TOP_K)
    wsel = jnp.take_along_axis(scores, idx, axis=-1)
    wsel = wsel / jnp.sum(wsel, axis=-1, keepdims=True)
    combine = jnp.sum(jax.nn.one_hot(idx, N_EXPERTS, dtype=F32) * wsel[..., None], axis=1)
    hid = jax.nn.silu(jnp.einsum('md,edf->mef', hf, w1)) * jnp.einsum('md,edf->mef', hf, w3)
    hid = hid * combine[..., None].astype(hid.dtype)
    return jnp.einsum('mef,efd->md', hid, w2).reshape(bsz, t, d)


def setup_inputs(seed: int = 0) -> dict:
    key = jax.random.key(seed)
    ks = iter(jax.random.split(key, 64))

    def nrm(shape, scale):
        return scale * jax.random.normal(next(ks), shape, F32)

    D = D_MODEL
    u = jax.random.uniform(next(ks), (N_AB, 2, D_B), F32, minval=0.9, maxval=0.999)
    base = u ** (1.0 / LRU_C)
    rg_lambda = jnp.log(base) - jnp.log1p(-base)
    return {
        'x_prompt': nrm((BATCH, SEQ, D), 1.0),
        'x_sample': nrm((DEC_BATCH, DEC_SEQ, D), 1.0),
        'state_hgrn': nrm((DEC_BATCH, N_AB, 2, A_HEADS, A_DK, A_DV), 0.3),
        'state_rglru': nrm((DEC_BATCH, N_AB, 2, D_B), 0.5),
        'state_rwkv': nrm((DEC_BATCH, N_C, 2, C_HEADS, C_HEAD, C_HEAD), 0.1),
        'c': nrm((DEC_BATCH, D), 1.0),
        'c_ctx': nrm((D,), 1.0),
        'norm_mix_g': 1.0 + nrm((DEPTH, D), 0.05),
        'norm_ffn_g': 1.0 + nrm((DEPTH, D), 0.05),
        'w_mod': nrm((DEPTH, D, 6 * D), 0.5 * D ** -0.5),
        'b_mod': nrm((DEPTH, 6 * D), 0.02),
        'ab_w_in': nrm((N_AB, D, D_IN_AB), D ** -0.5),
        'ab_w_out': nrm((N_AB, D_A + D_B, D), (D_A + D_B) ** -0.5),
        'hgrn_lb': nrm((2, N_AB + 1, D_A), 0.5),
        'hgrn_norm_g': 1.0 + nrm((N_AB, A_DV), 0.05),
        'rg_conv_w': nrm((N_AB, CONV_W, D_B), CONV_W ** -0.5),
        'rg_conv_b': nrm((N_AB, D_B), 0.02),
        'rg_wa': nrm((N_AB, 2, B_BLOCKS, B_BLOCK, B_BLOCK), B_BLOCK ** -0.5),
        'rg_ba': nrm((N_AB, 2, D_B), 0.02),
        'rg_wx': nrm((N_AB, 2, B_BLOCKS, B_BLOCK, B_BLOCK), B_BLOCK ** -0.5),
        'rg_bx': nrm((N_AB, 2, D_B), 0.02),
        'rg_lambda': rg_lambda,
        'rw_mu': jax.random.uniform(next(ks), (N_C, 6, D), F32),
        'rw_wr': nrm((N_C, D, D), D ** -0.5),
        'rw_wk': nrm((N_C, D, D), D ** -0.5),
        'rw_wv': nrm((N_C, D, D), D ** -0.5),
        'rw_wo': nrm((N_C, D, D), D ** -0.5),
        'rw_w0': nrm((N_C, 2, D), 1.0),
        'rw_w1': nrm((N_C, 2, D, W_LORA), D ** -0.5),
        'rw_w2': nrm((N_C, 2, W_LORA, D), 0.5 * W_LORA ** -0.5),
        'rw_a0': nrm((N_C, 2, D), 0.5),
        'rw_a1': nrm((N_C, 2, D, A_LORA), D ** -0.5),
        'rw_a2': nrm((N_C, 2, A_LORA, D), 0.5 * A_LORA ** -0.5),
        'rw_g1': nrm((N_C, D, G_LORA), D ** -0.5),
        'rw_g2': nrm((N_C, G_LORA, D), G_LORA ** -0.5),
        'rw_kk': 0.85 + nrm((N_C, D), 0.05),
        'rw_ka': 1.0 + nrm((N_C, D), 0.05),
        'rw_rk': nrm((N_C, C_HEADS, C_HEAD), 0.1),
        'rw_lnw': 1.0 + nrm((N_C, D), 0.05),
        'rw_lnb': nrm((N_C, D), 0.02),
        'moe_router': nrm((D, N_EXPERTS), D ** -0.5),
        'moe_router_bias': nrm((N_EXPERTS,), 0.01),
        'moe_w1': nrm((DEPTH, N_EXPERTS, D, D_EXPERT), D ** -0.5),
        'moe_w3': nrm((DEPTH, N_EXPERTS, D, D_EXPERT), D ** -0.5),
        'moe_w2': nrm((DEPTH, N_EXPERTS, D_EXPERT, D), D_EXPERT ** -0.5),
        'norm_f_g': 1.0 + nrm((D,), 0.05),
    }


def reference(x_prompt, x_sample, state_hgrn, state_rglru, state_rwkv, c, c_ctx,
              norm_mix_g, norm_ffn_g, w_mod, b_mod,
              ab_w_in, ab_w_out, hgrn_lb, hgrn_norm_g,
              rg_conv_w, rg_conv_b, rg_wa, rg_ba, rg_wx, rg_bx, rg_lambda,
              rw_mu, rw_wr, rw_wk, rw_wv, rw_wo, rw_w0, rw_w1, rw_w2, rw_a0, rw_a1, rw_a2,
              rw_g1, rw_g2, rw_kk, rw_ka, rw_rk, rw_lnw, rw_lnb,
              moe_router, moe_router_bias, moe_w1, moe_w3, moe_w2, norm_f_g):
    lower_bounds = jnp.cumsum(jax.nn.softmax(hgrn_lb.astype(F32), axis=1), axis=1)

    def mixer_ab(h, j, s_hgrn, s_rglru):
        bsz, t, _ = h.shape
        proj = jnp.matmul(h, ab_w_in[j])
        cuts = [D_A, 2 * D_A, 3 * D_A, 4 * D_A, 5 * D_A, 5 * D_A + D_B]
        q, f_fw, f_bw, v, og, xr, yr = jnp.split(proj, cuts, axis=-1)

        def heads(z):
            return z.astype(F32).reshape(bsz, t, A_HEADS, A_DK)
        qh = heads(jax.nn.silu(q))
        vh = v.astype(F32).reshape(bsz, t, A_HEADS, A_DV)
        outs, finals = [], []
        for d, f_raw in enumerate((f_fw, f_bw)):
            lb = lower_bounds[d, j]
            logf = jnp.log(lb + (1.0 - lb) * jax.nn.sigmoid(f_raw.astype(F32)))
            kh = heads(-jnp.expm1(logf))
            gh = heads(logf)
            fl = flip_t if d == 1 else ident
            o_d, s_d = gla_chunked(fl(qh), fl(kh), fl(vh), fl(gh), s_hgrn[:, d])
            outs.append(fl(o_d))
            finals.append(s_d)
        o_a = rmsnorm(outs[0] + outs[1], hgrn_norm_g[j]).reshape(bsz, t, D_A) * jax.nn.silu(og.astype(F32))
        xc = centred_dwconv(xr, rg_conv_w[j], rg_conv_b[j]).astype(F32)
        xblk = xc.reshape(bsz, t, B_BLOCKS, B_BLOCK)
        hs, lru_finals = [], []
        for d in range(2):
            gate_r = jax.nn.sigmoid(jnp.einsum('btgi,gij->btgj', xblk, rg_wa[j, d]).reshape(bsz, t, D_B) + rg_ba[j, d])
            gate_i = jax.nn.sigmoid(jnp.einsum('btgi,gij->btgj', xblk, rg_wx[j, d]).reshape(bsz, t, D_B) + rg_bx[j, d])
            log_a = -LRU_C * gate_r * jax.nn.softplus(-rg_lambda[j, d].astype(F32))
            b_in = jnp.sqrt(-jnp.expm1(2.0 * log_a)) * gate_i * xc
            fl = flip_t if d == 1 else ident
            h_d, s_d = linear_scan(fl(jnp.exp(log_a)), fl(b_in), s_rglru[:, d])
            hs.append(fl(h_d))
            lru_finals.append(s_d)
        o_b = (hs[0] + hs[1]) * jax.nn.gelu(yr.astype(F32))
        y = jnp.matmul(jnp.concatenate([o_a, o_b], axis=-1).astype(h.dtype), ab_w_out[j])
        return y, jnp.stack(finals, axis=1), jnp.stack(lru_finals, axis=1)

    def mixer_c(h, j, s_rwkv):
        bsz, t, _ = h.shape
        hp = jnp.pad(h, ((0, 0), (1, 1), (0, 0)))
        xx = 0.5 * (hp[:, :-2] + hp[:, 2:]) - h
        xr, xw, xk, xv, xa, xg = [h + xx * rw_mu[j, i] for i in range(6)]

        def heads(z):
            return z.astype(F32).reshape(bsz, t, C_HEADS, C_HEAD)
        r = heads(jnp.matmul(xr, rw_wr[j]))
        k = jnp.matmul(xk, rw_wk[j])
        v = heads(jnp.matmul(xv, rw_wv[j]))
        g = jnp.matmul(jax.nn.sigmoid(jnp.matmul(xg, rw_g1[j])), rw_g2[j]).astype(F32)
        kk = heads(k * rw_kk[j])
        kk = kk / jnp.maximum(jnp.sqrt(jnp.sum(kk * kk, axis=-1, keepdims=True)), 1e-12)
        kh = heads(k)
        k_a = rw_ka[j].astype(F32).reshape(C_HEADS, C_HEAD)
        r_k = rw_rk[j].astype(F32)
        o_sum = 0.0
        bonus = 0.0
        finals = []
        for d in range(2):
            w_in = rw_w0[j, d] + jnp.matmul(jnp.tanh(jnp.matmul(xw, rw_w1[j, d])), rw_w2[j, d])
            w = heads(jnp.exp(-W_DECAY_SCALE * jax.nn.sigmoid(w_in.astype(F32))))
            a = heads(jax.nn.sigmoid((rw_a0[j, d] + jnp.matmul(jnp.matmul(xa, rw_a1[j, d]), rw_a2[j, d])).astype(F32)))
            kd = kh * (1.0 + (a - 1.0) * k_a)
            fl = flip_t if d == 1 else ident
            o_d, s_d = rwkv7_scan(fl(r), fl(w), fl(kd), fl(v), fl(kk), fl(a), s_rwkv[:, d])
            o_sum = o_sum + fl(o_d)
            bonus = bonus + jnp.sum(r * kd * r_k, axis=-1, keepdims=True) * v
            finals.append(s_d)
        mu = jnp.mean(o_sum, axis=-1, keepdims=True)
        var = jnp.mean(jnp.square(o_sum - mu), axis=-1, keepdims=True)
        o = (o_sum - mu) * lax.rsqrt(var + GN_EPS) * rw_lnw[j].astype(F32).reshape(C_HEADS, C_HEAD) \
            + rw_lnb[j].astype(F32).reshape(C_HEADS, C_HEAD)
        o = (o + bonus).reshape(bsz, t, D_MODEL) * g
        return jnp.matmul(o.astype(h.dtype), rw_wo[j]), jnp.stack(finals, axis=1)

    def trunk(x, cvec, s_hgrn, s_rglru, s_rwkv):
        st_h, st_r, st_w = [], [], []
        for l in range(DEPTH):
            sh1, sc1, gt1, sh2, sc2, gt2 = [m.astype(x.dtype) for m in ada_modulation(cvec, w_mod[l], b_mod[l])]
            h = rmsnorm(x, norm_mix_g[l]) * (1.0 + sc1) + sh1
            j = l // 2
            if l % 2 == 0:
                y, s_a, s_b = mixer_ab(h, j, s_hgrn[:, j], s_rglru[:, j])
                st_h.append(s_a)
                st_r.append(s_b)
            else:
                y, s_c = mixer_c(h, j, s_rwkv[:, j])
                st_w.append(s_c)
            x = x + gt1 * y.astype(x.dtype)
            h = rmsnorm(x, norm_ffn_g[l]) * (1.0 + sc2) + sh2
            x = x + gt2 * moe_ffn(h, moe_router, moe_router_bias, moe_w1[l], moe_w3[l], moe_w2[l]).astype(x.dtype)
        return rmsnorm(x, norm_f_g), jnp.stack(st_h, axis=1), jnp.stack(st_r, axis=1), jnp.stack(st_w, axis=1)

    bp = x_prompt.shape[0]
    z_h = jnp.zeros((bp, N_AB, 2, A_HEADS, A_DK, A_DV), F32)
    z_r = jnp.zeros((bp, N_AB, 2, D_B), F32)
    z_w = jnp.zeros((bp, N_C, 2, C_HEADS, C_HEAD, C_HEAD), F32)
    y_prompt, new_state_hgrn, new_state_rglru, new_state_rwkv = trunk(x_prompt, c_ctx, z_h, z_r, z_w)

    xs = x_sample + grid_pos_embed(x_sample.shape[1]).astype(x_sample.dtype)
    y_sample, _, _, _ = trunk(xs, c, state_hgrn, state_rglru, state_rwkv)
    return (y_prompt, y_sample, new_state_hgrn, new_state_rglru, new_state_rwkv)
```

```python
import functools
import math

import numpy as np
import jax
import jax.numpy as jnp
from jax import lax
from jax.experimental import pallas as pl
from jax.experimental.pallas import tpu as pltpu

F32 = jnp.float32
BF16 = jnp.bfloat16

D = 1024
SEG = 256
N_CTX_SEG = 16
SEG_PER_SAMPLE = 4
N_SAMPLE = 4
NSEG = N_CTX_SEG + N_SAMPLE * SEG_PER_SAMPLE
M_TOK = NSEG * SEG
ROWS8_PER_SEG = SEG // 8

A_HEADS = 4
A_DK = 128
D_A = 512
D_B = 512
B_BLOCKS = 8
B_BLOCK = 64
LRU_C = 8.0
D_IN_AB = 5 * D_A + 2 * D_B
C_HEAD = 64
C_PAIRS = 8
RW_CHUNK = 64
W_DECAY_SCALE = math.exp(-0.5)
N_EXPERTS = 16
N_GROUPS = 4
GROUP = 4
D_EXPERT = 256
RMS_EPS = 1e-6
GN_EPS = 64e-5
POS_BASE = 10000.0
GRID_W = 64
MOE_TM = 1024
GLA_LEVELS = (1, 2, 4, 8, 16, 32, 64, 128)
EXP_CLAMP = 30.0

VMEM_LIMIT = 56 * 1024 * 1024


def _cp(sem):
    return pltpu.CompilerParams(dimension_semantics=sem, vmem_limit_bytes=VMEM_LIMIT)


def _sigmoid(x):
    return 1.0 / (1.0 + jnp.exp(-x))


def _silu(x):
    return x * _sigmoid(x)


def _gelu_tanh(x):
    return 0.5 * x * (1.0 + jnp.tanh(math.sqrt(2.0 / math.pi) * (x + 0.044715 * (x * x * x))))


def _rms_mod(x, g, scale, shift):
    ms = jnp.mean(x * x, axis=-1, keepdims=True)
    return x * lax.rsqrt(ms + RMS_EPS) * g * (1.0 + scale) + shift


def _dot(a, b):
    return jnp.dot(a.astype(BF16), b.astype(BF16), preferred_element_type=F32)


def _dot_nt(a, b):
    return lax.dot_general(a.astype(BF16), b.astype(BF16), (((1,), (1,)), ((), ())),
                           preferred_element_type=F32)


def _dot_tn(a, b):
    return lax.dot_general(a.astype(BF16), b.astype(BF16), (((0,), (0,)), ((), ())),
                           preferred_element_type=F32)


def _split3(x):
    h = x.astype(BF16)
    r1 = x - h.astype(F32)
    m = r1.astype(BF16)
    r2 = r1 - m.astype(F32)
    return h, m, r2.astype(BF16)


def _dot_sel(mat, x):
    h, m, l = _split3(x)
    return (jnp.dot(mat, h, preferred_element_type=F32) + jnp.dot(mat, m, preferred_element_type=F32)
            + jnp.dot(mat, l, preferred_element_type=F32))


def _dot_x3(a, b):
    ah = a.astype(BF16)
    al = (a - ah.astype(F32)).astype(BF16)
    bh = b.astype(BF16)
    bl = (b - bh.astype(F32)).astype(BF16)
    return (jnp.dot(ah, bh, preferred_element_type=F32) + jnp.dot(ah, bl, preferred_element_type=F32)
            + jnp.dot(al, bh, preferred_element_type=F32))


def _seg_is_first(i):
    return jnp.logical_or(i < N_CTX_SEG, lax.rem(i - N_CTX_SEG, SEG_PER_SAMPLE) == 0)


def _seg_is_last(i):
    return jnp.logical_or(i < N_CTX_SEG, lax.rem(i - N_CTX_SEG, SEG_PER_SAMPLE) == SEG_PER_SAMPLE - 1)


def _sample_of(i):
    return jnp.maximum(i - N_CTX_SEG, 0) // SEG_PER_SAMPLE


def _prev8(i):
    return jnp.maximum(i * ROWS8_PER_SEG - 1, 0)


def _next8(i):
    return jnp.minimum((i + 1) * ROWS8_PER_SEG, M_TOK // 8 - 1)


def _mod_kernel(cv_ref, w_ref, b_ref, o_ref):
    cv = cv_ref[...]
    o_ref[0] = _dot(_silu(cv), w_ref[0]) + b_ref[0]


def _modulations(c, c_ctx, w_mod, b_mod):
    depth = w_mod.shape[0]
    cv = jnp.concatenate([c_ctx[None, :], c, jnp.zeros((3, D), F32)], axis=0)
    n_t = 6
    mod = pl.pallas_call(
        _mod_kernel,
        grid=(depth, n_t),
        in_specs=[pl.BlockSpec((8, D), lambda l, n: (0, 0)),
                  pl.BlockSpec((1, D, D), lambda l, n: (l, 0, n)),
                  pl.BlockSpec((1, 1, D), lambda l, n: (l, 0, n))],
        out_specs=pl.BlockSpec((1, 8, D), lambda l, n: (l, 0, n)),
        out_shape=jax.ShapeDtypeStruct((depth, 8, 6 * D), F32),
        compiler_params=_cp(("arbitrary", "arbitrary")),
        name="adaln_mod",
    )(cv, w_mod, b_mod.reshape(depth, 1, 6 * D))
    row_of_seg = np.array([0] * N_CTX_SEG + [1 + s // SEG_PER_SAMPLE for s in range(N_SAMPLE * SEG_PER_SAMPLE)])
    return mod[:, row_of_seg].reshape(depth, NSEG, 6, D)


def _ab_in_kernel(x_ref, mod_ref, g_ref, w_ref, o_ref):
    m = mod_ref[0]
    h = _rms_mod(x_ref[...], g_ref[...], m[1:2], m[0:1])
    o_ref[...] = jnp.dot(h.astype(BF16), w_ref[...], preferred_element_type=F32)


def _ab_inproj(x, mods, g, w_in):
    return pl.pallas_call(
        _ab_in_kernel,
        grid=(NSEG,),
        in_specs=[pl.BlockSpec((SEG, D), lambda i: (i, 0)),
                  pl.BlockSpec((1, 6, D), lambda i: (i, 0, 0)),
                  pl.BlockSpec((1, D), lambda i: (0, 0)),
                  pl.BlockSpec((D, D_IN_AB), lambda i: (0, 0))],
        out_specs=pl.BlockSpec((SEG, D_IN_AB), lambda i: (i, 0)),
        out_shape=jax.ShapeDtypeStruct((M_TOK, D_IN_AB), F32),
        compiler_params=_cp(("arbitrary",)),
        name="ab_inproj",
    )(x, mods, g.reshape(1, D), w_in)


def _gla_consts():
    t = np.arange(SEG)
    tri_f = (t[None, :] <= t[:, None]).astype(np.float32)
    tri_b = (t[None, :] >= t[:, None]).astype(np.float32)
    sel_f = np.zeros((len(GLA_LEVELS) * SEG, SEG), np.float32)
    sel_b = np.zeros((len(GLA_LEVELS) * SEG, SEG), np.float32)
    for li, w in enumerate(GLA_LEVELS):
        base = (t // (2 * w)) * (2 * w)
        sel_f[li * SEG + t, base + w - 1] = 1.0
        sel_b[li * SEG + t, base + w] = 1.0
    return [jnp.asarray(a, BF16) for a in (tri_f, tri_b, sel_f, sel_b)]


def _gla_dir(qraw, fraw, v, lb, st, tri, sel, rev):
    q = _silu(qraw)
    f = lb + (1.0 - lb) * _sigmoid(fraw)
    g = jnp.log(f)
    k = 1.0 - f
    b = _dot_sel(tri, g)
    beta = jnp.dot(sel, b.astype(BF16), preferred_element_type=F32)
    rowi = lax.broadcasted_iota(jnp.int32, (SEG, A_DK), 0)
    xorm = jnp.bitwise_xor(lax.broadcasted_iota(jnp.int32, (SEG, SEG), 0),
                           lax.broadcasted_iota(jnp.int32, (SEG, SEG), 1))
    att = None
    for li, w in enumerate(GLA_LEVELS):
        x = b - beta[li * SEG:(li + 1) * SEG]
        upper = jnp.bitwise_and(rowi, w) != 0
        qside = jnp.logical_not(upper) if rev else upper
        qp = jnp.where(qside, q * jnp.exp(jnp.minimum(x, EXP_CLAMP)), 0.0)
        kp = jnp.where(qside, 0.0, k * jnp.exp(jnp.minimum(-x, EXP_CLAMP)))
        a = _dot_nt(qp, kp)
        if 2 * w < SEG:
            a = jnp.where(xorm < 2 * w, a, 0.0)
        att = a if att is None else att + a
    diag = jnp.sum(q * k, axis=-1, keepdims=True)
    o = _dot(att, v) + diag * v
    o = o + _dot_nt(q * jnp.exp(b), st)
    btot = b[0:1] if rev else b[SEG - 1:SEG]
    st_new = st * jnp.exp(btot) + _dot_tn(v, k * jnp.exp(btot - b))
    return o, st_new


def _gla_kernel(qf, ff, vf, qb, fb, vb, lb_ref, s0f, s0b, trif, trib, self_, selb,
                of_ref, ob_ref, sf_out, sb_out, st_f, st_b):
    j = pl.program_id(1)
    for rev, qr, fr, vr, s0, tri, sel, o_ref, s_out, st in (
            (False, qf, ff, vf, s0f, trif, self_, of_ref, sf_out, st_f),
            (True, qb, fb, vb, s0b, trib, selb, ob_ref, sb_out, st_b)):
        i = (NSEG - 1 - j) if rev else j
        starts = _seg_is_last(i) if rev else _seg_is_first(i)

        @pl.when(jnp.logical_and(starts, i < N_CTX_SEG))
        def _():
            st[...] = jnp.zeros_like(st)

        @pl.when(jnp.logical_and(starts, i >= N_CTX_SEG))
        def _():
            st[...] = s0[0, 0, 0]

        o, st_new = _gla_dir(qr[...], fr[...], vr[...], lb_ref[1 if rev else 0, 0], st[...],
                             tri[...], sel[...], rev)
        o_ref[...] = o
        st[...] = st_new

        @pl.when(i < N_CTX_SEG)
        def _():
            s_out[0, 0] = st_new


def _gla(proj, lb, s0t):
    consts = _gla_consts()
    blk = lambda col0, rev: pl.BlockSpec(
        (SEG, A_DK), (lambda h, j: (NSEG - 1 - j, col0 + h)) if rev else (lambda h, j: (j, col0 + h)))
    cspec = lambda a: pl.BlockSpec(a.shape, lambda h, j: (0, 0))
    out_blk = lambda rev: pl.BlockSpec(
        (SEG, A_DK), (lambda h, j: (NSEG - 1 - j, h)) if rev else (lambda h, j: (j, h)))
    fin_f = pl.BlockSpec((1, 1, A_DK, A_DK), lambda h, j: (jnp.minimum(j, N_CTX_SEG - 1), h, 0, 0))
    fin_b = pl.BlockSpec((1, 1, A_DK, A_DK), lambda h, j: (jnp.minimum(NSEG - 1 - j, N_CTX_SEG - 1), h, 0, 0))
    return pl.pallas_call(
        _gla_kernel,
        grid=(A_HEADS, NSEG),
        in_specs=[blk(0, False), blk(4, False), blk(12, False),
                  blk(0, True), blk(8, True), blk(12, True),
                  pl.BlockSpec((2, 1, 1, A_DK), lambda h, j: (0, h, 0, 0)),
                  pl.BlockSpec((1, 1, 1, A_DK, A_DK), lambda h, j: (_sample_of(j), 0, h, 0, 0)),
                  pl.BlockSpec((1, 1, 1, A_DK, A_DK), lambda h, j: (_sample_of(NSEG - 1 - j), 1, h, 0, 0)),
                  cspec(consts[0]), cspec(consts[1]), cspec(consts[2]), cspec(consts[3])],
        out_specs=[out_blk(False), out_blk(True), fin_f, fin_b],
        out_shape=[jax.ShapeDtypeStruct((M_TOK, D_A), F32), jax.ShapeDtypeStruct((M_TOK, D_A), F32),
                   jax.ShapeDtypeStruct((N_CTX_SEG, A_HEADS, A_DK, A_DK), F32),
                   jax.ShapeDtypeStruct((N_CTX_SEG, A_HEADS, A_DK, A_DK), F32)],
        scratch_shapes=[pltpu.VMEM((A_DK, A_DK), F32), pltpu.VMEM((A_DK, A_DK), F32)],
        compiler_params=_cp(("arbitrary", "arbitrary")),
        name="hgrn2_gla",
    )(proj, proj, proj, proj, proj, proj, lb, s0t, s0t, *consts)


def _lin_scan(a, b, rev):
    t_len = a.shape[0]
    rowi = lax.broadcasted_iota(jnp.int32, a.shape, 0)
    s = 1
    while s < t_len:
        shift = (t_len - s) if rev else s
        valid = (rowi < t_len - s) if rev else (rowi >= s)
        ap = jnp.where(valid, pltpu.roll(a, shift, 0), 1.0)
        bp = jnp.where(valid, pltpu.roll(b, shift, 0), 0.0)
        b = a * bp + b
        a = a * ap
        s *= 2
    return a, b


def _rglru_dir(x, xprev, xnext, first, last, cw, cb, wa, ba, wx, bx, lam, h_in, rev):
    zero = jnp.zeros_like(xprev)
    ext = jnp.concatenate([jnp.where(first, zero, xprev), x, jnp.where(last, zero, xnext)], axis=0)
    n = ext.shape[0]
    xm2 = pltpu.roll(ext, 2, 0)[8:8 + SEG]
    xm1 = pltpu.roll(ext, 1, 0)[8:8 + SEG]
    xp1 = pltpu.roll(ext, n - 1, 0)[8:8 + SEG]
    xc = cb + xm2 * cw[0:1] + xm1 * cw[1:2] + x * cw[2:3] + xp1 * cw[3:4]
    gate_r = _sigmoid(_dot(xc, wa) + ba)
    gate_i = _sigmoid(_dot(xc, wx) + bx)
    softplus_neg_lam = jnp.maximum(-lam, 0.0) + jnp.log(1.0 + jnp.exp(-jnp.abs(lam)))
    log_a = -LRU_C * gate_r * softplus_neg_lam
    a = jnp.exp(log_a)
    b_in = jnp.sqrt(1.0 - jnp.exp(2.0 * log_a)) * gate_i * xc
    a_c, b_c = _lin_scan(a, b_in, rev)
    h = a_c * h_in + b_c
    h_out = h[0:1] if rev else h[SEG - 1:SEG]
    return h, h_out


def _rglru_kernel(xf, xf_p, xf_n, xb, xb_p, xb_n, cw_ref, cb_ref, wa_ref, ba_ref, wx_ref, bx_ref, lam_ref,
                  s0f, s0b, hf_ref, hb_ref, ff_out, fb_out, hc_f, hc_b):
    j = pl.program_id(0)
    for rev, xr, xp, xn, s0, h_ref, f_out, hc in (
            (False, xf, xf_p, xf_n, s0f, hf_ref, ff_out, hc_f),
            (True, xb, xb_p, xb_n, s0b, hb_ref, fb_out, hc_b)):
        d = 1 if rev else 0
        i = (NSEG - 1 - j) if rev else j
        first = _seg_is_first(i)
        last = _seg_is_last(i)
        starts = last if rev else first

        @pl.when(jnp.logical_and(starts, i < N_CTX_SEG))
        def _():
            hc[...] = jnp.zeros_like(hc)

        @pl.when(jnp.logical_and(starts, i >= N_CTX_SEG))
        def _():
            hc[...] = s0[0, 0]

        h, h_out = _rglru_dir(xr[...], xp[...], xn[...], first, last, cw_ref[...], cb_ref[...],
                              wa_ref[d], ba_ref[d], wx_ref[d], bx_ref[d], lam_ref[d], hc[...], rev)
        h_ref[...] = h
        hc[...] = h_out

        @pl.when(i < N_CTX_SEG)
        def _():
            f_out[0] = h_out


def _rglru(proj, conv_w, conv_b, wa_bd, ba, wx_bd, bx, lam, s0):
    xcol = 5 * D_A // D_B
    fwd = lambda f: (lambda j: f(j))
    bwd = lambda f: (lambda j: f(NSEG - 1 - j))
    seg_blk = lambda m: pl.BlockSpec((SEG, D_B), m(lambda i: (i, xcol)))
    prev_blk = lambda m: pl.BlockSpec((8, D_B), m(lambda i: (_prev8(i), xcol)))
    next_blk = lambda m: pl.BlockSpec((8, D_B), m(lambda i: (_next8(i), xcol)))
    full = lambda a: pl.BlockSpec(a.shape, lambda j: (0,) * a.ndim)
    return pl.pallas_call(
        _rglru_kernel,
        grid=(NSEG,),
        in_specs=[seg_blk(fwd), prev_blk(fwd), next_blk(fwd), seg_blk(bwd), prev_blk(bwd), next_blk(bwd),
                  full(conv_w), full(conv_b), full(wa_bd), full(ba), full(wx_bd), full(bx), full(lam),
                  pl.BlockSpec((1, 1, 1, D_B), lambda j: (_sample_of(j), 0, 0, 0)),
                  pl.BlockSpec((1, 1, 1, D_B), lambda j: (_sample_of(NSEG - 1 - j), 1, 0, 0))],
        out_specs=[pl.BlockSpec((SEG, D_B), lambda j: (j, 0)),
                   pl.BlockSpec((SEG, D_B), lambda j: (NSEG - 1 - j, 0)),
                   pl.BlockSpec((1, 1, D_B), lambda j: (jnp.minimum(j, N_CTX_SEG - 1), 0, 0)),
                   pl.BlockSpec((1, 1, D_B), lambda j: (jnp.minimum(NSEG - 1 - j, N_CTX_SEG - 1), 0, 0))],
        out_shape=[jax.ShapeDtypeStruct((M_TOK, D_B), F32), jax.ShapeDtypeStruct((M_TOK, D_B), F32),
                   jax.ShapeDtypeStruct((N_CTX_SEG, 1, D_B), F32), jax.ShapeDtypeStruct((N_CTX_SEG, 1, D_B), F32)],
        scratch_shapes=[pltpu.VMEM((1, D_B), F32), pltpu.VMEM((1, D_B), F32)],
        compiler_params=_cp(("arbitrary",)),
        name="rglru",
    )(proj, proj, proj, proj, proj, proj, conv_w, conv_b, wa_bd, ba, wx_bd, bx, lam, s0, s0)


def _ab_out_kernel(x_ref, of_ref, ob_ref, og_ref, hf_ref, hb_ref, yr_ref, mod_ref, hg_ref, w_ref, o_ref):
    m = mod_ref[0]
    oa = of_ref[...] + ob_ref[...]
    hg = hg_ref[...]
    parts = []
    for h in range(A_HEADS):
        z = oa[:, h * A_DK:(h + 1) * A_DK]
        parts.append(z * lax.rsqrt(jnp.mean(z * z, axis=-1, keepdims=True) + RMS_EPS) * hg)
    o_a = jnp.concatenate(parts, axis=-1) * _silu(og_ref[...])
    o_b = (hf_ref[...] + hb_ref[...]) * _gelu_tanh(yr_ref[...])
    y = _dot(o_a, w_ref[0:D_A]) + _dot(o_b, w_ref[D_A:D_A + D_B])
    o_ref[...] = x_ref[...] + m[2:3] * y


def _ab_out(x, proj, o_f, o_b, h_f, h_b, mods, hg, w_out):
    seg = lambda width, col: pl.BlockSpec((SEG, width), lambda i: (i, col))
    return pl.pallas_call(
        _ab_out_kernel,
        grid=(NSEG,),
        in_specs=[seg(D, 0), seg(D_A, 0), seg(D_A, 0), seg(D_A, 4), seg(D_B, 0), seg(D_B, 0), seg(D_B, 6),
                  pl.BlockSpec((1, 6, D), lambda i: (i, 0, 0)),
                  pl.BlockSpec((1, A_DK), lambda i: (0, 0)),
                  pl.BlockSpec((D_A + D_B, D), lambda i: (0, 0))],
        out_specs=seg(D, 0),
        out_shape=jax.ShapeDtypeStruct((M_TOK, D), F32),
        compiler_params=_cp(("arbitrary",)),
        name="ab_out",
    )(x, o_f, o_b, proj, h_f, h_b, proj, mods, hg.reshape(1, A_DK), w_out)


def _route(scores, sel):
    cols = [sel[:, e:e + 1] for e in range(N_EXPERTS)]

    def rank(vals):
        out = []
        for i, vi in enumerate(vals):
            r = None
            for jx, vj in enumerate(vals):
                if jx == i:
                    continue
                beats = (vj >= vi) if jx < i else (vj > vi)
                r = beats.astype(F32) if r is None else r + beats.astype(F32)
            out.append(r)
        return out

    grp_scores, in_top2 = [], []
    for gi in range(N_GROUPS):
        vals = cols[gi * GROUP:(gi + 1) * GROUP]
        best_pair = None
        for a in range(GROUP):
            for bx in range(a + 1, GROUP):
                s = vals[a] + vals[bx]
                best_pair = s if best_pair is None else jnp.maximum(best_pair, s)
        grp_scores.append(best_pair)
        in_top2.extend([r < 2.0 for r in rank(vals)])
    grp_best = [r < 1.0 for r in rank(grp_scores)]
    lane = lax.broadcasted_iota(jnp.int32, scores.shape, 1)
    chosen = jnp.zeros(scores.shape, jnp.bool_)
    for e in range(N_EXPERTS):
        pick = jnp.logical_and(grp_best[e // GROUP], in_top2[e])
        chosen = jnp.logical_or(chosen, jnp.logical_and(lane == e, pick))
    picked = jnp.where(chosen, scores, 0.0)
    return picked / jnp.sum(picked, axis=-1, keepdims=True)


def _moe_kernel(final_norm, x_ref, mod_ref, g_ref, rw_ref, rb_ref, w13_ref, w2_ref, gf_ref, o_ref,
                h_sc, comb_sc, acc_sc):
    e = pl.program_id(1)

    @pl.when(e == 0)
    def _():
        for s in range(MOE_TM // SEG):
            m = mod_ref[s]
            rows = slice(s * SEG, (s + 1) * SEG)
            h = _rms_mod(x_ref[rows, :], g_ref[...], m[4:5], m[3:4])
            h_sc[rows, :] = h.astype(BF16)
            logits = _dot_x3(h, rw_ref[...])
            scores = _sigmoid(logits)
            comb_sc[rows, :] = _route(scores, scores + rb_ref[...])
        acc_sc[...] = jnp.zeros_like(acc_sc)

    lane = lax.broadcasted_iota(jnp.int32, (MOE_TM, N_EXPERTS), 1)
    comb = jnp.sum(jnp.where(lane == e, comb_sc[...], 0.0), axis=-1, keepdims=True)
    u = jnp.dot(h_sc[...], w13_ref[0], preferred_element_type=F32)
    hid = _silu(u[:, :D_EXPERT]) * u[:, D_EXPERT:] * comb
    acc_sc[...] += jnp.dot(hid.astype(BF16), w2_ref[0], preferred_element_type=F32)

    @pl.when(e == N_EXPERTS - 1)
    def _():
        for s in range(MOE_TM // SEG):
            rows = slice(s * SEG, (s + 1) * SEG)
            y = x_ref[rows, :] + mod_ref[s][5:6] * acc_sc[rows, :]
            if final_norm:
                y = y * lax.rsqrt(jnp.mean(y * y, axis=-1, keepdims=True) + RMS_EPS) * gf_ref[...]
            o_ref[rows, :] = y


def _moe(x, mods, g, router_w, router_b, w13, w2, gf, final_norm):
    spt = MOE_TM // SEG
    return pl.pallas_call(
        functools.partial(_moe_kernel, final_norm),
        grid=(M_TOK // MOE_TM, N_EXPERTS),
        in_specs=[pl.BlockSpec((MOE_TM, D), lambda t, e: (t, 0)),
                  pl.BlockSpec((spt, 6, D), lambda t, e: (t, 0, 0)),
                  pl.BlockSpec((1, D), lambda t, e: (0, 0)),
                  pl.BlockSpec((D, N_EXPERTS), lambda t, e: (0, 0)),
                  pl.BlockSpec((1, N_EXPERTS), lambda t, e: (0, 0)),
                  pl.BlockSpec((1, D, 2 * D_EXPERT), lambda t, e: (e, 0, 0)),
                  pl.BlockSpec((1, D_EXPERT, D), lambda t, e: (e, 0, 0)),
                  pl.BlockSpec((1, D), lambda t, e: (0, 0))],
        out_specs=pl.BlockSpec((MOE_TM, D), lambda t, e: (t, 0)),
        out_shape=jax.ShapeDtypeStruct((M_TOK, D), F32),
        scratch_shapes=[pltpu.VMEM((MOE_TM, D), BF16), pltpu.VMEM((MOE_TM, N_EXPERTS), F32),
                        pltpu.VMEM((MOE_TM, D), F32)],
        compiler_params=_cp(("arbitrary", "arbitrary")),
        name="moe",
    )(x, mods, g.reshape(1, D), router_w, router_b.reshape(1, N_EXPERTS), w13, w2, gf.reshape(1, D))


def _rw_in_kernel(x_ref, xp_ref, xn_ref, mod_ref, g_ref, mu_ref, wr_ref, wk_ref, wv_ref, g1_ref, g2_ref,
                  w1_ref, w2_ref, w0_ref, a1_ref, a2_ref, a0_ref,
                  r_ref, k_ref, v_ref, gg_ref, lw_ref, a_ref):
    i = pl.program_id(0)
    m = mod_ref[0]
    g = g_ref[...]
    h = _rms_mod(x_ref[...], g, m[1:2], m[0:1])
    hp = jnp.where(_seg_is_first(i), 0.0, _rms_mod(xp_ref[...], g, m[1:2], m[0:1]))
    hn = jnp.where(_seg_is_last(i), 0.0, _rms_mod(xn_ref[...], g, m[1:2], m[0:1]))
    ext = jnp.concatenate([hp, h, hn], axis=0)
    n = ext.shape[0]
    h_prev = pltpu.roll(ext, 1, 0)[8:8 + SEG]
    h_next = pltpu.roll(ext, n - 1, 0)[8:8 + SEG]
    xx = 0.5 * (h_prev + h_next) - h
    mu = mu_ref[...]
    xr, xw, xk, xv, xa, xg = [h + xx * mu[c:c + 1] for c in range(6)]
    r_ref[...] = _dot(xr, wr_ref[...])
    k_ref[...] = _dot(xk, wk_ref[...])
    v_ref[...] = _dot(xv, wv_ref[...])
    gg_ref[...] = _dot(_sigmoid(_dot(xg, g1_ref[...])), g2_ref[...])
    w_in = w0_ref[...] + _dot(jnp.tanh(_dot(xw, w1_ref[...])), w2_ref[...])
    lw_ref[...] = -W_DECAY_SCALE * _sigmoid(w_in)
    a_ref[...] = _sigmoid(a0_ref[...] + _dot(_dot(xa, a1_ref[...]), a2_ref[...]))


def _rw_inproj(x, mods, g, mu, wr, wk, wv, g1, g2, w1c, w2bd, w0c, a1c, a2bd, a0c):
    full = lambda a: pl.BlockSpec(a.shape, lambda i: (0,) * a.ndim)
    seg = lambda width: pl.BlockSpec((SEG, width), lambda i: (i, 0))
    outs = [jax.ShapeDtypeStruct((M_TOK, D), F32)] * 4 + [jax.ShapeDtypeStruct((M_TOK, 2 * D), F32)] * 2
    return pl.pallas_call(
        _rw_in_kernel,
        grid=(NSEG,),
        in_specs=[seg(D),
                  pl.BlockSpec((8, D), lambda i: (_prev8(i), 0)),
                  pl.BlockSpec((8, D), lambda i: (_next8(i), 0)),
                  pl.BlockSpec((1, 6, D), lambda i: (i, 0, 0)),
                  full(g), full(mu), full(wr), full(wk), full(wv), full(g1), full(g2),
                  full(w1c), full(w2bd), full(w0c), full(a1c), full(a2bd), full(a0c)],
        out_specs=[seg(D)] * 4 + [seg(2 * D)] * 2,
        out_shape=outs,
        compiler_params=_cp(("arbitrary",)),
        name="rwkv_inproj",
    )(x, x, x, mods, g, mu, wr, wk, wv, g1, g2, w1c, w2bd, w0c, a1c, a2bd, a0c)


def _rw_consts():
    t = np.arange(SEG)
    same = (t[:, None] // RW_CHUNK) == (t[None, :] // RW_CHUNK)
    tri_f = np.logical_and(same, t[None, :] <= t[:, None]).astype(np.float32)
    tri_b = np.logical_and(same, t[None, :] >= t[:, None]).astype(np.float32)
    return [jnp.asarray(a, BF16) for a in (tri_f, tri_b)]


def _pair_bd(y, head0):
    return jnp.concatenate([jnp.where(head0, y, 0.0), jnp.where(head0, 0.0, y)], axis=0)


def _rw_dir(r, k, v, lw, a, kk_g, ka_g, st, tri, rev):
    c = RW_CHUNK
    lane = lax.broadcasted_iota(jnp.int32, (SEG, 2 * C_HEAD), 1)
    head0 = lane < C_HEAD
    kx = k * kk_g
    ss = kx * kx
    n0 = jnp.sum(jnp.where(head0, ss, 0.0), axis=-1, keepdims=True)
    n1 = jnp.sum(jnp.where(head0, 0.0, ss), axis=-1, keepdims=True)
    kk = kx / jnp.maximum(jnp.sqrt(jnp.where(head0, n0, n1)), 1e-12)
    kd = k * (1.0 + (a - 1.0) * ka_g)
    bhat = kk * a
    cum = _dot_sel(tri, lw)
    e_incl = jnp.exp(cum)
    e_inv = jnp.exp(-cum)
    ae = -kk * jnp.exp(cum - lw)
    re = r * e_incl
    bi = bhat * e_inv
    ki = kd * e_inv

    h0c = lax.broadcasted_iota(jnp.int32, (c, 2 * C_HEAD), 1) < C_HEAD
    rowc = lax.broadcasted_iota(jnp.int32, (c, 2 * C_HEAD), 0)
    colc = jnp.bitwise_and(lax.broadcasted_iota(jnp.int32, (c, 2 * C_HEAD), 1), C_HEAD - 1)
    strict = (colc > rowc) if rev else (colc < rowc)
    incl = (colc >= rowc) if rev else (colc <= rowc)
    eye = (colc == rowc).astype(F32)
    rbd = lax.broadcasted_iota(jnp.int32, (2 * C_HEAD, 2 * C_HEAD), 0) < C_HEAD
    cbd = lax.broadcasted_iota(jnp.int32, (2 * C_HEAD, 2 * C_HEAD), 1) < C_HEAD
    bdmask = rbd == cbd

    outs = [None] * (SEG // c)
    order = range(SEG // c - 1, -1, -1) if rev else range(SEG // c)
    for ci in order:
        sl = slice(ci * c, (ci + 1) * c)
        left = jnp.concatenate([ae[sl], re[sl]], axis=0)
        right = jnp.concatenate([jnp.where(h0c, bi[sl], 0.0), jnp.where(h0c, 0.0, bi[sl]),
                                 jnp.where(h0c, ki[sl], 0.0), jnp.where(h0c, 0.0, ki[sl])], axis=0)
        gm = _dot_nt(left, right)
        n_ab = jnp.where(strict, gm[0:c, 0:2 * c], 0.0)
        a_ak = jnp.where(strict, gm[0:c, 2 * c:4 * c], 0.0)
        a_rb = jnp.where(incl, gm[c:2 * c, 0:2 * c], 0.0)
        a_rk = jnp.where(incl, gm[c:2 * c, 2 * c:4 * c], 0.0)
        tm = eye + n_ab
        p = n_ab
        for _ in range(5):
            p = _dot_x3(p, _pair_bd(p, h0c))
            tm = tm + _dot_x3(tm, _pair_bd(p, h0c))
        vc = v[sl]
        vbd = _pair_bd(vc, h0c)
        rhs = _dot_nt(ae[sl], st) + _dot(a_ak, vbd)
        u = _dot_x3(tm, _pair_bd(rhs, h0c))
        outs[ci] = _dot_nt(re[sl], st) + _dot(a_rb, _pair_bd(u, h0c)) + _dot(a_rk, vbd)
        ctot = cum[ci * c:ci * c + 1] if rev else cum[(ci + 1) * c - 1:(ci + 1) * c]
        dec = jnp.exp(ctot - cum[sl])
        upd = _dot_tn(u, bhat[sl] * dec) + _dot_tn(vc, kd[sl] * dec)
        st = st * jnp.exp(ctot) + jnp.where(bdmask, upd, 0.0)
    return jnp.concatenate(outs, axis=0), st


def _rw_scan_kernel(rf, kf, vf, lwf, af, rb, kb, vb, lwb, ab, kkg_ref, kag_ref, s0f, s0b, trif, trib,
                    of_ref, ob_ref, sf_out, sb_out, st_f, st_b):
    j = pl.program_id(1)
    for rev, refs, s0, tri, o_ref, s_out, st in (
            (False, (rf, kf, vf, lwf, af), s0f, trif, of_ref, sf_out, st_f),
            (True, (rb, kb, vb, lwb, ab), s0b, trib, ob_ref, sb_out, st_b)):
        i = (NSEG - 1 - j) if rev else j
        starts = _seg_is_last(i) if rev else _seg_is_first(i)

        @pl.when(jnp.logical_and(starts, i < N_CTX_SEG))
        def _():
            st[...] = jnp.zeros_like(st)

        @pl.when(jnp.logical_and(starts, i >= N_CTX_SEG))
        def _():
            st[...] = s0[0, 0, 0]

        r_, k_, v_, lw_, a_ = [z[...] for z in refs]
        o, st_new = _rw_dir(r_, k_, v_, lw_, a_, kkg_ref[...], kag_ref[...], st[...], tri[...], rev)
        o_ref[...] = o
        st[...] = st_new

        @pl.when(i < N_CTX_SEG)
        def _():
            s_out[0, 0] = st_new


def _rw_scan(r, k, v, lw, a, kk_g, ka_g, s0bd):
    consts = _rw_consts()
    w = 2 * C_HEAD
    blk = lambda col0, rev: pl.BlockSpec(
        (SEG, w), (lambda p, j: (NSEG - 1 - j, col0 + p)) if rev else (lambda p, j: (j, col0 + p)))
    cspec = lambda arr: pl.BlockSpec(arr.shape, lambda p, j: (0, 0))
    fin_f = pl.BlockSpec((1, 1, w, w), lambda p, j: (jnp.minimum(j, N_CTX_SEG - 1), p, 0, 0))
    fin_b = pl.BlockSpec((1, 1, w, w), lambda p, j: (jnp.minimum(NSEG - 1 - j, N_CTX_SEG - 1), p, 0, 0))
    return pl.pallas_call(
        _rw_scan_kernel,
        grid=(C_PAIRS, NSEG),
        in_specs=[blk(0, False), blk(0, False), blk(0, False), blk(0, False), blk(0, False),
                  blk(0, True), blk(0, True), blk(0, True), blk(C_PAIRS, True), blk(C_PAIRS, True),
                  pl.BlockSpec((1, w), lambda p, j: (0, p)), pl.BlockSpec((1, w), lambda p, j: (0, p)),
                  pl.BlockSpec((1, 1, 1, w, w), lambda p, j: (_sample_of(j), 0, p, 0, 0)),
                  pl.BlockSpec((1, 1, 1, w, w), lambda p, j: (_sample_of(NSEG - 1 - j), 1, p, 0, 0)),
                  cspec(consts[0]), cspec(consts[1])],
        out_specs=[blk(0, False), blk(0, True), fin_f, fin_b],
        out_shape=[jax.ShapeDtypeStruct((M_TOK, D), F32), jax.ShapeDtypeStruct((M_TOK, D), F32),
                   jax.ShapeDtypeStruct((N_CTX_SEG, C_PAIRS, w, w), F32),
                   jax.ShapeDtypeStruct((N_CTX_SEG, C_PAIRS, w, w), F32)],
        scratch_shapes=[pltpu.VMEM((w, w), F32), pltpu.VMEM((w, w), F32)],
        compiler_params=_cp(("arbitrary", "arbitrary")),
        name="rwkv7_scan",
    )(r, k, v, lw, a, r, k, v, lw, a, kk_g, ka_g, s0bd, s0bd, *consts)


def _rw_out_kernel(x_ref, of_ref, ob_ref, r_ref, k_ref, v_ref, a_ref, gg_ref, mod_ref, ka_ref, rk_ref,
                   lnw_ref, lnb_ref, ones_ref, wo_ref, o_ref):
    m = mod_ref[0]
    ones_bd = ones_ref[...]
    w = 2 * C_HEAD
    inv_n = 1.0 / C_HEAD
    parts = []
    for p in range(C_PAIRS):
        cs = slice(p * w, (p + 1) * w)
        osum = of_ref[:, cs] + ob_ref[:, cs]
        mu = _dot_sel_rhs(osum, ones_bd) * inv_n
        cen = osum - mu
        var = _dot_sel_rhs(cen * cen, ones_bd) * inv_n
        o = cen * lax.rsqrt(var + GN_EPS) * lnw_ref[:, cs] + lnb_ref[:, cs]
        r = r_ref[:, cs]
        k = k_ref[:, cs]
        ka = ka_ref[:, cs]
        rkr = r * k * rk_ref[:, cs]
        kd_sum = (1.0 + (a_ref[:, cs] - 1.0) * ka) + (1.0 + (a_ref[:, D + p * w:D + (p + 1) * w] - 1.0) * ka)
        bonus = _dot_sel_rhs(rkr * kd_sum, ones_bd) * v_ref[:, cs]
        parts.append((o + bonus) * gg_ref[:, cs])
    y = _dot(jnp.concatenate(parts, axis=-1), wo_ref[...])
    o_ref[...] = x_ref[...] + m[2:3] * y


def _dot_sel_rhs(x, mat):
    h, m, l = _split3(x)
    return (jnp.dot(h, mat, preferred_element_type=F32) + jnp.dot(m, mat, preferred_element_type=F32)
            + jnp.dot(l, mat, preferred_element_type=F32))


def _rw_out(x, o_f, o_b, r, k, v, a, gg, mods, ka, rk, lnw, lnb, wo):
    seg = lambda width: pl.BlockSpec((SEG, width), lambda i: (i, 0))
    row = pl.BlockSpec((1, D), lambda i: (0, 0))
    hh = np.arange(2 * C_HEAD) // C_HEAD
    ones_bd = jnp.asarray((hh[:, None] == hh[None, :]).astype(np.float32), BF16)
    return pl.pallas_call(
        _rw_out_kernel,
        grid=(NSEG,),
        in_specs=[seg(D), seg(D), seg(D), seg(D), seg(D), seg(D), seg(2 * D), seg(D),
                  pl.BlockSpec((1, 6, D), lambda i: (i, 0, 0)),
                  row, row, row, row,
                  pl.BlockSpec((2 * C_HEAD, 2 * C_HEAD), lambda i: (0, 0)),
                  pl.BlockSpec((D, D), lambda i: (0, 0))],
        out_specs=seg(D),
        out_shape=jax.ShapeDtypeStruct((M_TOK, D), F32),
        compiler_params=_cp(("arbitrary",)),
        name="rwkv_out",
    )(x, o_f, o_b, r, k, v, a, gg, mods, ka.reshape(1, D), rk.reshape(1, D), lnw.reshape(1, D),
      lnb.reshape(1, D), ones_bd, wo)


def _grid_pos_embed(n_tok):
    rows = n_tok // GRID_W
    r, cl = jnp.meshgrid(jnp.arange(rows, dtype=F32), jnp.arange(GRID_W, dtype=F32), indexing='ij')
    quarter = D // 4
    omega = 1.0 / (POS_BASE ** (jnp.arange(quarter, dtype=F32) / quarter))
    ang_r = r.reshape(-1, 1) * omega
    ang_c = cl.reshape(-1, 1) * omega
    return jnp.concatenate([jnp.sin(ang_r), jnp.cos(ang_r), jnp.sin(ang_c), jnp.cos(ang_c)], axis=-1)


def _block_diag(blocks):
    g, n, _ = blocks.shape
    eye = jnp.eye(g, dtype=blocks.dtype)
    return (eye[:, None, :, None] * blocks[:, :, None, :]).reshape(g * n, g * n)


def kernel(x_prompt, x_sample, state_hgrn, state_rglru, state_rwkv, c, c_ctx, norm_mix_g, norm_ffn_g, w_mod, b_mod, ab_w_in, ab_w_out, hgrn_lb, hgrn_norm_g, rg_conv_w, rg_conv_b, rg_wa, rg_ba, rg_wx, rg_bx, rg_lambda, rw_mu, rw_wr, rw_wk, rw_wv, rw_wo, rw_w0, rw_w1, rw_w2, rw_a0, rw_a1, rw_a2, rw_g1, rw_g2, rw_kk, rw_ka, rw_rk, rw_lnw, rw_lnb, moe_router, moe_router_bias, moe_w1, moe_w3, moe_w2, norm_f_g):
    bf = lambda z: z.astype(BF16)
    n_ctx = x_prompt.shape[0]
    xs = x_sample + _grid_pos_embed(x_sample.shape[1])[None]
    x = jnp.concatenate([x_prompt.reshape(-1, D), xs.reshape(-1, D)], axis=0)
    mods = _modulations(c, c_ctx, w_mod, b_mod)
    w13 = bf(jnp.concatenate([moe_w1, moe_w3], axis=-1))
    w2 = bf(moe_w2)

    lower_bounds = jnp.cumsum(jax.nn.softmax(hgrn_lb.astype(F32), axis=1), axis=1)
    lb = lower_bounds[:, 0].reshape(2, A_HEADS, 1, A_DK)
    proj = _ab_inproj(x, mods[0], norm_mix_g[0], bf(ab_w_in[0]))
    s0t = jnp.swapaxes(state_hgrn[:, 0], -1, -2)
    o_f, o_b, sf, sb = _gla(proj, lb, s0t)
    wa_bd = bf(jnp.stack([_block_diag(rg_wa[0, d]) for d in range(2)]))
    wx_bd = bf(jnp.stack([_block_diag(rg_wx[0, d]) for d in range(2)]))
    h_f, h_b, lru_f, lru_b = _rglru(
        proj, rg_conv_w[0], rg_conv_b[0].reshape(1, D_B), wa_bd, rg_ba[0].reshape(2, 1, D_B), wx_bd,
        rg_bx[0].reshape(2, 1, D_B), rg_lambda[0].reshape(2, 1, D_B), state_rglru[:, 0].reshape(-1, 2, 1, D_B))
    x = _ab_out(x, proj, o_f, o_b, h_f, h_b, mods[0], hgrn_norm_g[0], bf(ab_w_out[0]))
    x = _moe(x, mods[0], norm_ffn_g[0], moe_router, moe_router_bias, w13[0], w2[0], norm_f_g, False)

    w1c = bf(jnp.concatenate([rw_w1[0, 0], rw_w1[0, 1]], axis=-1))
    a1c = bf(jnp.concatenate([rw_a1[0, 0], rw_a1[0, 1]], axis=-1))
    w2bd = bf(jnp.concatenate([jnp.concatenate([rw_w2[0, 0], jnp.zeros_like(rw_w2[0, 0])], axis=-1),
                               jnp.concatenate([jnp.zeros_like(rw_w2[0, 1]), rw_w2[0, 1]], axis=-1)], axis=0))
    a2bd = bf(jnp.concatenate([jnp.concatenate([rw_a2[0, 0], jnp.zeros_like(rw_a2[0, 0])], axis=-1),
                               jnp.concatenate([jnp.zeros_like(rw_a2[0, 1]), rw_a2[0, 1]], axis=-1)], axis=0))
    r, k, v, gg, lw, a = _rw_inproj(
        x, mods[1], norm_mix_g[1].reshape(1, D), rw_mu[0], bf(rw_wr[0]), bf(rw_wk[0]), bf(rw_wv[0]),
        bf(rw_g1[0]), bf(rw_g2[0]), w1c, w2bd, rw_w0[0].reshape(1, 2 * D), a1c, a2bd, rw_a0[0].reshape(1, 2 * D))
    s0 = state_rwkv[:, 0].reshape(N_SAMPLE, 2, C_PAIRS, 2, C_HEAD, C_HEAD)
    zeros = jnp.zeros_like(s0[:, :, :, 0])
    s0bd = jnp.concatenate([jnp.concatenate([s0[:, :, :, 0], zeros], axis=-1),
                            jnp.concatenate([zeros, s0[:, :, :, 1]], axis=-1)], axis=-2)
    ow_f, ow_b, rs_f, rs_b = _rw_scan(r, k, v, lw, a, rw_kk[0].reshape(1, D), rw_ka[0].reshape(1, D), s0bd)
    x = _rw_out(x, ow_f, ow_b, r, k, v, a, gg, mods[1], rw_ka[0], rw_rk[0], rw_lnw[0], rw_lnb[0], bf(rw_wo[0]))
    y = _moe(x, mods[1], norm_ffn_g[1], moe_router, moe_router_bias, w13[1], w2[1], norm_f_g, True)

    n_p = n_ctx * x_prompt.shape[1]
    y_prompt = y[:n_p].reshape(x_prompt.shape)
    y_sample = y[n_p:].reshape(x_sample.shape)
    new_hgrn = jnp.swapaxes(jnp.stack([sf, sb], axis=1), -1, -2)[:, None]
    new_rglru = jnp.stack([lru_f[:, 0], lru_b[:, 0]], axis=1)[:, None]

    def unpair(s):
        h0 = s[:, :, :C_HEAD, :C_HEAD]
        h1 = s[:, :, C_HEAD:, C_HEAD:]
        return jnp.stack([h0, h1], axis=2).reshape(n_ctx, 2 * C_PAIRS, C_HEAD, C_HEAD)

    new_rwkv = jnp.stack([unpair(rs_f), unpair(rs_b)], axis=1)[:, None]
    return (y_prompt, y_sample, new_hgrn, new_rglru, new_rwkv)
```

```python
import functools
import math

import numpy as np
import jax
import jax.numpy as jnp
from jax import lax
from jax.experimental import pallas as pl
from jax.experimental.pallas import tpu as pltpu

F32 = jnp.float32
BF16 = jnp.bfloat16

D = 1024
SEG = 256
N_CTX_SEG = 16
SEG_PER_SAMPLE = 4
N_SAMPLE = 4
NSEG = N_CTX_SEG + N_SAMPLE * SEG_PER_SAMPLE
M_TOK = NSEG * SEG
ROWS8_PER_SEG = SEG // 8

A_HEADS = 4
A_DK = 128
D_A = 512
D_B = 512
B_BLOCKS = 8
B_BLOCK = 64
LRU_C = 8.0
D_IN_AB = 5 * D_A + 2 * D_B
C_HEAD = 64
C_PAIRS = 8
RW_CHUNK = 64
W_DECAY_SCALE = math.exp(-0.5)
N_EXPERTS = 16
N_GROUPS = 4
GROUP = 4
D_EXPERT = 256
RMS_EPS = 1e-6
GN_EPS = 64e-5
POS_BASE = 10000.0
GRID_W = 64
MOE_TM = 1024
GLA_LEVELS = (1, 2, 4, 8, 16, 32, 64, 128)
EXP_CLAMP = 30.0

VMEM_LIMIT = 56 * 1024 * 1024


def _cp(sem):
    return pltpu.CompilerParams(dimension_semantics=sem, vmem_limit_bytes=VMEM_LIMIT)


def _sigmoid(x):
    return 1.0 / (1.0 + jnp.exp(-x))


def _silu(x):
    return x * _sigmoid(x)


def _gelu_tanh(x):
    return 0.5 * x * (1.0 + jnp.tanh(math.sqrt(2.0 / math.pi) * (x + 0.044715 * (x * x * x))))


def _rms_mod(x, g, scale, shift):
    ms = jnp.mean(x * x, axis=-1, keepdims=True)
    return x * lax.rsqrt(ms + RMS_EPS) * g * (1.0 + scale) + shift


def _dot(a, b):
    return jnp.dot(a.astype(BF16), b.astype(BF16), preferred_element_type=F32)


def _dot_nt(a, b):
    return lax.dot_general(a.astype(BF16), b.astype(BF16), (((1,), (1,)), ((), ())),
                           preferred_element_type=F32)


def _dot_tn(a, b):
    return lax.dot_general(a.astype(BF16), b.astype(BF16), (((0,), (0,)), ((), ())),
                           preferred_element_type=F32)


def _split3(x):
    h = x.astype(BF16)
    r1 = x - h.astype(F32)
    m = r1.astype(BF16)
    r2 = r1 - m.astype(F32)
    return h, m, r2.astype(BF16)


def _dot_sel(mat, x):
    h, m, l = _split3(x)
    return (jnp.dot(mat, h, preferred_element_type=F32) + jnp.dot(mat, m, preferred_element_type=F32)
            + jnp.dot(mat, l, preferred_element_type=F32))


def _dot_x3(a, b):
    ah = a.astype(BF16)
    al = (a - ah.astype(F32)).astype(BF16)
    bh = b.astype(BF16)
    bl = (b - bh.astype(F32)).astype(BF16)
    return (jnp.dot(ah, bh, preferred_element_type=F32) + jnp.dot(ah, bl, preferred_element_type=F32)
            + jnp.dot(al, bh, preferred_element_type=F32))


def _seg_is_first(i):
    return jnp.logical_or(i < N_CTX_SEG, lax.rem(i - N_CTX_SEG, SEG_PER_SAMPLE) == 0)


def _seg_is_last(i):
    return jnp.logical_or(i < N_CTX_SEG, lax.rem(i - N_CTX_SEG, SEG_PER_SAMPLE) == SEG_PER_SAMPLE - 1)


def _sample_of(i):
    return jnp.maximum(i - N_CTX_SEG, 0) // SEG_PER_SAMPLE


def _prev8(i):
    return jnp.maximum(i * ROWS8_PER_SEG - 1, 0)


def _next8(i):
    return jnp.minimum((i + 1) * ROWS8_PER_SEG, M_TOK // 8 - 1)


def _mod_kernel(cv_ref, w_ref, b_ref, o_ref):
    cv = cv_ref[...]
    o_ref[0] = _dot(_silu(cv), w_ref[0]) + b_ref[0]


def _modulations(c, c_ctx, w_mod, b_mod):
    depth = w_mod.shape[0]
    cv = jnp.concatenate([c_ctx[None, :], c, jnp.zeros((3, D), F32)], axis=0)
    n_t = 6
    mod = pl.pallas_call(
        _mod_kernel,
        grid=(depth, n_t),
        in_specs=[pl.BlockSpec((8, D), lambda l, n: (0, 0)),
                  pl.BlockSpec((1, D, D), lambda l, n: (l, 0, n)),
                  pl.BlockSpec((1, 1, D), lambda l, n: (l, 0, n))],
        out_specs=pl.BlockSpec((1, 8, D), lambda l, n: (l, 0, n)),
        out_shape=jax.ShapeDtypeStruct((depth, 8, 6 * D), F32),
        compiler_params=_cp(("arbitrary", "arbitrary")),
        name="adaln_mod",
    )(cv, w_mod, b_mod.reshape(depth, 1, 6 * D))
    row_of_seg = np.array([0] * N_CTX_SEG + [1 + s // SEG_PER_SAMPLE for s in range(N_SAMPLE * SEG_PER_SAMPLE)])
    return mod[:, row_of_seg].reshape(depth, NSEG, 6, D)


def _ab_in_kernel(x_ref, mod_ref, g_ref, w_ref, o_ref):
    m = mod_ref[0]
    h = _rms_mod(x_ref[...], g_ref[...], m[1:2], m[0:1])
    o_ref[...] = jnp.dot(h.astype(BF16), w_ref[...], preferred_element_type=F32)


def _ab_inproj(x, mods, g, w_in):
    return pl.pallas_call(
        _ab_in_kernel,
        grid=(NSEG,),
        in_specs=[pl.BlockSpec((SEG, D), lambda i: (i, 0)),
                  pl.BlockSpec((1, 6, D), lambda i: (i, 0, 0)),
                  pl.BlockSpec((1, D), lambda i: (0, 0)),
                  pl.BlockSpec((D, D_IN_AB), lambda i: (0, 0))],
        out_specs=pl.BlockSpec((SEG, D_IN_AB), lambda i: (i, 0)),
        out_shape=jax.ShapeDtypeStruct((M_TOK, D_IN_AB), F32),
        compiler_params=_cp(("arbitrary",)),
        name="ab_inproj",
    )(x, mods, g.reshape(1, D), w_in)


def _gla_consts():
    t = np.arange(SEG)
    tri_f = (t[None, :] <= t[:, None]).astype(np.float32)
    tri_b = (t[None, :] >= t[:, None]).astype(np.float32)
    sel_f = np.zeros((len(GLA_LEVELS) * SEG, SEG), np.float32)
    sel_b = np.zeros((len(GLA_LEVELS) * SEG, SEG), np.float32)
    for li, w in enumerate(GLA_LEVELS):
        base = (t // (2 * w)) * (2 * w)
        sel_f[li * SEG + t, base + w - 1] = 1.0
        sel_b[li * SEG + t, base + w] = 1.0
    return [jnp.asarray(a, BF16) for a in (tri_f, tri_b, sel_f, sel_b)]


def _gla_dir(qraw, fraw, v, lb, st, tri, sel, rev):
    q = _silu(qraw)
    f = lb + (1.0 - lb) * _sigmoid(fraw)
    g = jnp.log(f)
    k = 1.0 - f
    b = _dot_sel(tri, g)
    beta = jnp.dot(sel, b.astype(BF16), preferred_element_type=F32)
    rowi = lax.broadcasted_iota(jnp.int32, (SEG, A_DK), 0)
    xorm = jnp.bitwise_xor(lax.broadcasted_iota(jnp.int32, (SEG, SEG), 0),
                           lax.broadcasted_iota(jnp.int32, (SEG, SEG), 1))
    att = None
    for li, w in enumerate(GLA_LEVELS):
        x = b - beta[li * SEG:(li + 1) * SEG]
        upper = jnp.bitwise_and(rowi, w) != 0
        qside = jnp.logical_not(upper) if rev else upper
        qp = jnp.where(qside, q * jnp.exp(jnp.minimum(x, EXP_CLAMP)), 0.0)
        kp = jnp.where(qside, 0.0, k * jnp.exp(jnp.minimum(-x, EXP_CLAMP)))
        a = _dot_nt(qp, kp)
        if 2 * w < SEG:
            a = jnp.where(xorm < 2 * w, a, 0.0)
        att = a if att is None else att + a
    diag = jnp.sum(q * k, axis=-1, keepdims=True)
    o = _dot(att, v) + diag * v
    o = o + _dot_nt(q * jnp.exp(b), st)
    btot = b[0:1] if rev else b[SEG - 1:SEG]
    st_new = st * jnp.exp(btot) + _dot_tn(v, k * jnp.exp(btot - b))
    return o, st_new


def _gla_kernel(qf, ff, vf, qb, fb, vb, lb_ref, s0f, s0b, trif, trib, self_, selb,
                of_ref, ob_ref, sf_out, sb_out, st_f, st_b):
    j = pl.program_id(1)
    for rev, qr, fr, vr, s0, tri, sel, o_ref, s_out, st in (
            (False, qf, ff, vf, s0f, trif, self_, of_ref, sf_out, st_f),
            (True, qb, fb, vb, s0b, trib, selb, ob_ref, sb_out, st_b)):
        i = (NSEG - 1 - j) if rev else j
        starts = _seg_is_last(i) if rev else _seg_is_first(i)

        @pl.when(jnp.logical_and(starts, i < N_CTX_SEG))
        def _():
            st[...] = jnp.zeros_like(st)

        @pl.when(jnp.logical_and(starts, i >= N_CTX_SEG))
        def _():
            st[...] = s0[0, 0, 0]

        o, st_new = _gla_dir(qr[...], fr[...], vr[...], lb_ref[1 if rev else 0, 0], st[...],
                             tri[...], sel[...], rev)
        o_ref[...] = o
        st[...] = st_new

        @pl.when(i < N_CTX_SEG)
        def _():
            s_out[0, 0] = st_new


def _gla(proj, lb, s0t):
    consts = _gla_consts()
    blk = lambda col0, rev: pl.BlockSpec(
        (SEG, A_DK), (lambda h, j: (NSEG - 1 - j, col0 + h)) if rev else (lambda h, j: (j, col0 + h)))
    cspec = lambda a: pl.BlockSpec(a.shape, lambda h, j: (0, 0))
    out_blk = lambda rev: pl.BlockSpec(
        (SEG, A_DK), (lambda h, j: (NSEG - 1 - j, h)) if rev else (lambda h, j: (j, h)))
    fin_f = pl.BlockSpec((1, 1, A_DK, A_DK), lambda h, j: (jnp.minimum(j, N_CTX_SEG - 1), h, 0, 0))
    fin_b = pl.BlockSpec((1, 1, A_DK, A_DK), lambda h, j: (jnp.minimum(NSEG - 1 - j, N_CTX_SEG - 1), h, 0, 0))
    return pl.pallas_call(
        _gla_kernel,
        grid=(A_HEADS, NSEG),
        in_specs=[blk(0, False), blk(4, False), blk(12, False),
                  blk(0, True), blk(8, True), blk(12, True),
                  pl.BlockSpec((2, 1, 1, A_DK), lambda h, j: (0, h, 0, 0)),
                  pl.BlockSpec((1, 1, 1, A_DK, A_DK), lambda h, j: (_sample_of(j), 0, h, 0, 0)),
                  pl.BlockSpec((1, 1, 1, A_DK, A_DK), lambda h, j: (_sample_of(NSEG - 1 - j), 1, h, 0, 0)),
                  cspec(consts[0]), cspec(consts[1]), cspec(consts[2]), cspec(consts[3])],
        out_specs=[out_blk(False), out_blk(True), fin_f, fin_b],
        out_shape=[jax.ShapeDtypeStruct((M_TOK, D_A), F32), jax.ShapeDtypeStruct((M_TOK, D_A), F32),
                   jax.ShapeDtypeStruct((N_CTX_SEG, A_HEADS, A_DK, A_DK), F32),
                   jax.ShapeDtypeStruct((N_CTX_SEG, A_HEADS, A_DK, A_DK), F32)],
        scratch_shapes=[pltpu.VMEM((A_DK, A_DK), F32), pltpu.VMEM((A_DK, A_DK), F32)],
        compiler_params=_cp(("arbitrary", "arbitrary")),
        name="hgrn2_gla",
    )(proj, proj, proj, proj, proj, proj, lb, s0t, s0t, *consts)


def _lin_scan(a, b, rev):
    t_len = a.shape[0]
    rowi = lax.broadcasted_iota(jnp.int32, a.shape, 0)
    s = 1
    while s < t_len:
        shift = (t_len - s) if rev else s
        valid = (rowi < t_len - s) if rev else (rowi >= s)
        ap = jnp.where(valid, pltpu.roll(a, shift, 0), 1.0)
        bp = jnp.where(valid, pltpu.roll(b, shift, 0), 0.0)
        b = a * bp + b
        a = a * ap
        s *= 2
    return a, b


def _rglru_dir(x, xprev, xnext, first, last, cw, cb, wa, ba, wx, bx, lam, h_in, rev):
    zero = jnp.zeros_like(xprev)
    ext = jnp.concatenate([jnp.where(first, zero, xprev), x, jnp.where(last, zero, xnext)], axis=0)
    n = ext.shape[0]
    xm2 = pltpu.roll(ext, 2, 0)[8:8 + SEG]
    xm1 = pltpu.roll(ext, 1, 0)[8:8 + SEG]
    xp1 = pltpu.roll(ext, n - 1, 0)[8:8 + SEG]
    xc = cb + xm2 * cw[0:1] + xm1 * cw[1:2] + x * cw[2:3] + xp1 * cw[3:4]
    gate_r = _sigmoid(_dot(xc, wa) + ba)
    gate_i = _sigmoid(_dot(xc, wx) + bx)
    softplus_neg_lam = jnp.maximum(-lam, 0.0) + jnp.log(1.0 + jnp.exp(-jnp.abs(lam)))
    log_a = -LRU_C * gate_r * softplus_neg_lam
    a = jnp.exp(log_a)
    b_in = jnp.sqrt(1.0 - jnp.exp(2.0 * log_a)) * gate_i * xc
    a_c, b_c = _lin_scan(a, b_in, rev)
    h = a_c * h_in + b_c
    h_out = h[0:1] if rev else h[SEG - 1:SEG]
    return h, h_out


def _rglru_kernel(xf, xf_p, xf_n, xb, xb_p, xb_n, cw_ref, cb_ref, wa_ref, ba_ref, wx_ref, bx_ref, lam_ref,
                  s0f, s0b, hf_ref, hb_ref, ff_out, fb_out, hc_f, hc_b):
    j = pl.program_id(0)
    for rev, xr, xp, xn, s0, h_ref, f_out, hc in (
            (False, xf, xf_p, xf_n, s0f, hf_ref, ff_out, hc_f),
            (True, xb, xb_p, xb_n, s0b, hb_ref, fb_out, hc_b)):
        d = 1 if rev else 0
        i = (NSEG - 1 - j) if rev else j
        first = _seg_is_first(i)
        last = _seg_is_last(i)
        starts = last if rev else first

        @pl.when(jnp.logical_and(starts, i < N_CTX_SEG))
        def _():
            hc[...] = jnp.zeros_like(hc)

        @pl.when(jnp.logical_and(starts, i >= N_CTX_SEG))
        def _():
            hc[...] = s0[0, 0]

        h, h_out = _rglru_dir(xr[...], xp[...], xn[...], first, last, cw_ref[...], cb_ref[...],
                              wa_ref[d], ba_ref[d], wx_ref[d], bx_ref[d], lam_ref[d], hc[...], rev)
        h_ref[...] = h
        hc[...] = h_out

        @pl.when(i < N_CTX_SEG)
        def _():
            f_out[0] = h_out


def _rglru(proj, conv_w, conv_b, wa_bd, ba, wx_bd, bx, lam, s0):
    xcol = 5 * D_A // D_B
    fwd = lambda f: (lambda j: f(j))
    bwd = lambda f: (lambda j: f(NSEG - 1 - j))
    seg_blk = lambda m: pl.BlockSpec((SEG, D_B), m(lambda i: (i, xcol)))
    prev_blk = lambda m: pl.BlockSpec((8, D_B), m(lambda i: (_prev8(i), xcol)))
    next_blk = lambda m: pl.BlockSpec((8, D_B), m(lambda i: (_next8(i), xcol)))
    full = lambda a: pl.BlockSpec(a.shape, lambda j: (0,) * a.ndim)
    return pl.pallas_call(
        _rglru_kernel,
        grid=(NSEG,),
        in_specs=[seg_blk(fwd), prev_blk(fwd), next_blk(fwd), seg_blk(bwd), prev_blk(bwd), next_blk(bwd),
                  full(conv_w), full(conv_b), full(wa_bd), full(ba), full(wx_bd), full(bx), full(lam),
                  pl.BlockSpec((1, 1, 1, D_B), lambda j: (_sample_of(j), 0, 0, 0)),
                  pl.BlockSpec((1, 1, 1, D_B), lambda j: (_sample_of(NSEG - 1 - j), 1, 0, 0))],
        out_specs=[pl.BlockSpec((SEG, D_B), lambda j: (j, 0)),
                   pl.BlockSpec((SEG, D_B), lambda j: (NSEG - 1 - j, 0)),
                   pl.BlockSpec((1, 1, D_B), lambda j: (jnp.minimum(j, N_CTX_SEG - 1), 0, 0)),
                   pl.BlockSpec((1, 1, D_B), lambda j: (jnp.minimum(NSEG - 1 - j, N_CTX_SEG - 1), 0, 0))],
        out_shape=[jax.ShapeDtypeStruct((M_TOK, D_B), F32), jax.ShapeDtypeStruct((M_TOK, D_B), F32),
                   jax.ShapeDtypeStruct((N_CTX_SEG, 1, D_B), F32), jax.ShapeDtypeStruct((N_CTX_SEG, 1, D_B), F32)],
        scratch_shapes=[pltpu.VMEM((1, D_B), F32), pltpu.VMEM((1, D_B), F32)],
        compiler_params=_cp(("arbitrary",)),
        name="rglru",
    )(proj, proj, proj, proj, proj, proj, conv_w, conv_b, wa_bd, ba, wx_bd, bx, lam, s0, s0)


def _ab_out_kernel(x_ref, of_ref, ob_ref, og_ref, hf_ref, hb_ref, yr_ref, mod_ref, hg_ref, w_ref, o_ref):
    m = mod_ref[0]
    oa = of_ref[...] + ob_ref[...]
    hg = hg_ref[...]
    parts = []
    for h in range(A_HEADS):
        z = oa[:, h * A_DK:(h + 1) * A_DK]
        parts.append(z * lax.rsqrt(jnp.mean(z * z, axis=-1, keepdims=True) + RMS_EPS) * hg)
    o_a = jnp.concatenate(parts, axis=-1) * _silu(og_ref[...])
    o_b = (hf_ref[...] + hb_ref[...]) * _gelu_tanh(yr_ref[...])
    y = _dot(o_a, w_ref[0:D_A]) + _dot(o_b, w_ref[D_A:D_A + D_B])
    o_ref[...] = x_ref[...] + m[2:3] * y


def _ab_out(x, proj, o_f, o_b, h_f, h_b, mods, hg, w_out):
    seg = lambda width, col: pl.BlockSpec((SEG, width), lambda i: (i, col))
    return pl.pallas_call(
        _ab_out_kernel,
        grid=(NSEG,),
        in_specs=[seg(D, 0), seg(D_A, 0), seg(D_A, 0), seg(D_A, 4), seg(D_B, 0), seg(D_B, 0), seg(D_B, 6),
                  pl.BlockSpec((1, 6, D), lambda i: (i, 0, 0)),
                  pl.BlockSpec((1, A_DK), lambda i: (0, 0)),
                  pl.BlockSpec((D_A + D_B, D), lambda i: (0, 0))],
        out_specs=seg(D, 0),
        out_shape=jax.ShapeDtypeStruct((M_TOK, D), F32),
        compiler_params=_cp(("arbitrary",)),
        name="ab_out",
    )(x, o_f, o_b, proj, h_f, h_b, proj, mods, hg.reshape(1, A_DK), w_out)


def _route(scores, sel):
    cols = [sel[:, e:e + 1] for e in range(N_EXPERTS)]

    def rank(vals):
        out = []
        for i, vi in enumerate(vals):
            r = None
            for jx, vj in enumerate(vals):
                if jx == i:
                    continue
                beats = (vj >= vi) if jx < i else (vj > vi)
                r = beats.astype(F32) if r is None else r + beats.astype(F32)
            out.append(r)
        return out

    grp_scores, in_top2 = [], []
    for gi in range(N_GROUPS):
        vals = cols[gi * GROUP:(gi + 1) * GROUP]
        best_pair = None
        for a in range(GROUP):
            for bx in range(a + 1, GROUP):
                s = vals[a] + vals[bx]
                best_pair = s if best_pair is None else jnp.maximum(best_pair, s)
        grp_scores.append(best_pair)
        in_top2.extend([r < 2.0 for r in rank(vals)])
    grp_best = [r < 1.0 for r in rank(grp_scores)]
    lane = lax.broadcasted_iota(jnp.int32, scores.shape, 1)
    chosen = jnp.zeros(scores.shape, jnp.bool_)
    for e in range(N_EXPERTS):
        pick = jnp.logical_and(grp_best[e // GROUP], in_top2[e])
        chosen = jnp.logical_or(chosen, jnp.logical_and(lane == e, pick))
    picked = jnp.where(chosen, scores, 0.0)
    return picked / jnp.sum(picked, axis=-1, keepdims=True)


def _moe_kernel(final_norm, x_ref, mod_ref, g_ref, rw_ref, rb_ref, w13_ref, w2_ref, gf_ref, o_ref,
                h_sc, comb_sc, acc_sc):
    e = pl.program_id(1)

    @pl.when(e == 0)
    def _():
        for s in range(MOE_TM // SEG):
            m = mod_ref[s]
            rows = slice(s * SEG, (s + 1) * SEG)
            h = _rms_mod(x_ref[rows, :], g_ref[...], m[4:5], m[3:4])
            h_sc[rows, :] = h.astype(BF16)
            logits = _dot_x3(h, rw_ref[...])
            scores = _sigmoid(logits)
            comb_sc[rows, :] = _route(scores, scores + rb_ref[...])
        acc_sc[...] = jnp.zeros_like(acc_sc)

    lane = lax.broadcasted_iota(jnp.int32, (MOE_TM, N_EXPERTS), 1)
    comb = jnp.sum(jnp.where(lane == e, comb_sc[...], 0.0), axis=-1, keepdims=True)
    u = jnp.dot(h_sc[...], w13_ref[0], preferred_element_type=F32)
    hid = _silu(u[:, :D_EXPERT]) * u[:, D_EXPERT:] * comb
    acc_sc[...] += jnp.dot(hid.astype(BF16), w2_ref[0], preferred_element_type=F32)

    @pl.when(e == N_EXPERTS - 1)
    def _():
        for s in range(MOE_TM // SEG):
            rows = slice(s * SEG, (s + 1) * SEG)
            y = x_ref[rows, :] + mod_ref[s][5:6] * acc_sc[rows, :]
            if final_norm:
                y = y * lax.rsqrt(jnp.mean(y * y, axis=-1, keepdims=True) + RMS_EPS) * gf_ref[...]
            o_ref[rows, :] = y


def _moe(x, mods, g, router_w, router_b, w13, w2, gf, final_norm):
    spt = MOE_TM // SEG
    return pl.pallas_call(
        functools.partial(_moe_kernel, final_norm),
        grid=(M_TOK // MOE_TM, N_EXPERTS),
        in_specs=[pl.BlockSpec((MOE_TM, D), lambda t, e: (t, 0)),
                  pl.BlockSpec((spt, 6, D), lambda t, e: (t, 0, 0)),
                  pl.BlockSpec((1, D), lambda t, e: (0, 0)),
                  pl.BlockSpec((D, N_EXPERTS), lambda t, e: (0, 0)),
                  pl.BlockSpec((1, N_EXPERTS), lambda t, e: (0, 0)),
                  pl.BlockSpec((1, D, 2 * D_EXPERT), lambda t, e: (e, 0, 0)),
                  pl.BlockSpec((1, D_EXPERT, D), lambda t, e: (e, 0, 0)),
                  pl.BlockSpec((1, D), lambda t, e: (0, 0))],
        out_specs=pl.BlockSpec((MOE_TM, D), lambda t, e: (t, 0)),
        out_shape=jax.ShapeDtypeStruct((M_TOK, D), F32),
        scratch_shapes=[pltpu.VMEM((MOE_TM, D), BF16), pltpu.VMEM((MOE_TM, N_EXPERTS), F32),
                        pltpu.VMEM((MOE_TM, D), F32)],
        compiler_params=_cp(("arbitrary", "arbitrary")),
        name="moe",
    )(x, mods, g.reshape(1, D), router_w, router_b.reshape(1, N_EXPERTS), w13, w2, gf.reshape(1, D))


def _rw_in_kernel(x_ref, xp_ref, xn_ref, mod_ref, g_ref, mu_ref, wr_ref, wk_ref, wv_ref, g1_ref, g2_ref,
                  w1_ref, w2_ref, w0_ref, a1_ref, a2_ref, a0_ref,
                  r_ref, k_ref, v_ref, gg_ref, lw_ref, a_ref):
    i = pl.program_id(0)
    m = mod_ref[0]
    g = g_ref[...]
    h = _rms_mod(x_ref[...], g, m[1:2], m[0:1])
    hp = jnp.where(_seg_is_first(i), 0.0, _rms_mod(xp_ref[...], g, m[1:2], m[0:1]))
    hn = jnp.where(_seg_is_last(i), 0.0, _rms_mod(xn_ref[...], g, m[1:2], m[0:1]))
    ext = jnp.concatenate([hp, h, hn], axis=0)
    n = ext.shape[0]
    h_prev = pltpu.roll(ext, 1, 0)[8:8 + SEG]
    h_next = pltpu.roll(ext, n - 1, 0)[8:8 + SEG]
    xx = 0.5 * (h_prev + h_next) - h
    mu = mu_ref[...]
    xr, xw, xk, xv, xa, xg = [h + xx * mu[c:c + 1] for c in range(6)]
    r_ref[...] = _dot(xr, wr_ref[...])
    k_ref[...] = _dot(xk, wk_ref[...])
    v_ref[...] = _dot(xv, wv_ref[...])
    gg_ref[...] = _dot(_sigmoid(_dot(xg, g1_ref[...])), g2_ref[...])
    w_in = w0_ref[...] + _dot(jnp.tanh(_dot(xw, w1_ref[...])), w2_ref[...])
    lw_ref[...] = -W_DECAY_SCALE * _sigmoid(w_in)
    a_ref[...] = _sigmoid(a0_ref[...] + _dot(_dot(xa, a1_ref[...]), a2_ref[...]))


def _rw_inproj(x, mods, g, mu, wr, wk, wv, g1, g2, w1c, w2bd, w0c, a1c, a2bd, a0c):
    full = lambda a: pl.BlockSpec(a.shape, lambda i: (0,) * a.ndim)
    seg = lambda width: pl.BlockSpec((SEG, width), lambda i: (i, 0))
    outs = [jax.ShapeDtypeStruct((M_TOK, D), F32)] * 4 + [jax.ShapeDtypeStruct((M_TOK, 2 * D), F32)] * 2
    return pl.pallas_call(
        _rw_in_kernel,
        grid=(NSEG,),
        in_specs=[seg(D),
                  pl.BlockSpec((8, D), lambda i: (_prev8(i), 0)),
                  pl.BlockSpec((8, D), lambda i: (_next8(i), 0)),
                  pl.BlockSpec((1, 6, D), lambda i: (i, 0, 0)),
                  full(g), full(mu), full(wr), full(wk), full(wv), full(g1), full(g2),
                  full(w1c), full(w2bd), full(w0c), full(a1c), full(a2bd), full(a0c)],
        out_specs=[seg(D)] * 4 + [seg(2 * D)] * 2,
        out_shape=outs,
        compiler_params=_cp(("arbitrary",)),
        name="rwkv_inproj",
    )(x, x, x, mods, g, mu, wr, wk, wv, g1, g2, w1c, w2bd, w0c, a1c, a2bd, a0c)


def _rw_consts():
    t = np.arange(SEG)
    same = (t[:, None] // RW_CHUNK) == (t[None, :] // RW_CHUNK)
    tri_f = np.logical_and(same, t[None, :] <= t[:, None]).astype(np.float32)
    tri_b = np.logical_and(same, t[None, :] >= t[:, None]).astype(np.float32)
    return [jnp.asarray(a, BF16) for a in (tri_f, tri_b)]


def _pair_bd(y, head0):
    return jnp.concatenate([jnp.where(head0, y, 0.0), jnp.where(head0, 0.0, y)], axis=0)


def _rw_dir(r, k, v, lw, a, kk_g, ka_g, tri, rev):
    c = RW_CHUNK
    lane = lax.broadcasted_iota(jnp.int32, (SEG, 2 * C_HEAD), 1)
    head0 = lane < C_HEAD
    kx = k * kk_g
    ss = kx * kx
    n0 = jnp.sum(jnp.where(head0, ss, 0.0), axis=-1, keepdims=True)
    n1 = jnp.sum(jnp.where(head0, 0.0, ss), axis=-1, keepdims=True)
    kk = kx / jnp.maximum(jnp.sqrt(jnp.where(head0, n0, n1)), 1e-12)
    kd = k * (1.0 + (a - 1.0) * ka_g)
    bhat = kk * a
    cum = _dot_sel(tri, lw)
    e_incl = jnp.exp(cum)
    e_inv = jnp.exp(-cum)
    ae = -kk * jnp.exp(cum - lw)
    re = r * e_incl
    bi = bhat * e_inv
    ki = kd * e_inv

    chunks = []
    for ci in range(SEG // c):
        sl = slice(ci * c, (ci + 1) * c)
        ctot = cum[ci * c:ci * c + 1] if rev else cum[(ci + 1) * c - 1:(ci + 1) * c]
        dec = jnp.exp(ctot - cum[sl])
        chunks.append(dict(ae=ae[sl], re=re[sl], bi=bi[sl], ki=ki[sl], v=v[sl], bdec=bhat[sl] * dec,
                           kdec=kd[sl] * dec, gam=jnp.exp(ctot), rev=rev))
    return chunks


def _rw_transitions(chunks):
    c = RW_CHUNK
    w2 = 2 * C_HEAD
    h0c = lax.broadcasted_iota(jnp.int32, (c, w2), 1) < C_HEAD
    rowc = lax.broadcasted_iota(jnp.int32, (c, w2), 0)
    colc = jnp.bitwise_and(lax.broadcasted_iota(jnp.int32, (c, w2), 1), C_HEAD - 1)
    eye = (colc == rowc).astype(F32)
    bdmask = (lax.broadcasted_iota(jnp.int32, (w2, w2), 0) < C_HEAD) == (
        lax.broadcasted_iota(jnp.int32, (w2, w2), 1) < C_HEAD)
    bd = lambda y: _pair_bd(y, h0c)

    n_ab, a_ak, a_rb, a_rk = [], [], [], []
    for ch in chunks:
        strict = (colc > rowc) if ch["rev"] else (colc < rowc)
        incl = (colc >= rowc) if ch["rev"] else (colc <= rowc)
        left = jnp.concatenate([ch["ae"], ch["re"]], axis=0)
        right = jnp.concatenate([jnp.where(h0c, ch["bi"], 0.0), jnp.where(h0c, 0.0, ch["bi"]),
                                 jnp.where(h0c, ch["ki"], 0.0), jnp.where(h0c, 0.0, ch["ki"])], axis=0)
        gm = _dot_nt(left, right)
        n_ab.append(jnp.where(strict, gm[0:c, 0:2 * c], 0.0))
        a_ak.append(jnp.where(strict, gm[0:c, 2 * c:4 * c], 0.0))
        a_rb.append(jnp.where(incl, gm[c:2 * c, 0:2 * c], 0.0))
        a_rk.append(jnp.where(incl, gm[c:2 * c, 2 * c:4 * c], 0.0))
    tm = [eye + n for n in n_ab]
    p = n_ab
    for _ in range(5):
        p = [_dot_x3(x, bd(x)) for x in p]
        tm = [t + _dot_x3(t, bd(x)) for t, x in zip(tm, p)]
    vbd = [bd(ch["v"]) for ch in chunks]
    ta = [_dot_x3(t, bd(ch["ae"])) for t, ch in zip(tm, chunks)]
    av = [_dot(x, vb) for x, vb in zip(a_ak, vbd)]
    tv = [_dot_x3(t, bd(x)) for t, x in zip(tm, av)]
    out = []
    for i, ch in enumerate(chunks):
        q = ch["re"] + _dot(a_rb[i], bd(ta[i]))
        y = _dot(a_rb[i], bd(tv[i])) + _dot(a_rk[i], vbd[i])
        w = jnp.where(bdmask, _dot_tn(ta[i], ch["bdec"]), 0.0)
        z = jnp.where(bdmask, _dot_tn(tv[i], ch["bdec"]) + _dot_tn(ch["v"], ch["kdec"]), 0.0)
        out.append((q, y, w, z, ch["gam"]))
    return out


def _rw_scan_kernel(rf, kf, vf, lwf, af, rb, kb, vb, lwb, ab, kkg_ref, kag_ref, s0f, s0b, trif, trib,
                    of_ref, ob_ref, sf_out, sb_out, st_f, st_b):
    j = pl.program_id(1)
    dirs = ((False, (rf, kf, vf, lwf, af), s0f, trif, of_ref, sf_out, st_f),
            (True, (rb, kb, vb, lwb, ab), s0b, trib, ob_ref, sb_out, st_b))
    chunks = []
    for rev, refs, s0, tri, o_ref, s_out, st in dirs:
        r_, k_, v_, lw_, a_ = [z[...] for z in refs]
        chunks.extend(_rw_dir(r_, k_, v_, lw_, a_, kkg_ref[...], kag_ref[...], tri[...], rev))
    trans = _rw_transitions(chunks)
    n_c = SEG // RW_CHUNK
    for d, (rev, refs, s0, tri, o_ref, s_out, st) in enumerate(dirs):
        i = (NSEG - 1 - j) if rev else j
        starts = _seg_is_last(i) if rev else _seg_is_first(i)

        @pl.when(jnp.logical_and(starts, i < N_CTX_SEG))
        def _():
            st[...] = jnp.zeros_like(st)

        @pl.when(jnp.logical_and(starts, i >= N_CTX_SEG))
        def _():
            st[...] = s0[0, 0, 0]

        s = st[...]
        for ci in (range(n_c - 1, -1, -1) if rev else range(n_c)):
            q, y, w, z, gam = trans[d * n_c + ci]
            o_ref[ci * RW_CHUNK:(ci + 1) * RW_CHUNK, :] = _dot_nt(q, s) + y
            s = s * gam + _dot(s, w) + z
        st[...] = s

        @pl.when(i < N_CTX_SEG)
        def _():
            s_out[0, 0] = s


def _rw_scan(r, k, v, lw, a, kk_g, ka_g, s0bd):
    consts = _rw_consts()
    w = 2 * C_HEAD
    blk = lambda col0, rev: pl.BlockSpec(
        (SEG, w), (lambda p, j: (NSEG - 1 - j, col0 + p)) if rev else (lambda p, j: (j, col0 + p)))
    cspec = lambda arr: pl.BlockSpec(arr.shape, lambda p, j: (0, 0))
    fin_f = pl.BlockSpec((1, 1, w, w), lambda p, j: (jnp.minimum(j, N_CTX_SEG - 1), p, 0, 0))
    fin_b = pl.BlockSpec((1, 1, w, w), lambda p, j: (jnp.minimum(NSEG - 1 - j, N_CTX_SEG - 1), p, 0, 0))
    return pl.pallas_call(
        _rw_scan_kernel,
        grid=(C_PAIRS, NSEG),
        in_specs=[blk(0, False), blk(0, False), blk(0, False), blk(0, False), blk(0, False),
                  blk(0, True), blk(0, True), blk(0, True), blk(C_PAIRS, True), blk(C_PAIRS, True),
                  pl.BlockSpec((1, w), lambda p, j: (0, p)), pl.BlockSpec((1, w), lambda p, j: (0, p)),
                  pl.BlockSpec((1, 1, 1, w, w), lambda p, j: (_sample_of(j), 0, p, 0, 0)),
                  pl.BlockSpec((1, 1, 1, w, w), lambda p, j: (_sample_of(NSEG - 1 - j), 1, p, 0, 0)),
                  cspec(consts[0]), cspec(consts[1])],
        out_specs=[blk(0, False), blk(0, True), fin_f, fin_b],
        out_shape=[jax.ShapeDtypeStruct((M_TOK, D), F32), jax.ShapeDtypeStruct((M_TOK, D), F32),
                   jax.ShapeDtypeStruct((N_CTX_SEG, C_PAIRS, w, w), F32),
                   jax.ShapeDtypeStruct((N_CTX_SEG, C_PAIRS, w, w), F32)],
        scratch_shapes=[pltpu.VMEM((w, w), F32), pltpu.VMEM((w, w), F32)],
        compiler_params=_cp(("arbitrary", "arbitrary")),
        name="rwkv7_scan",
    )(r, k, v, lw, a, r, k, v, lw, a, kk_g, ka_g, s0bd, s0bd, *consts)


def _rw_out_kernel(x_ref, of_ref, ob_ref, r_ref, k_ref, v_ref, a_ref, gg_ref, mod_ref, ka_ref, rk_ref,
                   lnw_ref, lnb_ref, ones_ref, wo_ref, o_ref):
    m = mod_ref[0]
    ones_bd = ones_ref[...]
    w = 2 * C_HEAD
    inv_n = 1.0 / C_HEAD
    parts = []
    for p in range(C_PAIRS):
        cs = slice(p * w, (p + 1) * w)
        osum = of_ref[:, cs] + ob_ref[:, cs]
        mu = _dot_sel_rhs(osum, ones_bd) * inv_n
        cen = osum - mu
        var = _dot_sel_rhs(cen * cen, ones_bd) * inv_n
        o = cen * lax.rsqrt(var + GN_EPS) * lnw_ref[:, cs] + lnb_ref[:, cs]
        r = r_ref[:, cs]
        k = k_ref[:, cs]
        ka = ka_ref[:, cs]
        rkr = r * k * rk_ref[:, cs]
        kd_sum = (1.0 + (a_ref[:, cs] - 1.0) * ka) + (1.0 + (a_ref[:, D + p * w:D + (p + 1) * w] - 1.0) * ka)
        bonus = _dot_sel_rhs(rkr * kd_sum, ones_bd) * v_ref[:, cs]
        parts.append((o + bonus) * gg_ref[:, cs])
    y = _dot(jnp.concatenate(parts, axis=-1), wo_ref[...])
    o_ref[...] = x_ref[...] + m[2:3] * y


def _dot_sel_rhs(x, mat):
    h, m, l = _split3(x)
    return (jnp.dot(h, mat, preferred_element_type=F32) + jnp.dot(m, mat, preferred_element_type=F32)
            + jnp.dot(l, mat, preferred_element_type=F32))


def _rw_out(x, o_f, o_b, r, k, v, a, gg, mods, ka, rk, lnw, lnb, wo):
    seg = lambda width: pl.BlockSpec((SEG, width), lambda i: (i, 0))
    row = pl.BlockSpec((1, D), lambda i: (0, 0))
    hh = np.arange(2 * C_HEAD) // C_HEAD
    ones_bd = jnp.asarray((hh[:, None] == hh[None, :]).astype(np.float32), BF16)
    return pl.pallas_call(
        _rw_out_kernel,
        grid=(NSEG,),
        in_specs=[seg(D), seg(D), seg(D), seg(D), seg(D), seg(D), seg(2 * D), seg(D),
                  pl.BlockSpec((1, 6, D), lambda i: (i, 0, 0)),
                  row, row, row, row,
                  pl.BlockSpec((2 * C_HEAD, 2 * C_HEAD), lambda i: (0, 0)),
                  pl.BlockSpec((D, D), lambda i: (0, 0))],
        out_specs=seg(D),
        out_shape=jax.ShapeDtypeStruct((M_TOK, D), F32),
        compiler_params=_cp(("arbitrary",)),
        name="rwkv_out",
    )(x, o_f, o_b, r, k, v, a, gg, mods, ka.reshape(1, D), rk.reshape(1, D), lnw.reshape(1, D),
      lnb.reshape(1, D), ones_bd, wo)


def _grid_pos_embed(n_tok):
    rows = n_tok // GRID_W
    r, cl = jnp.meshgrid(jnp.arange(rows, dtype=F32), jnp.arange(GRID_W, dtype=F32), indexing='ij')
    quarter = D // 4
    omega = 1.0 / (POS_BASE ** (jnp.arange(quarter, dtype=F32) / quarter))
    ang_r = r.reshape(-1, 1) * omega
    ang_c = cl.reshape(-1, 1) * omega
    return jnp.concatenate([jnp.sin(ang_r), jnp.cos(ang_r), jnp.sin(ang_c), jnp.cos(ang_c)], axis=-1)


def _block_diag(blocks):
    g, n, _ = blocks.shape
    eye = jnp.eye(g, dtype=blocks.dtype)
    return (eye[:, None, :, None] * blocks[:, :, None, :]).reshape(g * n, g * n)


def kernel(x_prompt, x_sample, state_hgrn, state_rglru, state_rwkv, c, c_ctx, norm_mix_g, norm_ffn_g, w_mod, b_mod, ab_w_in, ab_w_out, hgrn_lb, hgrn_norm_g, rg_conv_w, rg_conv_b, rg_wa, rg_ba, rg_wx, rg_bx, rg_lambda, rw_mu, rw_wr, rw_wk, rw_wv, rw_wo, rw_w0, rw_w1, rw_w2, rw_a0, rw_a1, rw_a2, rw_g1, rw_g2, rw_kk, rw_ka, rw_rk, rw_lnw, rw_lnb, moe_router, moe_router_bias, moe_w1, moe_w3, moe_w2, norm_f_g):
    bf = lambda z: z.astype(BF16)
    n_ctx = x_prompt.shape[0]
    xs = x_sample + _grid_pos_embed(x_sample.shape[1])[None]
    x = jnp.concatenate([x_prompt.reshape(-1, D), xs.reshape(-1, D)], axis=0)
    mods = _modulations(c, c_ctx, w_mod, b_mod)
    w13 = bf(jnp.concatenate([moe_w1, moe_w3], axis=-1))
    w2 = bf(moe_w2)

    lower_bounds = jnp.cumsum(jax.nn.softmax(hgrn_lb.astype(F32), axis=1), axis=1)
    lb = lower_bounds[:, 0].reshape(2, A_HEADS, 1, A_DK)
    proj = _ab_inproj(x, mods[0], norm_mix_g[0], bf(ab_w_in[0]))
    s0t = jnp.swapaxes(state_hgrn[:, 0], -1, -2)
    o_f, o_b, sf, sb = _gla(proj, lb, s0t)
    wa_bd = bf(jnp.stack([_block_diag(rg_wa[0, d]) for d in range(2)]))
    wx_bd = bf(jnp.stack([_block_diag(rg_wx[0, d]) for d in range(2)]))
    h_f, h_b, lru_f, lru_b = _rglru(
        proj, rg_conv_w[0], rg_conv_b[0].reshape(1, D_B), wa_bd, rg_ba[0].reshape(2, 1, D_B), wx_bd,
        rg_bx[0].reshape(2, 1, D_B), rg_lambda[0].reshape(2, 1, D_B), state_rglru[:, 0].reshape(-1, 2, 1, D_B))
    x = _ab_out(x, proj, o_f, o_b, h_f, h_b, mods[0], hgrn_norm_g[0], bf(ab_w_out[0]))
    x = _moe(x, mods[0], norm_ffn_g[0], moe_router, moe_router_bias, w13[0], w2[0], norm_f_g, False)

    w1c = bf(jnp.concatenate([rw_w1[0, 0], rw_w1[0, 1]], axis=-1))
    a1c = bf(jnp.concatenate([rw_a1[0, 0], rw_a1[0, 1]], axis=-1))
    w2bd = bf(jnp.concatenate([jnp.concatenate([rw_w2[0, 0], jnp.zeros_like(rw_w2[0, 0])], axis=-1),
                               jnp.concatenate([jnp.zeros_like(rw_w2[0, 1]), rw_w2[0, 1]], axis=-1)], axis=0))
    a2bd = bf(jnp.concatenate([jnp.concatenate([rw_a2[0, 0], jnp.zeros_like(rw_a2[0, 0])], axis=-1),
                               jnp.concatenate([jnp.zeros_like(rw_a2[0, 1]), rw_a2[0, 1]], axis=-1)], axis=0))
    r, k, v, gg, lw, a = _rw_inproj(
        x, mods[1], norm_mix_g[1].reshape(1, D), rw_mu[0], bf(rw_wr[0]), bf(rw_wk[0]), bf(rw_wv[0]),
        bf(rw_g1[0]), bf(rw_g2[0]), w1c, w2bd, rw_w0[0].reshape(1, 2 * D), a1c, a2bd, rw_a0[0].reshape(1, 2 * D))
    s0 = state_rwkv[:, 0].reshape(N_SAMPLE, 2, C_PAIRS, 2, C_HEAD, C_HEAD)
    zeros = jnp.zeros_like(s0[:, :, :, 0])
    s0bd = jnp.concatenate([jnp.concatenate([s0[:, :, :, 0], zeros], axis=-1),
                            jnp.concatenate([zeros, s0[:, :, :, 1]], axis=-1)], axis=-2)
    ow_f, ow_b, rs_f, rs_b = _rw_scan(r, k, v, lw, a, rw_kk[0].reshape(1, D), rw_ka[0].reshape(1, D), s0bd)
    x = _rw_out(x, ow_f, ow_b, r, k, v, a, gg, mods[1], rw_ka[0], rw_rk[0], rw_lnw[0], rw_lnb[0], bf(rw_wo[0]))
    y = _moe(x, mods[1], norm_ffn_g[1], moe_router, moe_router_bias, w13[1], w2[1], norm_f_g, True)

    n_p = n_ctx * x_prompt.shape[1]
    y_prompt = y[:n_p].reshape(x_prompt.shape)
    y_sample = y[n_p:].reshape(x_sample.shape)
    new_hgrn = jnp.swapaxes(jnp.stack([sf, sb], axis=1), -1, -2)[:, None]
    new_rglru = jnp.stack([lru_f[:, 0], lru_b[:, 0]], axis=1)[:, None]

    def unpair(s):
        h0 = s[:, :, :C_HEAD, :C_HEAD]
        h1 = s[:, :, C_HEAD:, C_HEAD:]
        return jnp.stack([h0, h1], axis=2).reshape(n_ctx, 2 * C_PAIRS, C_HEAD, C_HEAD)

    new_rwkv = jnp.stack([unpair(rs_f), unpair(rs_b)], axis=1)[:, None]
    return (y_prompt, y_sample, new_hgrn, new_rglru, new_rwkv)
```

```python
import functools
import math

import numpy as np
import jax
import jax.numpy as jnp
from jax import lax
from jax.experimental import pallas as pl
from jax.experimental.pallas import tpu as pltpu

F32 = jnp.float32
BF16 = jnp.bfloat16

D = 1024
SEG = 256
N_CTX_SEG = 16
SEG_PER_SAMPLE = 4
N_SAMPLE = 4
NSEG = N_CTX_SEG + N_SAMPLE * SEG_PER_SAMPLE
M_TOK = NSEG * SEG
ROWS8_PER_SEG = SEG // 8

A_HEADS = 4
A_DK = 128
D_A = 512
D_B = 512
B_BLOCKS = 8
B_BLOCK = 64
LRU_C = 8.0
D_IN_AB = 5 * D_A + 2 * D_B
C_HEAD = 64
C_PAIRS = 8
RW_CHUNK = 64
RW_PAIRS_STEP = 2
W_DECAY_SCALE = math.exp(-0.5)
N_EXPERTS = 16
N_GROUPS = 4
GROUP = 4
D_EXPERT = 256
RMS_EPS = 1e-6
GN_EPS = 64e-5
POS_BASE = 10000.0
GRID_W = 64
MOE_TM = 1024
COMB_LANES = 128
GLA_LEVELS = (1, 2, 4, 8, 16, 32, 64, 128)
GLA_HALF = 128

VMEM_LIMIT = 56 * 1024 * 1024


def _cp(sem):
    return pltpu.CompilerParams(dimension_semantics=sem, vmem_limit_bytes=VMEM_LIMIT)


def _sigmoid(x):
    return 1.0 / (1.0 + jnp.exp(-x))


def _silu(x):
    return x * _sigmoid(x)


def _gelu_tanh(x):
    return 0.5 * x * (1.0 + jnp.tanh(math.sqrt(2.0 / math.pi) * (x + 0.044715 * (x * x * x))))


def _rms_mod(x, g, scale, shift):
    ms = jnp.mean(x * x, axis=-1, keepdims=True)
    return x * lax.rsqrt(ms + RMS_EPS) * g * (1.0 + scale) + shift


def _dot(a, b):
    return jnp.dot(a.astype(BF16), b.astype(BF16), preferred_element_type=F32)


def _dot_nt(a, b):
    return lax.dot_general(a.astype(BF16), b.astype(BF16), (((1,), (1,)), ((), ())),
                           preferred_element_type=F32)


def _dot_tn(a, b):
    return lax.dot_general(a.astype(BF16), b.astype(BF16), (((0,), (0,)), ((), ())),
                           preferred_element_type=F32)


def _split3(x):
    h = x.astype(BF16)
    r1 = x - h.astype(F32)
    m = r1.astype(BF16)
    r2 = r1 - m.astype(F32)
    return h, m, r2.astype(BF16)


def _dot_sel(mat, x):
    h, m, l = _split3(x)
    return (jnp.dot(mat, h, preferred_element_type=F32) + jnp.dot(mat, m, preferred_element_type=F32)
            + jnp.dot(mat, l, preferred_element_type=F32))


def _dot_x3(a, b):
    ah = a.astype(BF16)
    al = (a - ah.astype(F32)).astype(BF16)
    bh = b.astype(BF16)
    bl = (b - bh.astype(F32)).astype(BF16)
    return (jnp.dot(ah, bh, preferred_element_type=F32) + jnp.dot(ah, bl, preferred_element_type=F32)
            + jnp.dot(al, bh, preferred_element_type=F32))


def _dot_x3_nt(a, b):
    dn = (((1,), (1,)), ((), ()))
    ah = a.astype(BF16)
    al = (a - ah.astype(F32)).astype(BF16)
    bh = b.astype(BF16)
    bl = (b - bh.astype(F32)).astype(BF16)
    return (lax.dot_general(ah, bh, dn, preferred_element_type=F32)
            + lax.dot_general(ah, bl, dn, preferred_element_type=F32)
            + lax.dot_general(al, bh, dn, preferred_element_type=F32))


def _x3_shared(lhs_list, rhs):
    n = lhs_list[0].shape[0]
    k = len(lhs_list)
    rh = rhs.astype(BF16)
    rl = (rhs - rh.astype(F32)).astype(BF16)
    his = [a.astype(BF16) for a in lhs_list]
    los = [(a - h.astype(F32)).astype(BF16) for a, h in zip(lhs_list, his)]
    r1 = jnp.dot(jnp.concatenate(his + los, axis=0), rh, preferred_element_type=F32)
    r2 = jnp.dot(his[0] if k == 1 else jnp.concatenate(his, axis=0), rl, preferred_element_type=F32)
    return [r1[i * n:(i + 1) * n] + r1[(k + i) * n:(k + i + 1) * n] + r2[i * n:(i + 1) * n] for i in range(k)]


def _seg_is_first(i):
    return jnp.logical_or(i < N_CTX_SEG, lax.rem(i - N_CTX_SEG, SEG_PER_SAMPLE) == 0)


def _seg_is_last(i):
    return jnp.logical_or(i < N_CTX_SEG, lax.rem(i - N_CTX_SEG, SEG_PER_SAMPLE) == SEG_PER_SAMPLE - 1)


def _sample_of(i):
    return jnp.maximum(i - N_CTX_SEG, 0) // SEG_PER_SAMPLE


def _prev8(i):
    return jnp.maximum(i * ROWS8_PER_SEG - 1, 0)


def _next8(i):
    return jnp.minimum((i + 1) * ROWS8_PER_SEG, M_TOK // 8 - 1)


def _mod_kernel(cv_ref, w_ref, b_ref, o_ref):
    cv = cv_ref[...]
    o_ref[0] = _dot(_silu(cv), w_ref[0]) + b_ref[0]


def _modulations(c, c_ctx, w_mod, b_mod):
    depth = w_mod.shape[0]
    cv = jnp.concatenate([c_ctx[None, :], c, jnp.zeros((3, D), F32)], axis=0)
    n_t = 6
    mod = pl.pallas_call(
        _mod_kernel,
        grid=(depth, n_t),
        in_specs=[pl.BlockSpec((8, D), lambda l, n: (0, 0)),
                  pl.BlockSpec((1, D, D), lambda l, n: (l, 0, n)),
                  pl.BlockSpec((1, 1, D), lambda l, n: (l, 0, n))],
        out_specs=pl.BlockSpec((1, 8, D), lambda l, n: (l, 0, n)),
        out_shape=jax.ShapeDtypeStruct((depth, 8, 6 * D), F32),
        compiler_params=_cp(("arbitrary", "arbitrary")),
        name="adaln_mod",
    )(cv, w_mod, b_mod.reshape(depth, 1, 6 * D))
    row_of_seg = np.array([0] * N_CTX_SEG + [1 + s // SEG_PER_SAMPLE for s in range(N_SAMPLE * SEG_PER_SAMPLE)])
    return mod[:, row_of_seg].reshape(depth, NSEG, 6, D)


def _ab_in_kernel(x_ref, mod_ref, g_ref, w_ref, o_ref):
    m = mod_ref[0]
    h = _rms_mod(x_ref[...], g_ref[...], m[1:2], m[0:1])
    o_ref[...] = jnp.dot(h.astype(BF16), w_ref[...], preferred_element_type=F32)


def _ab_inproj(x, mods, g, w_in):
    return pl.pallas_call(
        _ab_in_kernel,
        grid=(NSEG,),
        in_specs=[pl.BlockSpec((SEG, D), lambda i: (i, 0)),
                  pl.BlockSpec((1, 6, D), lambda i: (i, 0, 0)),
                  pl.BlockSpec((1, D), lambda i: (0, 0)),
                  pl.BlockSpec((D, D_IN_AB), lambda i: (0, 0))],
        out_specs=pl.BlockSpec((SEG, D_IN_AB), lambda i: (i, 0)),
        out_shape=jax.ShapeDtypeStruct((M_TOK, D_IN_AB), F32),
        compiler_params=_cp(("arbitrary",)),
        name="ab_inproj",
    )(x, mods, g.reshape(1, D), w_in)


def _gla_consts():
    t = np.arange(SEG)
    tri_f = (t[None, :] <= t[:, None]).astype(np.float32)
    tri_b = (t[None, :] >= t[:, None]).astype(np.float32)
    th = np.arange(GLA_HALF)
    xor = th[:, None] ^ th[None, :]
    hb = np.where(xor > 0, 1 << np.floor(np.log2(np.maximum(xor, 1))).astype(np.int64), 0)
    code_f = np.where(th[None, :] < th[:, None], hb, 0).astype(np.int32)
    code_b = np.where(th[None, :] > th[:, None], hb, 0).astype(np.int32)
    return [jnp.asarray(tri_f, BF16), jnp.asarray(tri_b, BF16), jnp.asarray(code_f), jnp.asarray(code_b)]


def _gla_level_offsets(b, g, qside, w, rev):
    if w == 1:
        return jnp.where(qside, g, 0.0)
    if 2 * w >= 16:
        nv = 2 * w // 8
        b4 = b.reshape(SEG // (2 * w), nv, 8, A_DK)
        ref = b4[:, w // 8:w // 8 + 1, 0:1, :] if rev else b4[:, w // 8 - 1:w // 8, 7:8, :]
        x = (b4 - ref).reshape(SEG, A_DK)
    else:
        b3 = b.reshape(SEG // 8, 8, A_DK)
        sub = lax.broadcasted_iota(jnp.int32, b3.shape, 1)
        beta = None
        for jb in range(8 // (2 * w)):
            r = jb * 2 * w + (w if rev else w - 1)
            cand = jnp.broadcast_to(b3[:, r:r + 1, :], b3.shape)
            beta = cand if beta is None else jnp.where(sub >= jb * 2 * w, cand, beta)
        x = (b3 - beta).reshape(SEG, A_DK)
    return jnp.where(qside, x, -x)


def _gla_dir(qraw, fraw, v, lb, st, tri, code, rev):
    hh = GLA_HALF
    q = _silu(qraw)
    f = lb + (1.0 - lb) * _sigmoid(fraw)
    g = jnp.log(f)
    k = 1.0 - f
    b = _dot_sel(tri, g)
    rowi = lax.broadcasted_iota(jnp.int32, (SEG, A_DK), 0)
    att = [jnp.zeros((hh, hh), F32), jnp.zeros((hh, hh), F32)]
    cross = None
    for w in GLA_LEVELS:
        upper = jnp.bitwise_and(rowi, w) != 0
        qside = jnp.logical_not(upper) if rev else upper
        z = _gla_level_offsets(b, g, qside, w, rev)
        m = (jnp.where(qside, q, k) * jnp.exp(z)).astype(BF16)
        if w == hh:
            cross = _dot_nt(m[0:hh], m[hh:SEG]) if rev else _dot_nt(m[hh:SEG], m[0:hh])
        else:
            for half in range(2):
                mh = m[half * hh:(half + 1) * hh]
                att[half] = jnp.where(code == w, _dot_nt(mh, mh), att[half])
    if rev:
        o_lo = _dot(jnp.concatenate([att[0], cross], axis=1), v)
        o_hi = _dot(att[1], v[hh:SEG])
    else:
        o_lo = _dot(att[0], v[0:hh])
        o_hi = _dot(jnp.concatenate([cross, att[1]], axis=1), v)
    diag = jnp.sum(q * k, axis=-1, keepdims=True)
    o = jnp.concatenate([o_lo, o_hi], axis=0) + diag * v + _dot_nt(q * jnp.exp(b), st)
    btot = b[0:1] if rev else b[SEG - 1:SEG]
    st_new = st * jnp.exp(btot) + _dot_tn(v, k * jnp.exp(btot - b))
    return o, st_new


def _gla_kernel(qf, ff, vf, qb, fb, vb, lb_ref, s0f, s0b, trif, trib, codef, codeb,
                of_ref, ob_ref, sf_out, sb_out, st_f, st_b):
    j = pl.program_id(1)
    for rev, qr, fr, vr, s0, tri, code, o_ref, s_out, st in (
            (False, qf, ff, vf, s0f, trif, codef, of_ref, sf_out, st_f),
            (True, qb, fb, vb, s0b, trib, codeb, ob_ref, sb_out, st_b)):
        i = (NSEG - 1 - j) if rev else j
        starts = _seg_is_last(i) if rev else _seg_is_first(i)

        @pl.when(jnp.logical_and(starts, i < N_CTX_SEG))
        def _():
            st[...] = jnp.zeros_like(st)

        @pl.when(jnp.logical_and(starts, i >= N_CTX_SEG))
        def _():
            st[...] = s0[0, 0, 0]

        o, st_new = _gla_dir(qr[...], fr[...], vr[...], lb_ref[1 if rev else 0, 0], st[...],
                             tri[...], code[...], rev)
        o_ref[...] = o
        st[...] = st_new

        @pl.when(i < N_CTX_SEG)
        def _():
            s_out[0, 0] = st_new


def _gla(proj, lb, s0t):
    consts = _gla_consts()
    blk = lambda col0, rev: pl.BlockSpec(
        (SEG, A_DK), (lambda h, j: (NSEG - 1 - j, col0 + h)) if rev else (lambda h, j: (j, col0 + h)))
    cspec = lambda a: pl.BlockSpec(a.shape, lambda h, j: (0, 0))
    out_blk = lambda rev: pl.BlockSpec(
        (SEG, A_DK), (lambda h, j: (NSEG - 1 - j, h)) if rev else (lambda h, j: (j, h)))
    fin_f = pl.BlockSpec((1, 1, A_DK, A_DK), lambda h, j: (jnp.minimum(j, N_CTX_SEG - 1), h, 0, 0))
    fin_b = pl.BlockSpec((1, 1, A_DK, A_DK), lambda h, j: (jnp.minimum(NSEG - 1 - j, N_CTX_SEG - 1), h, 0, 0))
    return pl.pallas_call(
        _gla_kernel,
        grid=(A_HEADS, NSEG),
        in_specs=[blk(0, False), blk(4, False), blk(12, False),
                  blk(0, True), blk(8, True), blk(12, True),
                  pl.BlockSpec((2, 1, 1, A_DK), lambda h, j: (0, h, 0, 0)),
                  pl.BlockSpec((1, 1, 1, A_DK, A_DK), lambda h, j: (_sample_of(j), 0, h, 0, 0)),
                  pl.BlockSpec((1, 1, 1, A_DK, A_DK), lambda h, j: (_sample_of(NSEG - 1 - j), 1, h, 0, 0)),
                  cspec(consts[0]), cspec(consts[1]), cspec(consts[2]), cspec(consts[3])],
        out_specs=[out_blk(False), out_blk(True), fin_f, fin_b],
        out_shape=[jax.ShapeDtypeStruct((M_TOK, D_A), F32), jax.ShapeDtypeStruct((M_TOK, D_A), F32),
                   jax.ShapeDtypeStruct((N_CTX_SEG, A_HEADS, A_DK, A_DK), F32),
                   jax.ShapeDtypeStruct((N_CTX_SEG, A_HEADS, A_DK, A_DK), F32)],
        scratch_shapes=[pltpu.VMEM((A_DK, A_DK), F32), pltpu.VMEM((A_DK, A_DK), F32)],
        compiler_params=_cp(("arbitrary", "arbitrary")),
        name="hgrn2_gla",
    )(proj, proj, proj, proj, proj, proj, lb, s0t, s0t, *consts)


def _lin_scan(a, b, rev):
    t_len = a.shape[0]
    rowi = lax.broadcasted_iota(jnp.int32, a.shape, 0)
    s = 1
    while s < t_len:
        shift = (t_len - s) if rev else s
        valid = (rowi < t_len - s) if rev else (rowi >= s)
        ap = jnp.where(valid, pltpu.roll(a, shift, 0), 1.0)
        bp = jnp.where(valid, pltpu.roll(b, shift, 0), 0.0)
        b = a * bp + b
        a = a * ap
        s *= 2
    return a, b


def _rglru_dir(x, xprev, xnext, first, last, cw, cb, wa, ba, wx, bx, lam, h_in, rev):
    zero = jnp.zeros_like(xprev)
    ext = jnp.concatenate([jnp.where(first, zero, xprev), x, jnp.where(last, zero, xnext)], axis=0)
    n = ext.shape[0]
    xm2 = pltpu.roll(ext, 2, 0)[8:8 + SEG]
    xm1 = pltpu.roll(ext, 1, 0)[8:8 + SEG]
    xp1 = pltpu.roll(ext, n - 1, 0)[8:8 + SEG]
    xc = cb + xm2 * cw[0:1] + xm1 * cw[1:2] + x * cw[2:3] + xp1 * cw[3:4]
    gate_r = _sigmoid(_dot(xc, wa) + ba)
    gate_i = _sigmoid(_dot(xc, wx) + bx)
    softplus_neg_lam = jnp.maximum(-lam, 0.0) + jnp.log(1.0 + jnp.exp(-jnp.abs(lam)))
    log_a = -LRU_C * gate_r * softplus_neg_lam
    a = jnp.exp(log_a)
    b_in = jnp.sqrt(1.0 - jnp.exp(2.0 * log_a)) * gate_i * xc
    a_c, b_c = _lin_scan(a, b_in, rev)
    h = a_c * h_in + b_c
    h_out = h[0:1] if rev else h[SEG - 1:SEG]
    return h, h_out


def _rglru_kernel(xf, xf_p, xf_n, xb, xb_p, xb_n, cw_ref, cb_ref, wa_ref, ba_ref, wx_ref, bx_ref, lam_ref,
                  s0f, s0b, hf_ref, hb_ref, ff_out, fb_out, hc_f, hc_b):
    j = pl.program_id(0)
    for rev, xr, xp, xn, s0, h_ref, f_out, hc in (
            (False, xf, xf_p, xf_n, s0f, hf_ref, ff_out, hc_f),
            (True, xb, xb_p, xb_n, s0b, hb_ref, fb_out, hc_b)):
        d = 1 if rev else 0
        i = (NSEG - 1 - j) if rev else j
        first = _seg_is_first(i)
        last = _seg_is_last(i)
        starts = last if rev else first

        @pl.when(jnp.logical_and(starts, i < N_CTX_SEG))
        def _():
            hc[...] = jnp.zeros_like(hc)

        @pl.when(jnp.logical_and(starts, i >= N_CTX_SEG))
        def _():
            hc[...] = s0[0, 0]

        h, h_out = _rglru_dir(xr[...], xp[...], xn[...], first, last, cw_ref[...], cb_ref[...],
                              wa_ref[d], ba_ref[d], wx_ref[d], bx_ref[d], lam_ref[d], hc[...], rev)
        h_ref[...] = h
        hc[...] = h_out

        @pl.when(i < N_CTX_SEG)
        def _():
            f_out[0] = h_out


def _rglru(proj, conv_w, conv_b, wa_bd, ba, wx_bd, bx, lam, s0):
    xcol = 5 * D_A // D_B
    fwd = lambda f: (lambda j: f(j))
    bwd = lambda f: (lambda j: f(NSEG - 1 - j))
    seg_blk = lambda m: pl.BlockSpec((SEG, D_B), m(lambda i: (i, xcol)))
    prev_blk = lambda m: pl.BlockSpec((8, D_B), m(lambda i: (_prev8(i), xcol)))
    next_blk = lambda m: pl.BlockSpec((8, D_B), m(lambda i: (_next8(i), xcol)))
    full = lambda a: pl.BlockSpec(a.shape, lambda j: (0,) * a.ndim)
    return pl.pallas_call(
        _rglru_kernel,
        grid=(NSEG,),
        in_specs=[seg_blk(fwd), prev_blk(fwd), next_blk(fwd), seg_blk(bwd), prev_blk(bwd), next_blk(bwd),
                  full(conv_w), full(conv_b), full(wa_bd), full(ba), full(wx_bd), full(bx), full(lam),
                  pl.BlockSpec((1, 1, 1, D_B), lambda j: (_sample_of(j), 0, 0, 0)),
                  pl.BlockSpec((1, 1, 1, D_B), lambda j: (_sample_of(NSEG - 1 - j), 1, 0, 0))],
        out_specs=[pl.BlockSpec((SEG, D_B), lambda j: (j, 0)),
                   pl.BlockSpec((SEG, D_B), lambda j: (NSEG - 1 - j, 0)),
                   pl.BlockSpec((1, 1, D_B), lambda j: (jnp.minimum(j, N_CTX_SEG - 1), 0, 0)),
                   pl.BlockSpec((1, 1, D_B), lambda j: (jnp.minimum(NSEG - 1 - j, N_CTX_SEG - 1), 0, 0))],
        out_shape=[jax.ShapeDtypeStruct((M_TOK, D_B), F32), jax.ShapeDtypeStruct((M_TOK, D_B), F32),
                   jax.ShapeDtypeStruct((N_CTX_SEG, 1, D_B), F32), jax.ShapeDtypeStruct((N_CTX_SEG, 1, D_B), F32)],
        scratch_shapes=[pltpu.VMEM((1, D_B), F32), pltpu.VMEM((1, D_B), F32)],
        compiler_params=_cp(("arbitrary",)),
        name="rglru",
    )(proj, proj, proj, proj, proj, proj, conv_w, conv_b, wa_bd, ba, wx_bd, bx, lam, s0, s0)


def _ab_out_kernel(x_ref, of_ref, ob_ref, og_ref, hf_ref, hb_ref, yr_ref, mod_ref, hg_ref, w_ref, o_ref):
    m = mod_ref[0]
    oa = of_ref[...] + ob_ref[...]
    hg = hg_ref[...]
    parts = []
    for h in range(A_HEADS):
        z = oa[:, h * A_DK:(h + 1) * A_DK]
        parts.append(z * lax.rsqrt(jnp.mean(z * z, axis=-1, keepdims=True) + RMS_EPS) * hg)
    o_a = jnp.concatenate(parts, axis=-1) * _silu(og_ref[...])
    o_b = (hf_ref[...] + hb_ref[...]) * _gelu_tanh(yr_ref[...])
    y = _dot(o_a, w_ref[0:D_A]) + _dot(o_b, w_ref[D_A:D_A + D_B])
    o_ref[...] = x_ref[...] + m[2:3] * y


def _ab_out(x, proj, o_f, o_b, h_f, h_b, mods, hg, w_out):
    seg = lambda width, col: pl.BlockSpec((SEG, width), lambda i: (i, col))
    return pl.pallas_call(
        _ab_out_kernel,
        grid=(NSEG,),
        in_specs=[seg(D, 0), seg(D_A, 0), seg(D_A, 0), seg(D_A, 4), seg(D_B, 0), seg(D_B, 0), seg(D_B, 6),
                  pl.BlockSpec((1, 6, D), lambda i: (i, 0, 0)),
                  pl.BlockSpec((1, A_DK), lambda i: (0, 0)),
                  pl.BlockSpec((D_A + D_B, D), lambda i: (0, 0))],
        out_specs=seg(D, 0),
        out_shape=jax.ShapeDtypeStruct((M_TOK, D), F32),
        compiler_params=_cp(("arbitrary",)),
        name="ab_out",
    )(x, o_f, o_b, proj, h_f, h_b, proj, mods, hg.reshape(1, A_DK), w_out)


def _route(scores, sel):
    cols = [sel[e:e + 1, :] for e in range(N_EXPERTS)]

    def rank(vals):
        out = []
        for i, vi in enumerate(vals):
            r = None
            for jx, vj in enumerate(vals):
                if jx == i:
                    continue
                beats = (vj >= vi) if jx < i else (vj > vi)
                r = beats.astype(F32) if r is None else r + beats.astype(F32)
            out.append(r)
        return out

    grp_scores, in_top2 = [], []
    for gi in range(N_GROUPS):
        vals = cols[gi * GROUP:(gi + 1) * GROUP]
        best_pair = None
        for a in range(GROUP):
            for bx in range(a + 1, GROUP):
                s = vals[a] + vals[bx]
                best_pair = s if best_pair is None else jnp.maximum(best_pair, s)
        grp_scores.append(best_pair)
        in_top2.extend([r < 2.0 for r in rank(vals)])
    grp_best = [r < 1.0 for r in rank(grp_scores)]
    picked = [jnp.where(jnp.logical_and(grp_best[e // GROUP], in_top2[e]), scores[e:e + 1, :], 0.0)
              for e in range(N_EXPERTS)]
    total = picked[0]
    for pe in picked[1:]:
        total = total + pe
    row = lax.broadcasted_iota(jnp.int32, scores.shape, 0)
    comb = jnp.zeros(scores.shape, F32)
    for e in range(N_EXPERTS):
        comb = jnp.where(row == e, picked[e] / total, comb)
    return comb


def _moe_kernel(final_norm, x_ref, mod_ref, g_ref, rw_ref, rb_ref, w1_ref, w3_ref, w2_ref, gf_ref, o_ref,
                h_sc, comb_sc, acc_sc):
    e = pl.program_id(1)

    @pl.when(e == 0)
    def _():
        for s in range(MOE_TM // SEG):
            m = mod_ref[s]
            rows = slice(s * SEG, (s + 1) * SEG)
            h = _rms_mod(x_ref[rows, :], g_ref[...], m[4:5], m[3:4])
            h_sc[rows, :] = h.astype(BF16)
            scores = _sigmoid(_dot_x3_nt(rw_ref[...], h))
            comb_t = _route(scores, scores + rb_ref[...])
            comb_t = jnp.concatenate([comb_t, jnp.zeros((COMB_LANES - N_EXPERTS, SEG), F32)], axis=0)
            comb_sc[rows, :] = comb_t.T
        acc_sc[...] = jnp.zeros_like(acc_sc)

    lane = lax.broadcasted_iota(jnp.int32, (MOE_TM, COMB_LANES), 1)
    comb = jnp.sum(jnp.where(lane == e, comb_sc[...], 0.0), axis=-1, keepdims=True)
    h = h_sc[...]
    u1 = jnp.dot(h, w1_ref[0, 0].astype(BF16), preferred_element_type=F32)
    u3 = jnp.dot(h, w3_ref[0, 0].astype(BF16), preferred_element_type=F32)
    hid = _silu(u1) * u3 * comb
    acc_sc[...] += jnp.dot(hid.astype(BF16), w2_ref[0, 0].astype(BF16), preferred_element_type=F32)

    @pl.when(e == N_EXPERTS - 1)
    def _():
        for s in range(MOE_TM // SEG):
            rows = slice(s * SEG, (s + 1) * SEG)
            y = x_ref[rows, :] + mod_ref[s][5:6] * acc_sc[rows, :]
            if final_norm:
                y = y * lax.rsqrt(jnp.mean(y * y, axis=-1, keepdims=True) + RMS_EPS) * gf_ref[...]
            o_ref[rows, :] = y


def _moe(x, mods, g, router_w, router_b, w1, w3, w2, layer, gf, final_norm):
    spt = MOE_TM // SEG
    return pl.pallas_call(
        functools.partial(_moe_kernel, final_norm),
        grid=(M_TOK // MOE_TM, N_EXPERTS),
        in_specs=[pl.BlockSpec((MOE_TM, D), lambda t, e: (t, 0)),
                  pl.BlockSpec((spt, 6, D), lambda t, e: (t, 0, 0)),
                  pl.BlockSpec((1, D), lambda t, e: (0, 0)),
                  pl.BlockSpec((N_EXPERTS, D), lambda t, e: (0, 0)),
                  pl.BlockSpec((N_EXPERTS, 1), lambda t, e: (0, 0)),
                  pl.BlockSpec((1, 1, D, D_EXPERT), lambda t, e: (layer, e, 0, 0)),
                  pl.BlockSpec((1, 1, D, D_EXPERT), lambda t, e: (layer, e, 0, 0)),
                  pl.BlockSpec((1, 1, D_EXPERT, D), lambda t, e: (layer, e, 0, 0)),
                  pl.BlockSpec((1, D), lambda t, e: (0, 0))],
        out_specs=pl.BlockSpec((MOE_TM, D), lambda t, e: (t, 0)),
        out_shape=jax.ShapeDtypeStruct((M_TOK, D), F32),
        scratch_shapes=[pltpu.VMEM((MOE_TM, D), BF16), pltpu.VMEM((MOE_TM, COMB_LANES), F32),
                        pltpu.VMEM((MOE_TM, D), F32)],
        compiler_params=_cp(("arbitrary", "arbitrary")),
        name="moe",
    )(x, mods, g.reshape(1, D), router_w.T, router_b.reshape(N_EXPERTS, 1), w1, w3, w2, gf.reshape(1, D))


def _rw_in_kernel(x_ref, xp_ref, xn_ref, mod_ref, g_ref, mu_ref, wr_ref, wk_ref, wv_ref, g1_ref, g2_ref,
                  w1_ref, w2_ref, w0_ref, a1_ref, a2_ref, a0_ref,
                  r_ref, k_ref, v_ref, gg_ref, lw_ref, a_ref):
    i = pl.program_id(0)
    m = mod_ref[0]
    g = g_ref[...]
    h = _rms_mod(x_ref[...], g, m[1:2], m[0:1])
    hp = jnp.where(_seg_is_first(i), 0.0, _rms_mod(xp_ref[...], g, m[1:2], m[0:1]))
    hn = jnp.where(_seg_is_last(i), 0.0, _rms_mod(xn_ref[...], g, m[1:2], m[0:1]))
    ext = jnp.concatenate([hp, h, hn], axis=0)
    n = ext.shape[0]
    h_prev = pltpu.roll(ext, 1, 0)[8:8 + SEG]
    h_next = pltpu.roll(ext, n - 1, 0)[8:8 + SEG]
    xx = 0.5 * (h_prev + h_next) - h
    mu = mu_ref[...]
    xr, xw, xk, xv, xa, xg = [h + xx * mu[c:c + 1] for c in range(6)]
    r_ref[...] = _dot(xr, wr_ref[...])
    k_ref[...] = _dot(xk, wk_ref[...])
    v_ref[...] = _dot(xv, wv_ref[...])
    gg_ref[...] = _dot(_sigmoid(_dot(xg, g1_ref[...])), g2_ref[...])
    w_in = w0_ref[...] + _dot(jnp.tanh(_dot(xw, w1_ref[...])), w2_ref[...])
    lw_ref[...] = -W_DECAY_SCALE * _sigmoid(w_in)
    a_ref[...] = _sigmoid(a0_ref[...] + _dot(_dot(xa, a1_ref[...]), a2_ref[...]))


def _rw_inproj(x, mods, g, mu, wr, wk, wv, g1, g2, w1c, w2bd, w0c, a1c, a2bd, a0c):
    full = lambda a: pl.BlockSpec(a.shape, lambda i: (0,) * a.ndim)
    seg = lambda width: pl.BlockSpec((SEG, width), lambda i: (i, 0))
    outs = [jax.ShapeDtypeStruct((M_TOK, D), F32)] * 4 + [jax.ShapeDtypeStruct((M_TOK, 2 * D), F32)] * 2
    return pl.pallas_call(
        _rw_in_kernel,
        grid=(NSEG,),
        in_specs=[seg(D),
                  pl.BlockSpec((8, D), lambda i: (_prev8(i), 0)),
                  pl.BlockSpec((8, D), lambda i: (_next8(i), 0)),
                  pl.BlockSpec((1, 6, D), lambda i: (i, 0, 0)),
                  full(g), full(mu), full(wr), full(wk), full(wv), full(g1), full(g2),
                  full(w1c), full(w2bd), full(w0c), full(a1c), full(a2bd), full(a0c)],
        out_specs=[seg(D)] * 4 + [seg(2 * D)] * 2,
        out_shape=outs,
        compiler_params=_cp(("arbitrary",)),
        name="rwkv_inproj",
    )(x, x, x, mods, g, mu, wr, wk, wv, g1, g2, w1c, w2bd, w0c, a1c, a2bd, a0c)


def _rw_consts():
    t = np.arange(SEG)
    same = (t[:, None] // RW_CHUNK) == (t[None, :] // RW_CHUNK)
    tri_f = np.logical_and(same, t[None, :] <= t[:, None]).astype(np.float32)
    tri_b = np.logical_and(same, t[None, :] >= t[:, None]).astype(np.float32)
    return [jnp.asarray(a, BF16) for a in (tri_f, tri_b)]


def _pair_bd(y, head0):
    return jnp.concatenate([jnp.where(head0, y, 0.0), jnp.where(head0, 0.0, y)], axis=0)


def _rw_dir(r, k, v, lw, a, kk_g, ka_g, tri, rev):
    c = RW_CHUNK
    lane = lax.broadcasted_iota(jnp.int32, (SEG, 2 * C_HEAD), 1)
    head0 = lane < C_HEAD
    kx = k * kk_g
    ss = kx * kx
    n0 = jnp.sum(jnp.where(head0, ss, 0.0), axis=-1, keepdims=True)
    n1 = jnp.sum(jnp.where(head0, 0.0, ss), axis=-1, keepdims=True)
    kk = kx / jnp.maximum(jnp.sqrt(jnp.where(head0, n0, n1)), 1e-12)
    kd = k * (1.0 + (a - 1.0) * ka_g)
    bhat = kk * a
    cum = _dot_sel(tri, lw)
    e_incl = jnp.exp(cum)
    e_inv = jnp.exp(-cum)
    ae = -kk * jnp.exp(cum - lw)
    re = r * e_incl
    bi = bhat * e_inv
    ki = kd * e_inv

    chunks = []
    for ci in range(SEG // c):
        sl = slice(ci * c, (ci + 1) * c)
        ctot = cum[ci * c:ci * c + 1] if rev else cum[(ci + 1) * c - 1:(ci + 1) * c]
        dec = jnp.exp(ctot - cum[sl])
        chunks.append(dict(ae=ae[sl], re=re[sl], bi=bi[sl], ki=ki[sl], v=v[sl], bdec=bhat[sl] * dec,
                           kdec=kd[sl] * dec, gam=jnp.exp(ctot), rev=rev))
    return chunks


def _rw_transitions(chunks):
    c = RW_CHUNK
    w2 = 2 * C_HEAD
    h0c = lax.broadcasted_iota(jnp.int32, (c, w2), 1) < C_HEAD
    rowc = lax.broadcasted_iota(jnp.int32, (c, w2), 0)
    colc = jnp.bitwise_and(lax.broadcasted_iota(jnp.int32, (c, w2), 1), C_HEAD - 1)
    eye = (colc == rowc).astype(F32)
    bdmask = (lax.broadcasted_iota(jnp.int32, (w2, w2), 0) < C_HEAD) == (
        lax.broadcasted_iota(jnp.int32, (w2, w2), 1) < C_HEAD)
    bd = lambda y: _pair_bd(y, h0c)

    n_ab, a_ak, a_rb, a_rk = [], [], [], []
    for ch in chunks:
        strict = (colc > rowc) if ch["rev"] else (colc < rowc)
        incl = (colc >= rowc) if ch["rev"] else (colc <= rowc)
        left = jnp.concatenate([ch["ae"], ch["re"]], axis=0)
        right = jnp.concatenate([jnp.where(h0c, ch["bi"], 0.0), jnp.where(h0c, 0.0, ch["bi"]),
                                 jnp.where(h0c, ch["ki"], 0.0), jnp.where(h0c, 0.0, ch["ki"])], axis=0)
        gm = _dot_nt(left, right)
        n_ab.append(jnp.where(strict, gm[0:c, 0:2 * c], 0.0))
        a_ak.append(jnp.where(strict, gm[0:c, 2 * c:4 * c], 0.0))
        a_rb.append(jnp.where(incl, gm[c:2 * c, 0:2 * c], 0.0))
        a_rk.append(jnp.where(incl, gm[c:2 * c, 2 * c:4 * c], 0.0))
    tm = [eye + n for n in n_ab]
    p = [_x3_shared([x], bd(x))[0] for x in n_ab]
    for it in range(4):
        if it < 2:
            res = [_x3_shared([x, t], bd(x)) for x, t in zip(p, tm)]
            p = [r[0] for r in res]
            tm = [t + r[1] for t, r in zip(tm, res)]
        else:
            res = [_dot(jnp.concatenate([x, t], axis=0), bd(x)) for x, t in zip(p, tm)]
            p = [r[0:c] for r in res]
            tm = [t + r[c:2 * c] for t, r in zip(tm, res)]
    tm = [t + _dot(t, bd(x)) for t, x in zip(tm, p)]
    vbd = [bd(ch["v"]) for ch in chunks]
    av = [_dot(x, vb) for x, vb in zip(a_ak, vbd)]
    tav = [_x3_shared([t], jnp.concatenate([bd(ch["ae"]), bd(x)], axis=1))[0]
           for t, ch, x in zip(tm, chunks, av)]
    out = []
    for i, ch in enumerate(chunks):
        ta, tv = tav[i][:, 0:w2], tav[i][:, w2:2 * w2]
        q = ch["re"] + _dot(a_rb[i], bd(ta))
        y = _dot(a_rb[i], bd(tv)) + _dot(a_rk[i], vbd[i])
        wz = _dot_tn(tav[i], ch["bdec"])
        w = jnp.where(bdmask, wz[0:w2], 0.0)
        z = jnp.where(bdmask, wz[w2:2 * w2] + _dot_tn(ch["v"], ch["kdec"]), 0.0)
        out.append((q, y, w, z, ch["gam"]))
    return out


def _rw_scan_kernel(rf, kf, vf, lwf, af, rb, kb, vb, lwb, ab, kkg_ref, kag_ref, s0f, s0b, trif, trib,
                    of_ref, ob_ref, sf_out, sb_out, st_f, st_b):
    j = pl.program_id(1)
    w2 = 2 * C_HEAD
    dirs = ((False, (rf, kf, vf, lwf, af), s0f, trif, of_ref, sf_out, st_f),
            (True, (rb, kb, vb, lwb, ab), s0b, trib, ob_ref, sb_out, st_b))
    chunks = []
    for rev, refs, s0, tri, o_ref, s_out, st in dirs:
        for pp in range(RW_PAIRS_STEP):
            lanes = slice(pp * w2, (pp + 1) * w2)
            r_, k_, v_, lw_, a_ = [z[:, lanes] for z in refs]
            chunks.extend(_rw_dir(r_, k_, v_, lw_, a_, kkg_ref[:, lanes], kag_ref[:, lanes], tri[...], rev))
    trans = _rw_transitions(chunks)
    n_c = SEG // RW_CHUNK
    for d, (rev, refs, s0, tri, o_ref, s_out, st) in enumerate(dirs):
        i = (NSEG - 1 - j) if rev else j
        starts = _seg_is_last(i) if rev else _seg_is_first(i)

        @pl.when(jnp.logical_and(starts, i < N_CTX_SEG))
        def _():
            st[...] = jnp.zeros_like(st)

        @pl.when(jnp.logical_and(starts, i >= N_CTX_SEG))
        def _():
            st[...] = s0[0, 0]

        for pp in range(RW_PAIRS_STEP):
            s = st[pp]
            for ci in (range(n_c - 1, -1, -1) if rev else range(n_c)):
                q, y, w, z, gam = trans[(d * RW_PAIRS_STEP + pp) * n_c + ci]
                o_ref[ci * RW_CHUNK:(ci + 1) * RW_CHUNK, pp * w2:(pp + 1) * w2] = _dot_nt(q, s) + y
                s = s * gam + _dot(s, w) + z
            st[pp] = s

            @pl.when(i < N_CTX_SEG)
            def _():
                s_out[0, pp] = s


def _rw_scan(r, k, v, lw, a, kk_g, ka_g, s0bd):
    consts = _rw_consts()
    w = 2 * C_HEAD
    pps = RW_PAIRS_STEP
    wb = pps * w
    n_steps = C_PAIRS // pps
    blk = lambda col0, rev: pl.BlockSpec(
        (SEG, wb), (lambda p, j: (NSEG - 1 - j, col0 + p)) if rev else (lambda p, j: (j, col0 + p)))
    cspec = lambda arr: pl.BlockSpec(arr.shape, lambda p, j: (0, 0))
    fin_f = pl.BlockSpec((1, pps, w, w), lambda p, j: (jnp.minimum(j, N_CTX_SEG - 1), p, 0, 0))
    fin_b = pl.BlockSpec((1, pps, w, w), lambda p, j: (jnp.minimum(NSEG - 1 - j, N_CTX_SEG - 1), p, 0, 0))
    return pl.pallas_call(
        _rw_scan_kernel,
        grid=(n_steps, NSEG),
        in_specs=[blk(0, False), blk(0, False), blk(0, False), blk(0, False), blk(0, False),
                  blk(0, True), blk(0, True), blk(0, True), blk(n_steps, True), blk(n_steps, True),
                  pl.BlockSpec((1, wb), lambda p, j: (0, p)), pl.BlockSpec((1, wb), lambda p, j: (0, p)),
                  pl.BlockSpec((1, 1, pps, w, w), lambda p, j: (_sample_of(j), 0, p, 0, 0)),
                  pl.BlockSpec((1, 1, pps, w, w), lambda p, j: (_sample_of(NSEG - 1 - j), 1, p, 0, 0)),
                  cspec(consts[0]), cspec(consts[1])],
        out_specs=[blk(0, False), blk(0, True), fin_f, fin_b],
        out_shape=[jax.ShapeDtypeStruct((M_TOK, D), F32), jax.ShapeDtypeStruct((M_TOK, D), F32),
                   jax.ShapeDtypeStruct((N_CTX_SEG, C_PAIRS, w, w), F32),
                   jax.ShapeDtypeStruct((N_CTX_SEG, C_PAIRS, w, w), F32)],
        scratch_shapes=[pltpu.VMEM((pps, w, w), F32), pltpu.VMEM((pps, w, w), F32)],
        compiler_params=_cp(("arbitrary", "arbitrary")),
        name="rwkv7_scan",
    )(r, k, v, lw, a, r, k, v, lw, a, kk_g, ka_g, s0bd, s0bd, *consts)


def _rw_out_kernel(x_ref, of_ref, ob_ref, r_ref, k_ref, v_ref, a_ref, gg_ref, mod_ref, ka_ref, rk_ref,
                   lnw_ref, lnb_ref, ones_ref, wo_ref, o_ref):
    m = mod_ref[0]
    ones_bd = ones_ref[...]
    w = 2 * C_HEAD
    inv_n = 1.0 / C_HEAD
    parts = []
    for p in range(C_PAIRS):
        cs = slice(p * w, (p + 1) * w)
        osum = of_ref[:, cs] + ob_ref[:, cs]
        mu = _dot_sel_rhs(osum, ones_bd) * inv_n
        cen = osum - mu
        var = _dot_sel_rhs(cen * cen, ones_bd) * inv_n
        o = cen * lax.rsqrt(var + GN_EPS) * lnw_ref[:, cs] + lnb_ref[:, cs]
        r = r_ref[:, cs]
        k = k_ref[:, cs]
        ka = ka_ref[:, cs]
        rkr = r * k * rk_ref[:, cs]
        kd_sum = (1.0 + (a_ref[:, cs] - 1.0) * ka) + (1.0 + (a_ref[:, D + p * w:D + (p + 1) * w] - 1.0) * ka)
        bonus = _dot_sel_rhs(rkr * kd_sum, ones_bd) * v_ref[:, cs]
        parts.append((o + bonus) * gg_ref[:, cs])
    y = _dot(jnp.concatenate(parts, axis=-1), wo_ref[...])
    o_ref[...] = x_ref[...] + m[2:3] * y


def _dot_sel_rhs(x, mat):
    h, m, l = _split3(x)
    return (jnp.dot(h, mat, preferred_element_type=F32) + jnp.dot(m, mat, preferred_element_type=F32)
            + jnp.dot(l, mat, preferred_element_type=F32))


def _rw_out(x, o_f, o_b, r, k, v, a, gg, mods, ka, rk, lnw, lnb, wo):
    seg = lambda width: pl.BlockSpec((SEG, width), lambda i: (i, 0))
    row = pl.BlockSpec((1, D), lambda i: (0, 0))
    hh = np.arange(2 * C_HEAD) // C_HEAD
    ones_bd = jnp.asarray((hh[:, None] == hh[None, :]).astype(np.float32), BF16)
    return pl.pallas_call(
        _rw_out_kernel,
        grid=(NSEG,),
        in_specs=[seg(D), seg(D), seg(D), seg(D), seg(D), seg(D), seg(2 * D), seg(D),
                  pl.BlockSpec((1, 6, D), lambda i: (i, 0, 0)),
                  row, row, row, row,
                  pl.BlockSpec((2 * C_HEAD, 2 * C_HEAD), lambda i: (0, 0)),
                  pl.BlockSpec((D, D), lambda i: (0, 0))],
        out_specs=seg(D),
        out_shape=jax.ShapeDtypeStruct((M_TOK, D), F32),
        compiler_params=_cp(("arbitrary",)),
        name="rwkv_out",
    )(x, o_f, o_b, r, k, v, a, gg, mods, ka.reshape(1, D), rk.reshape(1, D), lnw.reshape(1, D),
      lnb.reshape(1, D), ones_bd, wo)


def _grid_pos_embed(n_tok):
    rows = n_tok // GRID_W
    r, cl = jnp.meshgrid(jnp.arange(rows, dtype=F32), jnp.arange(GRID_W, dtype=F32), indexing='ij')
    quarter = D // 4
    omega = 1.0 / (POS_BASE ** (jnp.arange(quarter, dtype=F32) / quarter))
    ang_r = r.reshape(-1, 1) * omega
    ang_c = cl.reshape(-1, 1) * omega
    return jnp.concatenate([jnp.sin(ang_r), jnp.cos(ang_r), jnp.sin(ang_c), jnp.cos(ang_c)], axis=-1)


def _block_diag(blocks):
    g, n, _ = blocks.shape
    eye = jnp.eye(g, dtype=blocks.dtype)
    return (eye[:, None, :, None] * blocks[:, :, None, :]).reshape(g * n, g * n)


def kernel(x_prompt, x_sample, state_hgrn, state_rglru, state_rwkv, c, c_ctx, norm_mix_g, norm_ffn_g, w_mod, b_mod, ab_w_in, ab_w_out, hgrn_lb, hgrn_norm_g, rg_conv_w, rg_conv_b, rg_wa, rg_ba, rg_wx, rg_bx, rg_lambda, rw_mu, rw_wr, rw_wk, rw_wv, rw_wo, rw_w0, rw_w1, rw_w2, rw_a0, rw_a1, rw_a2, rw_g1, rw_g2, rw_kk, rw_ka, rw_rk, rw_lnw, rw_lnb, moe_router, moe_router_bias, moe_w1, moe_w3, moe_w2, norm_f_g):
    bf = lambda z: z.astype(BF16)
    n_ctx = x_prompt.shape[0]
    xs = x_sample + _grid_pos_embed(x_sample.shape[1])[None]
    x = jnp.concatenate([x_prompt.reshape(-1, D), xs.reshape(-1, D)], axis=0)
    mods = _modulations(c, c_ctx, w_mod, b_mod)

    lower_bounds = jnp.cumsum(jax.nn.softmax(hgrn_lb.astype(F32), axis=1), axis=1)
    lb = lower_bounds[:, 0].reshape(2, A_HEADS, 1, A_DK)
    proj = _ab_inproj(x, mods[0], norm_mix_g[0], bf(ab_w_in[0]))
    s0t = jnp.swapaxes(state_hgrn[:, 0], -1, -2)
    o_f, o_b, sf, sb = _gla(proj, lb, s0t)
    wa_bd = bf(jnp.stack([_block_diag(rg_wa[0, d]) for d in range(2)]))
    wx_bd = bf(jnp.stack([_block_diag(rg_wx[0, d]) for d in range(2)]))
    h_f, h_b, lru_f, lru_b = _rglru(
        proj, rg_conv_w[0], rg_conv_b[0].reshape(1, D_B), wa_bd, rg_ba[0].reshape(2, 1, D_B), wx_bd,
        rg_bx[0].reshape(2, 1, D_B), rg_lambda[0].reshape(2, 1, D_B), state_rglru[:, 0].reshape(-1, 2, 1, D_B))
    x = _ab_out(x, proj, o_f, o_b, h_f, h_b, mods[0], hgrn_norm_g[0], bf(ab_w_out[0]))
    x = _moe(x, mods[0], norm_ffn_g[0], moe_router, moe_router_bias, moe_w1, moe_w3, moe_w2, 0, norm_f_g, False)

    w1c = bf(jnp.concatenate([rw_w1[0, 0], rw_w1[0, 1]], axis=-1))
    a1c = bf(jnp.concatenate([rw_a1[0, 0], rw_a1[0, 1]], axis=-1))
    w2bd = bf(jnp.concatenate([jnp.concatenate([rw_w2[0, 0], jnp.zeros_like(rw_w2[0, 0])], axis=-1),
                               jnp.concatenate([jnp.zeros_like(rw_w2[0, 1]), rw_w2[0, 1]], axis=-1)], axis=0))
    a2bd = bf(jnp.concatenate([jnp.concatenate([rw_a2[0, 0], jnp.zeros_like(rw_a2[0, 0])], axis=-1),
                               jnp.concatenate([jnp.zeros_like(rw_a2[0, 1]), rw_a2[0, 1]], axis=-1)], axis=0))
    r, k, v, gg, lw, a = _rw_inproj(
        x, mods[1], norm_mix_g[1].reshape(1, D), rw_mu[0], bf(rw_wr[0]), bf(rw_wk[0]), bf(rw_wv[0]),
        bf(rw_g1[0]), bf(rw_g2[0]), w1c, w2bd, rw_w0[0].reshape(1, 2 * D), a1c, a2bd, rw_a0[0].reshape(1, 2 * D))
    s0 = state_rwkv[:, 0].reshape(N_SAMPLE, 2, C_PAIRS, 2, C_HEAD, C_HEAD)
    zeros = jnp.zeros_like(s0[:, :, :, 0])
    s0bd = jnp.concatenate([jnp.concatenate([s0[:, :, :, 0], zeros], axis=-1),
                            jnp.concatenate([zeros, s0[:, :, :, 1]], axis=-1)], axis=-2)
    ow_f, ow_b, rs_f, rs_b = _rw_scan(r, k, v, lw, a, rw_kk[0].reshape(1, D), rw_ka[0].reshape(1, D), s0bd)
    x = _rw_out(x, ow_f, ow_b, r, k, v, a, gg, mods[1], rw_ka[0], rw_rk[0], rw_lnw[0], rw_lnb[0], bf(rw_wo[0]))
    y = _moe(x, mods[1], norm_ffn_g[1], moe_router, moe_router_bias, moe_w1, moe_w3, moe_w2, 1, norm_f_g, True)

    n_p = n_ctx * x_prompt.shape[1]
    y_prompt = y[:n_p].reshape(x_prompt.shape)
    y_sample = y[n_p:].reshape(x_sample.shape)
    new_hgrn = jnp.swapaxes(jnp.stack([sf, sb], axis=1), -1, -2)[:, None]
    new_rglru = jnp.stack([lru_f[:, 0], lru_b[:, 0]], axis=1)[:, None]

    def unpair(s):
        h0 = s[:, :, :C_HEAD, :C_HEAD]
        h1 = s[:, :, C_HEAD:, C_HEAD:]
        return jnp.stack([h0, h1], axis=2).reshape(n_ctx, 2 * C_PAIRS, C_HEAD, C_HEAD)

    new_rwkv = jnp.stack([unpair(rs_f), unpair(rs_b)], axis=1)[:, None]
    return (y_prompt, y_sample, new_hgrn, new_rglru, new_rwkv)
```

```python
import functools
import math

import numpy as np
import jax
import jax.numpy as jnp
from jax import lax
from jax.experimental import pallas as pl
from jax.experimental.pallas import tpu as pltpu

F32 = jnp.float32
BF16 = jnp.bfloat16

D = 1024
SEG = 256
N_CTX_SEG = 16
SEG_PER_SAMPLE = 4
N_SAMPLE = 4
NSEG = N_CTX_SEG + N_SAMPLE * SEG_PER_SAMPLE
M_TOK = NSEG * SEG
SUBLANES = 8
ROWS8_PER_SEG = SEG // SUBLANES

A_HEADS = 4
A_DK = 128
D_A = 512
D_B = 512
B_BLOCKS = 8
B_BLOCK = 64
LRU_C = 8.0
D_IN_AB = 5 * D_A + 2 * D_B
AB_F32_COLS = 2 * D_A + D_B
C_HEAD = 64
C_PAIRS = 8
RW_CHUNK = 64
COEF_LANES = 128
RW_PAIRS_STEP = 2
W_DECAY_SCALE = math.exp(-0.5)
N_EXPERTS = 16
N_GROUPS = 4
GROUP = 4
D_EXPERT = 256
RMS_EPS = 1e-6
GN_EPS = 64e-5
POS_BASE = 10000.0
GRID_W = 64
MOE_TM = 1024
MOE_CTX_TILES = N_CTX_SEG * SEG // MOE_TM
COMB_LANES = 128
GLA_LEVELS = (1, 2, 4, 8, 16, 32, 64, 128)
GLA_HALF = 128

VMEM_LIMIT = 56 * 1024 * 1024


def _cp(sem):
    return pltpu.CompilerParams(dimension_semantics=sem, vmem_limit_bytes=VMEM_LIMIT)


def _sigmoid(x):
    return 0.5 * jnp.tanh(0.5 * x) + 0.5


def _silu(x):
    return x * _sigmoid(x)


def _gelu_tanh(x):
    return 0.5 * x * (1.0 + jnp.tanh(math.sqrt(2.0 / math.pi) * (x + 0.044715 * (x * x * x))))


def _rms_mod(x, g, scale, shift):
    ms = jnp.mean(x * x, axis=-1, keepdims=True)
    return x * lax.rsqrt(ms + RMS_EPS) * g * (1.0 + scale) + shift


def _dot(a, b):
    return jnp.dot(a.astype(BF16), b.astype(BF16), preferred_element_type=F32)


def _dot_nt(a, b):
    return lax.dot_general(a.astype(BF16), b.astype(BF16), (((1,), (1,)), ((), ())),
                           preferred_element_type=F32)


def _dot_tn(a, b):
    return lax.dot_general(a.astype(BF16), b.astype(BF16), (((0,), (0,)), ((), ())),
                           preferred_element_type=F32)


def _split3(x):
    h = x.astype(BF16)
    r1 = x - h.astype(F32)
    m = r1.astype(BF16)
    r2 = r1 - m.astype(F32)
    return h, m, r2.astype(BF16)


def _dot_sel(mat, x):
    h, m, l = _split3(x)
    return (jnp.dot(mat, h, preferred_element_type=F32) + jnp.dot(mat, m, preferred_element_type=F32)
            + jnp.dot(mat, l, preferred_element_type=F32))


def _dot_x3(a, b):
    ah = a.astype(BF16)
    al = (a - ah.astype(F32)).astype(BF16)
    bh = b.astype(BF16)
    bl = (b - bh.astype(F32)).astype(BF16)
    return (jnp.dot(ah, bh, preferred_element_type=F32) + jnp.dot(ah, bl, preferred_element_type=F32)
            + jnp.dot(al, bh, preferred_element_type=F32))


def _dot_x3_nt(a, b):
    dn = (((1,), (1,)), ((), ()))
    ah = a.astype(BF16)
    al = (a - ah.astype(F32)).astype(BF16)
    bh = b.astype(BF16)
    bl = (b - bh.astype(F32)).astype(BF16)
    return (lax.dot_general(ah, bh, dn, preferred_element_type=F32)
            + lax.dot_general(ah, bl, dn, preferred_element_type=F32)
            + lax.dot_general(al, bh, dn, preferred_element_type=F32))


def _x3_shared(lhs_list, rhs):
    n = lhs_list[0].shape[0]
    k = len(lhs_list)
    rh = rhs.astype(BF16)
    rl = (rhs - rh.astype(F32)).astype(BF16)
    his = [a.astype(BF16) for a in lhs_list]
    los = [(a - h.astype(F32)).astype(BF16) for a, h in zip(lhs_list, his)]
    r1 = jnp.dot(jnp.concatenate(his + los, axis=0), rh, preferred_element_type=F32)
    r2 = jnp.dot(his[0] if k == 1 else jnp.concatenate(his, axis=0), rl, preferred_element_type=F32)
    return [r1[i * n:(i + 1) * n] + r1[(k + i) * n:(k + i + 1) * n] + r2[i * n:(i + 1) * n] for i in range(k)]


def _seg_is_first(i):
    return jnp.logical_or(i < N_CTX_SEG, lax.rem(i - N_CTX_SEG, SEG_PER_SAMPLE) == 0)


def _seg_is_last(i):
    return jnp.logical_or(i < N_CTX_SEG, lax.rem(i - N_CTX_SEG, SEG_PER_SAMPLE) == SEG_PER_SAMPLE - 1)


def _sample_of(i):
    return jnp.maximum(i - N_CTX_SEG, 0) // SEG_PER_SAMPLE


def _prev8(i):
    return jnp.maximum(i * ROWS8_PER_SEG - 1, 0)


def _next8(i):
    return jnp.minimum((i + 1) * ROWS8_PER_SEG, M_TOK // 8 - 1)


def _mod_kernel(cv_ref, w_ref, b_ref, o_ref):
    cv = cv_ref[...]
    o_ref[0] = _dot(_silu(cv), w_ref[0]) + b_ref[0]


def _modulations(c, c_ctx, w_mod, b_mod):
    depth = w_mod.shape[0]
    cv = jnp.concatenate([c_ctx[None, :], c, jnp.zeros((3, D), F32)], axis=0)
    n_t = 6
    mod = pl.pallas_call(
        _mod_kernel,
        grid=(depth, n_t),
        in_specs=[pl.BlockSpec((8, D), lambda l, n: (0, 0)),
                  pl.BlockSpec((1, D, D), lambda l, n: (l, 0, n)),
                  pl.BlockSpec((1, 1, D), lambda l, n: (l, 0, n))],
        out_specs=pl.BlockSpec((1, 8, D), lambda l, n: (l, 0, n)),
        out_shape=jax.ShapeDtypeStruct((depth, 8, 6 * D), F32),
        compiler_params=_cp(("arbitrary", "arbitrary")),
        name="adaln_mod",
    )(cv, w_mod, b_mod.reshape(depth, 1, 6 * D))
    row_of_seg = np.array([0] * N_CTX_SEG + [1 + s // SEG_PER_SAMPLE for s in range(N_SAMPLE * SEG_PER_SAMPLE)])
    return mod[:, row_of_seg].reshape(depth, NSEG, 6, D)


def _x0_specs():
    return [pl.BlockSpec((SEG, D), lambda i: (jnp.minimum(i, N_CTX_SEG - 1), 0)),
            pl.BlockSpec((SEG, D), lambda i: (jnp.maximum(i - N_CTX_SEG, 0), 0)),
            pl.BlockSpec((SEG, D), lambda i: (lax.rem(jnp.maximum(i - N_CTX_SEG, 0), SEG_PER_SAMPLE), 0))]


def _x0(i, xp_ref, xs_ref, pos_ref):
    return jnp.where(i < N_CTX_SEG, xp_ref[...], xs_ref[...] + pos_ref[...])


def _ab_in_kernel(xp_ref, xs_ref, pos_ref, mod_ref, g_ref, w_ref, of_ref, oh_ref):
    m = mod_ref[0]
    x = _x0(pl.program_id(0), xp_ref, xs_ref, pos_ref)
    h = _rms_mod(x, g_ref[...], m[1:2], m[0:1])
    res = jnp.dot(h.astype(BF16), w_ref[...], preferred_element_type=F32)
    of_ref[...] = res[:, :AB_F32_COLS]
    oh_ref[...] = res[:, AB_F32_COLS:].astype(BF16)


def _ab_inproj(xp, xs, pos, mods, g, w_in):
    return pl.pallas_call(
        _ab_in_kernel,
        grid=(NSEG,),
        in_specs=_x0_specs() + [pl.BlockSpec((1, 6, D), lambda i: (i, 0, 0)),
                                pl.BlockSpec((1, D), lambda i: (0, 0)),
                                pl.BlockSpec((D, D_IN_AB), lambda i: (0, 0))],
        out_specs=[pl.BlockSpec((SEG, AB_F32_COLS), lambda i: (i, 0)),
                   pl.BlockSpec((SEG, D_IN_AB - AB_F32_COLS), lambda i: (i, 0))],
        out_shape=[jax.ShapeDtypeStruct((M_TOK, AB_F32_COLS), F32),
                   jax.ShapeDtypeStruct((M_TOK, D_IN_AB - AB_F32_COLS), BF16)],
        compiler_params=_cp(("arbitrary",)),
        name="ab_inproj",
    )(xp, xs, pos, mods, g.reshape(1, D), w_in)


def _gla_consts():
    t = np.arange(SEG)
    tri_f = (t[None, :] <= t[:, None]).astype(np.float32)
    tri_b = (t[None, :] >= t[:, None]).astype(np.float32)
    th = np.arange(GLA_HALF)
    xor = th[:, None] ^ th[None, :]
    hb = np.where(xor > 0, 1 << np.floor(np.log2(np.maximum(xor, 1))).astype(np.int64), 0)
    code_f = np.where(th[None, :] < th[:, None], hb, 0).astype(np.int32)
    code_b = np.where(th[None, :] > th[:, None], hb, 0).astype(np.int32)
    return [jnp.asarray(tri_f, BF16), jnp.asarray(tri_b, BF16), jnp.asarray(code_f), jnp.asarray(code_b)]


def _gla_level_offsets(b, g, qside, w, rev):
    if w == 1:
        return jnp.where(qside, g, 0.0)
    if 2 * w >= 16:
        nv = 2 * w // 8
        b4 = b.reshape(SEG // (2 * w), nv, 8, A_DK)
        ref = b4[:, w // 8:w // 8 + 1, 0:1, :] if rev else b4[:, w // 8 - 1:w // 8, 7:8, :]
        x = (b4 - ref).reshape(SEG, A_DK)
    else:
        b3 = b.reshape(SEG // 8, 8, A_DK)
        sub = lax.broadcasted_iota(jnp.int32, b3.shape, 1)
        beta = None
        for jb in range(8 // (2 * w)):
            r = jb * 2 * w + (w if rev else w - 1)
            cand = jnp.broadcast_to(b3[:, r:r + 1, :], b3.shape)
            beta = cand if beta is None else jnp.where(sub >= jb * 2 * w, cand, beta)
        x = (b3 - beta).reshape(SEG, A_DK)
    return jnp.where(qside, x, -x)


def _gla_dir(qraw, fraw, v, lb, st, tri, code, rev):
    hh = GLA_HALF
    q = _silu(qraw)
    f = lb + (1.0 - lb) * _sigmoid(fraw)
    g = jnp.log(f)
    k = 1.0 - f
    b = _dot_sel(tri, g)
    rowi = lax.broadcasted_iota(jnp.int32, (SEG, A_DK), 0)
    att = [jnp.zeros((hh, hh), F32), jnp.zeros((hh, hh), F32)]
    cross = None
    for w in GLA_LEVELS:
        upper = jnp.bitwise_and(rowi, w) != 0
        qside = jnp.logical_not(upper) if rev else upper
        z = _gla_level_offsets(b, g, qside, w, rev)
        m = (jnp.where(qside, q, k) * jnp.exp(z)).astype(BF16)
        if w == hh:
            cross = _dot_nt(m[0:hh], m[hh:SEG]) if rev else _dot_nt(m[hh:SEG], m[0:hh])
        else:
            for half in range(2):
                mh = m[half * hh:(half + 1) * hh]
                att[half] = jnp.where(code == w, _dot_nt(mh, mh), att[half])
    if rev:
        o_lo = _dot(jnp.concatenate([att[0], cross], axis=1), v)
        o_hi = _dot(att[1], v[hh:SEG])
    else:
        o_lo = _dot(att[0], v[0:hh])
        o_hi = _dot(jnp.concatenate([cross, att[1]], axis=1), v)
    diag = jnp.sum(q * k, axis=-1, keepdims=True)
    o = jnp.concatenate([o_lo, o_hi], axis=0) + diag * v + _dot_nt(q * jnp.exp(b), st)
    btot = b[0:1] if rev else b[SEG - 1:SEG]
    st_new = st * jnp.exp(btot) + _dot_tn(v, k * jnp.exp(btot - b))
    return o, st_new


def _gla_kernel(qf, ff, vf, qb, fb, vb, lb_ref, s0f, s0b, trif, trib, codef, codeb,
                of_ref, ob_ref, sf_out, sb_out, st_f, st_b):
    j = pl.program_id(1)
    for rev, qr, fr, vr, s0, tri, code, o_ref, s_out, st in (
            (False, qf, ff, vf, s0f, trif, codef, of_ref, sf_out, st_f),
            (True, qb, fb, vb, s0b, trib, codeb, ob_ref, sb_out, st_b)):
        i = (NSEG - 1 - j) if rev else j
        starts = _seg_is_last(i) if rev else _seg_is_first(i)

        @pl.when(jnp.logical_and(starts, i < N_CTX_SEG))
        def _():
            st[...] = jnp.zeros_like(st)

        @pl.when(jnp.logical_and(starts, i >= N_CTX_SEG))
        def _():
            st[...] = s0[0, 0, 0]

        o, st_new = _gla_dir(qr[...].astype(F32), fr[...], vr[...].astype(F32), lb_ref[1 if rev else 0, 0],
                             st[...], tri[...], code[...], rev)
        o_ref[...] = o.astype(BF16)
        st[...] = st_new

        @pl.when(i < N_CTX_SEG)
        def _():
            s_out[0, 0] = st_new


def _gla(proj_f, proj_h, lb, s0t):
    consts = _gla_consts()
    blk = lambda col0, rev: pl.BlockSpec(
        (SEG, A_DK), (lambda h, j: (NSEG - 1 - j, col0 + h)) if rev else (lambda h, j: (j, col0 + h)))
    cspec = lambda a: pl.BlockSpec(a.shape, lambda h, j: (0, 0))
    out_blk = lambda rev: pl.BlockSpec(
        (SEG, A_DK), (lambda h, j: (NSEG - 1 - j, h)) if rev else (lambda h, j: (j, h)))
    fin_f = pl.BlockSpec((1, 1, A_DK, A_DK), lambda h, j: (jnp.minimum(j, N_CTX_SEG - 1), h, 0, 0))
    fin_b = pl.BlockSpec((1, 1, A_DK, A_DK), lambda h, j: (jnp.minimum(NSEG - 1 - j, N_CTX_SEG - 1), h, 0, 0))
    return pl.pallas_call(
        _gla_kernel,
        grid=(A_HEADS, NSEG),
        in_specs=[blk(0, False), blk(0, False), blk(4, False),
                  blk(0, True), blk(4, True), blk(4, True),
                  pl.BlockSpec((2, 1, 1, A_DK), lambda h, j: (0, h, 0, 0)),
                  pl.BlockSpec((1, 1, 1, A_DK, A_DK), lambda h, j: (_sample_of(j), 0, h, 0, 0)),
                  pl.BlockSpec((1, 1, 1, A_DK, A_DK), lambda h, j: (_sample_of(NSEG - 1 - j), 1, h, 0, 0)),
                  cspec(consts[0]), cspec(consts[1]), cspec(consts[2]), cspec(consts[3])],
        out_specs=[out_blk(False), out_blk(True), fin_f, fin_b],
        out_shape=[jax.ShapeDtypeStruct((M_TOK, D_A), BF16), jax.ShapeDtypeStruct((M_TOK, D_A), BF16),
                   jax.ShapeDtypeStruct((N_CTX_SEG, A_HEADS, A_DK, A_DK), F32),
                   jax.ShapeDtypeStruct((N_CTX_SEG, A_HEADS, A_DK, A_DK), F32)],
        scratch_shapes=[pltpu.VMEM((A_DK, A_DK), F32), pltpu.VMEM((A_DK, A_DK), F32)],
        compiler_params=_cp(("arbitrary", "arbitrary")),
        name="hgrn2_gla",
    )(proj_h, proj_f, proj_h, proj_h, proj_f, proj_h, lb, s0t, s0t, *consts)


def _lin_scan(a, b, h_in, rev):
    t_len, c = a.shape
    ng = t_len // SUBLANES
    a3 = a.reshape(ng, SUBLANES, c)
    b3 = b.reshape(ng, SUBLANES, c)
    sub = lax.broadcasted_iota(jnp.int32, a3.shape, 1)
    s = 1
    while s < SUBLANES:
        shift = (SUBLANES - s) if rev else s
        valid = (sub < SUBLANES - s) if rev else (sub >= s)
        ap = jnp.where(valid, pltpu.roll(a3, shift, 1), 1.0)
        bp = jnp.where(valid, pltpu.roll(b3, shift, 1), 0.0)
        b3 = a3 * bp + b3
        a3 = a3 * ap
        s *= 2
    hs = [None] * ng
    carry = h_in
    for j in (range(ng - 1, -1, -1) if rev else range(ng)):
        hs[j] = a3[j] * carry + b3[j]
        carry = hs[j][0:1] if rev else hs[j][SUBLANES - 1:SUBLANES]
    return jnp.concatenate(hs, axis=0), carry


def _rglru_dir(x, xprev, xnext, first, last, cw, cb, wa, ba, wx, bx, lam, h_in, rev):
    zero = jnp.zeros_like(xprev)
    ext = jnp.concatenate([jnp.where(first, zero, xprev), x, jnp.where(last, zero, xnext)], axis=0)
    n = ext.shape[0]
    xm2 = pltpu.roll(ext, 2, 0)[8:8 + SEG]
    xm1 = pltpu.roll(ext, 1, 0)[8:8 + SEG]
    xp1 = pltpu.roll(ext, n - 1, 0)[8:8 + SEG]
    xc = cb + xm2 * cw[0:1] + xm1 * cw[1:2] + x * cw[2:3] + xp1 * cw[3:4]
    gate_r = _sigmoid(_dot(xc, wa) + ba)
    gate_i = _sigmoid(_dot(xc, wx) + bx)
    softplus_neg_lam = jnp.maximum(-lam, 0.0) + jnp.log(1.0 + jnp.exp(-jnp.abs(lam)))
    log_a = -LRU_C * gate_r * softplus_neg_lam
    a = jnp.exp(log_a)
    b_in = jnp.sqrt(1.0 - a * a) * gate_i * xc
    return _lin_scan(a, b_in, h_in, rev)


def _rglru_kernel(xf, xf_p, xf_n, xb, xb_p, xb_n, cw_ref, cb_ref, wa_ref, ba_ref, wx_ref, bx_ref, lam_ref,
                  s0f, s0b, hf_ref, hb_ref, ff_out, fb_out, hc_f, hc_b):
    j = pl.program_id(0)
    for rev, xr, xp, xn, s0, h_ref, f_out, hc in (
            (False, xf, xf_p, xf_n, s0f, hf_ref, ff_out, hc_f),
            (True, xb, xb_p, xb_n, s0b, hb_ref, fb_out, hc_b)):
        d = 1 if rev else 0
        i = (NSEG - 1 - j) if rev else j
        first = _seg_is_first(i)
        last = _seg_is_last(i)
        starts = last if rev else first

        @pl.when(jnp.logical_and(starts, i < N_CTX_SEG))
        def _():
            hc[...] = jnp.zeros_like(hc)

        @pl.when(jnp.logical_and(starts, i >= N_CTX_SEG))
        def _():
            hc[...] = s0[0, 0]

        h, h_out = _rglru_dir(xr[...], xp[...], xn[...], first, last, cw_ref[...], cb_ref[...],
                              wa_ref[d], ba_ref[d], wx_ref[d], bx_ref[d], lam_ref[d], hc[...], rev)
        h_ref[...] = h.astype(BF16)
        hc[...] = h_out

        @pl.when(i < N_CTX_SEG)
        def _():
            f_out[0] = h_out


def _rglru(proj, conv_w, conv_b, wa_bd, ba, wx_bd, bx, lam, s0):
    xcol = 2 * D_A // D_B
    fwd = lambda f: (lambda j: f(j))
    bwd = lambda f: (lambda j: f(NSEG - 1 - j))
    seg_blk = lambda m: pl.BlockSpec((SEG, D_B), m(lambda i: (i, xcol)))
    prev_blk = lambda m: pl.BlockSpec((8, D_B), m(lambda i: (_prev8(i), xcol)))
    next_blk = lambda m: pl.BlockSpec((8, D_B), m(lambda i: (_next8(i), xcol)))
    full = lambda a: pl.BlockSpec(a.shape, lambda j: (0,) * a.ndim)
    return pl.pallas_call(
        _rglru_kernel,
        grid=(NSEG,),
        in_specs=[seg_blk(fwd), prev_blk(fwd), next_blk(fwd), seg_blk(bwd), prev_blk(bwd), next_blk(bwd),
                  full(conv_w), full(conv_b), full(wa_bd), full(ba), full(wx_bd), full(bx), full(lam),
                  pl.BlockSpec((1, 1, 1, D_B), lambda j: (_sample_of(j), 0, 0, 0)),
                  pl.BlockSpec((1, 1, 1, D_B), lambda j: (_sample_of(NSEG - 1 - j), 1, 0, 0))],
        out_specs=[pl.BlockSpec((SEG, D_B), lambda j: (j, 0)),
                   pl.BlockSpec((SEG, D_B), lambda j: (NSEG - 1 - j, 0)),
                   pl.BlockSpec((1, 1, D_B), lambda j: (jnp.minimum(j, N_CTX_SEG - 1), 0, 0)),
                   pl.BlockSpec((1, 1, D_B), lambda j: (jnp.minimum(NSEG - 1 - j, N_CTX_SEG - 1), 0, 0))],
        out_shape=[jax.ShapeDtypeStruct((M_TOK, D_B), BF16), jax.ShapeDtypeStruct((M_TOK, D_B), BF16),
                   jax.ShapeDtypeStruct((N_CTX_SEG, 1, D_B), F32), jax.ShapeDtypeStruct((N_CTX_SEG, 1, D_B), F32)],
        scratch_shapes=[pltpu.VMEM((1, D_B), F32), pltpu.VMEM((1, D_B), F32)],
        compiler_params=_cp(("arbitrary",)),
        name="rglru",
    )(proj, proj, proj, proj, proj, proj, conv_w, conv_b, wa_bd, ba, wx_bd, bx, lam, s0, s0)


def _ab_out_kernel(xp_ref, xs_ref, pos_ref, of_ref, ob_ref, og_ref, hf_ref, hb_ref, yr_ref, mod_ref, hg_ref,
                   w_ref, o_ref):
    m = mod_ref[0]
    f32 = lambda ref: ref[...].astype(F32)
    oa = f32(of_ref) + f32(ob_ref)
    hg = hg_ref[...]
    parts = []
    for h in range(A_HEADS):
        z = oa[:, h * A_DK:(h + 1) * A_DK]
        parts.append(z * lax.rsqrt(jnp.mean(z * z, axis=-1, keepdims=True) + RMS_EPS) * hg)
    o_a = jnp.concatenate(parts, axis=-1) * _silu(f32(og_ref))
    o_b = (f32(hf_ref) + f32(hb_ref)) * _gelu_tanh(f32(yr_ref))
    y = _dot(o_a, w_ref[0:D_A]) + _dot(o_b, w_ref[D_A:D_A + D_B])
    o_ref[...] = _x0(pl.program_id(0), xp_ref, xs_ref, pos_ref) + m[2:3] * y


def _ab_out(xp, xs, pos, proj_h, o_f, o_b, h_f, h_b, mods, hg, w_out):
    seg = lambda width, col: pl.BlockSpec((SEG, width), lambda i: (i, col))
    return pl.pallas_call(
        _ab_out_kernel,
        grid=(NSEG,),
        in_specs=_x0_specs() + [seg(D_A, 0), seg(D_A, 0), seg(D_A, 2), seg(D_B, 0), seg(D_B, 0), seg(D_B, 3),
                                pl.BlockSpec((1, 6, D), lambda i: (i, 0, 0)),
                                pl.BlockSpec((1, A_DK), lambda i: (0, 0)),
                                pl.BlockSpec((D_A + D_B, D), lambda i: (0, 0))],
        out_specs=seg(D, 0),
        out_shape=jax.ShapeDtypeStruct((M_TOK, D), F32),
        compiler_params=_cp(("arbitrary",)),
        name="ab_out",
    )(xp, xs, pos, o_f, o_b, proj_h, h_f, h_b, proj_h, mods, hg.reshape(1, A_DK), w_out)


def _route(scores, sel):
    cols = [sel[e:e + 1, :] for e in range(N_EXPERTS)]

    def rank(vals):
        out = []
        for i, vi in enumerate(vals):
            r = None
            for jx, vj in enumerate(vals):
                if jx == i:
                    continue
                beats = (vj >= vi) if jx < i else (vj > vi)
                r = beats.astype(F32) if r is None else r + beats.astype(F32)
            out.append(r)
        return out

    grp_scores, in_top2 = [], []
    for gi in range(N_GROUPS):
        vals = cols[gi * GROUP:(gi + 1) * GROUP]
        best_pair = None
        for a in range(GROUP):
            for bx in range(a + 1, GROUP):
                s = vals[a] + vals[bx]
                best_pair = s if best_pair is None else jnp.maximum(best_pair, s)
        grp_scores.append(best_pair)
        in_top2.extend([r < 2.0 for r in rank(vals)])
    grp_best = [r < 1.0 for r in rank(grp_scores)]
    picked = [jnp.where(jnp.logical_and(grp_best[e // GROUP], in_top2[e]), scores[e:e + 1, :], 0.0)
              for e in range(N_EXPERTS)]
    total = picked[0]
    for pe in picked[1:]:
        total = total + pe
    row = lax.broadcasted_iota(jnp.int32, scores.shape, 0)
    comb = jnp.zeros(scores.shape, F32)
    for e in range(N_EXPERTS):
        comb = jnp.where(row == e, picked[e] / total, comb)
    return comb


def _moe_kernel(final_norm, x_ref, mod_ref, g_ref, rw_ref, rb_ref, w1_ref, w3_ref, w2_ref, gf_ref, *rest):
    o_refs, (h_sc, comb_sc, acc_sc) = rest[:-3], rest[-3:]
    e = pl.program_id(1)

    @pl.when(e == 0)
    def _():
        for s in range(MOE_TM // SEG):
            m = mod_ref[s]
            rows = slice(s * SEG, (s + 1) * SEG)
            h = _rms_mod(x_ref[rows, :], g_ref[...], m[4:5], m[3:4])
            h_sc[rows, :] = h.astype(BF16)
            scores = _sigmoid(_dot_x3_nt(rw_ref[...], h))
            comb_t = _route(scores, scores + rb_ref[...])
            comb_t = jnp.concatenate([comb_t, jnp.zeros((COMB_LANES - N_EXPERTS, SEG), F32)], axis=0)
            comb_sc[rows, :] = comb_t.T
        acc_sc[...] = jnp.zeros_like(acc_sc)

    lane = lax.broadcasted_iota(jnp.int32, (MOE_TM, COMB_LANES), 1)
    comb = jnp.sum(jnp.where(lane == e, comb_sc[...], 0.0), axis=-1, keepdims=True)
    h = h_sc[...]
    u1 = jnp.dot(h, w1_ref[0, 0].astype(BF16), preferred_element_type=F32)
    u3 = jnp.dot(h, w3_ref[0, 0].astype(BF16), preferred_element_type=F32)
    hid = _silu(u1) * u3 * comb
    acc_sc[...] += jnp.dot(hid.astype(BF16), w2_ref[0, 0].astype(BF16), preferred_element_type=F32)

    def emit(dst_ref):
        for s in range(MOE_TM // SEG):
            rows = slice(s * SEG, (s + 1) * SEG)
            y = x_ref[rows, :] + mod_ref[s][5:6] * acc_sc[rows, :]
            if final_norm:
                y = y * lax.rsqrt(jnp.mean(y * y, axis=-1, keepdims=True) + RMS_EPS) * gf_ref[...]
            dst_ref[rows, :] = y

    last = e == N_EXPERTS - 1
    if final_norm:
        is_ctx = pl.program_id(0) < MOE_CTX_TILES
        pl.when(jnp.logical_and(last, is_ctx))(lambda: emit(o_refs[0]))
        pl.when(jnp.logical_and(last, jnp.logical_not(is_ctx)))(lambda: emit(o_refs[1]))
    else:
        pl.when(last)(lambda: emit(o_refs[0]))


def _moe(x, mods, g, router_w, router_b, w1, w3, w2, layer, gf, final_norm):
    spt = MOE_TM // SEG
    tile = lambda f: pl.BlockSpec((MOE_TM, D), lambda t, e: (f(t), 0))
    if final_norm:
        n_half = M_TOK // 2
        out_specs = [tile(lambda t: jnp.minimum(t, MOE_CTX_TILES - 1)),
                     tile(lambda t: jnp.maximum(t - MOE_CTX_TILES, 0))]
        out_shape = [jax.ShapeDtypeStruct((n_half, D), F32), jax.ShapeDtypeStruct((M_TOK - n_half, D), F32)]
    else:
        out_specs = [tile(lambda t: t)]
        out_shape = [jax.ShapeDtypeStruct((M_TOK, D), F32)]
    return pl.pallas_call(
        functools.partial(_moe_kernel, final_norm),
        grid=(M_TOK // MOE_TM, N_EXPERTS),
        in_specs=[pl.BlockSpec((MOE_TM, D), lambda t, e: (t, 0)),
                  pl.BlockSpec((spt, 6, D), lambda t, e: (t, 0, 0)),
                  pl.BlockSpec((1, D), lambda t, e: (0, 0)),
                  pl.BlockSpec((N_EXPERTS, D), lambda t, e: (0, 0)),
                  pl.BlockSpec((N_EXPERTS, 1), lambda t, e: (0, 0)),
                  pl.BlockSpec((1, 1, D, D_EXPERT), lambda t, e: (layer, e, 0, 0)),
                  pl.BlockSpec((1, 1, D, D_EXPERT), lambda t, e: (layer, e, 0, 0)),
                  pl.BlockSpec((1, 1, D_EXPERT, D), lambda t, e: (layer, e, 0, 0)),
                  pl.BlockSpec((1, D), lambda t, e: (0, 0))],
        out_specs=out_specs,
        out_shape=out_shape,
        scratch_shapes=[pltpu.VMEM((MOE_TM, D), BF16), pltpu.VMEM((MOE_TM, COMB_LANES), F32),
                        pltpu.VMEM((MOE_TM, D), F32)],
        compiler_params=_cp(("arbitrary", "arbitrary")),
        name="moe",
    )(x, mods, g.reshape(1, D), router_w.T, router_b.reshape(N_EXPERTS, 1), w1, w3, w2, gf.reshape(1, D))


def _rw_in_kernel(x_ref, xp_ref, xn_ref, mod_ref, g_ref, mu_ref, wr_ref, wk_ref, wv_ref, g1_ref, g2_ref,
                  w1_ref, w2_ref, w0_ref, a1_ref, a2_ref, a0_ref, ka_ref, rk_ref, hsel_ref,
                  r_ref, k_ref, v_ref, gg_ref, lw_ref, a_ref, coef_ref):
    i = pl.program_id(0)
    m = mod_ref[0]
    g = g_ref[...]
    h = _rms_mod(x_ref[...], g, m[1:2], m[0:1])
    hp = jnp.where(_seg_is_first(i), 0.0, _rms_mod(xp_ref[...], g, m[1:2], m[0:1]))
    hn = jnp.where(_seg_is_last(i), 0.0, _rms_mod(xn_ref[...], g, m[1:2], m[0:1]))
    ext = jnp.concatenate([hp, h, hn], axis=0)
    n = ext.shape[0]
    h_prev = pltpu.roll(ext, 1, 0)[8:8 + SEG]
    h_next = pltpu.roll(ext, n - 1, 0)[8:8 + SEG]
    xx = 0.5 * (h_prev + h_next) - h
    mu = mu_ref[...]
    xr, xw, xk, xv, xa, xg = [h + xx * mu[c:c + 1] for c in range(6)]
    r = _dot(xr, wr_ref[...])
    k = _dot(xk, wk_ref[...])
    r_ref[...] = r.astype(BF16)
    k_ref[...] = k.astype(BF16)
    v_ref[...] = _dot(xv, wv_ref[...]).astype(BF16)
    gg_ref[...] = _dot(_sigmoid(_dot(xg, g1_ref[...])), g2_ref[...]).astype(BF16)
    w_in = w0_ref[...] + _dot(jnp.tanh(_dot(xw, w1_ref[...])), w2_ref[...])
    lw_ref[...] = -W_DECAY_SCALE * _sigmoid(w_in)
    a = _sigmoid(a0_ref[...] + _dot(_dot(xa, a1_ref[...]), a2_ref[...]))
    a_ref[...] = a
    kd_sum = k * (2.0 + (a[:, 0:D] + a[:, D:2 * D] - 2.0) * ka_ref[...])
    coef_ref[...] = _dot_sel_rhs(r * kd_sum * rk_ref[...], hsel_ref[...])


def _rw_inproj(x, mods, g, mu, wr, wk, wv, g1, g2, w1c, w2bd, w0c, a1c, a2bd, a0c, ka, rk):
    full = lambda a: pl.BlockSpec(a.shape, lambda i: (0,) * a.ndim)
    seg = lambda width: pl.BlockSpec((SEG, width), lambda i: (i, 0))
    outs = ([jax.ShapeDtypeStruct((M_TOK, D), BF16)] * 4 + [jax.ShapeDtypeStruct((M_TOK, 2 * D), F32)] * 2
            + [jax.ShapeDtypeStruct((M_TOK, COEF_LANES), F32)])
    hsel = jnp.asarray(np.arange(D)[:, None] // C_HEAD == np.arange(COEF_LANES)[None, :], BF16)
    return pl.pallas_call(
        _rw_in_kernel,
        grid=(NSEG,),
        in_specs=[seg(D),
                  pl.BlockSpec((8, D), lambda i: (_prev8(i), 0)),
                  pl.BlockSpec((8, D), lambda i: (_next8(i), 0)),
                  pl.BlockSpec((1, 6, D), lambda i: (i, 0, 0)),
                  full(g), full(mu), full(wr), full(wk), full(wv), full(g1), full(g2),
                  full(w1c), full(w2bd), full(w0c), full(a1c), full(a2bd), full(a0c),
                  full(ka), full(rk), full(hsel)],
        out_specs=[seg(D)] * 4 + [seg(2 * D)] * 2 + [seg(COEF_LANES)],
        out_shape=outs,
        compiler_params=_cp(("arbitrary",)),
        name="rwkv_inproj",
    )(x, x, x, mods, g, mu, wr, wk, wv, g1, g2, w1c, w2bd, w0c, a1c, a2bd, a0c, ka, rk, hsel)


def _rw_consts():
    t = np.arange(SEG)
    same = (t[:, None] // RW_CHUNK) == (t[None, :] // RW_CHUNK)
    tri_f = np.logical_and(same, t[None, :] <= t[:, None]).astype(np.float32)
    tri_b = np.logical_and(same, t[None, :] >= t[:, None]).astype(np.float32)
    return [jnp.asarray(a, BF16) for a in (tri_f, tri_b)]


def _pair_bd(y, head0):
    return jnp.concatenate([jnp.where(head0, y, 0.0), jnp.where(head0, 0.0, y)], axis=0)


def _rw_dir(r, k, v, lw, a, kk_g, ka_g, tri, rev):
    c = RW_CHUNK
    lane = lax.broadcasted_iota(jnp.int32, (SEG, 2 * C_HEAD), 1)
    head0 = lane < C_HEAD
    kx = k * kk_g
    ss = kx * kx
    n0 = jnp.sum(jnp.where(head0, ss, 0.0), axis=-1, keepdims=True)
    n1 = jnp.sum(jnp.where(head0, 0.0, ss), axis=-1, keepdims=True)
    kk = kx / jnp.maximum(jnp.sqrt(jnp.where(head0, n0, n1)), 1e-12)
    kd = k * (1.0 + (a - 1.0) * ka_g)
    bhat = kk * a
    cum = _dot_sel(tri, lw)
    e_incl = jnp.exp(cum)
    e_inv = jnp.exp(-cum)
    ae = -kk * jnp.exp(cum - lw)
    re = r * e_incl
    bi = bhat * e_inv
    ki = kd * e_inv

    chunks = []
    for ci in range(SEG // c):
        sl = slice(ci * c, (ci + 1) * c)
        ctot = cum[ci * c:ci * c + 1] if rev else cum[(ci + 1) * c - 1:(ci + 1) * c]
        dec = jnp.exp(ctot - cum[sl])
        chunks.append(dict(ae=ae[sl], re=re[sl], bi=bi[sl], ki=ki[sl], v=v[sl], bdec=bhat[sl] * dec,
                           kdec=kd[sl] * dec, gam=jnp.exp(ctot), rev=rev))
    return chunks


def _rw_transitions(chunks):
    c = RW_CHUNK
    w2 = 2 * C_HEAD
    h0c = lax.broadcasted_iota(jnp.int32, (c, w2), 1) < C_HEAD
    rowc = lax.broadcasted_iota(jnp.int32, (c, w2), 0)
    colc = jnp.bitwise_and(lax.broadcasted_iota(jnp.int32, (c, w2), 1), C_HEAD - 1)
    eye = (colc == rowc).astype(F32)
    bdmask = (lax.broadcasted_iota(jnp.int32, (w2, w2), 0) < C_HEAD) == (
        lax.broadcasted_iota(jnp.int32, (w2, w2), 1) < C_HEAD)
    bd = lambda y: _pair_bd(y, h0c)

    n_ab, a_ak, a_rb, a_rk = [], [], [], []
    for ch in chunks:
        strict = (colc > rowc) if ch["rev"] else (colc < rowc)
        incl = (colc >= rowc) if ch["rev"] else (colc <= rowc)
        left = jnp.concatenate([ch["ae"], ch["re"]], axis=0)
        right = jnp.concatenate([jnp.where(h0c, ch["bi"], 0.0), jnp.where(h0c, 0.0, ch["bi"]),
                                 jnp.where(h0c, ch["ki"], 0.0), jnp.where(h0c, 0.0, ch["ki"])], axis=0)
        gm = _dot_nt(left, right)
        n_ab.append(jnp.where(strict, gm[0:c, 0:2 * c], 0.0))
        a_ak.append(jnp.where(strict, gm[0:c, 2 * c:4 * c], 0.0))
        a_rb.append(jnp.where(incl, gm[c:2 * c, 0:2 * c], 0.0))
        a_rk.append(jnp.where(incl, gm[c:2 * c, 2 * c:4 * c], 0.0))
    tm = [eye + n for n in n_ab]
    p = [_x3_shared([x], bd(x))[0] for x in n_ab]
    for it in range(4):
        if it < 2:
            res = [_x3_shared([x, t], bd(x)) for x, t in zip(p, tm)]
            p = [r[0] for r in res]
            tm = [t + r[1] for t, r in zip(tm, res)]
        else:
            res = [_dot(jnp.concatenate([x, t], axis=0), bd(x)) for x, t in zip(p, tm)]
            p = [r[0:c] for r in res]
            tm = [t + r[c:2 * c] for t, r in zip(tm, res)]
    tm = [t + _dot(t, bd(x)) for t, x in zip(tm, p)]
    vbd = [bd(ch["v"]) for ch in chunks]
    av = [_dot(x, vb) for x, vb in zip(a_ak, vbd)]
    tav = [_x3_shared([t], jnp.concatenate([bd(ch["ae"]), bd(x)], axis=1))[0]
           for t, ch, x in zip(tm, chunks, av)]
    out = []
    for i, ch in enumerate(chunks):
        ta, tv = tav[i][:, 0:w2], tav[i][:, w2:2 * w2]
        q = ch["re"] + _dot(a_rb[i], bd(ta))
        y = _dot(a_rb[i], bd(tv)) + _dot(a_rk[i], vbd[i])
        wz = _dot_tn(tav[i], ch["bdec"])
        w = jnp.where(bdmask, wz[0:w2], 0.0)
        z = jnp.where(bdmask, wz[w2:2 * w2] + _dot_tn(ch["v"], ch["kdec"]), 0.0)
        out.append((q, y, w, z, ch["gam"]))
    return out


def _rw_scan_kernel(rf, kf, vf, lwf, af, rb, kb, vb, lwb, ab, kkg_ref, kag_ref, s0f, s0b, trif, trib,
                    of_ref, ob_ref, sf_out, sb_out, st_f, st_b):
    j = pl.program_id(1)
    w2 = 2 * C_HEAD
    dirs = ((False, (rf, kf, vf, lwf, af), s0f, trif, of_ref, sf_out, st_f),
            (True, (rb, kb, vb, lwb, ab), s0b, trib, ob_ref, sb_out, st_b))
    chunks = []
    for rev, refs, s0, tri, o_ref, s_out, st in dirs:
        for pp in range(RW_PAIRS_STEP):
            lanes = slice(pp * w2, (pp + 1) * w2)
            r_, k_, v_, lw_, a_ = [z[:, lanes].astype(F32) for z in refs]
            chunks.extend(_rw_dir(r_, k_, v_, lw_, a_, kkg_ref[:, lanes], kag_ref[:, lanes], tri[...], rev))
    trans = _rw_transitions(chunks)
    n_c = SEG // RW_CHUNK
    for d, (rev, refs, s0, tri, o_ref, s_out, st) in enumerate(dirs):
        i = (NSEG - 1 - j) if rev else j
        starts = _seg_is_last(i) if rev else _seg_is_first(i)

        @pl.when(jnp.logical_and(starts, i < N_CTX_SEG))
        def _():
            st[...] = jnp.zeros_like(st)

        @pl.when(jnp.logical_and(starts, i >= N_CTX_SEG))
        def _():
            st[...] = s0[0, 0]

        for pp in range(RW_PAIRS_STEP):
            s = st[pp]
            for ci in (range(n_c - 1, -1, -1) if rev else range(n_c)):
                q, y, w, z, gam = trans[(d * RW_PAIRS_STEP + pp) * n_c + ci]
                o_ref[ci * RW_CHUNK:(ci + 1) * RW_CHUNK, pp * w2:(pp + 1) * w2] = (_dot_nt(q, s) + y).astype(BF16)
                s = s * gam + _dot(s, w) + z
            st[pp] = s

            @pl.when(i < N_CTX_SEG)
            def _():
                s_out[0, 2 * pp] = s[0:C_HEAD, 0:C_HEAD]
                s_out[0, 2 * pp + 1] = s[C_HEAD:w2, C_HEAD:w2]


def _rw_scan(r, k, v, lw, a, kk_g, ka_g, s0bd):
    consts = _rw_consts()
    w = 2 * C_HEAD
    pps = RW_PAIRS_STEP
    wb = pps * w
    n_steps = C_PAIRS // pps
    blk = lambda col0, rev: pl.BlockSpec(
        (SEG, wb), (lambda p, j: (NSEG - 1 - j, col0 + p)) if rev else (lambda p, j: (j, col0 + p)))
    cspec = lambda arr: pl.BlockSpec(arr.shape, lambda p, j: (0, 0))
    hps = 2 * pps
    fin_f = pl.BlockSpec((1, hps, C_HEAD, C_HEAD), lambda p, j: (jnp.minimum(j, N_CTX_SEG - 1), p, 0, 0))
    fin_b = pl.BlockSpec((1, hps, C_HEAD, C_HEAD),
                         lambda p, j: (jnp.minimum(NSEG - 1 - j, N_CTX_SEG - 1), p, 0, 0))
    return pl.pallas_call(
        _rw_scan_kernel,
        grid=(n_steps, NSEG),
        in_specs=[blk(0, False), blk(0, False), blk(0, False), blk(0, False), blk(0, False),
                  blk(0, True), blk(0, True), blk(0, True), blk(n_steps, True), blk(n_steps, True),
                  pl.BlockSpec((1, wb), lambda p, j: (0, p)), pl.BlockSpec((1, wb), lambda p, j: (0, p)),
                  pl.BlockSpec((1, 1, pps, w, w), lambda p, j: (_sample_of(j), 0, p, 0, 0)),
                  pl.BlockSpec((1, 1, pps, w, w), lambda p, j: (_sample_of(NSEG - 1 - j), 1, p, 0, 0)),
                  cspec(consts[0]), cspec(consts[1])],
        out_specs=[blk(0, False), blk(0, True), fin_f, fin_b],
        out_shape=[jax.ShapeDtypeStruct((M_TOK, D), BF16), jax.ShapeDtypeStruct((M_TOK, D), BF16),
                   jax.ShapeDtypeStruct((N_CTX_SEG, 2 * C_PAIRS, C_HEAD, C_HEAD), F32),
                   jax.ShapeDtypeStruct((N_CTX_SEG, 2 * C_PAIRS, C_HEAD, C_HEAD), F32)],
        scratch_shapes=[pltpu.VMEM((pps, w, w), F32), pltpu.VMEM((pps, w, w), F32)],
        compiler_params=_cp(("arbitrary", "arbitrary")),
        name="rwkv7_scan",
    )(r, k, v, lw, a, r, k, v, lw, a, kk_g, ka_g, s0bd, s0bd, *consts)


def _rw_out_kernel(x_ref, of_ref, ob_ref, v_ref, gg_ref, coef_ref, mod_ref, lnw_ref, lnb_ref, ones_ref,
                   hexp_ref, wo_ref, o_ref):
    m = mod_ref[0]
    ones_bd = ones_ref[...]
    w = 2 * C_HEAD
    inv_n = 1.0 / C_HEAD
    coef = _dot_sel_rhs(coef_ref[...], hexp_ref[...])
    parts = []
    for p in range(C_PAIRS):
        cs = slice(p * w, (p + 1) * w)
        osum = of_ref[:, cs].astype(F32) + ob_ref[:, cs].astype(F32)
        mu = _dot_sel_rhs(osum, ones_bd) * inv_n
        cen = osum - mu
        var = _dot_sel_rhs(cen * cen, ones_bd) * inv_n
        o = cen * lax.rsqrt(var + GN_EPS) * lnw_ref[:, cs] + lnb_ref[:, cs]
        bonus = coef[:, cs] * v_ref[:, cs].astype(F32)
        parts.append((o + bonus) * gg_ref[:, cs].astype(F32))
    y = _dot(jnp.concatenate(parts, axis=-1), wo_ref[...])
    o_ref[...] = x_ref[...] + m[2:3] * y


def _dot_sel_rhs(x, mat):
    h, m, l = _split3(x)
    return (jnp.dot(h, mat, preferred_element_type=F32) + jnp.dot(m, mat, preferred_element_type=F32)
            + jnp.dot(l, mat, preferred_element_type=F32))


def _rw_out(x, o_f, o_b, v, gg, coef, mods, lnw, lnb, wo):
    seg = lambda width: pl.BlockSpec((SEG, width), lambda i: (i, 0))
    row = pl.BlockSpec((1, D), lambda i: (0, 0))
    hh = np.arange(2 * C_HEAD) // C_HEAD
    ones_bd = jnp.asarray((hh[:, None] == hh[None, :]).astype(np.float32), BF16)
    hexp = jnp.asarray(np.arange(COEF_LANES)[:, None] == np.arange(D)[None, :] // C_HEAD, BF16)
    return pl.pallas_call(
        _rw_out_kernel,
        grid=(NSEG,),
        in_specs=[seg(D), seg(D), seg(D), seg(D), seg(D), seg(COEF_LANES),
                  pl.BlockSpec((1, 6, D), lambda i: (i, 0, 0)),
                  row, row,
                  pl.BlockSpec((2 * C_HEAD, 2 * C_HEAD), lambda i: (0, 0)),
                  pl.BlockSpec((COEF_LANES, D), lambda i: (0, 0)),
                  pl.BlockSpec((D, D), lambda i: (0, 0))],
        out_specs=seg(D),
        out_shape=jax.ShapeDtypeStruct((M_TOK, D), F32),
        compiler_params=_cp(("arbitrary",)),
        name="rwkv_out",
    )(x, o_f, o_b, v, gg, coef, mods, lnw.reshape(1, D), lnb.reshape(1, D), ones_bd, hexp, wo)


def _grid_pos_table(n_tok):
    rows = n_tok // GRID_W
    r, cl = np.meshgrid(np.arange(rows, dtype=np.float32), np.arange(GRID_W, dtype=np.float32), indexing='ij')
    quarter = D // 4
    omega = (1.0 / (np.float32(POS_BASE) ** (np.arange(quarter, dtype=np.float32) / np.float32(quarter))))
    ang_r = (r.reshape(-1, 1) * omega).astype(np.float32)
    ang_c = (cl.reshape(-1, 1) * omega).astype(np.float32)
    table = np.concatenate([np.sin(ang_r), np.cos(ang_r), np.sin(ang_c), np.cos(ang_c)], axis=-1)
    return jnp.asarray(table.astype(np.float32))


def _block_diag(blocks):
    g, n, _ = blocks.shape
    eye = jnp.eye(g, dtype=blocks.dtype)
    return (eye[:, None, :, None] * blocks[:, :, None, :]).reshape(g * n, g * n)


def kernel(x_prompt, x_sample, state_hgrn, state_rglru, state_rwkv, c, c_ctx, norm_mix_g, norm_ffn_g, w_mod, b_mod, ab_w_in, ab_w_out, hgrn_lb, hgrn_norm_g, rg_conv_w, rg_conv_b, rg_wa, rg_ba, rg_wx, rg_bx, rg_lambda, rw_mu, rw_wr, rw_wk, rw_wv, rw_wo, rw_w0, rw_w1, rw_w2, rw_a0, rw_a1, rw_a2, rw_g1, rw_g2, rw_kk, rw_ka, rw_rk, rw_lnw, rw_lnb, moe_router, moe_router_bias, moe_w1, moe_w3, moe_w2, norm_f_g):
    bf = lambda z: z.astype(BF16)
    xp = x_prompt.reshape(-1, D)
    xs = x_sample.reshape(-1, D)
    pos = _grid_pos_table(x_sample.shape[1])
    mods = _modulations(c, c_ctx, w_mod, b_mod)

    lower_bounds = jnp.cumsum(jax.nn.softmax(hgrn_lb.astype(F32), axis=1), axis=1)
    lb = lower_bounds[:, 0].reshape(2, A_HEADS, 1, A_DK)
    w_in = ab_w_in[0]
    w_in = bf(jnp.concatenate([w_in[:, D_A:3 * D_A], w_in[:, 5 * D_A:5 * D_A + D_B], w_in[:, 0:D_A],
                               w_in[:, 3 * D_A:5 * D_A], w_in[:, 5 * D_A + D_B:]], axis=1))
    proj_f, proj_h = _ab_inproj(xp, xs, pos, mods[0], norm_mix_g[0], w_in)
    s0t = jnp.swapaxes(state_hgrn[:, 0], -1, -2)
    o_f, o_b, sf, sb = _gla(proj_f, proj_h, lb, s0t)
    wa_bd = bf(jnp.stack([_block_diag(rg_wa[0, d]) for d in range(2)]))
    wx_bd = bf(jnp.stack([_block_diag(rg_wx[0, d]) for d in range(2)]))
    h_f, h_b, lru_f, lru_b = _rglru(
        proj_f, rg_conv_w[0], rg_conv_b[0].reshape(1, D_B), wa_bd, rg_ba[0].reshape(2, 1, D_B), wx_bd,
        rg_bx[0].reshape(2, 1, D_B), rg_lambda[0].reshape(2, 1, D_B), state_rglru[:, 0].reshape(-1, 2, 1, D_B))
    x = _ab_out(xp, xs, pos, proj_h, o_f, o_b, h_f, h_b, mods[0], hgrn_norm_g[0], bf(ab_w_out[0]))
    x, = _moe(x, mods[0], norm_ffn_g[0], moe_router, moe_router_bias, moe_w1, moe_w3, moe_w2, 0, norm_f_g, False)

    w1c = bf(jnp.concatenate([rw_w1[0, 0], rw_w1[0, 1]], axis=-1))
    a1c = bf(jnp.concatenate([rw_a1[0, 0], rw_a1[0, 1]], axis=-1))
    w2bd = bf(jnp.concatenate([jnp.concatenate([rw_w2[0, 0], jnp.zeros_like(rw_w2[0, 0])], axis=-1),
                               jnp.concatenate([jnp.zeros_like(rw_w2[0, 1]), rw_w2[0, 1]], axis=-1)], axis=0))
    a2bd = bf(jnp.concatenate([jnp.concatenate([rw_a2[0, 0], jnp.zeros_like(rw_a2[0, 0])], axis=-1),
                               jnp.concatenate([jnp.zeros_like(rw_a2[0, 1]), rw_a2[0, 1]], axis=-1)], axis=0))
    r, k, v, gg, lw, a, coef = _rw_inproj(
        x, mods[1], norm_mix_g[1].reshape(1, D), rw_mu[0], bf(rw_wr[0]), bf(rw_wk[0]), bf(rw_wv[0]),
        bf(rw_g1[0]), bf(rw_g2[0]), w1c, w2bd, rw_w0[0].reshape(1, 2 * D), a1c, a2bd, rw_a0[0].reshape(1, 2 * D),
        rw_ka[0].reshape(1, D), rw_rk[0].reshape(1, D))
    s0 = state_rwkv[:, 0].reshape(N_SAMPLE, 2, C_PAIRS, 2, C_HEAD, C_HEAD)
    zeros = jnp.zeros_like(s0[:, :, :, 0])
    s0bd = jnp.concatenate([jnp.concatenate([s0[:, :, :, 0], zeros], axis=-1),
                            jnp.concatenate([zeros, s0[:, :, :, 1]], axis=-1)], axis=-2)
    ow_f, ow_b, rs_f, rs_b = _rw_scan(r, k, v, lw, a, rw_kk[0].reshape(1, D), rw_ka[0].reshape(1, D), s0bd)
    x = _rw_out(x, ow_f, ow_b, v, gg, coef, mods[1], rw_lnw[0], rw_lnb[0], bf(rw_wo[0]))
    y_p, y_s = _moe(x, mods[1], norm_ffn_g[1], moe_router, moe_router_bias, moe_w1, moe_w3, moe_w2, 1, norm_f_g, True)

    new_hgrn = jnp.swapaxes(jnp.stack([sf, sb], axis=1), -1, -2)[:, None]
    new_rglru = jnp.stack([lru_f[:, 0], lru_b[:, 0]], axis=1)[:, None]
    new_rwkv = jnp.stack([rs_f, rs_b], axis=1)[:, None]
    return (y_p.reshape(x_prompt.shape), y_s.reshape(x_sample.shape), new_hgrn, new_rglru, new_rwkv)
```

```python
import functools
import math

import numpy as np
import jax
import jax.numpy as jnp
from jax import lax
from jax.experimental import pallas as pl
from jax.experimental.pallas import tpu as pltpu

F32 = jnp.float32
BF16 = jnp.bfloat16

D = 1024
SEG = 256
N_CTX_SEG = 16
SEG_PER_SAMPLE = 4
N_SAMPLE = 4
NSEG = N_CTX_SEG + N_SAMPLE * SEG_PER_SAMPLE
M_TOK = NSEG * SEG
SUBLANES = 8
ROWS8_PER_SEG = SEG // SUBLANES

A_HEADS = 4
A_DK = 128
D_A = 512
D_B = 512
B_BLOCKS = 8
B_BLOCK = 64
LRU_C = 8.0
D_IN_AB = 5 * D_A + 2 * D_B
AB_F32_COLS = 2 * D_A + D_B
C_HEAD = 64
C_PAIRS = 8
RW_CHUNK = 64
COEF_LANES = 128
RW_PAIRS_STEP = 4
W_DECAY_SCALE = math.exp(-0.5)
N_EXPERTS = 16
N_GROUPS = 4
GROUP = 4
D_EXPERT = 256
RMS_EPS = 1e-6
GN_EPS = 64e-5
POS_BASE = 10000.0
GRID_W = 64
MOE_TM = 1024
MOE_CTX_TILES = N_CTX_SEG * SEG // MOE_TM
COMB_LANES = 128
GLA_LEVELS = (1, 2, 4, 8, 16, 32, 64, 128)
GLA_HALF = 128
GLA_HEADS_STEP = 4

VMEM_LIMIT = 56 * 1024 * 1024


def _cp(sem):
    return pltpu.CompilerParams(dimension_semantics=sem, vmem_limit_bytes=VMEM_LIMIT)


def _sigmoid(x):
    return 0.5 * jnp.tanh(0.5 * x) + 0.5


def _silu(x):
    return x * _sigmoid(x)


def _gelu_tanh(x):
    return 0.5 * x * (1.0 + jnp.tanh(math.sqrt(2.0 / math.pi) * (x + 0.044715 * (x * x * x))))


def _rms_mod(x, g, scale, shift):
    ms = jnp.mean(x * x, axis=-1, keepdims=True)
    return x * lax.rsqrt(ms + RMS_EPS) * g * (1.0 + scale) + shift


def _dot(a, b):
    return jnp.dot(a.astype(BF16), b.astype(BF16), preferred_element_type=F32)


def _dot_nt(a, b):
    return lax.dot_general(a.astype(BF16), b.astype(BF16), (((1,), (1,)), ((), ())),
                           preferred_element_type=F32)


def _dot_tn(a, b):
    return lax.dot_general(a.astype(BF16), b.astype(BF16), (((0,), (0,)), ((), ())),
                           preferred_element_type=F32)


def _split3(x):
    h = x.astype(BF16)
    r1 = x - h.astype(F32)
    m = r1.astype(BF16)
    r2 = r1 - m.astype(F32)
    return h, m, r2.astype(BF16)


def _dot_sel(mat, x):
    h, m, l = _split3(x)
    return (jnp.dot(mat, h, preferred_element_type=F32) + jnp.dot(mat, m, preferred_element_type=F32)
            + jnp.dot(mat, l, preferred_element_type=F32))


def _dot_x3(a, b):
    ah = a.astype(BF16)
    al = (a - ah.astype(F32)).astype(BF16)
    bh = b.astype(BF16)
    bl = (b - bh.astype(F32)).astype(BF16)
    return (jnp.dot(ah, bh, preferred_element_type=F32) + jnp.dot(ah, bl, preferred_element_type=F32)
            + jnp.dot(al, bh, preferred_element_type=F32))


def _dot_x3_nt(a, b):
    dn = (((1,), (1,)), ((), ()))
    ah = a.astype(BF16)
    al = (a - ah.astype(F32)).astype(BF16)
    bh = b.astype(BF16)
    bl = (b - bh.astype(F32)).astype(BF16)
    return (lax.dot_general(ah, bh, dn, preferred_element_type=F32)
            + lax.dot_general(ah, bl, dn, preferred_element_type=F32)
            + lax.dot_general(al, bh, dn, preferred_element_type=F32))


def _x3_shared(lhs_list, rhs_pairs, bd16):
    n = lhs_list[0].shape[0]
    k = len(lhs_list)
    r_his = [y.astype(BF16) for y in rhs_pairs]
    r_los = [(y - h.astype(F32)).astype(BF16) for y, h in zip(rhs_pairs, r_his)]
    cat = lambda parts: parts[0] if len(parts) == 1 else jnp.concatenate(parts, axis=1)
    rh = cat([bd16(h) for h in r_his])
    rl = cat([bd16(l) for l in r_los])
    his = [a.astype(BF16) for a in lhs_list]
    los = [(a - h.astype(F32)).astype(BF16) for a, h in zip(lhs_list, his)]
    r1 = jnp.dot(jnp.concatenate(his + los, axis=0), rh, preferred_element_type=F32)
    r2 = jnp.dot(his[0] if k == 1 else jnp.concatenate(his, axis=0), rl, preferred_element_type=F32)
    return [r1[i * n:(i + 1) * n] + r1[(k + i) * n:(k + i + 1) * n] + r2[i * n:(i + 1) * n] for i in range(k)]


def _seg_is_first(i):
    return jnp.logical_or(i < N_CTX_SEG, lax.rem(i - N_CTX_SEG, SEG_PER_SAMPLE) == 0)


def _seg_is_last(i):
    return jnp.logical_or(i < N_CTX_SEG, lax.rem(i - N_CTX_SEG, SEG_PER_SAMPLE) == SEG_PER_SAMPLE - 1)


def _sample_of(i):
    return jnp.maximum(i - N_CTX_SEG, 0) // SEG_PER_SAMPLE


def _prev8(i):
    return jnp.maximum(i * ROWS8_PER_SEG - 1, 0)


def _next8(i):
    return jnp.minimum((i + 1) * ROWS8_PER_SEG, M_TOK // 8 - 1)


def _mod_kernel(cv_ref, w_ref, b_ref, o_ref):
    cv = cv_ref[...]
    o_ref[0] = _dot(_silu(cv), w_ref[0]) + b_ref[0]


def _modulations(c, c_ctx, w_mod, b_mod):
    depth = w_mod.shape[0]
    cv = jnp.concatenate([c_ctx[None, :], c, jnp.zeros((3, D), F32)], axis=0)
    n_t = 6
    mod = pl.pallas_call(
        _mod_kernel,
        grid=(depth, n_t),
        in_specs=[pl.BlockSpec((8, D), lambda l, n: (0, 0)),
                  pl.BlockSpec((1, D, D), lambda l, n: (l, 0, n)),
                  pl.BlockSpec((1, 1, D), lambda l, n: (l, 0, n))],
        out_specs=pl.BlockSpec((1, 8, D), lambda l, n: (l, 0, n)),
        out_shape=jax.ShapeDtypeStruct((depth, 8, 6 * D), F32),
        compiler_params=_cp(("arbitrary", "arbitrary")),
        name="adaln_mod",
    )(cv, w_mod, b_mod.reshape(depth, 1, 6 * D))
    row_of_seg = np.array([0] * N_CTX_SEG + [1 + s // SEG_PER_SAMPLE for s in range(N_SAMPLE * SEG_PER_SAMPLE)])
    return mod[:, row_of_seg].reshape(depth, NSEG, 6, D)


def _x0_specs():
    return [pl.BlockSpec((SEG, D), lambda i: (jnp.minimum(i, N_CTX_SEG - 1), 0)),
            pl.BlockSpec((SEG, D), lambda i: (jnp.maximum(i - N_CTX_SEG, 0), 0)),
            pl.BlockSpec((SEG, D), lambda i: (lax.rem(jnp.maximum(i - N_CTX_SEG, 0), SEG_PER_SAMPLE), 0))]


def _x0(i, xp_ref, xs_ref, pos_ref):
    return jnp.where(i < N_CTX_SEG, xp_ref[...], xs_ref[...] + pos_ref[...])


def _ab_in_kernel(xp_ref, xs_ref, pos_ref, mod_ref, g_ref, w_ref, of_ref, oh_ref):
    m = mod_ref[0]
    x = _x0(pl.program_id(0), xp_ref, xs_ref, pos_ref)
    h = _rms_mod(x, g_ref[...], m[1:2], m[0:1])
    res = jnp.dot(h.astype(BF16), w_ref[...], preferred_element_type=F32)
    of_ref[:, 0:2 * D_A] = res[:, D_A:3 * D_A]
    of_ref[:, 2 * D_A:AB_F32_COLS] = res[:, 5 * D_A:5 * D_A + D_B]
    oh_ref[:, 0:D_A] = res[:, 0:D_A].astype(BF16)
    oh_ref[:, D_A:3 * D_A] = res[:, 3 * D_A:5 * D_A].astype(BF16)
    oh_ref[:, 3 * D_A:3 * D_A + D_B] = res[:, 5 * D_A + D_B:D_IN_AB].astype(BF16)


def _ab_inproj(xp, xs, pos, mods, g, w_in):
    return pl.pallas_call(
        _ab_in_kernel,
        grid=(NSEG,),
        in_specs=_x0_specs() + [pl.BlockSpec((1, 6, D), lambda i: (i, 0, 0)),
                                pl.BlockSpec((1, D), lambda i: (0, 0)),
                                pl.BlockSpec((D, D_IN_AB), lambda i: (0, 0))],
        out_specs=[pl.BlockSpec((SEG, AB_F32_COLS), lambda i: (i, 0)),
                   pl.BlockSpec((SEG, D_IN_AB - AB_F32_COLS), lambda i: (i, 0))],
        out_shape=[jax.ShapeDtypeStruct((M_TOK, AB_F32_COLS), F32),
                   jax.ShapeDtypeStruct((M_TOK, D_IN_AB - AB_F32_COLS), BF16)],
        compiler_params=_cp(("arbitrary",)),
        name="ab_inproj",
    )(xp, xs, pos, mods, g.reshape(1, D), w_in)


def _gla_consts():
    t = np.arange(SEG)
    tri_f = (t[None, :] <= t[:, None]).astype(np.float32)
    tri_b = (t[None, :] >= t[:, None]).astype(np.float32)
    th = np.arange(GLA_HALF)
    xor = th[:, None] ^ th[None, :]
    hb = np.where(xor > 0, 1 << np.floor(np.log2(np.maximum(xor, 1))).astype(np.int64), 0)
    code_f = np.where(th[None, :] < th[:, None], hb, 0).astype(np.int32)
    code_b = np.where(th[None, :] > th[:, None], hb, 0).astype(np.int32)
    return [jnp.asarray(tri_f, BF16), jnp.asarray(tri_b, BF16), jnp.asarray(code_f), jnp.asarray(code_b)]


def _gla_level_operand(q, k, b, g, rowi, w, rev):
    upper = jnp.bitwise_and(rowi, w) != 0
    qside = jnp.logical_not(upper) if rev else upper
    if w == 1:
        z = jnp.where(qside, g, 0.0)
    elif w >= SUBLANES:
        nv = 2 * w // SUBLANES
        b4 = b.reshape(SEG // (2 * w), nv, SUBLANES, A_DK)
        ref = (b4[:, nv // 2:nv // 2 + 1, 0:1, :] if rev
               else b4[:, nv // 2 - 1:nv // 2, SUBLANES - 1:SUBLANES, :])
        x = (b4 - ref).reshape(SEG, A_DK)
        z = jnp.where(qside, x, -x)
    else:
        b3 = b.reshape(SEG // SUBLANES, SUBLANES, A_DK)
        sub = lax.broadcasted_iota(jnp.int32, b3.shape, 1)
        beta = None
        for jb in range(SUBLANES // (2 * w)):
            r = jb * 2 * w + (w if rev else w - 1)
            cand = jnp.broadcast_to(b3[:, r:r + 1, :], b3.shape)
            beta = cand if beta is None else jnp.where(sub >= jb * 2 * w, cand, beta)
        x = (b3 - beta).reshape(SEG, A_DK)
        z = jnp.where(qside, x, -x)
    return jnp.where(qside, q, k) * jnp.exp(z)


def _gla_dir(qraw, fraw, v, lb, st, tri, code, rev):
    hh = GLA_HALF
    q = _silu(qraw)
    f = lb + (1.0 - lb) * _sigmoid(fraw)
    g = jnp.log(f)
    k = 1.0 - f
    b = _dot_sel(tri, g)
    rowi = lax.broadcasted_iota(jnp.int32, (SEG, A_DK), 0)
    att = [jnp.zeros((hh, hh), F32), jnp.zeros((hh, hh), F32)]
    cross = None
    for w in GLA_LEVELS:
        m = _gla_level_operand(q, k, b, g, rowi, w, rev).astype(BF16)
        if w == hh:
            cross = _dot_nt(m[0:hh], m[hh:SEG]) if rev else _dot_nt(m[hh:SEG], m[0:hh])
        else:
            for half in range(2):
                mh = m[half * hh:(half + 1) * hh]
                att[half] = jnp.where(code == w, _dot_nt(mh, mh), att[half])
    if rev:
        o_lo = _dot(jnp.concatenate([att[0], cross], axis=1), v)
        o_hi = _dot(att[1], v[hh:SEG])
    else:
        o_lo = _dot(att[0], v[0:hh])
        o_hi = _dot(jnp.concatenate([cross, att[1]], axis=1), v)
    diag = jnp.sum(q * k, axis=-1, keepdims=True)
    o = jnp.concatenate([o_lo, o_hi], axis=0) + diag * v + _dot_nt(q * jnp.exp(b), st)
    btot = b[0:1] if rev else b[SEG - 1:SEG]
    st_new = st * jnp.exp(btot) + _dot_tn(v, k * jnp.exp(btot - b))
    return o, st_new


def _gla_kernel(qf, ff, vf, qb, fb, vb, lb_ref, s0f, s0b, trif, trib, codef, codeb,
                of_ref, ob_ref, fin_ref, st_f, st_b):
    j = pl.program_id(1)
    dirs = ((False, qf, ff, vf, s0f, trif, codef, of_ref, st_f),
            (True, qb, fb, vb, s0b, trib, codeb, ob_ref, st_b))
    for rev, qr, fr, vr, s0, tri, code, o_ref, st in dirs:
        i = (NSEG - 1 - j) if rev else j
        starts = _seg_is_last(i) if rev else _seg_is_first(i)

        @pl.when(jnp.logical_and(starts, i < N_CTX_SEG))
        def _():
            st[...] = jnp.zeros_like(st)

        @pl.when(jnp.logical_and(starts, i >= N_CTX_SEG))
        def _():
            st[...] = s0[0, 0]

    finals = []
    for d, (rev, qr, fr, vr, s0, tri, code, o_ref, st) in enumerate(dirs):
        for hh in range(GLA_HEADS_STEP):
            lanes = slice(hh * A_DK, (hh + 1) * A_DK)
            o, st_new = _gla_dir(qr[:, lanes].astype(F32), fr[:, lanes], vr[:, lanes].astype(F32),
                                 lb_ref[d, hh], st[hh], tri[...], code[...], rev)
            o_ref[:, lanes] = o.astype(BF16)
            finals.append(st_new)
    for d, (rev, qr, fr, vr, s0, tri, code, o_ref, st) in enumerate(dirs):
        for hh in range(GLA_HEADS_STEP):
            st[hh] = finals[d * GLA_HEADS_STEP + hh]
    for d, (rev, qr, fr, vr, s0, tri, code, o_ref, st) in enumerate(dirs):
        i = (NSEG - 1 - j) if rev else j

        @pl.when(i < N_CTX_SEG)
        def _():
            for hh in range(GLA_HEADS_STEP):
                fin_ref[i, 0, d, pl.program_id(0) * GLA_HEADS_STEP + hh] = finals[d * GLA_HEADS_STEP + hh].T


def _gla(proj_f, proj_h, lb, s0t):
    consts = _gla_consts()
    hs = GLA_HEADS_STEP
    wb = hs * A_DK
    n_col = D_A // wb
    blk = lambda col0, rev: pl.BlockSpec(
        (SEG, wb), (lambda h, j: (NSEG - 1 - j, col0 + h)) if rev else (lambda h, j: (j, col0 + h)))
    cspec = lambda a: pl.BlockSpec(a.shape, lambda h, j: (0, 0))
    fin_shape = (N_CTX_SEG, 1, 2, A_HEADS, A_DK, A_DK)
    fin = pl.BlockSpec(fin_shape, lambda h, j: (0,) * len(fin_shape))
    return pl.pallas_call(
        _gla_kernel,
        grid=(A_HEADS // hs, NSEG),
        in_specs=[blk(0, False), blk(0, False), blk(n_col, False),
                  blk(0, True), blk(n_col, True), blk(n_col, True),
                  pl.BlockSpec((2, hs, 1, A_DK), lambda h, j: (0, h, 0, 0)),
                  pl.BlockSpec((1, 1, hs, A_DK, A_DK), lambda h, j: (_sample_of(j), 0, h, 0, 0)),
                  pl.BlockSpec((1, 1, hs, A_DK, A_DK), lambda h, j: (_sample_of(NSEG - 1 - j), 1, h, 0, 0)),
                  cspec(consts[0]), cspec(consts[1]), cspec(consts[2]), cspec(consts[3])],
        out_specs=[blk(0, False), blk(0, True), fin],
        out_shape=[jax.ShapeDtypeStruct((M_TOK, D_A), BF16), jax.ShapeDtypeStruct((M_TOK, D_A), BF16),
                   jax.ShapeDtypeStruct(fin_shape, F32)],
        scratch_shapes=[pltpu.VMEM((hs, A_DK, A_DK), F32), pltpu.VMEM((hs, A_DK, A_DK), F32)],
        compiler_params=_cp(("arbitrary", "arbitrary")),
        name="hgrn2_gla",
    )(proj_h, proj_f, proj_h, proj_h, proj_f, proj_h, lb, s0t, s0t, *consts)


def _lin_scan(a, b, h_in, rev):
    t_len, c = a.shape
    ng = t_len // SUBLANES
    a3 = a.reshape(ng, SUBLANES, c)
    b3 = b.reshape(ng, SUBLANES, c)
    sub = lax.broadcasted_iota(jnp.int32, a3.shape, 1)
    s = 1
    while s < SUBLANES:
        shift = (SUBLANES - s) if rev else s
        valid = (sub < SUBLANES - s) if rev else (sub >= s)
        ap = jnp.where(valid, pltpu.roll(a3, shift, 1), 1.0)
        bp = jnp.where(valid, pltpu.roll(b3, shift, 1), 0.0)
        b3 = a3 * bp + b3
        a3 = a3 * ap
        s *= 2
    hs = [None] * ng
    carry = h_in
    for j in (range(ng - 1, -1, -1) if rev else range(ng)):
        hs[j] = a3[j] * carry + b3[j]
        carry = hs[j][0:1] if rev else hs[j][SUBLANES - 1:SUBLANES]
    return jnp.concatenate(hs, axis=0), carry


def _rglru_dir(x, xprev, xnext, first, last, cw, cb, wa, ba, wx, bx, lam, h_in, rev):
    zero = jnp.zeros_like(xprev)
    ext = jnp.concatenate([jnp.where(first, zero, xprev), x, jnp.where(last, zero, xnext)], axis=0)
    n = ext.shape[0]
    xm2 = pltpu.roll(ext, 2, 0)[8:8 + SEG]
    xm1 = pltpu.roll(ext, 1, 0)[8:8 + SEG]
    xp1 = pltpu.roll(ext, n - 1, 0)[8:8 + SEG]
    xc = cb + xm2 * cw[0:1] + xm1 * cw[1:2] + x * cw[2:3] + xp1 * cw[3:4]
    gate_r = _sigmoid(_dot(xc, wa) + ba)
    gate_i = _sigmoid(_dot(xc, wx) + bx)
    softplus_neg_lam = jnp.maximum(-lam, 0.0) + jnp.log(1.0 + jnp.exp(-jnp.abs(lam)))
    log_a = -LRU_C * gate_r * softplus_neg_lam
    a = jnp.exp(log_a)
    b_in = jnp.sqrt(1.0 - a * a) * gate_i * xc
    return _lin_scan(a, b_in, h_in, rev)


def _rglru_kernel(xf, xf_p, xf_n, xb, xb_p, xb_n, cw_ref, cb_ref, wa_ref, ba_ref, wx_ref, bx_ref, lam_ref,
                  s0f, s0b, hf_ref, hb_ref, ff_out, fb_out, hc_f, hc_b):
    j = pl.program_id(0)
    dirs = ((False, xf, xf_p, xf_n, s0f, hf_ref, ff_out, hc_f),
            (True, xb, xb_p, xb_n, s0b, hb_ref, fb_out, hc_b))
    for rev, xr, xp, xn, s0, h_ref, f_out, hc in dirs:
        i = (NSEG - 1 - j) if rev else j
        starts = _seg_is_last(i) if rev else _seg_is_first(i)

        @pl.when(jnp.logical_and(starts, i < N_CTX_SEG))
        def _():
            hc[...] = jnp.zeros_like(hc)

        @pl.when(jnp.logical_and(starts, i >= N_CTX_SEG))
        def _():
            hc[...] = s0[0, 0]

    outs = []
    for d, (rev, xr, xp, xn, s0, h_ref, f_out, hc) in enumerate(dirs):
        i = (NSEG - 1 - j) if rev else j
        h, h_out = _rglru_dir(xr[...], xp[...], xn[...], _seg_is_first(i), _seg_is_last(i), cw_ref[...],
                              cb_ref[...], wa_ref[d], ba_ref[d], wx_ref[d], bx_ref[d], lam_ref[d], hc[...], rev)
        h_ref[...] = h.astype(BF16)
        outs.append(h_out)
    for d, (rev, xr, xp, xn, s0, h_ref, f_out, hc) in enumerate(dirs):
        hc[...] = outs[d]
    for d, (rev, xr, xp, xn, s0, h_ref, f_out, hc) in enumerate(dirs):
        i = (NSEG - 1 - j) if rev else j

        @pl.when(i < N_CTX_SEG)
        def _():
            f_out[0] = outs[d]


def _rglru(proj, conv_w, conv_b, wa_bd, ba, wx_bd, bx, lam, s0):
    xcol = 2 * D_A // D_B
    fwd = lambda f: (lambda j: f(j))
    bwd = lambda f: (lambda j: f(NSEG - 1 - j))
    seg_blk = lambda m: pl.BlockSpec((SEG, D_B), m(lambda i: (i, xcol)))
    prev_blk = lambda m: pl.BlockSpec((8, D_B), m(lambda i: (_prev8(i), xcol)))
    next_blk = lambda m: pl.BlockSpec((8, D_B), m(lambda i: (_next8(i), xcol)))
    full = lambda a: pl.BlockSpec(a.shape, lambda j: (0,) * a.ndim)
    return pl.pallas_call(
        _rglru_kernel,
        grid=(NSEG,),
        in_specs=[seg_blk(fwd), prev_blk(fwd), next_blk(fwd), seg_blk(bwd), prev_blk(bwd), next_blk(bwd),
                  full(conv_w), full(conv_b), full(wa_bd), full(ba), full(wx_bd), full(bx), full(lam),
                  pl.BlockSpec((1, 1, 1, D_B), lambda j: (_sample_of(j), 0, 0, 0)),
                  pl.BlockSpec((1, 1, 1, D_B), lambda j: (_sample_of(NSEG - 1 - j), 1, 0, 0))],
        out_specs=[pl.BlockSpec((SEG, D_B), lambda j: (j, 0)),
                   pl.BlockSpec((SEG, D_B), lambda j: (NSEG - 1 - j, 0)),
                   pl.BlockSpec((1, 1, D_B), lambda j: (jnp.minimum(j, N_CTX_SEG - 1), 0, 0)),
                   pl.BlockSpec((1, 1, D_B), lambda j: (jnp.minimum(NSEG - 1 - j, N_CTX_SEG - 1), 0, 0))],
        out_shape=[jax.ShapeDtypeStruct((M_TOK, D_B), BF16), jax.ShapeDtypeStruct((M_TOK, D_B), BF16),
                   jax.ShapeDtypeStruct((N_CTX_SEG, 1, D_B), F32), jax.ShapeDtypeStruct((N_CTX_SEG, 1, D_B), F32)],
        scratch_shapes=[pltpu.VMEM((1, D_B), F32), pltpu.VMEM((1, D_B), F32)],
        compiler_params=_cp(("arbitrary",)),
        name="rglru",
    )(proj, proj, proj, proj, proj, proj, conv_w, conv_b, wa_bd, ba, wx_bd, bx, lam, s0, s0)


def _ab_out_kernel(xp_ref, xs_ref, pos_ref, of_ref, ob_ref, og_ref, hf_ref, hb_ref, yr_ref, mod_ref, hg_ref,
                   w_ref, o_ref):
    m = mod_ref[0]
    f32 = lambda ref: ref[...].astype(F32)
    oa = f32(of_ref) + f32(ob_ref)
    hg = hg_ref[...]
    parts = []
    for h in range(A_HEADS):
        z = oa[:, h * A_DK:(h + 1) * A_DK]
        parts.append(z * lax.rsqrt(jnp.mean(z * z, axis=-1, keepdims=True) + RMS_EPS) * hg)
    o_a = jnp.concatenate(parts, axis=-1) * _silu(f32(og_ref))
    o_b = (f32(hf_ref) + f32(hb_ref)) * _gelu_tanh(f32(yr_ref))
    y = _dot(o_a, w_ref[0:D_A]) + _dot(o_b, w_ref[D_A:D_A + D_B])
    o_ref[...] = _x0(pl.program_id(0), xp_ref, xs_ref, pos_ref) + m[2:3] * y


def _ab_out(xp, xs, pos, proj_h, o_f, o_b, h_f, h_b, mods, hg, w_out):
    seg = lambda width, col: pl.BlockSpec((SEG, width), lambda i: (i, col))
    return pl.pallas_call(
        _ab_out_kernel,
        grid=(NSEG,),
        in_specs=_x0_specs() + [seg(D_A, 0), seg(D_A, 0), seg(D_A, 2), seg(D_B, 0), seg(D_B, 0), seg(D_B, 3),
                                pl.BlockSpec((1, 6, D), lambda i: (i, 0, 0)),
                                pl.BlockSpec((1, A_DK), lambda i: (0, 0)),
                                pl.BlockSpec((D_A + D_B, D), lambda i: (0, 0))],
        out_specs=seg(D, 0),
        out_shape=jax.ShapeDtypeStruct((M_TOK, D), F32),
        compiler_params=_cp(("arbitrary",)),
        name="ab_out",
    )(xp, xs, pos, o_f, o_b, proj_h, h_f, h_b, proj_h, mods, hg.reshape(1, A_DK), w_out)


def _route(scores, sel):
    cols = [sel[e:e + 1, :] for e in range(N_EXPERTS)]

    def rank(vals):
        out = []
        for i, vi in enumerate(vals):
            r = None
            for jx, vj in enumerate(vals):
                if jx == i:
                    continue
                beats = (vj >= vi) if jx < i else (vj > vi)
                r = beats.astype(F32) if r is None else r + beats.astype(F32)
            out.append(r)
        return out

    grp_scores, in_top2 = [], []
    for gi in range(N_GROUPS):
        vals = cols[gi * GROUP:(gi + 1) * GROUP]
        best_pair = None
        for a in range(GROUP):
            for bx in range(a + 1, GROUP):
                s = vals[a] + vals[bx]
                best_pair = s if best_pair is None else jnp.maximum(best_pair, s)
        grp_scores.append(best_pair)
        in_top2.extend([r < 2.0 for r in rank(vals)])
    grp_best = [r < 1.0 for r in rank(grp_scores)]
    picked = [jnp.where(jnp.logical_and(grp_best[e // GROUP], in_top2[e]), scores[e:e + 1, :], 0.0)
              for e in range(N_EXPERTS)]
    total = picked[0]
    for pe in picked[1:]:
        total = total + pe
    row = lax.broadcasted_iota(jnp.int32, scores.shape, 0)
    comb = jnp.zeros(scores.shape, F32)
    for e in range(N_EXPERTS):
        comb = jnp.where(row == e, picked[e] / total, comb)
    return comb


def _moe_kernel(final_norm, x_ref, mod_ref, g_ref, rw_ref, rb_ref, w1_ref, w3_ref, w2_ref, gf_ref, *rest):
    o_refs, (h_sc, comb_sc, acc_sc) = rest[:-3], rest[-3:]
    e = pl.program_id(1)

    @pl.when(e == 0)
    def _():
        for s in range(MOE_TM // SEG):
            m = mod_ref[s]
            rows = slice(s * SEG, (s + 1) * SEG)
            h = _rms_mod(x_ref[rows, :], g_ref[...], m[4:5], m[3:4])
            h_sc[rows, :] = h.astype(BF16)
            scores = _sigmoid(_dot_x3_nt(rw_ref[...], h))
            comb_t = _route(scores, scores + rb_ref[...])
            comb_t = jnp.concatenate([comb_t, jnp.zeros((COMB_LANES - N_EXPERTS, SEG), F32)], axis=0)
            comb_sc[rows, :] = comb_t.T
        acc_sc[...] = jnp.zeros_like(acc_sc)

    lane = lax.broadcasted_iota(jnp.int32, (MOE_TM, COMB_LANES), 1)
    comb = jnp.sum(jnp.where(lane == e, comb_sc[...], 0.0), axis=-1, keepdims=True)
    h = h_sc[...]
    u1 = jnp.dot(h, w1_ref[0, 0].astype(BF16), preferred_element_type=F32)
    u3 = jnp.dot(h, w3_ref[0, 0].astype(BF16), preferred_element_type=F32)
    hid = _silu(u1) * u3 * comb
    acc_sc[...] += jnp.dot(hid.astype(BF16), w2_ref[0, 0].astype(BF16), preferred_element_type=F32)

    def emit(dst_ref):
        for s in range(MOE_TM // SEG):
            rows = slice(s * SEG, (s + 1) * SEG)
            y = x_ref[rows, :] + mod_ref[s][5:6] * acc_sc[rows, :]
            if final_norm:
                y = y * lax.rsqrt(jnp.mean(y * y, axis=-1, keepdims=True) + RMS_EPS) * gf_ref[...]
            dst_ref[rows, :] = y

    last = e == N_EXPERTS - 1
    if final_norm:
        is_ctx = pl.program_id(0) < MOE_CTX_TILES
        pl.when(jnp.logical_and(last, is_ctx))(lambda: emit(o_refs[0]))
        pl.when(jnp.logical_and(last, jnp.logical_not(is_ctx)))(lambda: emit(o_refs[1]))
    else:
        pl.when(last)(lambda: emit(o_refs[0]))


def _moe(x, mods, g, router_w, router_b, w1, w3, w2, layer, gf, final_norm):
    spt = MOE_TM // SEG
    tile = lambda f: pl.BlockSpec((MOE_TM, D), lambda t, e: (f(t), 0))
    if final_norm:
        n_half = M_TOK // 2
        out_specs = [tile(lambda t: jnp.minimum(t, MOE_CTX_TILES - 1)),
                     tile(lambda t: jnp.maximum(t - MOE_CTX_TILES, 0))]
        out_shape = [jax.ShapeDtypeStruct((n_half, D), F32), jax.ShapeDtypeStruct((M_TOK - n_half, D), F32)]
    else:
        out_specs = [tile(lambda t: t)]
        out_shape = [jax.ShapeDtypeStruct((M_TOK, D), F32)]
    return pl.pallas_call(
        functools.partial(_moe_kernel, final_norm),
        grid=(M_TOK // MOE_TM, N_EXPERTS),
        in_specs=[pl.BlockSpec((MOE_TM, D), lambda t, e: (t, 0)),
                  pl.BlockSpec((spt, 6, D), lambda t, e: (t, 0, 0)),
                  pl.BlockSpec((1, D), lambda t, e: (0, 0)),
                  pl.BlockSpec((N_EXPERTS, D), lambda t, e: (0, 0)),
                  pl.BlockSpec((N_EXPERTS, 1), lambda t, e: (0, 0)),
                  pl.BlockSpec((1, 1, D, D_EXPERT), lambda t, e: (layer, e, 0, 0)),
                  pl.BlockSpec((1, 1, D, D_EXPERT), lambda t, e: (layer, e, 0, 0)),
                  pl.BlockSpec((1, 1, D_EXPERT, D), lambda t, e: (layer, e, 0, 0)),
                  pl.BlockSpec((1, D), lambda t, e: (0, 0))],
        out_specs=out_specs,
        out_shape=out_shape,
        scratch_shapes=[pltpu.VMEM((MOE_TM, D), BF16), pltpu.VMEM((MOE_TM, COMB_LANES), F32),
                        pltpu.VMEM((MOE_TM, D), F32)],
        compiler_params=_cp(("arbitrary", "arbitrary")),
        name="moe",
    )(x, mods, g.reshape(1, D), router_w.T, router_b.reshape(N_EXPERTS, 1), w1, w3, w2, gf.reshape(1, D))


def _rw_in_kernel(x_ref, xp_ref, xn_ref, mod_ref, g_ref, mu_ref, wr_ref, wk_ref, wv_ref, g1_ref, g2_ref,
                  w1_ref, w2_ref, w0_ref, a1_ref, a2_ref, a0_ref, ka_ref, rk_ref, hsel_ref,
                  r_ref, k_ref, v_ref, gg_ref, lw_ref, a_ref, coef_ref):
    i = pl.program_id(0)
    m = mod_ref[0]
    g = g_ref[...]
    h = _rms_mod(x_ref[...], g, m[1:2], m[0:1])
    hp = jnp.where(_seg_is_first(i), 0.0, _rms_mod(xp_ref[...], g, m[1:2], m[0:1]))
    hn = jnp.where(_seg_is_last(i), 0.0, _rms_mod(xn_ref[...], g, m[1:2], m[0:1]))
    ext = jnp.concatenate([hp, h, hn], axis=0)
    n = ext.shape[0]
    h_prev = pltpu.roll(ext, 1, 0)[8:8 + SEG]
    h_next = pltpu.roll(ext, n - 1, 0)[8:8 + SEG]
    xx = 0.5 * (h_prev + h_next) - h
    mu = mu_ref[...]
    xr, xw, xk, xv, xa, xg = [h + xx * mu[c:c + 1] for c in range(6)]
    r = _dot(xr, wr_ref[...])
    k = _dot(xk, wk_ref[...])
    r_ref[...] = r.astype(BF16)
    k_ref[...] = k.astype(BF16)
    v_ref[...] = _dot(xv, wv_ref[...]).astype(BF16)
    gg_ref[...] = _dot(_sigmoid(_dot(xg, g1_ref[...])), g2_ref[...]).astype(BF16)
    w_in = w0_ref[...] + _dot(jnp.tanh(_dot(xw, w1_ref[...])), w2_ref[...])
    lw_ref[...] = -W_DECAY_SCALE * _sigmoid(w_in)
    a = _sigmoid(a0_ref[...] + _dot(_dot(xa, a1_ref[...]), a2_ref[...]))
    a_ref[...] = a
    kd_sum = k * (2.0 + (a[:, 0:D] + a[:, D:2 * D] - 2.0) * ka_ref[...])
    coef_ref[...] = _dot_sel_rhs(r * kd_sum * rk_ref[...], hsel_ref[...])


def _rw_inproj(x, mods, g, mu, wr, wk, wv, g1, g2, w1c, w2bd, w0c, a1c, a2bd, a0c, ka, rk):
    full = lambda a: pl.BlockSpec(a.shape, lambda i: (0,) * a.ndim)
    seg = lambda width: pl.BlockSpec((SEG, width), lambda i: (i, 0))
    outs = ([jax.ShapeDtypeStruct((M_TOK, D), BF16)] * 4 + [jax.ShapeDtypeStruct((M_TOK, 2 * D), F32)] * 2
            + [jax.ShapeDtypeStruct((M_TOK, COEF_LANES), F32)])
    hsel = jnp.asarray(np.arange(D)[:, None] // C_HEAD == np.arange(COEF_LANES)[None, :], BF16)
    return pl.pallas_call(
        _rw_in_kernel,
        grid=(NSEG,),
        in_specs=[seg(D),
                  pl.BlockSpec((8, D), lambda i: (_prev8(i), 0)),
                  pl.BlockSpec((8, D), lambda i: (_next8(i), 0)),
                  pl.BlockSpec((1, 6, D), lambda i: (i, 0, 0)),
                  full(g), full(mu), full(wr), full(wk), full(wv), full(g1), full(g2),
                  full(w1c), full(w2bd), full(w0c), full(a1c), full(a2bd), full(a0c),
                  full(ka), full(rk), full(hsel)],
        out_specs=[seg(D)] * 4 + [seg(2 * D)] * 2 + [seg(COEF_LANES)],
        out_shape=outs,
        compiler_params=_cp(("arbitrary",)),
        name="rwkv_inproj",
    )(x, x, x, mods, g, mu, wr, wk, wv, g1, g2, w1c, w2bd, w0c, a1c, a2bd, a0c, ka, rk, hsel)


def _rw_consts():
    t = np.arange(SEG)
    same = (t[:, None] // RW_CHUNK) == (t[None, :] // RW_CHUNK)
    tri_f = np.logical_and(same, t[None, :] <= t[:, None]).astype(np.float32)
    tri_b = np.logical_and(same, t[None, :] >= t[:, None]).astype(np.float32)
    return [jnp.asarray(a, BF16) for a in (tri_f, tri_b)]


def _rw_dir(r, k, v, lw, a, kk_g, ka_g, tri, rev):
    c = RW_CHUNK
    lane = lax.broadcasted_iota(jnp.int32, (SEG, 2 * C_HEAD), 1)
    head0 = lane < C_HEAD
    kx = k * kk_g
    ss = kx * kx
    n0 = jnp.sum(jnp.where(head0, ss, 0.0), axis=-1, keepdims=True)
    n1 = jnp.sum(jnp.where(head0, 0.0, ss), axis=-1, keepdims=True)
    kk = kx / jnp.maximum(jnp.sqrt(jnp.where(head0, n0, n1)), 1e-12)
    kd = k * (1.0 + (a - 1.0) * ka_g)
    bhat = kk * a
    cum = _dot_sel(tri, lw)
    e_incl = jnp.exp(cum)
    e_inv = jnp.exp(-cum)
    ae = -kk * jnp.exp(cum - lw)
    re = r * e_incl
    bi = bhat * e_inv
    ki = kd * e_inv

    chunks = []
    for ci in range(SEG // c):
        sl = slice(ci * c, (ci + 1) * c)
        ctot = cum[ci * c:ci * c + 1] if rev else cum[(ci + 1) * c - 1:(ci + 1) * c]
        dec = jnp.exp(ctot - cum[sl])
        chunks.append(dict(ae=ae[sl], re=re[sl], bi=bi[sl], ki=ki[sl], v=v[sl], bdec=bhat[sl] * dec,
                           kdec=kd[sl] * dec, gam=jnp.exp(ctot), rev=rev))
    return chunks


def _rw_transitions(chunks):
    c = RW_CHUNK
    w2 = 2 * C_HEAD
    h0c = lax.broadcasted_iota(jnp.int32, (c, w2), 1) < C_HEAD
    rowc = lax.broadcasted_iota(jnp.int32, (c, w2), 0)
    colc = jnp.bitwise_and(lax.broadcasted_iota(jnp.int32, (c, w2), 1), C_HEAD - 1)
    eye = (colc == rowc).astype(F32)
    bdmask = (lax.broadcasted_iota(jnp.int32, (w2, w2), 0) < C_HEAD) == (
        lax.broadcasted_iota(jnp.int32, (w2, w2), 1) < C_HEAD)
    keep0 = jnp.where(h0c, 1.0, 0.0).astype(BF16)
    keep1 = jnp.where(h0c, 0.0, 1.0).astype(BF16)

    def bd16(yb):
        return jnp.concatenate([yb * keep0, yb * keep1], axis=0)

    def dot_bd(x, pairs):
        blocks = [bd16(y.astype(BF16)) for y in pairs]
        rhs = blocks[0] if len(blocks) == 1 else jnp.concatenate(blocks, axis=1)
        return jnp.dot(x.astype(BF16), rhs, preferred_element_type=F32)

    n_ab, a_ak, a_rb, a_rk = [], [], [], []
    for ch in chunks:
        strict = (colc > rowc) if ch["rev"] else (colc < rowc)
        incl = (colc >= rowc) if ch["rev"] else (colc <= rowc)
        left = jnp.concatenate([ch["ae"], ch["re"]], axis=0)
        right = jnp.concatenate([jnp.where(h0c, ch["bi"], 0.0), jnp.where(h0c, 0.0, ch["bi"]),
                                 jnp.where(h0c, ch["ki"], 0.0), jnp.where(h0c, 0.0, ch["ki"])], axis=0)
        gm = _dot_nt(left, right)
        n_ab.append(jnp.where(strict, gm[0:c, 0:2 * c], 0.0))
        a_ak.append(jnp.where(strict, gm[0:c, 2 * c:4 * c], 0.0))
        a_rb.append(jnp.where(incl, gm[c:2 * c, 0:2 * c], 0.0))
        a_rk.append(jnp.where(incl, gm[c:2 * c, 2 * c:4 * c], 0.0))
    tm = [eye + n for n in n_ab]
    p = [_x3_shared([x], [x], bd16)[0] for x in n_ab]
    for it in range(4):
        if it < 2:
            res = [_x3_shared([x, t], [x], bd16) for x, t in zip(p, tm)]
            p = [r[0] for r in res]
            tm = [t + r[1] for t, r in zip(tm, res)]
        else:
            res = [dot_bd(jnp.concatenate([x, t], axis=0), [x]) for x, t in zip(p, tm)]
            p = [r[0:c] for r in res]
            tm = [t + r[c:2 * c] for t, r in zip(tm, res)]
    tm = [t + dot_bd(t, [x]) for t, x in zip(tm, p)]
    akv = [dot_bd(jnp.concatenate([x, y], axis=0), [ch["v"]])
           for x, y, ch in zip(a_ak, a_rk, chunks)]
    tav = [_x3_shared([t], [ch["ae"], x[0:c]], bd16)[0]
           for t, ch, x in zip(tm, chunks, akv)]
    out = []
    for i, ch in enumerate(chunks):
        ta, tv = tav[i][:, 0:w2], tav[i][:, w2:2 * w2]
        qy = dot_bd(a_rb[i], [ta, tv])
        q = ch["re"] + qy[:, 0:w2]
        y = qy[:, w2:2 * w2] + akv[i][c:2 * c]
        wz = _dot_tn(tav[i], ch["bdec"])
        w = jnp.where(bdmask, wz[0:w2], 0.0)
        z = jnp.where(bdmask, wz[w2:2 * w2] + _dot_tn(ch["v"], ch["kdec"]), 0.0)
        out.append((q, y, w, z, ch["gam"]))
    return out


def _rw_scan_kernel(rf, kf, vf, lwf, af, rb, kb, vb, lwb, ab, kkg_ref, kag_ref, s0f, s0b, trif, trib,
                    of_ref, ob_ref, sf_out, sb_out, st_f, st_b):
    j = pl.program_id(1)
    w2 = 2 * C_HEAD
    dirs = ((False, (rf, kf, vf, lwf, af), s0f, trif, of_ref, sf_out, st_f),
            (True, (rb, kb, vb, lwb, ab), s0b, trib, ob_ref, sb_out, st_b))
    for rev, refs, s0, tri, o_ref, s_out, st in dirs:
        i = (NSEG - 1 - j) if rev else j
        starts = _seg_is_last(i) if rev else _seg_is_first(i)

        @pl.when(jnp.logical_and(starts, i < N_CTX_SEG))
        def _():
            st[...] = jnp.zeros_like(st)

        @pl.when(jnp.logical_and(starts, i >= N_CTX_SEG))
        def _():
            st[...] = s0[0, 0]

    chunks = []
    for rev, refs, s0, tri, o_ref, s_out, st in dirs:
        for pp in range(RW_PAIRS_STEP):
            lanes = slice(pp * w2, (pp + 1) * w2)
            r_, k_, v_, lw_, a_ = [z[:, lanes].astype(F32) for z in refs]
            chunks.extend(_rw_dir(r_, k_, v_, lw_, a_, kkg_ref[:, lanes], kag_ref[:, lanes], tri[...], rev))
    trans = _rw_transitions(chunks)
    n_c = SEG // RW_CHUNK
    states = [[st[pp] for pp in range(RW_PAIRS_STEP)] for (_, _, _, _, _, _, st) in dirs]
    for step in range(n_c):
        for d, (rev, refs, s0, tri, o_ref, s_out, st) in enumerate(dirs):
            ci = (n_c - 1 - step) if rev else step
            for pp in range(RW_PAIRS_STEP):
                q, y, w, z, gam = trans[(d * RW_PAIRS_STEP + pp) * n_c + ci]
                s = states[d][pp]
                o_ref[ci * RW_CHUNK:(ci + 1) * RW_CHUNK, pp * w2:(pp + 1) * w2] = (_dot_nt(q, s) + y).astype(BF16)
                states[d][pp] = s * gam + _dot(s, w) + z
    for d, (rev, refs, s0, tri, o_ref, s_out, st) in enumerate(dirs):
        for pp in range(RW_PAIRS_STEP):
            st[pp] = states[d][pp]
    for d, (rev, refs, s0, tri, o_ref, s_out, st) in enumerate(dirs):
        i = (NSEG - 1 - j) if rev else j

        @pl.when(i < N_CTX_SEG)
        def _():
            for pp in range(RW_PAIRS_STEP):
                s = states[d][pp]
                s_out[0, 2 * pp] = s[0:C_HEAD, 0:C_HEAD]
                s_out[0, 2 * pp + 1] = s[C_HEAD:w2, C_HEAD:w2]


def _rw_scan(r, k, v, lw, a, kk_g, ka_g, s0bd):
    consts = _rw_consts()
    w = 2 * C_HEAD
    pps = RW_PAIRS_STEP
    wb = pps * w
    n_steps = C_PAIRS // pps
    blk = lambda col0, rev: pl.BlockSpec(
        (SEG, wb), (lambda p, j: (NSEG - 1 - j, col0 + p)) if rev else (lambda p, j: (j, col0 + p)))
    cspec = lambda arr: pl.BlockSpec(arr.shape, lambda p, j: (0, 0))
    hps = 2 * pps
    fin_f = pl.BlockSpec((1, hps, C_HEAD, C_HEAD), lambda p, j: (jnp.minimum(j, N_CTX_SEG - 1), p, 0, 0))
    fin_b = pl.BlockSpec((1, hps, C_HEAD, C_HEAD),
                         lambda p, j: (jnp.minimum(NSEG - 1 - j, N_CTX_SEG - 1), p, 0, 0))
    return pl.pallas_call(
        _rw_scan_kernel,
        grid=(n_steps, NSEG),
        in_specs=[blk(0, False), blk(0, False), blk(0, False), blk(0, False), blk(0, False),
                  blk(0, True), blk(0, True), blk(0, True), blk(n_steps, True), blk(n_steps, True),
                  pl.BlockSpec((1, wb), lambda p, j: (0, p)), pl.BlockSpec((1, wb), lambda p, j: (0, p)),
                  pl.BlockSpec((1, 1, pps, w, w), lambda p, j: (_sample_of(j), 0, p, 0, 0)),
                  pl.BlockSpec((1, 1, pps, w, w), lambda p, j: (_sample_of(NSEG - 1 - j), 1, p, 0, 0)),
                  cspec(consts[0]), cspec(consts[1])],
        out_specs=[blk(0, False), blk(0, True), fin_f, fin_b],
        out_shape=[jax.ShapeDtypeStruct((M_TOK, D), BF16), jax.ShapeDtypeStruct((M_TOK, D), BF16),
                   jax.ShapeDtypeStruct((N_CTX_SEG, 2 * C_PAIRS, C_HEAD, C_HEAD), F32),
                   jax.ShapeDtypeStruct((N_CTX_SEG, 2 * C_PAIRS, C_HEAD, C_HEAD), F32)],
        scratch_shapes=[pltpu.VMEM((pps, w, w), F32), pltpu.VMEM((pps, w, w), F32)],
        compiler_params=_cp(("arbitrary", "arbitrary")),
        name="rwkv7_scan",
    )(r, k, v, lw, a, r, k, v, lw, a, kk_g, ka_g, s0bd, s0bd, *consts)


def _rw_out_kernel(x_ref, of_ref, ob_ref, v_ref, gg_ref, coef_ref, mod_ref, lnw_ref, lnb_ref, ones_ref,
                   hexp_ref, wo_ref, o_ref):
    m = mod_ref[0]
    ones_bd = ones_ref[...]
    w = 2 * C_HEAD
    inv_n = 1.0 / C_HEAD
    coef = _dot_sel_rhs(coef_ref[...], hexp_ref[...])
    parts = []
    for p in range(C_PAIRS):
        cs = slice(p * w, (p + 1) * w)
        osum = of_ref[:, cs].astype(F32) + ob_ref[:, cs].astype(F32)
        mu = _dot_sel_rhs(osum, ones_bd) * inv_n
        cen = osum - mu
        var = _dot_sel_rhs(cen * cen, ones_bd) * inv_n
        o = cen * lax.rsqrt(var + GN_EPS) * lnw_ref[:, cs] + lnb_ref[:, cs]
        bonus = coef[:, cs] * v_ref[:, cs].astype(F32)
        parts.append((o + bonus) * gg_ref[:, cs].astype(F32))
    y = _dot(jnp.concatenate(parts, axis=-1), wo_ref[...])
    o_ref[...] = x_ref[...] + m[2:3] * y


def _dot_sel_rhs(x, mat):
    h, m, l = _split3(x)
    return (jnp.dot(h, mat, preferred_element_type=F32) + jnp.dot(m, mat, preferred_element_type=F32)
            + jnp.dot(l, mat, preferred_element_type=F32))


def _rw_out(x, o_f, o_b, v, gg, coef, mods, lnw, lnb, wo):
    seg = lambda width: pl.BlockSpec((SEG, width), lambda i: (i, 0))
    row = pl.BlockSpec((1, D), lambda i: (0, 0))
    hh = np.arange(2 * C_HEAD) // C_HEAD
    ones_bd = jnp.asarray((hh[:, None] == hh[None, :]).astype(np.float32), BF16)
    hexp = jnp.asarray(np.arange(COEF_LANES)[:, None] == np.arange(D)[None, :] // C_HEAD, BF16)
    return pl.pallas_call(
        _rw_out_kernel,
        grid=(NSEG,),
        in_specs=[seg(D), seg(D), seg(D), seg(D), seg(D), seg(COEF_LANES),
                  pl.BlockSpec((1, 6, D), lambda i: (i, 0, 0)),
                  row, row,
                  pl.BlockSpec((2 * C_HEAD, 2 * C_HEAD), lambda i: (0, 0)),
                  pl.BlockSpec((COEF_LANES, D), lambda i: (0, 0)),
                  pl.BlockSpec((D, D), lambda i: (0, 0))],
        out_specs=seg(D),
        out_shape=jax.ShapeDtypeStruct((M_TOK, D), F32),
        compiler_params=_cp(("arbitrary",)),
        name="rwkv_out",
    )(x, o_f, o_b, v, gg, coef, mods, lnw.reshape(1, D), lnb.reshape(1, D), ones_bd, hexp, wo)


def _grid_pos_table(n_tok):
    rows = n_tok // GRID_W
    r, cl = np.meshgrid(np.arange(rows, dtype=np.float32), np.arange(GRID_W, dtype=np.float32), indexing='ij')
    quarter = D // 4
    omega = (1.0 / (np.float32(POS_BASE) ** (np.arange(quarter, dtype=np.float32) / np.float32(quarter))))
    ang_r = (r.reshape(-1, 1) * omega).astype(np.float32)
    ang_c = (cl.reshape(-1, 1) * omega).astype(np.float32)
    table = np.concatenate([np.sin(ang_r), np.cos(ang_r), np.sin(ang_c), np.cos(ang_c)], axis=-1)
    return jnp.asarray(table.astype(np.float32))


def _block_diag(blocks):
    g, n, _ = blocks.shape
    eye = jnp.eye(g, dtype=blocks.dtype)
    return (eye[:, None, :, None] * blocks[:, :, None, :]).reshape(g * n, g * n)


def kernel(x_prompt, x_sample, state_hgrn, state_rglru, state_rwkv, c, c_ctx, norm_mix_g, norm_ffn_g, w_mod, b_mod, ab_w_in, ab_w_out, hgrn_lb, hgrn_norm_g, rg_conv_w, rg_conv_b, rg_wa, rg_ba, rg_wx, rg_bx, rg_lambda, rw_mu, rw_wr, rw_wk, rw_wv, rw_wo, rw_w0, rw_w1, rw_w2, rw_a0, rw_a1, rw_a2, rw_g1, rw_g2, rw_kk, rw_ka, rw_rk, rw_lnw, rw_lnb, moe_router, moe_router_bias, moe_w1, moe_w3, moe_w2, norm_f_g):
    bf = lambda z: z.astype(BF16)
    xp = x_prompt.reshape(-1, D)
    xs = x_sample.reshape(-1, D)
    pos = _grid_pos_table(x_sample.shape[1])
    mods = _modulations(c, c_ctx, w_mod, b_mod)

    lower_bounds = jnp.cumsum(jax.nn.softmax(hgrn_lb.astype(F32), axis=1), axis=1)
    lb = lower_bounds[:, 0].reshape(2, A_HEADS, 1, A_DK)
    proj_f, proj_h = _ab_inproj(xp, xs, pos, mods[0], norm_mix_g[0], bf(ab_w_in[0]))
    s0t = jnp.swapaxes(state_hgrn[:, 0], -1, -2)
    o_f, o_b, new_hgrn = _gla(proj_f, proj_h, lb, s0t)
    wa_bd = bf(jnp.stack([_block_diag(rg_wa[0, d]) for d in range(2)]))
    wx_bd = bf(jnp.stack([_block_diag(rg_wx[0, d]) for d in range(2)]))
    h_f, h_b, lru_f, lru_b = _rglru(
        proj_f, rg_conv_w[0], rg_conv_b[0].reshape(1, D_B), wa_bd, rg_ba[0].reshape(2, 1, D_B), wx_bd,
        rg_bx[0].reshape(2, 1, D_B), rg_lambda[0].reshape(2, 1, D_B), state_rglru[:, 0].reshape(-1, 2, 1, D_B))
    x = _ab_out(xp, xs, pos, proj_h, o_f, o_b, h_f, h_b, mods[0], hgrn_norm_g[0], bf(ab_w_out[0]))
    x, = _moe(x, mods[0], norm_ffn_g[0], moe_router, moe_router_bias, moe_w1, moe_w3, moe_w2, 0, norm_f_g, False)

    w1c = bf(jnp.concatenate([rw_w1[0, 0], rw_w1[0, 1]], axis=-1))
    a1c = bf(jnp.concatenate([rw_a1[0, 0], rw_a1[0, 1]], axis=-1))
    w2bd = bf(jnp.concatenate([jnp.concatenate([rw_w2[0, 0], jnp.zeros_like(rw_w2[0, 0])], axis=-1),
                               jnp.concatenate([jnp.zeros_like(rw_w2[0, 1]), rw_w2[0, 1]], axis=-1)], axis=0))
    a2bd = bf(jnp.concatenate([jnp.concatenate([rw_a2[0, 0], jnp.zeros_like(rw_a2[0, 0])], axis=-1),
                               jnp.concatenate([jnp.zeros_like(rw_a2[0, 1]), rw_a2[0, 1]], axis=-1)], axis=0))
    r, k, v, gg, lw, a, coef = _rw_inproj(
        x, mods[1], norm_mix_g[1].reshape(1, D), rw_mu[0], bf(rw_wr[0]), bf(rw_wk[0]), bf(rw_wv[0]),
        bf(rw_g1[0]), bf(rw_g2[0]), w1c, w2bd, rw_w0[0].reshape(1, 2 * D), a1c, a2bd, rw_a0[0].reshape(1, 2 * D),
        rw_ka[0].reshape(1, D), rw_rk[0].reshape(1, D))
    s0 = state_rwkv[:, 0].reshape(N_SAMPLE, 2, C_PAIRS, 2, C_HEAD, C_HEAD)
    zeros = jnp.zeros_like(s0[:, :, :, 0])
    s0bd = jnp.concatenate([jnp.concatenate([s0[:, :, :, 0], zeros], axis=-1),
                            jnp.concatenate([zeros, s0[:, :, :, 1]], axis=-1)], axis=-2)
    ow_f, ow_b, rs_f, rs_b = _rw_scan(r, k, v, lw, a, rw_kk[0].reshape(1, D), rw_ka[0].reshape(1, D), s0bd)
    x = _rw_out(x, ow_f, ow_b, v, gg, coef, mods[1], rw_lnw[0], rw_lnb[0], bf(rw_wo[0]))
    y_p, y_s = _moe(x, mods[1], norm_ffn_g[1], moe_router, moe_router_bias, moe_w1, moe_w3, moe_w2, 1, norm_f_g, True)

    new_rglru = jnp.stack([lru_f[:, 0], lru_b[:, 0]], axis=1)[:, None]
    new_rwkv = jnp.stack([rs_f, rs_b], axis=1)[:, None]
    return (y_p.reshape(x_prompt.shape), y_s.reshape(x_sample.shape), new_hgrn, new_rglru, new_rwkv)
```

```python
import functools
import math

import numpy as np
import jax
import jax.numpy as jnp
from jax import lax
from jax.experimental import pallas as pl
from jax.experimental.pallas import tpu as pltpu

F32 = jnp.float32
BF16 = jnp.bfloat16

D = 1024
SEG = 256
N_CTX_SEG = 16
SEG_PER_SAMPLE = 4
N_SAMPLE = 4
NSEG = N_CTX_SEG + N_SAMPLE * SEG_PER_SAMPLE
M_TOK = NSEG * SEG
SUBLANES = 8
ROWS8_PER_SEG = SEG // SUBLANES

A_HEADS = 4
A_DK = 128
D_A = 512
D_B = 512
B_BLOCKS = 8
B_BLOCK = 64
LRU_C = 8.0
D_IN_AB = 5 * D_A + 2 * D_B
AB_F32_COLS = 2 * D_A + D_B
C_HEAD = 64
C_PAIRS = 8
RW_CHUNK = 64
COEF_LANES = 128
RW_PAIRS_STEP = 4
W_DECAY_SCALE = math.exp(-0.5)
N_EXPERTS = 16
N_GROUPS = 4
GROUP = 4
D_EXPERT = 256
RMS_EPS = 1e-6
GN_EPS = 64e-5
POS_BASE = 10000.0
GRID_W = 64
MOE_TM = 1024
MOE_CTX_TILES = N_CTX_SEG * SEG // MOE_TM
COMB_LANES = 128
GLA_LEVELS = (1, 2, 4, 8, 16, 32, 64, 128)
GLA_HALF = 128
GLA_HEADS_STEP = 4

VMEM_LIMIT = 56 * 1024 * 1024


def _cp(sem):
    return pltpu.CompilerParams(dimension_semantics=sem, vmem_limit_bytes=VMEM_LIMIT)


def _sigmoid(x):
    return 0.5 * jnp.tanh(0.5 * x) + 0.5


def _silu(x):
    return x * _sigmoid(x)


def _gelu_tanh(x):
    return 0.5 * x * (1.0 + jnp.tanh(math.sqrt(2.0 / math.pi) * (x + 0.044715 * (x * x * x))))


def _rms_mod(x, g, scale, shift):
    ms = jnp.mean(x * x, axis=-1, keepdims=True)
    return x * lax.rsqrt(ms + RMS_EPS) * (g * (1.0 + scale)) + shift


def _dot(a, b):
    return jnp.dot(a.astype(BF16), b.astype(BF16), preferred_element_type=F32)


def _dot_nt(a, b):
    return lax.dot_general(a.astype(BF16), b.astype(BF16), (((1,), (1,)), ((), ())),
                           preferred_element_type=F32)


def _dot_tn(a, b):
    return lax.dot_general(a.astype(BF16), b.astype(BF16), (((0,), (0,)), ((), ())),
                           preferred_element_type=F32)


def _split3(x):
    h = x.astype(BF16)
    r1 = x - h.astype(F32)
    m = r1.astype(BF16)
    r2 = r1 - m.astype(F32)
    return h, m, r2.astype(BF16)


def _dot_sel(mat, x):
    h, m, l = _split3(x)
    return (jnp.dot(mat, h, preferred_element_type=F32) + jnp.dot(mat, m, preferred_element_type=F32)
            + jnp.dot(mat, l, preferred_element_type=F32))


def _dot_x3_nt(a, b):
    dn = (((1,), (1,)), ((), ()))
    ah = a.astype(BF16)
    al = (a - ah.astype(F32)).astype(BF16)
    bh = b.astype(BF16)
    bl = (b - bh.astype(F32)).astype(BF16)
    return (lax.dot_general(ah, bh, dn, preferred_element_type=F32)
            + lax.dot_general(ah, bl, dn, preferred_element_type=F32)
            + lax.dot_general(al, bh, dn, preferred_element_type=F32))


def _seg_is_first(i):
    return jnp.logical_or(i < N_CTX_SEG, lax.rem(i - N_CTX_SEG, SEG_PER_SAMPLE) == 0)


def _seg_is_last(i):
    return jnp.logical_or(i < N_CTX_SEG, lax.rem(i - N_CTX_SEG, SEG_PER_SAMPLE) == SEG_PER_SAMPLE - 1)


def _sample_of(i):
    return jnp.maximum(i - N_CTX_SEG, 0) // SEG_PER_SAMPLE


def _prev8(i):
    return jnp.maximum(i * ROWS8_PER_SEG - 1, 0)


def _next8(i):
    return jnp.minimum((i + 1) * ROWS8_PER_SEG, M_TOK // 8 - 1)


def _mod_kernel(cv_ref, w_ref, b_ref, o_ref):
    cv = cv_ref[...]
    o_ref[0] = _dot(_silu(cv), w_ref[0]) + b_ref[0]


def _modulations(c, c_ctx, w_mod, b_mod):
    depth = w_mod.shape[0]
    cv = jnp.concatenate([c_ctx[None, :], c, jnp.zeros((3, D), F32)], axis=0)
    n_t = 6
    mod = pl.pallas_call(
        _mod_kernel,
        grid=(depth, n_t),
        in_specs=[pl.BlockSpec((8, D), lambda l, n: (0, 0)),
                  pl.BlockSpec((1, D, D), lambda l, n: (l, 0, n)),
                  pl.BlockSpec((1, 1, D), lambda l, n: (l, 0, n))],
        out_specs=pl.BlockSpec((1, 8, D), lambda l, n: (l, 0, n)),
        out_shape=jax.ShapeDtypeStruct((depth, 8, 6 * D), F32),
        compiler_params=_cp(("arbitrary", "arbitrary")),
        name="adaln_mod",
    )(cv, w_mod, b_mod.reshape(depth, 1, 6 * D))
    row_of_seg = np.array([0] * N_CTX_SEG + [1 + s // SEG_PER_SAMPLE for s in range(N_SAMPLE * SEG_PER_SAMPLE)])
    return mod[:, row_of_seg].reshape(depth, NSEG, 6, D)


def _x0_specs():
    return [pl.BlockSpec((SEG, D), lambda i: (jnp.minimum(i, N_CTX_SEG - 1), 0)),
            pl.BlockSpec((SEG, D), lambda i: (jnp.maximum(i - N_CTX_SEG, 0), 0)),
            pl.BlockSpec((SEG, D), lambda i: (lax.rem(jnp.maximum(i - N_CTX_SEG, 0), SEG_PER_SAMPLE), 0))]


def _x0(i, xp_ref, xs_ref, pos_ref):
    return jnp.where(i < N_CTX_SEG, xp_ref[...], xs_ref[...] + pos_ref[...])


def _ab_in_kernel(xp_ref, xs_ref, pos_ref, mod_ref, g_ref, w_ref, of_ref, oh_ref, w_sc):
    @pl.when(pl.program_id(0) == 0)
    def _():
        w_sc[...] = w_ref[0].astype(BF16)

    m = mod_ref[0]
    x = _x0(pl.program_id(0), xp_ref, xs_ref, pos_ref)
    h = _rms_mod(x, g_ref[...], m[1:2], m[0:1])
    res = jnp.dot(h.astype(BF16), w_sc[...], preferred_element_type=F32)
    of_ref[:, 0:2 * D_A] = res[:, D_A:3 * D_A]
    of_ref[:, 2 * D_A:AB_F32_COLS] = res[:, 5 * D_A:5 * D_A + D_B]
    oh_ref[:, 0:D_A] = res[:, 0:D_A].astype(BF16)
    oh_ref[:, D_A:3 * D_A] = res[:, 3 * D_A:5 * D_A].astype(BF16)
    oh_ref[:, 3 * D_A:3 * D_A + D_B] = res[:, 5 * D_A + D_B:D_IN_AB].astype(BF16)


def _ab_inproj(xp, xs, pos, mods, g, w_in):
    return pl.pallas_call(
        _ab_in_kernel,
        grid=(NSEG,),
        in_specs=_x0_specs() + [pl.BlockSpec((1, 6, D), lambda i: (i, 0, 0)),
                                pl.BlockSpec((1, D), lambda i: (0, 0)),
                                pl.BlockSpec((1, D, D_IN_AB), lambda i: (0, 0, 0),
                                             pipeline_mode=pl.Buffered(1))],
        out_specs=[pl.BlockSpec((SEG, AB_F32_COLS), lambda i: (i, 0)),
                   pl.BlockSpec((SEG, D_IN_AB - AB_F32_COLS), lambda i: (i, 0))],
        out_shape=[jax.ShapeDtypeStruct((M_TOK, AB_F32_COLS), F32),
                   jax.ShapeDtypeStruct((M_TOK, D_IN_AB - AB_F32_COLS), BF16)],
        scratch_shapes=[pltpu.VMEM((D, D_IN_AB), BF16)],
        compiler_params=_cp(("arbitrary",)),
        name="ab_inproj",
    )(xp, xs, pos, mods, g.reshape(1, D), w_in)


def _gla_consts():
    t = np.arange(SEG)
    tri_f = (t[None, :] <= t[:, None]).astype(np.float32)
    tri_b = (t[None, :] >= t[:, None]).astype(np.float32)
    th = np.arange(GLA_HALF)
    xor = th[:, None] ^ th[None, :]
    hb = np.where(xor > 0, 1 << np.floor(np.log2(np.maximum(xor, 1))).astype(np.int64), 0)
    code_f = np.where(th[None, :] < th[:, None], hb, 0).astype(np.int32)
    code_b = np.where(th[None, :] > th[:, None], hb, 0).astype(np.int32)
    return [jnp.asarray(tri_f, BF16), jnp.asarray(tri_b, BF16), jnp.asarray(code_f), jnp.asarray(code_b)]


def _gla_level_operand(q, k, b, g, rowi, w, rev):
    upper = jnp.bitwise_and(rowi, w) != 0
    qside = jnp.logical_not(upper) if rev else upper
    if w == 1:
        z = jnp.where(qside, g, 0.0)
    elif w >= SUBLANES:
        nv = 2 * w // SUBLANES
        b4 = b.reshape(SEG // (2 * w), nv, SUBLANES, A_DK)
        ref = (b4[:, nv // 2:nv // 2 + 1, 0:1, :] if rev
               else b4[:, nv // 2 - 1:nv // 2, SUBLANES - 1:SUBLANES, :])
        x = (b4 - ref).reshape(SEG, A_DK)
        z = jnp.where(qside, x, -x)
    else:
        b3 = b.reshape(SEG // SUBLANES, SUBLANES, A_DK)
        sub = lax.broadcasted_iota(jnp.int32, b3.shape, 1)
        beta = None
        for jb in range(SUBLANES // (2 * w)):
            r = jb * 2 * w + (w if rev else w - 1)
            cand = jnp.broadcast_to(b3[:, r:r + 1, :], b3.shape)
            beta = cand if beta is None else jnp.where(sub >= jb * 2 * w, cand, beta)
        x = (b3 - beta).reshape(SEG, A_DK)
        z = jnp.where(qside, x, -x)
    return jnp.where(qside, q, k) * jnp.exp(z)


def _gla_dir(qraw, fraw, v, lb, st, tri, code, rev):
    hh = GLA_HALF
    q = _silu(qraw)
    f = lb + (1.0 - lb) * _sigmoid(fraw)
    g = jnp.log(f)
    k = 1.0 - f
    b = _dot_sel(tri, g)
    rowi = lax.broadcasted_iota(jnp.int32, (SEG, A_DK), 0)
    att = [jnp.zeros((hh, hh), F32), jnp.zeros((hh, hh), F32)]
    cross = None
    for w in GLA_LEVELS:
        m = _gla_level_operand(q, k, b, g, rowi, w, rev).astype(BF16)
        if w == hh:
            cross = _dot_nt(m[0:hh], m[hh:SEG]) if rev else _dot_nt(m[hh:SEG], m[0:hh])
        else:
            for half in range(2):
                mh = m[half * hh:(half + 1) * hh]
                att[half] = jnp.where(code == w, _dot_nt(mh, mh), att[half])
    if rev:
        o_lo = _dot(jnp.concatenate([att[0], cross], axis=1), v)
        o_hi = _dot(att[1], v[hh:SEG])
    else:
        o_lo = _dot(att[0], v[0:hh])
        o_hi = _dot(jnp.concatenate([cross, att[1]], axis=1), v)
    diag = jnp.sum(q * k, axis=-1, keepdims=True)
    o = jnp.concatenate([o_lo, o_hi], axis=0) + diag * v + _dot_nt(q * jnp.exp(b), st)
    btot = b[0:1] if rev else b[SEG - 1:SEG]
    st_new = st * jnp.exp(btot) + _dot_tn(v, k * jnp.exp(btot - b))
    return o, st_new


def _gla_kernel(qf, ff, vf, qb, fb, vb, lb_ref, s0f, s0b, trif, trib, codef, codeb,
                of_ref, ob_ref, fin_ref, st_f, st_b):
    j = pl.program_id(1)
    dirs = ((False, qf, ff, vf, s0f, trif, codef, of_ref, st_f),
            (True, qb, fb, vb, s0b, trib, codeb, ob_ref, st_b))
    for rev, qr, fr, vr, s0, tri, code, o_ref, st in dirs:
        i = (NSEG - 1 - j) if rev else j
        starts = _seg_is_last(i) if rev else _seg_is_first(i)

        @pl.when(jnp.logical_and(starts, i < N_CTX_SEG))
        def _():
            st[...] = jnp.zeros_like(st)

        @pl.when(jnp.logical_and(starts, i >= N_CTX_SEG))
        def _():
            st[...] = s0[0, 0]

    finals = []
    for d, (rev, qr, fr, vr, s0, tri, code, o_ref, st) in enumerate(dirs):
        for hh in range(GLA_HEADS_STEP):
            lanes = slice(hh * A_DK, (hh + 1) * A_DK)
            o, st_new = _gla_dir(qr[:, lanes].astype(F32), fr[:, lanes], vr[:, lanes].astype(F32),
                                 lb_ref[d, hh], st[hh], tri[...], code[...], rev)
            o_ref[:, lanes] = o.astype(BF16)
            finals.append(st_new)
    for d, (rev, qr, fr, vr, s0, tri, code, o_ref, st) in enumerate(dirs):
        for hh in range(GLA_HEADS_STEP):
            st[hh] = finals[d * GLA_HEADS_STEP + hh]
    for d, (rev, qr, fr, vr, s0, tri, code, o_ref, st) in enumerate(dirs):
        i = (NSEG - 1 - j) if rev else j

        @pl.when(i < N_CTX_SEG)
        def _():
            for hh in range(GLA_HEADS_STEP):
                fin_ref[i, 0, d, pl.program_id(0) * GLA_HEADS_STEP + hh] = finals[d * GLA_HEADS_STEP + hh].T


def _gla(proj_f, proj_h, lb, s0t):
    consts = _gla_consts()
    hs = GLA_HEADS_STEP
    wb = hs * A_DK
    n_col = D_A // wb
    blk = lambda col0, rev: pl.BlockSpec(
        (SEG, wb), (lambda h, j: (NSEG - 1 - j, col0 + h)) if rev else (lambda h, j: (j, col0 + h)))
    cspec = lambda a: pl.BlockSpec(a.shape, lambda h, j: (0, 0))
    fin_shape = (N_CTX_SEG, 1, 2, A_HEADS, A_DK, A_DK)
    fin = pl.BlockSpec(fin_shape, lambda h, j: (0,) * len(fin_shape))
    return pl.pallas_call(
        _gla_kernel,
        grid=(A_HEADS // hs, NSEG),
        in_specs=[blk(0, False), blk(0, False), blk(n_col, False),
                  blk(0, True), blk(n_col, True), blk(n_col, True),
                  pl.BlockSpec((2, hs, 1, A_DK), lambda h, j: (0, h, 0, 0)),
                  pl.BlockSpec((1, 1, hs, A_DK, A_DK), lambda h, j: (_sample_of(j), 0, h, 0, 0)),
                  pl.BlockSpec((1, 1, hs, A_DK, A_DK), lambda h, j: (_sample_of(NSEG - 1 - j), 1, h, 0, 0)),
                  cspec(consts[0]), cspec(consts[1]), cspec(consts[2]), cspec(consts[3])],
        out_specs=[blk(0, False), blk(0, True), fin],
        out_shape=[jax.ShapeDtypeStruct((M_TOK, D_A), BF16), jax.ShapeDtypeStruct((M_TOK, D_A), BF16),
                   jax.ShapeDtypeStruct(fin_shape, F32)],
        scratch_shapes=[pltpu.VMEM((hs, A_DK, A_DK), F32), pltpu.VMEM((hs, A_DK, A_DK), F32)],
        compiler_params=_cp(("arbitrary", "arbitrary")),
        name="hgrn2_gla",
    )(proj_h, proj_f, proj_h, proj_h, proj_f, proj_h, lb, s0t, s0t, *consts)


def _lin_scan(a, b, h_in, rev):
    t_len, c = a.shape
    ng = t_len // SUBLANES
    a3 = a.reshape(ng, SUBLANES, c)
    b3 = b.reshape(ng, SUBLANES, c)
    sub = lax.broadcasted_iota(jnp.int32, a3.shape, 1)
    s = 1
    while s < SUBLANES:
        shift = (SUBLANES - s) if rev else s
        valid = (sub < SUBLANES - s) if rev else (sub >= s)
        ap = jnp.where(valid, pltpu.roll(a3, shift, 1), 1.0)
        bp = jnp.where(valid, pltpu.roll(b3, shift, 1), 0.0)
        b3 = a3 * bp + b3
        a3 = a3 * ap
        s *= 2
    hs = [None] * ng
    carry = h_in
    for j in (range(ng - 1, -1, -1) if rev else range(ng)):
        hs[j] = a3[j] * carry + b3[j]
        carry = hs[j][0:1] if rev else hs[j][SUBLANES - 1:SUBLANES]
    return jnp.concatenate(hs, axis=0), carry


def _rglru_dir(x, xprev, xnext, first, last, cw, cb, wa, ba, wx, bx, lam, h_in, rev):
    zero = jnp.zeros_like(xprev)
    ext = jnp.concatenate([jnp.where(first, zero, xprev), x, jnp.where(last, zero, xnext)], axis=0)
    n = ext.shape[0]
    xm2 = pltpu.roll(ext, 2, 0)[8:8 + SEG]
    xm1 = pltpu.roll(ext, 1, 0)[8:8 + SEG]
    xp1 = pltpu.roll(ext, n - 1, 0)[8:8 + SEG]
    xc = cb + xm2 * cw[0:1] + xm1 * cw[1:2] + x * cw[2:3] + xp1 * cw[3:4]
    gate_r = _sigmoid(_dot(xc, wa) + ba)
    gate_i = _sigmoid(_dot(xc, wx) + bx)
    softplus_neg_lam = jnp.maximum(-lam, 0.0) + jnp.log(1.0 + jnp.exp(-jnp.abs(lam)))
    log_a = -LRU_C * gate_r * softplus_neg_lam
    a = jnp.exp(log_a)
    b_in = jnp.sqrt(1.0 - a * a) * gate_i * xc
    return _lin_scan(a, b_in, h_in, rev)


def _rglru_kernel(xf, xf_p, xf_n, xb, xb_p, xb_n, cw_ref, cb_ref, wa_ref, ba_ref, wx_ref, bx_ref, lam_ref,
                  s0f, s0b, hf_ref, hb_ref, ff_out, fb_out, hc_f, hc_b):
    j = pl.program_id(0)
    dirs = ((False, xf, xf_p, xf_n, s0f, hf_ref, ff_out, hc_f),
            (True, xb, xb_p, xb_n, s0b, hb_ref, fb_out, hc_b))
    for rev, xr, xp, xn, s0, h_ref, f_out, hc in dirs:
        i = (NSEG - 1 - j) if rev else j
        starts = _seg_is_last(i) if rev else _seg_is_first(i)

        @pl.when(jnp.logical_and(starts, i < N_CTX_SEG))
        def _():
            hc[...] = jnp.zeros_like(hc)

        @pl.when(jnp.logical_and(starts, i >= N_CTX_SEG))
        def _():
            hc[...] = s0[0, 0]

    outs = []
    for d, (rev, xr, xp, xn, s0, h_ref, f_out, hc) in enumerate(dirs):
        i = (NSEG - 1 - j) if rev else j
        h, h_out = _rglru_dir(xr[...], xp[...], xn[...], _seg_is_first(i), _seg_is_last(i), cw_ref[...],
                              cb_ref[...], wa_ref[d], ba_ref[d], wx_ref[d], bx_ref[d], lam_ref[d], hc[...], rev)
        h_ref[...] = h.astype(BF16)
        outs.append(h_out)
    for d, (rev, xr, xp, xn, s0, h_ref, f_out, hc) in enumerate(dirs):
        hc[...] = outs[d]
    for d, (rev, xr, xp, xn, s0, h_ref, f_out, hc) in enumerate(dirs):
        i = (NSEG - 1 - j) if rev else j

        @pl.when(i < N_CTX_SEG)
        def _():
            f_out[0] = outs[d]


def _rglru(proj, conv_w, conv_b, wa_bd, ba, wx_bd, bx, lam, s0):
    xcol = 2 * D_A // D_B
    fwd = lambda f: (lambda j: f(j))
    bwd = lambda f: (lambda j: f(NSEG - 1 - j))
    seg_blk = lambda m: pl.BlockSpec((SEG, D_B), m(lambda i: (i, xcol)))
    prev_blk = lambda m: pl.BlockSpec((8, D_B), m(lambda i: (_prev8(i), xcol)))
    next_blk = lambda m: pl.BlockSpec((8, D_B), m(lambda i: (_next8(i), xcol)))
    full = lambda a: pl.BlockSpec(a.shape, lambda j: (0,) * a.ndim)
    return pl.pallas_call(
        _rglru_kernel,
        grid=(NSEG,),
        in_specs=[seg_blk(fwd), prev_blk(fwd), next_blk(fwd), seg_blk(bwd), prev_blk(bwd), next_blk(bwd),
                  full(conv_w), full(conv_b), full(wa_bd), full(ba), full(wx_bd), full(bx), full(lam),
                  pl.BlockSpec((1, 1, 1, D_B), lambda j: (_sample_of(j), 0, 0, 0)),
                  pl.BlockSpec((1, 1, 1, D_B), lambda j: (_sample_of(NSEG - 1 - j), 1, 0, 0))],
        out_specs=[pl.BlockSpec((SEG, D_B), lambda j: (j, 0)),
                   pl.BlockSpec((SEG, D_B), lambda j: (NSEG - 1 - j, 0)),
                   pl.BlockSpec((1, 1, D_B), lambda j: (jnp.minimum(j, N_CTX_SEG - 1), 0, 0)),
                   pl.BlockSpec((1, 1, D_B), lambda j: (jnp.minimum(NSEG - 1 - j, N_CTX_SEG - 1), 0, 0))],
        out_shape=[jax.ShapeDtypeStruct((M_TOK, D_B), BF16), jax.ShapeDtypeStruct((M_TOK, D_B), BF16),
                   jax.ShapeDtypeStruct((N_CTX_SEG, 1, D_B), F32), jax.ShapeDtypeStruct((N_CTX_SEG, 1, D_B), F32)],
        scratch_shapes=[pltpu.VMEM((1, D_B), F32), pltpu.VMEM((1, D_B), F32)],
        compiler_params=_cp(("arbitrary",)),
        name="rglru",
    )(proj, proj, proj, proj, proj, proj, conv_w, conv_b, wa_bd, ba, wx_bd, bx, lam, s0, s0)


def _ab_out_kernel(xp_ref, xs_ref, pos_ref, of_ref, ob_ref, og_ref, hf_ref, hb_ref, yr_ref, mod_ref, hg_ref,
                   w_ref, o_ref, w_sc):
    @pl.when(pl.program_id(0) == 0)
    def _():
        w_sc[...] = w_ref[0].astype(BF16)

    m = mod_ref[0]
    f32 = lambda ref: ref[...].astype(F32)
    oa = f32(of_ref) + f32(ob_ref)
    hg = hg_ref[...]
    parts = []
    for h in range(A_HEADS):
        z = oa[:, h * A_DK:(h + 1) * A_DK]
        parts.append(z * lax.rsqrt(jnp.mean(z * z, axis=-1, keepdims=True) + RMS_EPS) * hg)
    o_a = jnp.concatenate(parts, axis=-1) * _silu(f32(og_ref))
    o_b = (f32(hf_ref) + f32(hb_ref)) * _gelu_tanh(f32(yr_ref))
    y = _dot(o_a, w_sc[0:D_A]) + _dot(o_b, w_sc[D_A:D_A + D_B])
    o_ref[...] = _x0(pl.program_id(0), xp_ref, xs_ref, pos_ref) + m[2:3] * y


def _ab_out(xp, xs, pos, proj_h, o_f, o_b, h_f, h_b, mods, hg, w_out):
    seg = lambda width, col: pl.BlockSpec((SEG, width), lambda i: (i, col))
    return pl.pallas_call(
        _ab_out_kernel,
        grid=(NSEG,),
        in_specs=_x0_specs() + [seg(D_A, 0), seg(D_A, 0), seg(D_A, 2), seg(D_B, 0), seg(D_B, 0), seg(D_B, 3),
                                pl.BlockSpec((1, 6, D), lambda i: (i, 0, 0)),
                                pl.BlockSpec((1, A_DK), lambda i: (0, 0)),
                                pl.BlockSpec((1, D_A + D_B, D), lambda i: (0, 0, 0),
                                             pipeline_mode=pl.Buffered(1))],
        out_specs=seg(D, 0),
        out_shape=jax.ShapeDtypeStruct((M_TOK, D), F32),
        scratch_shapes=[pltpu.VMEM((D_A + D_B, D), BF16)],
        compiler_params=_cp(("arbitrary",)),
        name="ab_out",
    )(xp, xs, pos, o_f, o_b, proj_h, h_f, h_b, proj_h, mods, hg.reshape(1, A_DK), w_out)


def _route(scores, sel):
    cols = [sel[e:e + 1, :] for e in range(N_EXPERTS)]

    def rank(vals):
        out = []
        for i, vi in enumerate(vals):
            r = None
            for jx, vj in enumerate(vals):
                if jx == i:
                    continue
                beats = (vj >= vi) if jx < i else (vj > vi)
                r = beats.astype(F32) if r is None else r + beats.astype(F32)
            out.append(r)
        return out

    grp_scores, in_top2 = [], []
    for gi in range(N_GROUPS):
        vals = cols[gi * GROUP:(gi + 1) * GROUP]
        best_pair = None
        for a in range(GROUP):
            for bx in range(a + 1, GROUP):
                s = vals[a] + vals[bx]
                best_pair = s if best_pair is None else jnp.maximum(best_pair, s)
        grp_scores.append(best_pair)
        in_top2.extend([r < 2.0 for r in rank(vals)])
    grp_best = [r < 1.0 for r in rank(grp_scores)]
    picked = [jnp.where(jnp.logical_and(grp_best[e // GROUP], in_top2[e]), scores[e:e + 1, :], 0.0)
              for e in range(N_EXPERTS)]
    total = picked[0]
    for pe in picked[1:]:
        total = total + pe
    row = lax.broadcasted_iota(jnp.int32, scores.shape, 0)
    comb = jnp.zeros(scores.shape, F32)
    for e in range(N_EXPERTS):
        comb = jnp.where(row == e, picked[e] / total, comb)
    return comb


def _moe_kernel(final_norm, x_ref, mod_ref, g_ref, rw_ref, rb_ref, w1_ref, w3_ref, w2_ref, gf_ref, *rest):
    o_refs, (h_sc, comb_sc, acc_sc) = rest[:-3], rest[-3:]
    e = pl.program_id(1)

    @pl.when(e == 0)
    def _():
        for s in range(MOE_TM // SEG):
            m = mod_ref[s]
            rows = slice(s * SEG, (s + 1) * SEG)
            h = _rms_mod(x_ref[rows, :], g_ref[...], m[4:5], m[3:4])
            h_sc[rows, :] = h.astype(BF16)
            scores = _sigmoid(_dot_x3_nt(rw_ref[...], h))
            comb_t = _route(scores, scores + rb_ref[...])
            comb_t = jnp.concatenate([comb_t, jnp.zeros((COMB_LANES - N_EXPERTS, SEG), F32)], axis=0)
            comb_sc[rows, :] = comb_t.T
        acc_sc[...] = jnp.zeros_like(acc_sc)

    lane = lax.broadcasted_iota(jnp.int32, (MOE_TM, COMB_LANES), 1)
    comb = jnp.sum(jnp.where(lane == e, comb_sc[...], 0.0), axis=-1, keepdims=True)
    h = h_sc[...]
    u1 = jnp.dot(h, w1_ref[0, 0].astype(BF16), preferred_element_type=F32)
    u3 = jnp.dot(h, w3_ref[0, 0].astype(BF16), preferred_element_type=F32)
    hid = _silu(u1) * u3 * comb
    acc_sc[...] += jnp.dot(hid.astype(BF16), w2_ref[0, 0].astype(BF16), preferred_element_type=F32)

    def emit(dst_ref):
        for s in range(MOE_TM // SEG):
            rows = slice(s * SEG, (s + 1) * SEG)
            y = x_ref[rows, :] + mod_ref[s][5:6] * acc_sc[rows, :]
            if final_norm:
                y = y * lax.rsqrt(jnp.mean(y * y, axis=-1, keepdims=True) + RMS_EPS) * gf_ref[...]
            dst_ref[rows, :] = y

    last = e == N_EXPERTS - 1
    if final_norm:
        is_ctx = pl.program_id(0) < MOE_CTX_TILES
        pl.when(jnp.logical_and(last, is_ctx))(lambda: emit(o_refs[0]))
        pl.when(jnp.logical_and(last, jnp.logical_not(is_ctx)))(lambda: emit(o_refs[1]))
    else:
        pl.when(last)(lambda: emit(o_refs[0]))


def _moe(x, mods, g, router_w, router_b, w1, w3, w2, layer, gf, final_norm):
    spt = MOE_TM // SEG
    tile = lambda f: pl.BlockSpec((MOE_TM, D), lambda t, e: (f(t), 0))
    if final_norm:
        n_half = M_TOK // 2
        out_specs = [tile(lambda t: jnp.minimum(t, MOE_CTX_TILES - 1)),
                     tile(lambda t: jnp.maximum(t - MOE_CTX_TILES, 0))]
        out_shape = [jax.ShapeDtypeStruct((n_half, D), F32), jax.ShapeDtypeStruct((M_TOK - n_half, D), F32)]
    else:
        out_specs = [tile(lambda t: t)]
        out_shape = [jax.ShapeDtypeStruct((M_TOK, D), F32)]
    return pl.pallas_call(
        functools.partial(_moe_kernel, final_norm),
        grid=(M_TOK // MOE_TM, N_EXPERTS),
        in_specs=[pl.BlockSpec((MOE_TM, D), lambda t, e: (t, 0)),
                  pl.BlockSpec((spt, 6, D), lambda t, e: (t, 0, 0)),
                  pl.BlockSpec((1, D), lambda t, e: (0, 0)),
                  pl.BlockSpec((N_EXPERTS, D), lambda t, e: (0, 0)),
                  pl.BlockSpec((N_EXPERTS, 1), lambda t, e: (0, 0)),
                  pl.BlockSpec((1, 1, D, D_EXPERT), lambda t, e: (layer, e, 0, 0)),
                  pl.BlockSpec((1, 1, D, D_EXPERT), lambda t, e: (layer, e, 0, 0)),
                  pl.BlockSpec((1, 1, D_EXPERT, D), lambda t, e: (layer, e, 0, 0)),
                  pl.BlockSpec((1, D), lambda t, e: (0, 0))],
        out_specs=out_specs,
        out_shape=out_shape,
        scratch_shapes=[pltpu.VMEM((MOE_TM, D), BF16), pltpu.VMEM((MOE_TM, COMB_LANES), F32),
                        pltpu.VMEM((MOE_TM, D), F32)],
        compiler_params=_cp(("arbitrary", "arbitrary")),
        name="moe",
    )(x, mods, g.reshape(1, D), router_w.T, router_b.reshape(N_EXPERTS, 1), w1, w3, w2, gf.reshape(1, D))


def _rw_in_kernel(x_ref, xp_ref, xn_ref, mod_ref, g_ref, mu_ref, wr_ref, wk_ref, wv_ref, g1_ref, g2_ref,
                  w1_ref, w2_ref, w0_ref, a1_ref, a2_ref, a0_ref, ka_ref, rk_ref, hsel_ref,
                  r_ref, k_ref, v_ref, gg_ref, lw_ref, a_ref, coef_ref, wrkv_sc):
    i = pl.program_id(0)

    @pl.when(i == 0)
    def _():
        for c, w_ref in enumerate((wr_ref, wk_ref, wv_ref)):
            wrkv_sc[c] = w_ref[0].astype(BF16)

    m = mod_ref[0]
    g = g_ref[...]
    h = _rms_mod(x_ref[...], g, m[1:2], m[0:1])
    hp = jnp.where(_seg_is_first(i), 0.0, _rms_mod(xp_ref[...], g, m[1:2], m[0:1]))
    hn = jnp.where(_seg_is_last(i), 0.0, _rms_mod(xn_ref[...], g, m[1:2], m[0:1]))
    ext = jnp.concatenate([hp, h, hn], axis=0)
    n = ext.shape[0]
    h_prev = pltpu.roll(ext, 1, 0)[8:8 + SEG]
    h_next = pltpu.roll(ext, n - 1, 0)[8:8 + SEG]
    xx = 0.5 * (h_prev + h_next) - h
    mu = mu_ref[...]
    xr, xw, xk, xv, xa, xg = [h + xx * mu[c:c + 1] for c in range(6)]
    r = _dot(xr, wrkv_sc[0])
    k = _dot(xk, wrkv_sc[1])
    r_ref[...] = r.astype(BF16)
    k_ref[...] = k.astype(BF16)
    v_ref[...] = _dot(xv, wrkv_sc[2]).astype(BF16)
    gg_ref[...] = _dot(_sigmoid(_dot(xg, g1_ref[...])), g2_ref[...]).astype(BF16)
    half_w_in = w0_ref[...] + _dot(jnp.tanh(_dot(xw, w1_ref[...])), w2_ref[...])
    lw_ref[...] = (-0.5 * W_DECAY_SCALE) * jnp.tanh(half_w_in) - 0.5 * W_DECAY_SCALE
    a = 0.5 * jnp.tanh(a0_ref[...] + _dot(_dot(xa, a1_ref[...]), a2_ref[...])) + 0.5
    a_ref[...] = a
    kd_sum = k * (2.0 + (a[:, 0:D] + a[:, D:2 * D] - 2.0) * ka_ref[...])
    prod = r * kd_sum * rk_ref[...]
    p_hi = prod.astype(BF16)
    p_lo = (prod - p_hi.astype(F32)).astype(BF16)
    hsel = hsel_ref[...]
    coef_ref[...] = (jnp.dot(p_hi, hsel, preferred_element_type=F32)
                     + jnp.dot(p_lo, hsel, preferred_element_type=F32))


def _rw_inproj(x, mods, g, mu, wr, wk, wv, g1, g2, w1c, w2bd, w0c, a1c, a2bd, a0c, ka, rk):
    full = lambda a: pl.BlockSpec(a.shape, lambda i: (0,) * a.ndim)
    once = lambda a: pl.BlockSpec((1,) + a.shape[1:], lambda i: (0,) * a.ndim, pipeline_mode=pl.Buffered(1))
    seg = lambda width: pl.BlockSpec((SEG, width), lambda i: (i, 0))
    outs = ([jax.ShapeDtypeStruct((M_TOK, D), BF16)] * 4 + [jax.ShapeDtypeStruct((M_TOK, 2 * D), F32)] * 2
            + [jax.ShapeDtypeStruct((M_TOK, COEF_LANES), F32)])
    hsel = jnp.asarray(np.arange(D)[:, None] // C_HEAD == np.arange(COEF_LANES)[None, :], BF16)
    return pl.pallas_call(
        _rw_in_kernel,
        grid=(NSEG,),
        in_specs=[seg(D),
                  pl.BlockSpec((8, D), lambda i: (_prev8(i), 0)),
                  pl.BlockSpec((8, D), lambda i: (_next8(i), 0)),
                  pl.BlockSpec((1, 6, D), lambda i: (i, 0, 0)),
                  full(g), full(mu), once(wr), once(wk), once(wv), full(g1), full(g2),
                  full(w1c), full(w2bd), full(w0c), full(a1c), full(a2bd), full(a0c),
                  full(ka), full(rk), full(hsel)],
        out_specs=[seg(D)] * 4 + [seg(2 * D)] * 2 + [seg(COEF_LANES)],
        out_shape=outs,
        scratch_shapes=[pltpu.VMEM((3, D, D), BF16)],
        compiler_params=_cp(("arbitrary",)),
        name="rwkv_inproj",
    )(x, x, x, mods, g, mu, wr, wk, wv, g1, g2, w1c, w2bd, w0c, a1c, a2bd, a0c, ka, rk, hsel)


def _rw_consts():
    t = np.arange(SEG)
    same = (t[:, None] // RW_CHUNK) == (t[None, :] // RW_CHUNK)
    tri_f = np.logical_and(same, t[None, :] <= t[:, None]).astype(np.float32)
    tri_b = np.logical_and(same, t[None, :] >= t[:, None]).astype(np.float32)
    return [jnp.asarray(a, BF16) for a in (tri_f, tri_b)]


def _rw_dir(r, k, v, lw, a, kk_g, ka_g, tri, rev):
    c = RW_CHUNK
    lane = lax.broadcasted_iota(jnp.int32, (SEG, 2 * C_HEAD), 1)
    head0 = lane < C_HEAD
    kx = k * kk_g
    ss = kx * kx
    n0 = jnp.sum(jnp.where(head0, ss, 0.0), axis=-1, keepdims=True)
    n1 = jnp.sum(jnp.where(head0, 0.0, ss), axis=-1, keepdims=True)
    kk = kx / jnp.maximum(jnp.sqrt(jnp.where(head0, n0, n1)), 1e-12)
    kd = k * (1.0 + (a - 1.0) * ka_g)
    bhat = kk * a
    cum = _dot_sel(tri, lw)
    e_incl = jnp.exp(cum)
    e_inv = jnp.exp(-cum)
    ae = -kk * jnp.exp(cum - lw)
    re = r * e_incl
    bi = bhat * e_inv
    ki = kd * e_inv

    chunks = []
    for ci in range(SEG // c):
        sl = slice(ci * c, (ci + 1) * c)
        ctot = cum[ci * c:ci * c + 1] if rev else cum[(ci + 1) * c - 1:(ci + 1) * c]
        dec = jnp.exp(ctot - cum[sl])
        chunks.append(dict(ae=ae[sl], re=re[sl], bi=bi[sl], ki=ki[sl], v=v[sl], bdec=bhat[sl] * dec,
                           kdec=kd[sl] * dec, gam=jnp.exp(ctot), rev=rev))
    return chunks


def _rw_transitions(chunks):
    c = RW_CHUNK
    w2 = 2 * C_HEAD
    h0c = lax.broadcasted_iota(jnp.int32, (c, w2), 1) < C_HEAD
    rowc = lax.broadcasted_iota(jnp.int32, (c, w2), 0)
    colc = jnp.bitwise_and(lax.broadcasted_iota(jnp.int32, (c, w2), 1), C_HEAD - 1)
    eye = (colc == rowc).astype(F32)
    bdmask = (lax.broadcasted_iota(jnp.int32, (w2, w2), 0) < C_HEAD) == (
        lax.broadcasted_iota(jnp.int32, (w2, w2), 1) < C_HEAD)
    keep0 = jnp.where(h0c, 1.0, 0.0).astype(BF16)
    keep1 = jnp.where(h0c, 0.0, 1.0).astype(BF16)

    def bd16(yb):
        return jnp.concatenate([yb * keep0, yb * keep1], axis=0)

    def dot_bd(x, pairs):
        blocks = [bd16(y.astype(BF16)) for y in pairs]
        rhs = blocks[0] if len(blocks) == 1 else jnp.concatenate(blocks, axis=1)
        return jnp.dot(x.astype(BF16), rhs, preferred_element_type=F32)

    mm_inv = lambda x, y: dot_bd(x, [y])

    n_ab, a_ak, a_rb, a_rk = [], [], [], []
    for ch in chunks:
        strict = (colc > rowc) if ch["rev"] else (colc < rowc)
        incl = (colc >= rowc) if ch["rev"] else (colc <= rowc)
        left = jnp.concatenate([ch["ae"], ch["re"]], axis=0)
        right = jnp.concatenate([jnp.where(h0c, ch["bi"], 0.0), jnp.where(h0c, 0.0, ch["bi"]),
                                 jnp.where(h0c, ch["ki"], 0.0), jnp.where(h0c, 0.0, ch["ki"])], axis=0)
        gm = _dot_nt(left, right)
        n_ab.append(jnp.where(strict, gm[0:c, 0:2 * c], 0.0))
        a_ak.append(jnp.where(strict, gm[0:c, 2 * c:4 * c], 0.0))
        a_rb.append(jnp.where(incl, gm[c:2 * c, 0:2 * c], 0.0))
        a_rk.append(jnp.where(incl, gm[c:2 * c, 2 * c:4 * c], 0.0))
    xorc = jnp.bitwise_xor(rowc, colc)
    tm = [eye + jnp.where(xorc < 2, n, 0.0) for n in n_ab]
    blk = 2
    while blk < c:
        couple = jnp.logical_and(xorc >= blk, xorc < 2 * blk)
        xs = [mm_inv(jnp.where(couple, n, 0.0), t) for n, t in zip(n_ab, tm)]
        tm = [t + mm_inv(t, x) for t, x in zip(tm, xs)]
        blk *= 2
    akv = [dot_bd(jnp.concatenate([x, y], axis=0), [ch["v"]])
           for x, y, ch in zip(a_ak, a_rk, chunks)]
    tav = [dot_bd(t, [ch["ae"], x[0:c]]) for t, ch, x in zip(tm, chunks, akv)]
    out = []
    for i, ch in enumerate(chunks):
        ta, tv = tav[i][:, 0:w2], tav[i][:, w2:2 * w2]
        qy = dot_bd(a_rb[i], [ta, tv])
        q = ch["re"] + qy[:, 0:w2]
        y = qy[:, w2:2 * w2] + akv[i][c:2 * c]
        wz = _dot_tn(tav[i], ch["bdec"])
        w = jnp.where(bdmask, wz[0:w2], 0.0)
        z = jnp.where(bdmask, wz[w2:2 * w2] + _dot_tn(ch["v"], ch["kdec"]), 0.0)
        out.append((q, y, w, z, ch["gam"]))
    return out


def _rw_scan_kernel(rf, kf, vf, lwf, af, rb, kb, vb, lwb, ab, kkg_ref, kag_ref, s0f, s0b, trif, trib,
                    of_ref, ob_ref, sf_out, sb_out, st_f, st_b):
    j = pl.program_id(1)
    w2 = 2 * C_HEAD
    dirs = ((False, (rf, kf, vf, lwf, af), s0f, trif, of_ref, sf_out, st_f),
            (True, (rb, kb, vb, lwb, ab), s0b, trib, ob_ref, sb_out, st_b))
    for rev, refs, s0, tri, o_ref, s_out, st in dirs:
        i = (NSEG - 1 - j) if rev else j
        starts = _seg_is_last(i) if rev else _seg_is_first(i)

        @pl.when(jnp.logical_and(starts, i < N_CTX_SEG))
        def _():
            st[...] = jnp.zeros_like(st)

        @pl.when(jnp.logical_and(starts, i >= N_CTX_SEG))
        def _():
            st[...] = s0[0, 0]

    chunks = []
    for rev, refs, s0, tri, o_ref, s_out, st in dirs:
        for pp in range(RW_PAIRS_STEP):
            lanes = slice(pp * w2, (pp + 1) * w2)
            r_, k_, v_, lw_, a_ = [z[:, lanes].astype(F32) for z in refs]
            chunks.extend(_rw_dir(r_, k_, v_, lw_, a_, kkg_ref[:, lanes], kag_ref[:, lanes], tri[...], rev))
    trans = _rw_transitions(chunks)
    n_c = SEG // RW_CHUNK
    states = [[st[pp] for pp in range(RW_PAIRS_STEP)] for (_, _, _, _, _, _, st) in dirs]
    for step in range(n_c):
        for d, (rev, refs, s0, tri, o_ref, s_out, st) in enumerate(dirs):
            ci = (n_c - 1 - step) if rev else step
            for pp in range(RW_PAIRS_STEP):
                q, y, w, z, gam = trans[(d * RW_PAIRS_STEP + pp) * n_c + ci]
                s = states[d][pp]
                o_ref[ci * RW_CHUNK:(ci + 1) * RW_CHUNK, pp * w2:(pp + 1) * w2] = (_dot_nt(q, s) + y).astype(BF16)
                states[d][pp] = s * gam + _dot(s, w) + z
    for d, (rev, refs, s0, tri, o_ref, s_out, st) in enumerate(dirs):
        for pp in range(RW_PAIRS_STEP):
            st[pp] = states[d][pp]
    for d, (rev, refs, s0, tri, o_ref, s_out, st) in enumerate(dirs):
        i = (NSEG - 1 - j) if rev else j

        @pl.when(i < N_CTX_SEG)
        def _():
            for pp in range(RW_PAIRS_STEP):
                s = states[d][pp]
                s_out[0, 2 * pp] = s[0:C_HEAD, 0:C_HEAD]
                s_out[0, 2 * pp + 1] = s[C_HEAD:w2, C_HEAD:w2]


def _rw_scan(r, k, v, lw, a, kk_g, ka_g, s0bd):
    consts = _rw_consts()
    w = 2 * C_HEAD
    pps = RW_PAIRS_STEP
    wb = pps * w
    n_steps = C_PAIRS // pps
    blk = lambda col0, rev: pl.BlockSpec(
        (SEG, wb), (lambda p, j: (NSEG - 1 - j, col0 + p)) if rev else (lambda p, j: (j, col0 + p)))
    cspec = lambda arr: pl.BlockSpec(arr.shape, lambda p, j: (0, 0))
    hps = 2 * pps
    fin_f = pl.BlockSpec((1, hps, C_HEAD, C_HEAD), lambda p, j: (jnp.minimum(j, N_CTX_SEG - 1), p, 0, 0))
    fin_b = pl.BlockSpec((1, hps, C_HEAD, C_HEAD),
                         lambda p, j: (jnp.minimum(NSEG - 1 - j, N_CTX_SEG - 1), p, 0, 0))
    return pl.pallas_call(
        _rw_scan_kernel,
        grid=(n_steps, NSEG),
        in_specs=[blk(0, False), blk(0, False), blk(0, False), blk(0, False), blk(0, False),
                  blk(0, True), blk(0, True), blk(0, True), blk(n_steps, True), blk(n_steps, True),
                  pl.BlockSpec((1, wb), lambda p, j: (0, p)), pl.BlockSpec((1, wb), lambda p, j: (0, p)),
                  pl.BlockSpec((1, 1, pps, w, w), lambda p, j: (_sample_of(j), 0, p, 0, 0)),
                  pl.BlockSpec((1, 1, pps, w, w), lambda p, j: (_sample_of(NSEG - 1 - j), 1, p, 0, 0)),
                  cspec(consts[0]), cspec(consts[1])],
        out_specs=[blk(0, False), blk(0, True), fin_f, fin_b],
        out_shape=[jax.ShapeDtypeStruct((M_TOK, D), BF16), jax.ShapeDtypeStruct((M_TOK, D), BF16),
                   jax.ShapeDtypeStruct((N_CTX_SEG, 2 * C_PAIRS, C_HEAD, C_HEAD), F32),
                   jax.ShapeDtypeStruct((N_CTX_SEG, 2 * C_PAIRS, C_HEAD, C_HEAD), F32)],
        scratch_shapes=[pltpu.VMEM((pps, w, w), F32), pltpu.VMEM((pps, w, w), F32)],
        compiler_params=_cp(("arbitrary", "arbitrary")),
        name="rwkv7_scan",
    )(r, k, v, lw, a, r, k, v, lw, a, kk_g, ka_g, s0bd, s0bd, *consts)


def _rw_out_kernel(x_ref, of_ref, ob_ref, v_ref, gg_ref, coef_ref, mod_ref, lnw_ref, lnb_ref, ones_ref,
                   hexp_ref, wo_ref, o_ref, wo_sc):
    @pl.when(pl.program_id(0) == 0)
    def _():
        wo_sc[...] = wo_ref[0].astype(BF16)

    m = mod_ref[0]
    ones_bd = ones_ref[...]
    w = 2 * C_HEAD
    inv_n = 1.0 / C_HEAD
    coef = _dot_sel_rhs(coef_ref[...], hexp_ref[...])
    parts = []
    for p in range(C_PAIRS):
        cs = slice(p * w, (p + 1) * w)
        osum = of_ref[:, cs].astype(F32) + ob_ref[:, cs].astype(F32)
        mu = _dot_sel_rhs(osum, ones_bd) * inv_n
        cen = osum - mu
        var = _dot_sel_rhs(cen * cen, ones_bd) * inv_n
        o = cen * lax.rsqrt(var + GN_EPS) * lnw_ref[:, cs] + lnb_ref[:, cs]
        bonus = coef[:, cs] * v_ref[:, cs].astype(F32)
        parts.append((o + bonus) * gg_ref[:, cs].astype(F32))
    y = _dot(jnp.concatenate(parts, axis=-1), wo_sc[...])
    o_ref[...] = x_ref[...] + m[2:3] * y


def _dot_sel_rhs(x, mat):
    h, m, l = _split3(x)
    return (jnp.dot(h, mat, preferred_element_type=F32) + jnp.dot(m, mat, preferred_element_type=F32)
            + jnp.dot(l, mat, preferred_element_type=F32))


def _rw_out(x, o_f, o_b, v, gg, coef, mods, lnw, lnb, wo):
    seg = lambda width: pl.BlockSpec((SEG, width), lambda i: (i, 0))
    row = pl.BlockSpec((1, D), lambda i: (0, 0))
    hh = np.arange(2 * C_HEAD) // C_HEAD
    ones_bd = jnp.asarray((hh[:, None] == hh[None, :]).astype(np.float32), BF16)
    hexp = jnp.asarray(np.arange(COEF_LANES)[:, None] == np.arange(D)[None, :] // C_HEAD, BF16)
    return pl.pallas_call(
        _rw_out_kernel,
        grid=(NSEG,),
        in_specs=[seg(D), seg(D), seg(D), seg(D), seg(D), seg(COEF_LANES),
                  pl.BlockSpec((1, 6, D), lambda i: (i, 0, 0)),
                  row, row,
                  pl.BlockSpec((2 * C_HEAD, 2 * C_HEAD), lambda i: (0, 0)),
                  pl.BlockSpec((COEF_LANES, D), lambda i: (0, 0)),
                  pl.BlockSpec((1, D, D), lambda i: (0, 0, 0), pipeline_mode=pl.Buffered(1))],
        out_specs=seg(D),
        out_shape=jax.ShapeDtypeStruct((M_TOK, D), F32),
        scratch_shapes=[pltpu.VMEM((D, D), BF16)],
        compiler_params=_cp(("arbitrary",)),
        name="rwkv_out",
    )(x, o_f, o_b, v, gg, coef, mods, lnw.reshape(1, D), lnb.reshape(1, D), ones_bd, hexp, wo)


def _grid_pos_table(n_tok):
    rows = n_tok // GRID_W
    r, cl = np.meshgrid(np.arange(rows, dtype=np.float32), np.arange(GRID_W, dtype=np.float32), indexing='ij')
    quarter = D // 4
    omega = (1.0 / (np.float32(POS_BASE) ** (np.arange(quarter, dtype=np.float32) / np.float32(quarter))))
    ang_r = (r.reshape(-1, 1) * omega).astype(np.float32)
    ang_c = (cl.reshape(-1, 1) * omega).astype(np.float32)
    table = np.concatenate([np.sin(ang_r), np.cos(ang_r), np.sin(ang_c), np.cos(ang_c)], axis=-1)
    return jnp.asarray(table.astype(np.float32))


def _block_diag(blocks):
    g, n, _ = blocks.shape
    eye = jnp.eye(g, dtype=blocks.dtype)
    return (eye[:, None, :, None] * blocks[:, :, None, :]).reshape(g * n, g * n)


def kernel(x_prompt, x_sample, state_hgrn, state_rglru, state_rwkv, c, c_ctx, norm_mix_g, norm_ffn_g, w_mod, b_mod, ab_w_in, ab_w_out, hgrn_lb, hgrn_norm_g, rg_conv_w, rg_conv_b, rg_wa, rg_ba, rg_wx, rg_bx, rg_lambda, rw_mu, rw_wr, rw_wk, rw_wv, rw_wo, rw_w0, rw_w1, rw_w2, rw_a0, rw_a1, rw_a2, rw_g1, rw_g2, rw_kk, rw_ka, rw_rk, rw_lnw, rw_lnb, moe_router, moe_router_bias, moe_w1, moe_w3, moe_w2, norm_f_g):
    bf = lambda z: z.astype(BF16)
    xp = x_prompt.reshape(-1, D)
    xs = x_sample.reshape(-1, D)
    pos = _grid_pos_table(x_sample.shape[1])
    mods = _modulations(c, c_ctx, w_mod, b_mod)

    lower_bounds = jnp.cumsum(jax.nn.softmax(hgrn_lb.astype(F32), axis=1), axis=1)
    lb = lower_bounds[:, 0].reshape(2, A_HEADS, 1, A_DK)
    proj_f, proj_h = _ab_inproj(xp, xs, pos, mods[0], norm_mix_g[0], ab_w_in)
    s0t = jnp.swapaxes(state_hgrn[:, 0], -1, -2)
    o_f, o_b, new_hgrn = _gla(proj_f, proj_h, lb, s0t)
    wa_bd = bf(jnp.stack([_block_diag(rg_wa[0, d]) for d in range(2)]))
    wx_bd = bf(jnp.stack([_block_diag(rg_wx[0, d]) for d in range(2)]))
    h_f, h_b, lru_f, lru_b = _rglru(
        proj_f, rg_conv_w[0], rg_conv_b[0].reshape(1, D_B), wa_bd, rg_ba[0].reshape(2, 1, D_B), wx_bd,
        rg_bx[0].reshape(2, 1, D_B), rg_lambda[0].reshape(2, 1, D_B), state_rglru[:, 0].reshape(-1, 2, 1, D_B))
    x = _ab_out(xp, xs, pos, proj_h, o_f, o_b, h_f, h_b, mods[0], hgrn_norm_g[0], ab_w_out)
    x, = _moe(x, mods[0], norm_ffn_g[0], moe_router, moe_router_bias, moe_w1, moe_w3, moe_w2, 0, norm_f_g, False)

    w1c = bf(jnp.concatenate([rw_w1[0, 0], rw_w1[0, 1]], axis=-1))
    a1c = bf(jnp.concatenate([rw_a1[0, 0], rw_a1[0, 1]], axis=-1))
    half_w2, half_a2 = 0.5 * rw_w2[0], 0.5 * rw_a2[0]
    w2bd = bf(jnp.concatenate([jnp.concatenate([half_w2[0], jnp.zeros_like(half_w2[0])], axis=-1),
                               jnp.concatenate([jnp.zeros_like(half_w2[1]), half_w2[1]], axis=-1)], axis=0))
    a2bd = bf(jnp.concatenate([jnp.concatenate([half_a2[0], jnp.zeros_like(half_a2[0])], axis=-1),
                               jnp.concatenate([jnp.zeros_like(half_a2[1]), half_a2[1]], axis=-1)], axis=0))
    r, k, v, gg, lw, a, coef = _rw_inproj(
        x, mods[1], norm_mix_g[1].reshape(1, D), rw_mu[0], rw_wr, rw_wk, rw_wv,
        bf(rw_g1[0]), bf(rw_g2[0]), w1c, w2bd, 0.5 * rw_w0[0].reshape(1, 2 * D), a1c, a2bd,
        0.5 * rw_a0[0].reshape(1, 2 * D), rw_ka[0].reshape(1, D), rw_rk[0].reshape(1, D))
    s0 = state_rwkv[:, 0].reshape(N_SAMPLE, 2, C_PAIRS, 2, C_HEAD, C_HEAD)
    zeros = jnp.zeros_like(s0[:, :, :, 0])
    s0bd = jnp.concatenate([jnp.concatenate([s0[:, :, :, 0], zeros], axis=-1),
                            jnp.concatenate([zeros, s0[:, :, :, 1]], axis=-1)], axis=-2)
    ow_f, ow_b, rs_f, rs_b = _rw_scan(r, k, v, lw, a, rw_kk[0].reshape(1, D), rw_ka[0].reshape(1, D), s0bd)
    x = _rw_out(x, ow_f, ow_b, v, gg, coef, mods[1], rw_lnw[0], rw_lnb[0], rw_wo)
    y_p, y_s = _moe(x, mods[1], norm_ffn_g[1], moe_router, moe_router_bias, moe_w1, moe_w3, moe_w2, 1, norm_f_g, True)

    new_rglru = jnp.stack([lru_f[:, 0], lru_b[:, 0]], axis=1)[:, None]
    new_rwkv = jnp.stack([rs_f, rs_b], axis=1)[:, None]
    return (y_p.reshape(x_prompt.shape), y_s.reshape(x_sample.shape), new_hgrn, new_rglru, new_rwkv)
```

```python
import functools
import math

import numpy as np
import jax
import jax.numpy as jnp
from jax import lax
from jax.experimental import pallas as pl
from jax.experimental.pallas import tpu as pltpu

F32 = jnp.float32
BF16 = jnp.bfloat16

D = 1024
SEG = 256
N_CTX_SEG = 16
SEG_PER_SAMPLE = 4
N_SAMPLE = 4
NSEG = N_CTX_SEG + N_SAMPLE * SEG_PER_SAMPLE
M_TOK = NSEG * SEG
SUBLANES = 8
ROWS8_PER_SEG = SEG // SUBLANES

A_HEADS = 4
A_DK = 128
D_A = 512
D_B = 512
B_BLOCKS = 8
B_BLOCK = 64
LRU_C = 8.0
D_IN_AB = 5 * D_A + 2 * D_B
AB_F32_COLS = 2 * D_A + D_B
C_HEAD = 64
C_PAIRS = 8
RW_CHUNK = 64
COEF_LANES = 128
RW_PAIRS_STEP = 4
W_DECAY_SCALE = math.exp(-0.5)
N_EXPERTS = 16
N_GROUPS = 4
GROUP = 4
D_EXPERT = 256
RMS_EPS = 1e-6
GN_EPS = 64e-5
POS_BASE = 10000.0
GRID_W = 64
MOE_TM = 1024
MOE_CTX_TILES = N_CTX_SEG * SEG // MOE_TM
MOE_EXPERTS_STEP = 2
COMB_LANES = 128
GLA_LEVELS = (1, 2, 4, 8, 16, 32, 64, 128)
GLA_HALF = 128
GLA_HEADS_STEP = 4

VMEM_LIMIT = 56 * 1024 * 1024


def _cp(sem):
    return pltpu.CompilerParams(dimension_semantics=sem, vmem_limit_bytes=VMEM_LIMIT)


def _sigmoid(x):
    return 0.5 * jnp.tanh(0.5 * x) + 0.5


def _silu(x):
    return x * _sigmoid(x)


def _gelu_tanh(x):
    return 0.5 * x * (1.0 + jnp.tanh(math.sqrt(2.0 / math.pi) * (x + 0.044715 * (x * x * x))))


def _rms_mod(x, g, scale, shift):
    ms = jnp.mean(x * x, axis=-1, keepdims=True)
    return x * lax.rsqrt(ms + RMS_EPS) * (g * (1.0 + scale)) + shift


def _dot(a, b):
    return jnp.dot(a.astype(BF16), b.astype(BF16), preferred_element_type=F32)


def _dot_nt(a, b):
    return lax.dot_general(a.astype(BF16), b.astype(BF16), (((1,), (1,)), ((), ())),
                           preferred_element_type=F32)


def _dot_tn(a, b):
    return lax.dot_general(a.astype(BF16), b.astype(BF16), (((0,), (0,)), ((), ())),
                           preferred_element_type=F32)


def _split3(x):
    h = x.astype(BF16)
    r1 = x - h.astype(F32)
    m = r1.astype(BF16)
    r2 = r1 - m.astype(F32)
    return h, m, r2.astype(BF16)


def _dot_sel(mat, x):
    h, m, l = _split3(x)
    return (jnp.dot(mat, h, preferred_element_type=F32) + jnp.dot(mat, m, preferred_element_type=F32)
            + jnp.dot(mat, l, preferred_element_type=F32))


def _dot_x3_nt(a, b):
    dn = (((1,), (1,)), ((), ()))
    ah = a.astype(BF16)
    al = (a - ah.astype(F32)).astype(BF16)
    bh = b.astype(BF16)
    bl = (b - bh.astype(F32)).astype(BF16)
    return (lax.dot_general(ah, bh, dn, preferred_element_type=F32)
            + lax.dot_general(ah, bl, dn, preferred_element_type=F32)
            + lax.dot_general(al, bh, dn, preferred_element_type=F32))


def _seg_is_first(i):
    return jnp.logical_or(i < N_CTX_SEG, lax.rem(i - N_CTX_SEG, SEG_PER_SAMPLE) == 0)


def _seg_is_last(i):
    return jnp.logical_or(i < N_CTX_SEG, lax.rem(i - N_CTX_SEG, SEG_PER_SAMPLE) == SEG_PER_SAMPLE - 1)


def _sample_of(i):
    return jnp.maximum(i - N_CTX_SEG, 0) // SEG_PER_SAMPLE


def _prev8(i):
    return jnp.maximum(i * ROWS8_PER_SEG - 1, 0)


def _next8(i):
    return jnp.minimum((i + 1) * ROWS8_PER_SEG, M_TOK // 8 - 1)


def _mod_kernel(cv_ref, w_ref, b_ref, o_ref):
    cv = cv_ref[...]
    o_ref[0] = _dot(_silu(cv), w_ref[0]) + b_ref[0]


def _modulations(c, c_ctx, w_mod, b_mod):
    depth = w_mod.shape[0]
    cv = jnp.concatenate([c_ctx[None, :], c, jnp.zeros((3, D), F32)], axis=0)
    n_t = 6
    mod = pl.pallas_call(
        _mod_kernel,
        grid=(depth, n_t),
        in_specs=[pl.BlockSpec((8, D), lambda l, n: (0, 0)),
                  pl.BlockSpec((1, D, D), lambda l, n: (l, 0, n)),
                  pl.BlockSpec((1, 1, D), lambda l, n: (l, 0, n))],
        out_specs=pl.BlockSpec((1, 8, D), lambda l, n: (l, 0, n)),
        out_shape=jax.ShapeDtypeStruct((depth, 8, 6 * D), F32),
        compiler_params=_cp(("arbitrary", "arbitrary")),
        name="adaln_mod",
    )(cv, w_mod, b_mod.reshape(depth, 1, 6 * D))
    row_of_seg = np.array([0] * N_CTX_SEG + [1 + s // SEG_PER_SAMPLE for s in range(N_SAMPLE * SEG_PER_SAMPLE)])
    return mod[:, row_of_seg].reshape(depth, NSEG, 6, D)


def _x0_specs():
    return [pl.BlockSpec((SEG, D), lambda i: (jnp.minimum(i, N_CTX_SEG - 1), 0)),
            pl.BlockSpec((SEG, D), lambda i: (jnp.maximum(i - N_CTX_SEG, 0), 0)),
            pl.BlockSpec((SEG, D), lambda i: (lax.rem(jnp.maximum(i - N_CTX_SEG, 0), SEG_PER_SAMPLE), 0))]


def _x0(i, xp_ref, xs_ref, pos_ref):
    return jnp.where(i < N_CTX_SEG, xp_ref[...], xs_ref[...] + pos_ref[...])


def _ab_in_kernel(xp_ref, xs_ref, pos_ref, mod_ref, g_ref, w_ref, of_ref, oh_ref, w_sc):
    @pl.when(pl.program_id(0) == 0)
    def _():
        w_sc[...] = w_ref[0].astype(BF16)

    m = mod_ref[0]
    x = _x0(pl.program_id(0), xp_ref, xs_ref, pos_ref)
    h = _rms_mod(x, g_ref[...], m[1:2], m[0:1])
    res = jnp.dot(h.astype(BF16), w_sc[...], preferred_element_type=F32)
    of_ref[:, 0:2 * D_A] = res[:, D_A:3 * D_A]
    of_ref[:, 2 * D_A:AB_F32_COLS] = res[:, 5 * D_A:5 * D_A + D_B]
    oh_ref[:, 0:D_A] = res[:, 0:D_A].astype(BF16)
    oh_ref[:, D_A:3 * D_A] = res[:, 3 * D_A:5 * D_A].astype(BF16)
    oh_ref[:, 3 * D_A:3 * D_A + D_B] = res[:, 5 * D_A + D_B:D_IN_AB].astype(BF16)


def _ab_inproj(xp, xs, pos, mods, g, w_in):
    return pl.pallas_call(
        _ab_in_kernel,
        grid=(NSEG,),
        in_specs=_x0_specs() + [pl.BlockSpec((1, 6, D), lambda i: (i, 0, 0)),
                                pl.BlockSpec((1, D), lambda i: (0, 0)),
                                pl.BlockSpec((1, D, D_IN_AB), lambda i: (0, 0, 0),
                                             pipeline_mode=pl.Buffered(1))],
        out_specs=[pl.BlockSpec((SEG, AB_F32_COLS), lambda i: (i, 0)),
                   pl.BlockSpec((SEG, D_IN_AB - AB_F32_COLS), lambda i: (i, 0))],
        out_shape=[jax.ShapeDtypeStruct((M_TOK, AB_F32_COLS), F32),
                   jax.ShapeDtypeStruct((M_TOK, D_IN_AB - AB_F32_COLS), BF16)],
        scratch_shapes=[pltpu.VMEM((D, D_IN_AB), BF16)],
        compiler_params=_cp(("arbitrary",)),
        name="ab_inproj",
    )(xp, xs, pos, mods, g.reshape(1, D), w_in)


def _gla_consts():
    t = np.arange(SEG)
    tri_f = (t[None, :] <= t[:, None]).astype(np.float32)
    tri_b = (t[None, :] >= t[:, None]).astype(np.float32)
    th = np.arange(GLA_HALF)
    xor = th[:, None] ^ th[None, :]
    hb = np.where(xor > 0, 1 << np.floor(np.log2(np.maximum(xor, 1))).astype(np.int64), 0)
    code_f = np.where(th[None, :] < th[:, None], hb, 0).astype(np.int32)
    code_b = np.where(th[None, :] > th[:, None], hb, 0).astype(np.int32)
    return [jnp.asarray(tri_f, BF16), jnp.asarray(tri_b, BF16), jnp.asarray(code_f), jnp.asarray(code_b)]


def _gla_level_operand(q, k, b, g, rowi, w, rev):
    upper = jnp.bitwise_and(rowi, w) != 0
    qside = jnp.logical_not(upper) if rev else upper
    if w == 1:
        z = jnp.where(qside, g, 0.0)
    elif w >= SUBLANES:
        nv = 2 * w // SUBLANES
        b4 = b.reshape(SEG // (2 * w), nv, SUBLANES, A_DK)
        ref = (b4[:, nv // 2:nv // 2 + 1, 0:1, :] if rev
               else b4[:, nv // 2 - 1:nv // 2, SUBLANES - 1:SUBLANES, :])
        x = (b4 - ref).reshape(SEG, A_DK)
        z = jnp.where(qside, x, -x)
    else:
        b3 = b.reshape(SEG // SUBLANES, SUBLANES, A_DK)
        sub = lax.broadcasted_iota(jnp.int32, b3.shape, 1)
        beta = None
        for jb in range(SUBLANES // (2 * w)):
            r = jb * 2 * w + (w if rev else w - 1)
            cand = jnp.broadcast_to(b3[:, r:r + 1, :], b3.shape)
            beta = cand if beta is None else jnp.where(sub >= jb * 2 * w, cand, beta)
        x = (b3 - beta).reshape(SEG, A_DK)
        z = jnp.where(qside, x, -x)
    return jnp.where(qside, q, k) * jnp.exp(z)


def _gla_dir(qraw, fraw, v, lb, st, tri, code, rev):
    hh = GLA_HALF
    q = _silu(qraw)
    f = lb + (1.0 - lb) * _sigmoid(fraw)
    g = jnp.log(f)
    k = 1.0 - f
    b = _dot_sel(tri, g)
    rowi = lax.broadcasted_iota(jnp.int32, (SEG, A_DK), 0)
    att = [jnp.zeros((hh, hh), F32), jnp.zeros((hh, hh), F32)]
    cross = None
    for w in GLA_LEVELS:
        m = _gla_level_operand(q, k, b, g, rowi, w, rev).astype(BF16)
        if w == hh:
            cross = _dot_nt(m[0:hh], m[hh:SEG]) if rev else _dot_nt(m[hh:SEG], m[0:hh])
        else:
            for half in range(2):
                mh = m[half * hh:(half + 1) * hh]
                att[half] = jnp.where(code == w, _dot_nt(mh, mh), att[half])
    if rev:
        o_lo = _dot(jnp.concatenate([att[0], cross], axis=1), v)
        o_hi = _dot(att[1], v[hh:SEG])
    else:
        o_lo = _dot(att[0], v[0:hh])
        o_hi = _dot(jnp.concatenate([cross, att[1]], axis=1), v)
    diag = jnp.sum(q * k, axis=-1, keepdims=True)
    o = jnp.concatenate([o_lo, o_hi], axis=0) + diag * v + _dot_nt(q * jnp.exp(b), st)
    btot = b[0:1] if rev else b[SEG - 1:SEG]
    st_new = st * jnp.exp(btot) + _dot_tn(v, k * jnp.exp(btot - b))
    return o, st_new


def _gla_kernel(qf, ff, vf, qb, fb, vb, lb_ref, s0f, s0b, trif, trib, codef, codeb,
                of_ref, ob_ref, fin_ref, st_f, st_b):
    j = pl.program_id(1)
    dirs = ((False, qf, ff, vf, s0f, trif, codef, of_ref, st_f),
            (True, qb, fb, vb, s0b, trib, codeb, ob_ref, st_b))
    for rev, qr, fr, vr, s0, tri, code, o_ref, st in dirs:
        i = (NSEG - 1 - j) if rev else j
        starts = _seg_is_last(i) if rev else _seg_is_first(i)

        @pl.when(jnp.logical_and(starts, i < N_CTX_SEG))
        def _():
            st[...] = jnp.zeros_like(st)

        @pl.when(jnp.logical_and(starts, i >= N_CTX_SEG))
        def _():
            st[...] = s0[0, 0]

    finals = []
    for d, (rev, qr, fr, vr, s0, tri, code, o_ref, st) in enumerate(dirs):
        for hh in range(GLA_HEADS_STEP):
            lanes = slice(hh * A_DK, (hh + 1) * A_DK)
            o, st_new = _gla_dir(qr[:, lanes].astype(F32), fr[:, lanes], vr[:, lanes].astype(F32),
                                 lb_ref[d, hh], st[hh], tri[...], code[...], rev)
            o_ref[:, lanes] = o.astype(BF16)
            finals.append(st_new)
    for d, (rev, qr, fr, vr, s0, tri, code, o_ref, st) in enumerate(dirs):
        for hh in range(GLA_HEADS_STEP):
            st[hh] = finals[d * GLA_HEADS_STEP + hh]
    for d, (rev, qr, fr, vr, s0, tri, code, o_ref, st) in enumerate(dirs):
        i = (NSEG - 1 - j) if rev else j

        @pl.when(i < N_CTX_SEG)
        def _():
            for hh in range(GLA_HEADS_STEP):
                fin_ref[i, 0, d, pl.program_id(0) * GLA_HEADS_STEP + hh] = finals[d * GLA_HEADS_STEP + hh].T


def _gla(proj_f, proj_h, lb, s0t):
    consts = _gla_consts()
    hs = GLA_HEADS_STEP
    wb = hs * A_DK
    n_col = D_A // wb
    blk = lambda col0, rev: pl.BlockSpec(
        (SEG, wb), (lambda h, j: (NSEG - 1 - j, col0 + h)) if rev else (lambda h, j: (j, col0 + h)))
    cspec = lambda a: pl.BlockSpec(a.shape, lambda h, j: (0, 0))
    fin_shape = (N_CTX_SEG, 1, 2, A_HEADS, A_DK, A_DK)
    fin = pl.BlockSpec(fin_shape, lambda h, j: (0,) * len(fin_shape))
    return pl.pallas_call(
        _gla_kernel,
        grid=(A_HEADS // hs, NSEG),
        in_specs=[blk(0, False), blk(0, False), blk(n_col, False),
                  blk(0, True), blk(n_col, True), blk(n_col, True),
                  pl.BlockSpec((2, hs, 1, A_DK), lambda h, j: (0, h, 0, 0)),
                  pl.BlockSpec((1, 1, hs, A_DK, A_DK), lambda h, j: (_sample_of(j), 0, h, 0, 0)),
                  pl.BlockSpec((1, 1, hs, A_DK, A_DK), lambda h, j: (_sample_of(NSEG - 1 - j), 1, h, 0, 0)),
                  cspec(consts[0]), cspec(consts[1]), cspec(consts[2]), cspec(consts[3])],
        out_specs=[blk(0, False), blk(0, True), fin],
        out_shape=[jax.ShapeDtypeStruct((M_TOK, D_A), BF16), jax.ShapeDtypeStruct((M_TOK, D_A), BF16),
                   jax.ShapeDtypeStruct(fin_shape, F32)],
        scratch_shapes=[pltpu.VMEM((hs, A_DK, A_DK), F32), pltpu.VMEM((hs, A_DK, A_DK), F32)],
        compiler_params=_cp(("arbitrary", "arbitrary")),
        name="hgrn2_gla",
    )(proj_h, proj_f, proj_h, proj_h, proj_f, proj_h, lb, s0t, s0t, *consts)


def _lin_scan(a, b, h_in, rev):
    t_len, c = a.shape
    ng = t_len // SUBLANES
    a3 = a.reshape(ng, SUBLANES, c)
    b3 = b.reshape(ng, SUBLANES, c)
    sub = lax.broadcasted_iota(jnp.int32, a3.shape, 1)
    s = 1
    while s < SUBLANES:
        shift = (SUBLANES - s) if rev else s
        valid = (sub < SUBLANES - s) if rev else (sub >= s)
        ap = jnp.where(valid, pltpu.roll(a3, shift, 1), 1.0)
        bp = jnp.where(valid, pltpu.roll(b3, shift, 1), 0.0)
        b3 = a3 * bp + b3
        a3 = a3 * ap
        s *= 2
    hs = [None] * ng
    carry = h_in
    for j in (range(ng - 1, -1, -1) if rev else range(ng)):
        hs[j] = a3[j] * carry + b3[j]
        carry = hs[j][0:1] if rev else hs[j][SUBLANES - 1:SUBLANES]
    return jnp.concatenate(hs, axis=0), carry


def _rglru_dir(x, xprev, xnext, first, last, cw, cb, wa, ba, wx, bx, lam, h_in, rev):
    zero = jnp.zeros_like(xprev)
    ext = jnp.concatenate([jnp.where(first, zero, xprev), x, jnp.where(last, zero, xnext)], axis=0)
    n = ext.shape[0]
    xm2 = pltpu.roll(ext, 2, 0)[8:8 + SEG]
    xm1 = pltpu.roll(ext, 1, 0)[8:8 + SEG]
    xp1 = pltpu.roll(ext, n - 1, 0)[8:8 + SEG]
    xc = cb + xm2 * cw[0:1] + xm1 * cw[1:2] + x * cw[2:3] + xp1 * cw[3:4]
    gate_r = _sigmoid(_dot(xc, wa) + ba)
    gate_i = _sigmoid(_dot(xc, wx) + bx)
    softplus_neg_lam = jnp.maximum(-lam, 0.0) + jnp.log(1.0 + jnp.exp(-jnp.abs(lam)))
    log_a = -LRU_C * gate_r * softplus_neg_lam
    a = jnp.exp(log_a)
    b_in = jnp.sqrt(1.0 - a * a) * gate_i * xc
    return _lin_scan(a, b_in, h_in, rev)


def _rglru_kernel(xf, xf_p, xf_n, xb, xb_p, xb_n, cw_ref, cb_ref, wa_ref, ba_ref, wx_ref, bx_ref, lam_ref,
                  s0f, s0b, hf_ref, hb_ref, ff_out, fb_out, hc_f, hc_b):
    j = pl.program_id(0)
    dirs = ((False, xf, xf_p, xf_n, s0f, hf_ref, ff_out, hc_f),
            (True, xb, xb_p, xb_n, s0b, hb_ref, fb_out, hc_b))
    for rev, xr, xp, xn, s0, h_ref, f_out, hc in dirs:
        i = (NSEG - 1 - j) if rev else j
        starts = _seg_is_last(i) if rev else _seg_is_first(i)

        @pl.when(jnp.logical_and(starts, i < N_CTX_SEG))
        def _():
            hc[...] = jnp.zeros_like(hc)

        @pl.when(jnp.logical_and(starts, i >= N_CTX_SEG))
        def _():
            hc[...] = s0[0, 0]

    outs = []
    for d, (rev, xr, xp, xn, s0, h_ref, f_out, hc) in enumerate(dirs):
        i = (NSEG - 1 - j) if rev else j
        h, h_out = _rglru_dir(xr[...], xp[...], xn[...], _seg_is_first(i), _seg_is_last(i), cw_ref[...],
                              cb_ref[...], wa_ref[d], ba_ref[d], wx_ref[d], bx_ref[d], lam_ref[d], hc[...], rev)
        h_ref[...] = h.astype(BF16)
        outs.append(h_out)
    for d, (rev, xr, xp, xn, s0, h_ref, f_out, hc) in enumerate(dirs):
        hc[...] = outs[d]
    for d, (rev, xr, xp, xn, s0, h_ref, f_out, hc) in enumerate(dirs):
        i = (NSEG - 1 - j) if rev else j

        @pl.when(i < N_CTX_SEG)
        def _():
            f_out[0] = outs[d]


def _rglru(proj, conv_w, conv_b, wa_bd, ba, wx_bd, bx, lam, s0):
    xcol = 2 * D_A // D_B
    fwd = lambda f: (lambda j: f(j))
    bwd = lambda f: (lambda j: f(NSEG - 1 - j))
    seg_blk = lambda m: pl.BlockSpec((SEG, D_B), m(lambda i: (i, xcol)))
    prev_blk = lambda m: pl.BlockSpec((8, D_B), m(lambda i: (_prev8(i), xcol)))
    next_blk = lambda m: pl.BlockSpec((8, D_B), m(lambda i: (_next8(i), xcol)))
    full = lambda a: pl.BlockSpec(a.shape, lambda j: (0,) * a.ndim)
    return pl.pallas_call(
        _rglru_kernel,
        grid=(NSEG,),
        in_specs=[seg_blk(fwd), prev_blk(fwd), next_blk(fwd), seg_blk(bwd), prev_blk(bwd), next_blk(bwd),
                  full(conv_w), full(conv_b), full(wa_bd), full(ba), full(wx_bd), full(bx), full(lam),
                  pl.BlockSpec((1, 1, 1, D_B), lambda j: (_sample_of(j), 0, 0, 0)),
                  pl.BlockSpec((1, 1, 1, D_B), lambda j: (_sample_of(NSEG - 1 - j), 1, 0, 0))],
        out_specs=[pl.BlockSpec((SEG, D_B), lambda j: (j, 0)),
                   pl.BlockSpec((SEG, D_B), lambda j: (NSEG - 1 - j, 0)),
                   pl.BlockSpec((1, 1, D_B), lambda j: (jnp.minimum(j, N_CTX_SEG - 1), 0, 0)),
                   pl.BlockSpec((1, 1, D_B), lambda j: (jnp.minimum(NSEG - 1 - j, N_CTX_SEG - 1), 0, 0))],
        out_shape=[jax.ShapeDtypeStruct((M_TOK, D_B), BF16), jax.ShapeDtypeStruct((M_TOK, D_B), BF16),
                   jax.ShapeDtypeStruct((N_CTX_SEG, 1, D_B), F32), jax.ShapeDtypeStruct((N_CTX_SEG, 1, D_B), F32)],
        scratch_shapes=[pltpu.VMEM((1, D_B), F32), pltpu.VMEM((1, D_B), F32)],
        compiler_params=_cp(("arbitrary",)),
        name="rglru",
    )(proj, proj, proj, proj, proj, proj, conv_w, conv_b, wa_bd, ba, wx_bd, bx, lam, s0, s0)


def _ab_out_kernel(xp_ref, xs_ref, pos_ref, of_ref, ob_ref, og_ref, hf_ref, hb_ref, yr_ref, mod_ref, hg_ref,
                   w_ref, o_ref, w_sc):
    @pl.when(pl.program_id(0) == 0)
    def _():
        w_sc[...] = w_ref[0].astype(BF16)

    m = mod_ref[0]
    f32 = lambda ref: ref[...].astype(F32)
    oa = f32(of_ref) + f32(ob_ref)
    hg = hg_ref[...]
    parts = []
    for h in range(A_HEADS):
        z = oa[:, h * A_DK:(h + 1) * A_DK]
        parts.append(z * lax.rsqrt(jnp.mean(z * z, axis=-1, keepdims=True) + RMS_EPS) * hg)
    o_a = jnp.concatenate(parts, axis=-1) * _silu(f32(og_ref))
    o_b = (f32(hf_ref) + f32(hb_ref)) * _gelu_tanh(f32(yr_ref))
    y = _dot(o_a, w_sc[0:D_A]) + _dot(o_b, w_sc[D_A:D_A + D_B])
    o_ref[...] = _x0(pl.program_id(0), xp_ref, xs_ref, pos_ref) + m[2:3] * y


def _ab_out(xp, xs, pos, proj_h, o_f, o_b, h_f, h_b, mods, hg, w_out):
    seg = lambda width, col: pl.BlockSpec((SEG, width), lambda i: (i, col))
    return pl.pallas_call(
        _ab_out_kernel,
        grid=(NSEG,),
        in_specs=_x0_specs() + [seg(D_A, 0), seg(D_A, 0), seg(D_A, 2), seg(D_B, 0), seg(D_B, 0), seg(D_B, 3),
                                pl.BlockSpec((1, 6, D), lambda i: (i, 0, 0)),
                                pl.BlockSpec((1, A_DK), lambda i: (0, 0)),
                                pl.BlockSpec((1, D_A + D_B, D), lambda i: (0, 0, 0),
                                             pipeline_mode=pl.Buffered(1))],
        out_specs=seg(D, 0),
        out_shape=jax.ShapeDtypeStruct((M_TOK, D), F32),
        scratch_shapes=[pltpu.VMEM((D_A + D_B, D), BF16)],
        compiler_params=_cp(("arbitrary",)),
        name="ab_out",
    )(xp, xs, pos, o_f, o_b, proj_h, h_f, h_b, proj_h, mods, hg.reshape(1, A_DK), w_out)


def _route(scores, sel):
    cols = [sel[e:e + 1, :] for e in range(N_EXPERTS)]

    def rank(vals):
        out = []
        for i, vi in enumerate(vals):
            r = None
            for jx, vj in enumerate(vals):
                if jx == i:
                    continue
                beats = (vj >= vi) if jx < i else (vj > vi)
                r = beats.astype(F32) if r is None else r + beats.astype(F32)
            out.append(r)
        return out

    grp_scores, in_top2 = [], []
    for gi in range(N_GROUPS):
        vals = cols[gi * GROUP:(gi + 1) * GROUP]
        best_pair = None
        for a in range(GROUP):
            for bx in range(a + 1, GROUP):
                s = vals[a] + vals[bx]
                best_pair = s if best_pair is None else jnp.maximum(best_pair, s)
        grp_scores.append(best_pair)
        in_top2.extend([r < 2.0 for r in rank(vals)])
    grp_best = [r < 1.0 for r in rank(grp_scores)]
    picked = [jnp.where(jnp.logical_and(grp_best[e // GROUP], in_top2[e]), scores[e:e + 1, :], 0.0)
              for e in range(N_EXPERTS)]
    total = picked[0]
    for pe in picked[1:]:
        total = total + pe
    row = lax.broadcasted_iota(jnp.int32, scores.shape, 0)
    comb = jnp.zeros(scores.shape, F32)
    for e in range(N_EXPERTS):
        comb = jnp.where(row == e, picked[e] / total, comb)
    return comb


def _moe_kernel(final_norm, x_ref, mod_ref, g_ref, rw_ref, rb_ref, w1_ref, w3_ref, w2_ref, gf_ref, *rest):
    o_refs, (h_sc, comb_sc, acc_sc) = rest[:-3], rest[-3:]
    e = pl.program_id(1)

    @pl.when(e == 0)
    def _():
        for s in range(MOE_TM // SEG):
            m = mod_ref[s]
            rows = slice(s * SEG, (s + 1) * SEG)
            h = _rms_mod(x_ref[rows, :], g_ref[...], m[4:5], m[3:4])
            h_sc[rows, :] = h.astype(BF16)
            scores = _sigmoid(_dot_x3_nt(rw_ref[...], h))
            comb_t = _route(scores, scores + rb_ref[...])
            comb_t = jnp.concatenate([comb_t, jnp.zeros((COMB_LANES - N_EXPERTS, SEG), F32)], axis=0)
            comb_sc[rows, :] = comb_t.T
        acc_sc[...] = jnp.zeros_like(acc_sc)

    lane = lax.broadcasted_iota(jnp.int32, (MOE_TM, COMB_LANES), 1)
    h = h_sc[...]
    hids = []
    for ee in range(MOE_EXPERTS_STEP):
        comb = jnp.sum(jnp.where(lane == e * MOE_EXPERTS_STEP + ee, comb_sc[...], 0.0), axis=-1, keepdims=True)
        u1 = jnp.dot(h, w1_ref[0, ee].astype(BF16), preferred_element_type=F32)
        u3 = jnp.dot(h, w3_ref[0, ee].astype(BF16), preferred_element_type=F32)
        hids.append((_silu(u1) * u3 * comb).astype(BF16))
    w2 = w2_ref[0].reshape(MOE_EXPERTS_STEP * D_EXPERT, D).astype(BF16)
    acc_sc[...] += jnp.dot(jnp.concatenate(hids, axis=1), w2, preferred_element_type=F32)

    def emit(dst_ref):
        for s in range(MOE_TM // SEG):
            rows = slice(s * SEG, (s + 1) * SEG)
            y = x_ref[rows, :] + mod_ref[s][5:6] * acc_sc[rows, :]
            if final_norm:
                y = y * lax.rsqrt(jnp.mean(y * y, axis=-1, keepdims=True) + RMS_EPS) * gf_ref[...]
            dst_ref[rows, :] = y

    last = e == N_EXPERTS // MOE_EXPERTS_STEP - 1
    if final_norm:
        is_ctx = pl.program_id(0) < MOE_CTX_TILES
        pl.when(jnp.logical_and(last, is_ctx))(lambda: emit(o_refs[0]))
        pl.when(jnp.logical_and(last, jnp.logical_not(is_ctx)))(lambda: emit(o_refs[1]))
    else:
        pl.when(last)(lambda: emit(o_refs[0]))


def _moe(x, mods, g, router_w, router_b, w1, w3, w2, layer, gf, final_norm):
    spt = MOE_TM // SEG
    es = MOE_EXPERTS_STEP
    tile = lambda f: pl.BlockSpec((MOE_TM, D), lambda t, e: (f(t), 0))
    if final_norm:
        n_half = M_TOK // 2
        out_specs = [tile(lambda t: jnp.minimum(t, MOE_CTX_TILES - 1)),
                     tile(lambda t: jnp.maximum(t - MOE_CTX_TILES, 0))]
        out_shape = [jax.ShapeDtypeStruct((n_half, D), F32), jax.ShapeDtypeStruct((M_TOK - n_half, D), F32)]
    else:
        out_specs = [tile(lambda t: t)]
        out_shape = [jax.ShapeDtypeStruct((M_TOK, D), F32)]
    return pl.pallas_call(
        functools.partial(_moe_kernel, final_norm),
        grid=(M_TOK // MOE_TM, N_EXPERTS // es),
        in_specs=[pl.BlockSpec((MOE_TM, D), lambda t, e: (t, 0)),
                  pl.BlockSpec((spt, 6, D), lambda t, e: (t, 0, 0)),
                  pl.BlockSpec((1, D), lambda t, e: (0, 0)),
                  pl.BlockSpec((N_EXPERTS, D), lambda t, e: (0, 0)),
                  pl.BlockSpec((N_EXPERTS, 1), lambda t, e: (0, 0)),
                  pl.BlockSpec((1, es, D, D_EXPERT), lambda t, e: (layer, e, 0, 0)),
                  pl.BlockSpec((1, es, D, D_EXPERT), lambda t, e: (layer, e, 0, 0)),
                  pl.BlockSpec((1, es, D_EXPERT, D), lambda t, e: (layer, e, 0, 0)),
                  pl.BlockSpec((1, D), lambda t, e: (0, 0))],
        out_specs=out_specs,
        out_shape=out_shape,
        scratch_shapes=[pltpu.VMEM((MOE_TM, D), BF16), pltpu.VMEM((MOE_TM, COMB_LANES), F32),
                        pltpu.VMEM((MOE_TM, D), F32)],
        compiler_params=_cp(("arbitrary", "arbitrary")),
        name="moe",
    )(x, mods, g.reshape(1, D), router_w.T, router_b.reshape(N_EXPERTS, 1), w1, w3, w2, gf.reshape(1, D))


def _rw_in_kernel(x_ref, xp_ref, xn_ref, mod_ref, g_ref, mu_ref, wr_ref, wk_ref, wv_ref, g1_ref, g2_ref,
                  w1_ref, w2_ref, w0_ref, a1_ref, a2_ref, a0_ref, ka_ref, rk_ref, hsel_ref,
                  r_ref, k_ref, v_ref, gg_ref, lw_ref, a_ref, coef_ref, wrkv_sc):
    i = pl.program_id(0)

    @pl.when(i == 0)
    def _():
        for c, w_ref in enumerate((wr_ref, wk_ref, wv_ref)):
            wrkv_sc[c] = w_ref[0].astype(BF16)

    m = mod_ref[0]
    g = g_ref[...]
    h = _rms_mod(x_ref[...], g, m[1:2], m[0:1])
    hp = jnp.where(_seg_is_first(i), 0.0, _rms_mod(xp_ref[...], g, m[1:2], m[0:1]))
    hn = jnp.where(_seg_is_last(i), 0.0, _rms_mod(xn_ref[...], g, m[1:2], m[0:1]))
    ext = jnp.concatenate([hp, h, hn], axis=0)
    n = ext.shape[0]
    h_prev = pltpu.roll(ext, 1, 0)[8:8 + SEG]
    h_next = pltpu.roll(ext, n - 1, 0)[8:8 + SEG]
    xx = 0.5 * (h_prev + h_next) - h
    mu = mu_ref[...]
    xr, xw, xk, xv, xa, xg = [h + xx * mu[c:c + 1] for c in range(6)]
    r = _dot(xr, wrkv_sc[0])
    k = _dot(xk, wrkv_sc[1])
    r_ref[...] = r.astype(BF16)
    k_ref[...] = k.astype(BF16)
    v_ref[...] = _dot(xv, wrkv_sc[2]).astype(BF16)
    gg_ref[...] = _dot(_sigmoid(_dot(xg, g1_ref[...])), g2_ref[...]).astype(BF16)
    half_w_in = w0_ref[...] + _dot(jnp.tanh(_dot(xw, w1_ref[...])), w2_ref[...])
    lw_ref[...] = (-0.5 * W_DECAY_SCALE) * jnp.tanh(half_w_in) - 0.5 * W_DECAY_SCALE
    a = 0.5 * jnp.tanh(a0_ref[...] + _dot(_dot(xa, a1_ref[...]), a2_ref[...])) + 0.5
    a_ref[...] = a
    kd_sum = k * (2.0 + (a[:, 0:D] + a[:, D:2 * D] - 2.0) * ka_ref[...])
    coef_ref[...] = _dot_sel_rhs(r * kd_sum * rk_ref[...], hsel_ref[...])


def _rw_inproj(x, mods, g, mu, wr, wk, wv, g1, g2, w1c, w2bd, w0c, a1c, a2bd, a0c, ka, rk):
    full = lambda a: pl.BlockSpec(a.shape, lambda i: (0,) * a.ndim)
    once = lambda a: pl.BlockSpec((1,) + a.shape[1:], lambda i: (0,) * a.ndim, pipeline_mode=pl.Buffered(1))
    seg = lambda width: pl.BlockSpec((SEG, width), lambda i: (i, 0))
    outs = ([jax.ShapeDtypeStruct((M_TOK, D), BF16)] * 4 + [jax.ShapeDtypeStruct((M_TOK, 2 * D), F32)] * 2
            + [jax.ShapeDtypeStruct((M_TOK, COEF_LANES), F32)])
    hsel = jnp.asarray(np.arange(D)[:, None] // C_HEAD == np.arange(COEF_LANES)[None, :], BF16)
    return pl.pallas_call(
        _rw_in_kernel,
        grid=(NSEG,),
        in_specs=[seg(D),
                  pl.BlockSpec((8, D), lambda i: (_prev8(i), 0)),
                  pl.BlockSpec((8, D), lambda i: (_next8(i), 0)),
                  pl.BlockSpec((1, 6, D), lambda i: (i, 0, 0)),
                  full(g), full(mu), once(wr), once(wk), once(wv), full(g1), full(g2),
                  full(w1c), full(w2bd), full(w0c), full(a1c), full(a2bd), full(a0c),
                  full(ka), full(rk), full(hsel)],
        out_specs=[seg(D)] * 4 + [seg(2 * D)] * 2 + [seg(COEF_LANES)],
        out_shape=outs,
        scratch_shapes=[pltpu.VMEM((3, D, D), BF16)],
        compiler_params=_cp(("arbitrary",)),
        name="rwkv_inproj",
    )(x, x, x, mods, g, mu, wr, wk, wv, g1, g2, w1c, w2bd, w0c, a1c, a2bd, a0c, ka, rk, hsel)


def _rw_consts():
    t = np.arange(SEG)
    same = (t[:, None] // RW_CHUNK) == (t[None, :] // RW_CHUNK)
    tri_f = np.logical_and(same, t[None, :] <= t[:, None]).astype(np.float32)
    tri_b = np.logical_and(same, t[None, :] >= t[:, None]).astype(np.float32)
    return [jnp.asarray(a, BF16) for a in (tri_f, tri_b)]


def _rw_dir(r, k, v, lw, a, kk_g, ka_g, tri, rev):
    c = RW_CHUNK
    lane = lax.broadcasted_iota(jnp.int32, (SEG, 2 * C_HEAD), 1)
    head0 = lane < C_HEAD
    kx = k * kk_g
    ss = kx * kx
    n0 = jnp.sum(jnp.where(head0, ss, 0.0), axis=-1, keepdims=True)
    n1 = jnp.sum(jnp.where(head0, 0.0, ss), axis=-1, keepdims=True)
    kk = kx / jnp.maximum(jnp.sqrt(jnp.where(head0, n0, n1)), 1e-12)
    kd = k * (1.0 + (a - 1.0) * ka_g)
    bhat = kk * a
    cum = _dot_sel(tri, lw)
    e_incl = jnp.exp(cum)
    e_inv = jnp.exp(-cum)
    ae = -kk * jnp.exp(cum - lw)
    re = r * e_incl
    bi = bhat * e_inv
    ki = kd * e_inv

    chunks = []
    for ci in range(SEG // c):
        sl = slice(ci * c, (ci + 1) * c)
        ctot = cum[ci * c:ci * c + 1] if rev else cum[(ci + 1) * c - 1:(ci + 1) * c]
        dec = jnp.exp(ctot - cum[sl])
        chunks.append(dict(ae=ae[sl], re=re[sl], bi=bi[sl], ki=ki[sl], v=v[sl], bdec=bhat[sl] * dec,
                           kdec=kd[sl] * dec, gam=jnp.exp(ctot), rev=rev))
    return chunks


def _rw_transitions(chunks):
    c = RW_CHUNK
    w2 = 2 * C_HEAD
    h0c = lax.broadcasted_iota(jnp.int32, (c, w2), 1) < C_HEAD
    rowc = lax.broadcasted_iota(jnp.int32, (c, w2), 0)
    colc = jnp.bitwise_and(lax.broadcasted_iota(jnp.int32, (c, w2), 1), C_HEAD - 1)
    eye = (colc == rowc).astype(F32)
    bdmask = (lax.broadcasted_iota(jnp.int32, (w2, w2), 0) < C_HEAD) == (
        lax.broadcasted_iota(jnp.int32, (w2, w2), 1) < C_HEAD)
    keep0 = jnp.where(h0c, 1.0, 0.0).astype(BF16)
    keep1 = jnp.where(h0c, 0.0, 1.0).astype(BF16)

    def bd16(yb):
        return jnp.concatenate([yb * keep0, yb * keep1], axis=0)

    def dot_bd(x, pairs):
        blocks = [bd16(y.astype(BF16)) for y in pairs]
        rhs = blocks[0] if len(blocks) == 1 else jnp.concatenate(blocks, axis=1)
        return jnp.dot(x.astype(BF16), rhs, preferred_element_type=F32)

    mm_inv = lambda x, y: dot_bd(x, [y])

    n_ab, a_ak, a_rb, a_rk = [], [], [], []
    for ch in chunks:
        strict = (colc > rowc) if ch["rev"] else (colc < rowc)
        incl = (colc >= rowc) if ch["rev"] else (colc <= rowc)
        left = jnp.concatenate([ch["ae"], ch["re"]], axis=0)
        right = jnp.concatenate([jnp.where(h0c, ch["bi"], 0.0), jnp.where(h0c, 0.0, ch["bi"]),
                                 jnp.where(h0c, ch["ki"], 0.0), jnp.where(h0c, 0.0, ch["ki"])], axis=0)
        gm = _dot_nt(left, right)
        n_ab.append(jnp.where(strict, gm[0:c, 0:2 * c], 0.0))
        a_ak.append(jnp.where(strict, gm[0:c, 2 * c:4 * c], 0.0))
        a_rb.append(jnp.where(incl, gm[c:2 * c, 0:2 * c], 0.0))
        a_rk.append(jnp.where(incl, gm[c:2 * c, 2 * c:4 * c], 0.0))
    xorc = jnp.bitwise_xor(rowc, colc)
    tm = [eye + jnp.where(xorc < 2, n, 0.0) for n in n_ab]
    blk = 2
    while blk < c:
        couple = jnp.logical_and(xorc >= blk, xorc < 2 * blk)
        xs = [mm_inv(jnp.where(couple, n, 0.0), t) for n, t in zip(n_ab, tm)]
        tm = [t + mm_inv(t, x) for t, x in zip(tm, xs)]
        blk *= 2
    akv = [dot_bd(jnp.concatenate([x, y], axis=0), [ch["v"]])
           for x, y, ch in zip(a_ak, a_rk, chunks)]
    tav = [dot_bd(t, [ch["ae"], x[0:c]]) for t, ch, x in zip(tm, chunks, akv)]
    out = []
    for i, ch in enumerate(chunks):
        ta, tv = tav[i][:, 0:w2], tav[i][:, w2:2 * w2]
        qy = dot_bd(a_rb[i], [ta, tv])
        q = ch["re"] + qy[:, 0:w2]
        y = qy[:, w2:2 * w2] + akv[i][c:2 * c]
        wz = _dot_tn(tav[i], ch["bdec"])
        w = jnp.where(bdmask, wz[0:w2], 0.0)
        z = jnp.where(bdmask, wz[w2:2 * w2] + _dot_tn(ch["v"], ch["kdec"]), 0.0)
        out.append((q, y, w, z, ch["gam"]))
    return out


def _rw_scan_kernel(rf, kf, vf, lwf, af, rb, kb, vb, lwb, ab, kkg_ref, kag_ref, s0f, s0b, trif, trib,
                    of_ref, ob_ref, sf_out, sb_out, st_f, st_b):
    j = pl.program_id(1)
    w2 = 2 * C_HEAD
    dirs = ((False, (rf, kf, vf, lwf, af), s0f, trif, of_ref, sf_out, st_f),
            (True, (rb, kb, vb, lwb, ab), s0b, trib, ob_ref, sb_out, st_b))
    for rev, refs, s0, tri, o_ref, s_out, st in dirs:
        i = (NSEG - 1 - j) if rev else j
        starts = _seg_is_last(i) if rev else _seg_is_first(i)

        @pl.when(jnp.logical_and(starts, i < N_CTX_SEG))
        def _():
            st[...] = jnp.zeros_like(st)

        @pl.when(jnp.logical_and(starts, i >= N_CTX_SEG))
        def _():
            st[...] = s0[0, 0]

    chunks = []
    for rev, refs, s0, tri, o_ref, s_out, st in dirs:
        for pp in range(RW_PAIRS_STEP):
            lanes = slice(pp * w2, (pp + 1) * w2)
            r_, k_, v_, lw_, a_ = [z[:, lanes].astype(F32) for z in refs]
            chunks.extend(_rw_dir(r_, k_, v_, lw_, a_, kkg_ref[:, lanes], kag_ref[:, lanes], tri[...], rev))
    trans = _rw_transitions(chunks)
    n_c = SEG // RW_CHUNK
    states = [[st[pp] for pp in range(RW_PAIRS_STEP)] for (_, _, _, _, _, _, st) in dirs]
    for step in range(n_c):
        for d, (rev, refs, s0, tri, o_ref, s_out, st) in enumerate(dirs):
            ci = (n_c - 1 - step) if rev else step
            for pp in range(RW_PAIRS_STEP):
                q, y, w, z, gam = trans[(d * RW_PAIRS_STEP + pp) * n_c + ci]
                s = states[d][pp]
                o_ref[ci * RW_CHUNK:(ci + 1) * RW_CHUNK, pp * w2:(pp + 1) * w2] = (_dot_nt(q, s) + y).astype(BF16)
                states[d][pp] = s * gam + _dot(s, w) + z
    for d, (rev, refs, s0, tri, o_ref, s_out, st) in enumerate(dirs):
        for pp in range(RW_PAIRS_STEP):
            st[pp] = states[d][pp]
    for d, (rev, refs, s0, tri, o_ref, s_out, st) in enumerate(dirs):
        i = (NSEG - 1 - j) if rev else j

        @pl.when(i < N_CTX_SEG)
        def _():
            for pp in range(RW_PAIRS_STEP):
                s = states[d][pp]
                s_out[0, 2 * pp] = s[0:C_HEAD, 0:C_HEAD]
                s_out[0, 2 * pp + 1] = s[C_HEAD:w2, C_HEAD:w2]


def _rw_scan(r, k, v, lw, a, kk_g, ka_g, s0bd):
    consts = _rw_consts()
    w = 2 * C_HEAD
    pps = RW_PAIRS_STEP
    wb = pps * w
    n_steps = C_PAIRS // pps
    blk = lambda col0, rev: pl.BlockSpec(
        (SEG, wb), (lambda p, j: (NSEG - 1 - j, col0 + p)) if rev else (lambda p, j: (j, col0 + p)))
    cspec = lambda arr: pl.BlockSpec(arr.shape, lambda p, j: (0, 0))
    hps = 2 * pps
    fin_f = pl.BlockSpec((1, hps, C_HEAD, C_HEAD), lambda p, j: (jnp.minimum(j, N_CTX_SEG - 1), p, 0, 0))
    fin_b = pl.BlockSpec((1, hps, C_HEAD, C_HEAD),
                         lambda p, j: (jnp.minimum(NSEG - 1 - j, N_CTX_SEG - 1), p, 0, 0))
    return pl.pallas_call(
        _rw_scan_kernel,
        grid=(n_steps, NSEG),
        in_specs=[blk(0, False), blk(0, False), blk(0, False), blk(0, False), blk(0, False),
                  blk(0, True), blk(0, True), blk(0, True), blk(n_steps, True), blk(n_steps, True),
                  pl.BlockSpec((1, wb), lambda p, j: (0, p)), pl.BlockSpec((1, wb), lambda p, j: (0, p)),
                  pl.BlockSpec((1, 1, pps, w, w), lambda p, j: (_sample_of(j), 0, p, 0, 0)),
                  pl.BlockSpec((1, 1, pps, w, w), lambda p, j: (_sample_of(NSEG - 1 - j), 1, p, 0, 0)),
                  cspec(consts[0]), cspec(consts[1])],
        out_specs=[blk(0, False), blk(0, True), fin_f, fin_b],
        out_shape=[jax.ShapeDtypeStruct((M_TOK, D), BF16), jax.ShapeDtypeStruct((M_TOK, D), BF16),
                   jax.ShapeDtypeStruct((N_CTX_SEG, 2 * C_PAIRS, C_HEAD, C_HEAD), F32),
                   jax.ShapeDtypeStruct((N_CTX_SEG, 2 * C_PAIRS, C_HEAD, C_HEAD), F32)],
        scratch_shapes=[pltpu.VMEM((pps, w, w), F32), pltpu.VMEM((pps, w, w), F32)],
        compiler_params=_cp(("arbitrary", "arbitrary")),
        name="rwkv7_scan",
    )(r, k, v, lw, a, r, k, v, lw, a, kk_g, ka_g, s0bd, s0bd, *consts)


def _rw_out_kernel(x_ref, of_ref, ob_ref, v_ref, gg_ref, coef_ref, mod_ref, lnw_ref, lnb_ref, ones_ref,
                   hexp_ref, wo_ref, o_ref, wo_sc):
    @pl.when(pl.program_id(0) == 0)
    def _():
        wo_sc[...] = wo_ref[0].astype(BF16)

    m = mod_ref[0]
    ones_bd = ones_ref[...]
    w = 2 * C_HEAD
    inv_n = 1.0 / C_HEAD
    coef = _dot_sel_rhs(coef_ref[...], hexp_ref[...])
    parts = []
    for p in range(C_PAIRS):
        cs = slice(p * w, (p + 1) * w)
        osum = of_ref[:, cs].astype(F32) + ob_ref[:, cs].astype(F32)
        mu = _dot_sel_rhs(osum, ones_bd) * inv_n
        cen = osum - mu
        var = _dot_sel_rhs(cen * cen, ones_bd) * inv_n
        o = cen * lax.rsqrt(var + GN_EPS) * lnw_ref[:, cs] + lnb_ref[:, cs]
        bonus = coef[:, cs] * v_ref[:, cs].astype(F32)
        parts.append((o + bonus) * gg_ref[:, cs].astype(F32))
    y = _dot(jnp.concatenate(parts, axis=-1), wo_sc[...])
    o_ref[...] = x_ref[...] + m[2:3] * y


def _dot_sel_rhs(x, mat):
    h = x.astype(BF16)
    l = (x - h.astype(F32)).astype(BF16)
    return jnp.dot(h, mat, preferred_element_type=F32) + jnp.dot(l, mat, preferred_element_type=F32)


def _rw_out(x, o_f, o_b, v, gg, coef, mods, lnw, lnb, wo):
    seg = lambda width: pl.BlockSpec((SEG, width), lambda i: (i, 0))
    row = pl.BlockSpec((1, D), lambda i: (0, 0))
    hh = np.arange(2 * C_HEAD) // C_HEAD
    ones_bd = jnp.asarray((hh[:, None] == hh[None, :]).astype(np.float32), BF16)
    hexp = jnp.asarray(np.arange(COEF_LANES)[:, None] == np.arange(D)[None, :] // C_HEAD, BF16)
    return pl.pallas_call(
        _rw_out_kernel,
        grid=(NSEG,),
        in_specs=[seg(D), seg(D), seg(D), seg(D), seg(D), seg(COEF_LANES),
                  pl.BlockSpec((1, 6, D), lambda i: (i, 0, 0)),
                  row, row,
                  pl.BlockSpec((2 * C_HEAD, 2 * C_HEAD), lambda i: (0, 0)),
                  pl.BlockSpec((COEF_LANES, D), lambda i: (0, 0)),
                  pl.BlockSpec((1, D, D), lambda i: (0, 0, 0), pipeline_mode=pl.Buffered(1))],
        out_specs=seg(D),
        out_shape=jax.ShapeDtypeStruct((M_TOK, D), F32),
        scratch_shapes=[pltpu.VMEM((D, D), BF16)],
        compiler_params=_cp(("arbitrary",)),
        name="rwkv_out",
    )(x, o_f, o_b, v, gg, coef, mods, lnw.reshape(1, D), lnb.reshape(1, D), ones_bd, hexp, wo)


def _grid_pos_table(n_tok):
    rows = n_tok // GRID_W
    r, cl = np.meshgrid(np.arange(rows, dtype=np.float32), np.arange(GRID_W, dtype=np.float32), indexing='ij')
    quarter = D // 4
    omega = (1.0 / (np.float32(POS_BASE) ** (np.arange(quarter, dtype=np.float32) / np.float32(quarter))))
    ang_r = (r.reshape(-1, 1) * omega).astype(np.float32)
    ang_c = (cl.reshape(-1, 1) * omega).astype(np.float32)
    table = np.concatenate([np.sin(ang_r), np.cos(ang_r), np.sin(ang_c), np.cos(ang_c)], axis=-1)
    return jnp.asarray(table.astype(np.float32))


def _block_diag(blocks):
    g, n, _ = blocks.shape
    eye = jnp.eye(g, dtype=blocks.dtype)
    return (eye[:, None, :, None] * blocks[:, :, None, :]).reshape(g * n, g * n)


def kernel(x_prompt, x_sample, state_hgrn, state_rglru, state_rwkv, c, c_ctx, norm_mix_g, norm_ffn_g, w_mod, b_mod, ab_w_in, ab_w_out, hgrn_lb, hgrn_norm_g, rg_conv_w, rg_conv_b, rg_wa, rg_ba, rg_wx, rg_bx, rg_lambda, rw_mu, rw_wr, rw_wk, rw_wv, rw_wo, rw_w0, rw_w1, rw_w2, rw_a0, rw_a1, rw_a2, rw_g1, rw_g2, rw_kk, rw_ka, rw_rk, rw_lnw, rw_lnb, moe_router, moe_router_bias, moe_w1, moe_w3, moe_w2, norm_f_g):
    bf = lambda z: z.astype(BF16)
    xp = x_prompt.reshape(-1, D)
    xs = x_sample.reshape(-1, D)
    pos = _grid_pos_table(x_sample.shape[1])
    mods = _modulations(c, c_ctx, w_mod, b_mod)

    lower_bounds = jnp.cumsum(jax.nn.softmax(hgrn_lb.astype(F32), axis=1), axis=1)
    lb = lower_bounds[:, 0].reshape(2, A_HEADS, 1, A_DK)
    proj_f, proj_h = _ab_inproj(xp, xs, pos, mods[0], norm_mix_g[0], ab_w_in)
    s0t = jnp.swapaxes(state_hgrn[:, 0], -1, -2)
    o_f, o_b, new_hgrn = _gla(proj_f, proj_h, lb, s0t)
    wa_bd = bf(jnp.stack([_block_diag(rg_wa[0, d]) for d in range(2)]))
    wx_bd = bf(jnp.stack([_block_diag(rg_wx[0, d]) for d in range(2)]))
    h_f, h_b, lru_f, lru_b = _rglru(
        proj_f, rg_conv_w[0], rg_conv_b[0].reshape(1, D_B), wa_bd, rg_ba[0].reshape(2, 1, D_B), wx_bd,
        rg_bx[0].reshape(2, 1, D_B), rg_lambda[0].reshape(2, 1, D_B), state_rglru[:, 0].reshape(-1, 2, 1, D_B))
    x = _ab_out(xp, xs, pos, proj_h, o_f, o_b, h_f, h_b, mods[0], hgrn_norm_g[0], ab_w_out)
    x, = _moe(x, mods[0], norm_ffn_g[0], moe_router, moe_router_bias, moe_w1, moe_w3, moe_w2, 0, norm_f_g, False)

    w1c = bf(jnp.concatenate([rw_w1[0, 0], rw_w1[0, 1]], axis=-1))
    a1c = bf(jnp.concatenate([rw_a1[0, 0], rw_a1[0, 1]], axis=-1))
    half_w2, half_a2 = 0.5 * rw_w2[0], 0.5 * rw_a2[0]
    w2bd = bf(jnp.concatenate([jnp.concatenate([half_w2[0], jnp.zeros_like(half_w2[0])], axis=-1),
                               jnp.concatenate([jnp.zeros_like(half_w2[1]), half_w2[1]], axis=-1)], axis=0))
    a2bd = bf(jnp.concatenate([jnp.concatenate([half_a2[0], jnp.zeros_like(half_a2[0])], axis=-1),
                               jnp.concatenate([jnp.zeros_like(half_a2[1]), half_a2[1]], axis=-1)], axis=0))
    r, k, v, gg, lw, a, coef = _rw_inproj(
        x, mods[1], norm_mix_g[1].reshape(1, D), rw_mu[0], rw_wr, rw_wk, rw_wv,
        bf(rw_g1[0]), bf(rw_g2[0]), w1c, w2bd, 0.5 * rw_w0[0].reshape(1, 2 * D), a1c, a2bd,
        0.5 * rw_a0[0].reshape(1, 2 * D), rw_ka[0].reshape(1, D), rw_rk[0].reshape(1, D))
    s0 = state_rwkv[:, 0].reshape(N_SAMPLE, 2, C_PAIRS, 2, C_HEAD, C_HEAD)
    zeros = jnp.zeros_like(s0[:, :, :, 0])
    s0bd = jnp.concatenate([jnp.concatenate([s0[:, :, :, 0], zeros], axis=-1),
                            jnp.concatenate([zeros, s0[:, :, :, 1]], axis=-1)], axis=-2)
    ow_f, ow_b, rs_f, rs_b = _rw_scan(r, k, v, lw, a, rw_kk[0].reshape(1, D), rw_ka[0].reshape(1, D), s0bd)
    x = _rw_out(x, ow_f, ow_b, v, gg, coef, mods[1], rw_lnw[0], rw_lnb[0], rw_wo)
    y_p, y_s = _moe(x, mods[1], norm_ffn_g[1], moe_router, moe_router_bias, moe_w1, moe_w3, moe_w2, 1, norm_f_g, True)

    new_rglru = jnp.stack([lru_f[:, 0], lru_b[:, 0]], axis=1)[:, None]
    new_rwkv = jnp.stack([rs_f, rs_b], axis=1)[:, None]
    return (y_p.reshape(x_prompt.shape), y_s.reshape(x_sample.shape), new_hgrn, new_rglru, new_rwkv)
```

```python
import functools
import math

import numpy as np
import jax
import jax.numpy as jnp
from jax import lax
from jax.experimental import pallas as pl
from jax.experimental.pallas import tpu as pltpu

F32 = jnp.float32
BF16 = jnp.bfloat16

D = 1024
SEG = 256
N_CTX_SEG = 16
SEG_PER_SAMPLE = 4
N_SAMPLE = 4
NSEG = N_CTX_SEG + N_SAMPLE * SEG_PER_SAMPLE
M_TOK = NSEG * SEG
SUBLANES = 8
ROWS8_PER_SEG = SEG // SUBLANES

A_HEADS = 4
A_DK = 128
D_A = 512
D_B = 512
B_BLOCKS = 8
B_BLOCK = 64
LRU_C = 8.0
D_IN_AB = 5 * D_A + 2 * D_B
AB_F32_COLS = 2 * D_A + D_B
C_HEAD = 64
C_PAIRS = 8
RW_CHUNK = 64
COEF_LANES = 128
RW_PAIRS_STEP = 4
W_DECAY_SCALE = math.exp(-0.5)
LOG2E = math.log2(math.e)
N_EXPERTS = 16
N_GROUPS = 4
GROUP = 4
D_EXPERT = 256
RMS_EPS = 1e-6
GN_EPS = 64e-5
POS_BASE = 10000.0
GRID_W = 64
MOE_TM = 1024
MOE_CTX_TILES = N_CTX_SEG * SEG // MOE_TM
MOE_EXPERTS_STEP = 4
MOE_EXPERTS_STEP_FINAL = 2
COMB_LANES = 128
GLA_LEVELS = (1, 2, 4, 8, 16, 32, 64, 128)
GLA_HALF = 128
GLA_HEADS_STEP = 4

VMEM_LIMIT = 56 * 1024 * 1024


def _cp(sem):
    return pltpu.CompilerParams(dimension_semantics=sem, vmem_limit_bytes=VMEM_LIMIT)


def _sigmoid(x):
    return 0.5 * jnp.tanh(0.5 * x) + 0.5


def _silu(x):
    return x * _sigmoid(x)


def _gelu_tanh(x):
    return 0.5 * x * (1.0 + jnp.tanh(math.sqrt(2.0 / math.pi) * (x + 0.044715 * (x * x * x))))


def _rms_mod(x, g, scale, shift):
    ms = jnp.mean(x * x, axis=-1, keepdims=True)
    return x * lax.rsqrt(ms + RMS_EPS) * (g * (1.0 + scale)) + shift


def _dot(a, b):
    return jnp.dot(a.astype(BF16), b.astype(BF16), preferred_element_type=F32)


def _dot_nt(a, b):
    return lax.dot_general(a.astype(BF16), b.astype(BF16), (((1,), (1,)), ((), ())),
                           preferred_element_type=F32)


def _dot_tn(a, b):
    return lax.dot_general(a.astype(BF16), b.astype(BF16), (((0,), (0,)), ((), ())),
                           preferred_element_type=F32)


def _split3(x):
    h = x.astype(BF16)
    r1 = x - h.astype(F32)
    m = r1.astype(BF16)
    r2 = r1 - m.astype(F32)
    return h, m, r2.astype(BF16)


def _dot_sel(mat, x):
    h, m, l = _split3(x)
    return (jnp.dot(mat, h, preferred_element_type=F32) + jnp.dot(mat, m, preferred_element_type=F32)
            + jnp.dot(mat, l, preferred_element_type=F32))


def _dot_x3_nt(a, b):
    dn = (((1,), (1,)), ((), ()))
    ah = a.astype(BF16)
    al = (a - ah.astype(F32)).astype(BF16)
    bh = b.astype(BF16)
    bl = (b - bh.astype(F32)).astype(BF16)
    return (lax.dot_general(ah, bh, dn, preferred_element_type=F32)
            + lax.dot_general(ah, bl, dn, preferred_element_type=F32)
            + lax.dot_general(al, bh, dn, preferred_element_type=F32))


def _seg_is_first(i):
    return jnp.logical_or(i < N_CTX_SEG, lax.rem(i - N_CTX_SEG, SEG_PER_SAMPLE) == 0)


def _seg_is_last(i):
    return jnp.logical_or(i < N_CTX_SEG, lax.rem(i - N_CTX_SEG, SEG_PER_SAMPLE) == SEG_PER_SAMPLE - 1)


def _sample_of(i):
    return jnp.maximum(i - N_CTX_SEG, 0) // SEG_PER_SAMPLE


def _prev8(i):
    return jnp.maximum(i * ROWS8_PER_SEG - 1, 0)


def _next8(i):
    return jnp.minimum((i + 1) * ROWS8_PER_SEG, M_TOK // 8 - 1)


def _mod_kernel(cv_ref, w_ref, b_ref, o_ref):
    cv = cv_ref[...]
    o_ref[0] = _dot(_silu(cv), w_ref[0]) + b_ref[0]


def _modulations(c, c_ctx, w_mod, b_mod):
    depth = w_mod.shape[0]
    cv = jnp.concatenate([c_ctx[None, :], c, jnp.zeros((3, D), F32)], axis=0)
    wt = 2 * D
    mod = pl.pallas_call(
        _mod_kernel,
        grid=(depth, 6 * D // wt),
        in_specs=[pl.BlockSpec((8, D), lambda l, n: (0, 0)),
                  pl.BlockSpec((1, D, wt), lambda l, n: (l, 0, n)),
                  pl.BlockSpec((1, 1, wt), lambda l, n: (l, 0, n))],
        out_specs=pl.BlockSpec((1, 8, wt), lambda l, n: (l, 0, n)),
        out_shape=jax.ShapeDtypeStruct((depth, 8, 6 * D), F32),
        compiler_params=_cp(("arbitrary", "arbitrary")),
        name="adaln_mod",
    )(cv, w_mod, b_mod.reshape(depth, 1, 6 * D))
    row_of_seg = np.array([0] * N_CTX_SEG + [1 + s // SEG_PER_SAMPLE for s in range(N_SAMPLE * SEG_PER_SAMPLE)])
    return mod[:, row_of_seg].reshape(depth, NSEG, 6, D)


def _x0_specs():
    return [pl.BlockSpec((SEG, D), lambda i: (jnp.minimum(i, N_CTX_SEG - 1), 0)),
            pl.BlockSpec((SEG, D), lambda i: (jnp.maximum(i - N_CTX_SEG, 0), 0)),
            pl.BlockSpec((SEG, D), lambda i: (lax.rem(jnp.maximum(i - N_CTX_SEG, 0), SEG_PER_SAMPLE), 0))]


def _x0(i, xp_ref, xs_ref, pos_ref):
    return jnp.where(i < N_CTX_SEG, xp_ref[...], xs_ref[...] + pos_ref[...])


def _ab_in_kernel(xp_ref, xs_ref, pos_ref, mod_ref, g_ref, w_ref, of_ref, oh_ref, w_sc):
    @pl.when(pl.program_id(0) == 0)
    def _():
        w_sc[...] = w_ref[0].astype(BF16)

    m = mod_ref[0]
    x = _x0(pl.program_id(0), xp_ref, xs_ref, pos_ref)
    h = _rms_mod(x, g_ref[...], m[1:2], m[0:1])
    res = jnp.dot(h.astype(BF16), w_sc[...], preferred_element_type=F32)
    of_ref[:, 0:2 * D_A] = res[:, D_A:3 * D_A]
    of_ref[:, 2 * D_A:AB_F32_COLS] = res[:, 5 * D_A:5 * D_A + D_B]
    oh_ref[:, 0:D_A] = res[:, 0:D_A].astype(BF16)
    oh_ref[:, D_A:3 * D_A] = res[:, 3 * D_A:5 * D_A].astype(BF16)
    oh_ref[:, 3 * D_A:3 * D_A + D_B] = res[:, 5 * D_A + D_B:D_IN_AB].astype(BF16)


def _ab_inproj(xp, xs, pos, mods, g, w_in):
    return pl.pallas_call(
        _ab_in_kernel,
        grid=(NSEG,),
        in_specs=_x0_specs() + [pl.BlockSpec((1, 6, D), lambda i: (i, 0, 0)),
                                pl.BlockSpec((1, D), lambda i: (0, 0)),
                                pl.BlockSpec((1, D, D_IN_AB), lambda i: (0, 0, 0),
                                             pipeline_mode=pl.Buffered(1))],
        out_specs=[pl.BlockSpec((SEG, AB_F32_COLS), lambda i: (i, 0)),
                   pl.BlockSpec((SEG, D_IN_AB - AB_F32_COLS), lambda i: (i, 0))],
        out_shape=[jax.ShapeDtypeStruct((M_TOK, AB_F32_COLS), F32),
                   jax.ShapeDtypeStruct((M_TOK, D_IN_AB - AB_F32_COLS), BF16)],
        scratch_shapes=[pltpu.VMEM((D, D_IN_AB), BF16)],
        compiler_params=_cp(("arbitrary",)),
        name="ab_inproj",
    )(xp, xs, pos, mods, g.reshape(1, D), w_in)


def _gla_consts():
    t = np.arange(SEG)
    tri_f = (t[None, :] <= t[:, None]).astype(np.float32)
    tri_b = (t[None, :] >= t[:, None]).astype(np.float32)
    th = np.arange(GLA_HALF)
    xor = th[:, None] ^ th[None, :]
    hb = np.where(xor > 0, 1 << np.floor(np.log2(np.maximum(xor, 1))).astype(np.int64), 0)
    code_f = np.where(th[None, :] < th[:, None], hb, 0).astype(np.int32)
    code_b = np.where(th[None, :] > th[:, None], hb, 0).astype(np.int32)
    return [jnp.asarray(tri_f, BF16), jnp.asarray(tri_b, BF16), jnp.asarray(code_f), jnp.asarray(code_b)]


def _gla_level_operand(q, k, b, g, rowi, w, rev):
    upper = jnp.bitwise_and(rowi, w) != 0
    qside = jnp.logical_not(upper) if rev else upper
    if w == 1:
        z = jnp.where(qside, g, 0.0)
    elif w >= SUBLANES:
        nv = 2 * w // SUBLANES
        b4 = b.reshape(SEG // (2 * w), nv, SUBLANES, A_DK)
        ref = (b4[:, nv // 2:nv // 2 + 1, 0:1, :] if rev
               else b4[:, nv // 2 - 1:nv // 2, SUBLANES - 1:SUBLANES, :])
        x = (b4 - ref).reshape(SEG, A_DK)
        z = jnp.where(qside, x, -x)
    else:
        b3 = b.reshape(SEG // SUBLANES, SUBLANES, A_DK)
        sub = lax.broadcasted_iota(jnp.int32, b3.shape, 1)
        beta = None
        for jb in range(SUBLANES // (2 * w)):
            r = jb * 2 * w + (w if rev else w - 1)
            cand = jnp.broadcast_to(b3[:, r:r + 1, :], b3.shape)
            beta = cand if beta is None else jnp.where(sub >= jb * 2 * w, cand, beta)
        x = (b3 - beta).reshape(SEG, A_DK)
        z = jnp.where(qside, x, -x)
    return jnp.where(qside, q, k) * jnp.exp2(z)


def _gla_dir(qraw, fraw, v, lb, st, tri, code, rev):
    hh = GLA_HALF
    q = _silu(qraw)
    f = lb + (1.0 - lb) * _sigmoid(fraw)
    g = jnp.log2(f)
    k = 1.0 - f
    b = _dot_sel(tri, g)
    rowi = lax.broadcasted_iota(jnp.int32, (SEG, A_DK), 0)
    att = [jnp.zeros((hh, hh), F32), jnp.zeros((hh, hh), F32)]
    cross = None
    for w in GLA_LEVELS:
        m = _gla_level_operand(q, k, b, g, rowi, w, rev).astype(BF16)
        if w == hh:
            cross = _dot_nt(m[0:hh], m[hh:SEG]) if rev else _dot_nt(m[hh:SEG], m[0:hh])
        else:
            for half in range(2):
                mh = m[half * hh:(half + 1) * hh]
                att[half] = jnp.where(code == w, _dot_nt(mh, mh), att[half])
    if rev:
        o_lo = _dot(jnp.concatenate([att[0], cross], axis=1), v)
        o_hi = _dot(att[1], v[hh:SEG])
    else:
        o_lo = _dot(att[0], v[0:hh])
        o_hi = _dot(jnp.concatenate([cross, att[1]], axis=1), v)
    diag = jnp.sum(q * k, axis=-1, keepdims=True)
    o = jnp.concatenate([o_lo, o_hi], axis=0) + diag * v + _dot_nt(q * jnp.exp2(b), st)
    btot = b[0:1] if rev else b[SEG - 1:SEG]
    st_new = st * jnp.exp2(btot) + _dot_tn(v, k * jnp.exp2(btot - b))
    return o, st_new


def _gla_kernel(qf, ff, vf, qb, fb, vb, lb_ref, s0f, s0b, trif, trib, codef, codeb,
                of_ref, ob_ref, fin_ref, st_f, st_b):
    j = pl.program_id(1)
    dirs = ((False, qf, ff, vf, s0f, trif, codef, of_ref, st_f),
            (True, qb, fb, vb, s0b, trib, codeb, ob_ref, st_b))
    for rev, qr, fr, vr, s0, tri, code, o_ref, st in dirs:
        i = (NSEG - 1 - j) if rev else j
        starts = _seg_is_last(i) if rev else _seg_is_first(i)

        @pl.when(jnp.logical_and(starts, i < N_CTX_SEG))
        def _():
            st[...] = jnp.zeros_like(st)

        @pl.when(jnp.logical_and(starts, i >= N_CTX_SEG))
        def _():
            st[...] = s0[0, 0]

    finals = []
    for d, (rev, qr, fr, vr, s0, tri, code, o_ref, st) in enumerate(dirs):
        for hh in range(GLA_HEADS_STEP):
            lanes = slice(hh * A_DK, (hh + 1) * A_DK)
            o, st_new = _gla_dir(qr[:, lanes].astype(F32), fr[:, lanes], vr[:, lanes].astype(F32),
                                 lb_ref[d, hh], st[hh], tri[...], code[...], rev)
            o_ref[:, lanes] = o.astype(BF16)
            finals.append(st_new)
    for d, (rev, qr, fr, vr, s0, tri, code, o_ref, st) in enumerate(dirs):
        for hh in range(GLA_HEADS_STEP):
            st[hh] = finals[d * GLA_HEADS_STEP + hh]
    for d, (rev, qr, fr, vr, s0, tri, code, o_ref, st) in enumerate(dirs):
        i = (NSEG - 1 - j) if rev else j

        @pl.when(i < N_CTX_SEG)
        def _():
            for hh in range(GLA_HEADS_STEP):
                fin_ref[i, 0, d, pl.program_id(0) * GLA_HEADS_STEP + hh] = finals[d * GLA_HEADS_STEP + hh].T


def _gla(proj_f, proj_h, lb, s0t):
    consts = _gla_consts()
    hs = GLA_HEADS_STEP
    wb = hs * A_DK
    n_col = D_A // wb
    blk = lambda col0, rev: pl.BlockSpec(
        (SEG, wb), (lambda h, j: (NSEG - 1 - j, col0 + h)) if rev else (lambda h, j: (j, col0 + h)))
    cspec = lambda a: pl.BlockSpec(a.shape, lambda h, j: (0, 0))
    fin_shape = (N_CTX_SEG, 1, 2, A_HEADS, A_DK, A_DK)
    fin = pl.BlockSpec(fin_shape, lambda h, j: (0,) * len(fin_shape))
    return pl.pallas_call(
        _gla_kernel,
        grid=(A_HEADS // hs, NSEG),
        in_specs=[blk(0, False), blk(0, False), blk(n_col, False),
                  blk(0, True), blk(n_col, True), blk(n_col, True),
                  pl.BlockSpec((2, hs, 1, A_DK), lambda h, j: (0, h, 0, 0)),
                  pl.BlockSpec((1, 1, hs, A_DK, A_DK), lambda h, j: (_sample_of(j), 0, h, 0, 0)),
                  pl.BlockSpec((1, 1, hs, A_DK, A_DK), lambda h, j: (_sample_of(NSEG - 1 - j), 1, h, 0, 0)),
                  cspec(consts[0]), cspec(consts[1]), cspec(consts[2]), cspec(consts[3])],
        out_specs=[blk(0, False), blk(0, True), fin],
        out_shape=[jax.ShapeDtypeStruct((M_TOK, D_A), BF16), jax.ShapeDtypeStruct((M_TOK, D_A), BF16),
                   jax.ShapeDtypeStruct(fin_shape, F32)],
        scratch_shapes=[pltpu.VMEM((hs, A_DK, A_DK), F32), pltpu.VMEM((hs, A_DK, A_DK), F32)],
        compiler_params=_cp(("arbitrary", "arbitrary")),
        name="hgrn2_gla",
    )(proj_h, proj_f, proj_h, proj_h, proj_f, proj_h, lb, s0t, s0t, *consts)


def _lin_scan(a, b, h_in, rev):
    t_len, c = a.shape
    ng = t_len // SUBLANES
    a3 = a.reshape(ng, SUBLANES, c)
    b3 = b.reshape(ng, SUBLANES, c)
    sub = lax.broadcasted_iota(jnp.int32, a3.shape, 1)
    s = 1
    while s < SUBLANES:
        shift = (SUBLANES - s) if rev else s
        valid = (sub < SUBLANES - s) if rev else (sub >= s)
        ap = jnp.where(valid, pltpu.roll(a3, shift, 1), 1.0)
        bp = jnp.where(valid, pltpu.roll(b3, shift, 1), 0.0)
        b3 = a3 * bp + b3
        a3 = a3 * ap
        s *= 2
    hs = [None] * ng
    carry = h_in
    for j in (range(ng - 1, -1, -1) if rev else range(ng)):
        hs[j] = a3[j] * carry + b3[j]
        carry = hs[j][0:1] if rev else hs[j][SUBLANES - 1:SUBLANES]
    return jnp.concatenate(hs, axis=0), carry


def _rglru_dir(x, xprev, xnext, first, last, cw, cb, wa, ba, wx, bx, lam, h_in, rev):
    zero = jnp.zeros_like(xprev)
    ext = jnp.concatenate([jnp.where(first, zero, xprev), x, jnp.where(last, zero, xnext)], axis=0)
    n = ext.shape[0]
    xm2 = pltpu.roll(ext, 2, 0)[8:8 + SEG]
    xm1 = pltpu.roll(ext, 1, 0)[8:8 + SEG]
    xp1 = pltpu.roll(ext, n - 1, 0)[8:8 + SEG]
    xc = cb + xm2 * cw[0:1] + xm1 * cw[1:2] + x * cw[2:3] + xp1 * cw[3:4]
    gate_r = _sigmoid(_dot(xc, wa) + ba)
    gate_i = _sigmoid(_dot(xc, wx) + bx)
    softplus_neg_lam = jnp.maximum(-lam, 0.0) + jnp.log(1.0 + jnp.exp(-jnp.abs(lam)))
    a = jnp.exp2((-LRU_C * LOG2E) * gate_r * softplus_neg_lam)
    b_in = jnp.sqrt(1.0 - a * a) * gate_i * xc
    return _lin_scan(a, b_in, h_in, rev)


def _rglru_kernel(xf, xf_p, xf_n, xb, xb_p, xb_n, cw_ref, cb_ref, wa_ref, ba_ref, wx_ref, bx_ref, lam_ref,
                  s0f, s0b, hf_ref, hb_ref, ff_out, fb_out, hc_f, hc_b):
    j = pl.program_id(0)
    dirs = ((False, xf, xf_p, xf_n, s0f, hf_ref, ff_out, hc_f),
            (True, xb, xb_p, xb_n, s0b, hb_ref, fb_out, hc_b))
    for rev, xr, xp, xn, s0, h_ref, f_out, hc in dirs:
        i = (NSEG - 1 - j) if rev else j
        starts = _seg_is_last(i) if rev else _seg_is_first(i)

        @pl.when(jnp.logical_and(starts, i < N_CTX_SEG))
        def _():
            hc[...] = jnp.zeros_like(hc)

        @pl.when(jnp.logical_and(starts, i >= N_CTX_SEG))
        def _():
            hc[...] = s0[0, 0]

    outs = []
    for d, (rev, xr, xp, xn, s0, h_ref, f_out, hc) in enumerate(dirs):
        i = (NSEG - 1 - j) if rev else j
        h, h_out = _rglru_dir(xr[...], xp[...], xn[...], _seg_is_first(i), _seg_is_last(i), cw_ref[...],
                              cb_ref[...], wa_ref[d], ba_ref[d], wx_ref[d], bx_ref[d], lam_ref[d], hc[...], rev)
        h_ref[...] = h.astype(BF16)
        outs.append(h_out)
    for d, (rev, xr, xp, xn, s0, h_ref, f_out, hc) in enumerate(dirs):
        hc[...] = outs[d]
    for d, (rev, xr, xp, xn, s0, h_ref, f_out, hc) in enumerate(dirs):
        i = (NSEG - 1 - j) if rev else j

        @pl.when(i < N_CTX_SEG)
        def _():
            f_out[0] = outs[d]


def _rglru(proj, conv_w, conv_b, wa_bd, ba, wx_bd, bx, lam, s0):
    xcol = 2 * D_A // D_B
    fwd = lambda f: (lambda j: f(j))
    bwd = lambda f: (lambda j: f(NSEG - 1 - j))
    seg_blk = lambda m: pl.BlockSpec((SEG, D_B), m(lambda i: (i, xcol)))
    prev_blk = lambda m: pl.BlockSpec((8, D_B), m(lambda i: (_prev8(i), xcol)))
    next_blk = lambda m: pl.BlockSpec((8, D_B), m(lambda i: (_next8(i), xcol)))
    full = lambda a: pl.BlockSpec(a.shape, lambda j: (0,) * a.ndim)
    return pl.pallas_call(
        _rglru_kernel,
        grid=(NSEG,),
        in_specs=[seg_blk(fwd), prev_blk(fwd), next_blk(fwd), seg_blk(bwd), prev_blk(bwd), next_blk(bwd),
                  full(conv_w), full(conv_b), full(wa_bd), full(ba), full(wx_bd), full(bx), full(lam),
                  pl.BlockSpec((1, 1, 1, D_B), lambda j: (_sample_of(j), 0, 0, 0)),
                  pl.BlockSpec((1, 1, 1, D_B), lambda j: (_sample_of(NSEG - 1 - j), 1, 0, 0))],
        out_specs=[pl.BlockSpec((SEG, D_B), lambda j: (j, 0)),
                   pl.BlockSpec((SEG, D_B), lambda j: (NSEG - 1 - j, 0)),
                   pl.BlockSpec((1, 1, D_B), lambda j: (jnp.minimum(j, N_CTX_SEG - 1), 0, 0)),
                   pl.BlockSpec((1, 1, D_B), lambda j: (jnp.minimum(NSEG - 1 - j, N_CTX_SEG - 1), 0, 0))],
        out_shape=[jax.ShapeDtypeStruct((M_TOK, D_B), BF16), jax.ShapeDtypeStruct((M_TOK, D_B), BF16),
                   jax.ShapeDtypeStruct((N_CTX_SEG, 1, D_B), F32), jax.ShapeDtypeStruct((N_CTX_SEG, 1, D_B), F32)],
        scratch_shapes=[pltpu.VMEM((1, D_B), F32), pltpu.VMEM((1, D_B), F32)],
        compiler_params=_cp(("arbitrary",)),
        name="rglru",
    )(proj, proj, proj, proj, proj, proj, conv_w, conv_b, wa_bd, ba, wx_bd, bx, lam, s0, s0)


def _ab_out_kernel(xp_ref, xs_ref, pos_ref, of_ref, ob_ref, og_ref, hf_ref, hb_ref, yr_ref, mod_ref, hg_ref,
                   w_ref, o_ref, w_sc):
    @pl.when(pl.program_id(0) == 0)
    def _():
        w_sc[...] = w_ref[0].astype(BF16)

    m = mod_ref[0]
    f32 = lambda ref: ref[...].astype(F32)
    oa = f32(of_ref) + f32(ob_ref)
    hg = hg_ref[...]
    parts = []
    for h in range(A_HEADS):
        z = oa[:, h * A_DK:(h + 1) * A_DK]
        parts.append(z * lax.rsqrt(jnp.mean(z * z, axis=-1, keepdims=True) + RMS_EPS) * hg)
    o_a = jnp.concatenate(parts, axis=-1) * _silu(f32(og_ref))
    o_b = (f32(hf_ref) + f32(hb_ref)) * _gelu_tanh(f32(yr_ref))
    y = _dot(o_a, w_sc[0:D_A]) + _dot(o_b, w_sc[D_A:D_A + D_B])
    o_ref[...] = _x0(pl.program_id(0), xp_ref, xs_ref, pos_ref) + m[2:3] * y


def _ab_out(xp, xs, pos, proj_h, o_f, o_b, h_f, h_b, mods, hg, w_out):
    seg = lambda width, col: pl.BlockSpec((SEG, width), lambda i: (i, col))
    return pl.pallas_call(
        _ab_out_kernel,
        grid=(NSEG,),
        in_specs=_x0_specs() + [seg(D_A, 0), seg(D_A, 0), seg(D_A, 2), seg(D_B, 0), seg(D_B, 0), seg(D_B, 3),
                                pl.BlockSpec((1, 6, D), lambda i: (i, 0, 0)),
                                pl.BlockSpec((1, A_DK), lambda i: (0, 0)),
                                pl.BlockSpec((1, D_A + D_B, D), lambda i: (0, 0, 0),
                                             pipeline_mode=pl.Buffered(1))],
        out_specs=seg(D, 0),
        out_shape=jax.ShapeDtypeStruct((M_TOK, D), F32),
        scratch_shapes=[pltpu.VMEM((D_A + D_B, D), BF16)],
        compiler_params=_cp(("arbitrary",)),
        name="ab_out",
    )(xp, xs, pos, o_f, o_b, proj_h, h_f, h_b, proj_h, mods, hg.reshape(1, A_DK), w_out)


def _route(scores, sel):
    cols = [sel[e:e + 1, :] for e in range(N_EXPERTS)]

    def rank(vals):
        out = []
        for i, vi in enumerate(vals):
            r = None
            for jx, vj in enumerate(vals):
                if jx == i:
                    continue
                beats = (vj >= vi) if jx < i else (vj > vi)
                r = beats.astype(F32) if r is None else r + beats.astype(F32)
            out.append(r)
        return out

    grp_scores, in_top2 = [], []
    for gi in range(N_GROUPS):
        vals = cols[gi * GROUP:(gi + 1) * GROUP]
        best_pair = None
        for a in range(GROUP):
            for bx in range(a + 1, GROUP):
                s = vals[a] + vals[bx]
                best_pair = s if best_pair is None else jnp.maximum(best_pair, s)
        grp_scores.append(best_pair)
        in_top2.extend([r < 2.0 for r in rank(vals)])
    grp_best = [r < 1.0 for r in rank(grp_scores)]
    picked = [jnp.where(jnp.logical_and(grp_best[e // GROUP], in_top2[e]), scores[e:e + 1, :], 0.0)
              for e in range(N_EXPERTS)]
    total = picked[0]
    for pe in picked[1:]:
        total = total + pe
    row = lax.broadcasted_iota(jnp.int32, scores.shape, 0)
    comb = jnp.zeros(scores.shape, F32)
    for e in range(N_EXPERTS):
        comb = jnp.where(row == e, picked[e] / total, comb)
    return comb


def _moe_kernel(final_norm, x_ref, mod_ref, g_ref, rw_ref, rb_ref, w1_ref, w3_ref, w2_ref, gf_ref, *rest):
    o_refs, (h_sc, comb_sc, acc_sc) = rest[:-3], rest[-3:]
    e = pl.program_id(1)

    @pl.when(e == 0)
    def _():
        for s in range(MOE_TM // SEG):
            m = mod_ref[s]
            rows = slice(s * SEG, (s + 1) * SEG)
            h = _rms_mod(x_ref[rows, :], g_ref[...], m[4:5], m[3:4])
            h_sc[rows, :] = h.astype(BF16)
            scores = _sigmoid(_dot_x3_nt(rw_ref[...], h))
            comb_t = _route(scores, scores + rb_ref[...])
            comb_t = jnp.concatenate([comb_t, jnp.zeros((COMB_LANES - N_EXPERTS, SEG), F32)], axis=0)
            comb_sc[rows, :] = comb_t.T
        acc_sc[...] = jnp.zeros_like(acc_sc)

    lane = lax.broadcasted_iota(jnp.int32, (MOE_TM, COMB_LANES), 1)
    h = h_sc[...]
    es = w1_ref.shape[1]
    hids = []
    for ee in range(es):
        comb = jnp.sum(jnp.where(lane == e * es + ee, comb_sc[...], 0.0), axis=-1, keepdims=True)
        u1 = jnp.dot(h, w1_ref[0, ee].astype(BF16), preferred_element_type=F32)
        u3 = jnp.dot(h, w3_ref[0, ee].astype(BF16), preferred_element_type=F32)
        hids.append((_silu(u1) * u3 * comb).astype(BF16))
    w2 = w2_ref[0].reshape(es * D_EXPERT, D).astype(BF16)
    acc_sc[...] += jnp.dot(jnp.concatenate(hids, axis=1), w2, preferred_element_type=F32)

    def emit(dst_ref):
        for s in range(MOE_TM // SEG):
            rows = slice(s * SEG, (s + 1) * SEG)
            y = x_ref[rows, :] + mod_ref[s][5:6] * acc_sc[rows, :]
            if final_norm:
                y = y * lax.rsqrt(jnp.mean(y * y, axis=-1, keepdims=True) + RMS_EPS) * gf_ref[...]
            dst_ref[rows, :] = y

    last = e == N_EXPERTS // es - 1
    if final_norm:
        is_ctx = pl.program_id(0) < MOE_CTX_TILES
        pl.when(jnp.logical_and(last, is_ctx))(lambda: emit(o_refs[0]))
        pl.when(jnp.logical_and(last, jnp.logical_not(is_ctx)))(lambda: emit(o_refs[1]))
    else:
        pl.when(last)(lambda: emit(o_refs[0]))


def _moe(x, mods, g, router_w, router_b, w1, w3, w2, layer, gf, final_norm):
    spt = MOE_TM // SEG
    es = MOE_EXPERTS_STEP_FINAL if final_norm else MOE_EXPERTS_STEP
    tile = lambda f: pl.BlockSpec((MOE_TM, D), lambda t, e: (f(t), 0))
    if final_norm:
        n_half = M_TOK // 2
        out_specs = [tile(lambda t: jnp.minimum(t, MOE_CTX_TILES - 1)),
                     tile(lambda t: jnp.maximum(t - MOE_CTX_TILES, 0))]
        out_shape = [jax.ShapeDtypeStruct((n_half, D), F32), jax.ShapeDtypeStruct((M_TOK - n_half, D), F32)]
    else:
        out_specs = [tile(lambda t: t)]
        out_shape = [jax.ShapeDtypeStruct((M_TOK, D), F32)]
    return pl.pallas_call(
        functools.partial(_moe_kernel, final_norm),
        grid=(M_TOK // MOE_TM, N_EXPERTS // es),
        in_specs=[pl.BlockSpec((MOE_TM, D), lambda t, e: (t, 0)),
                  pl.BlockSpec((spt, 6, D), lambda t, e: (t, 0, 0)),
                  pl.BlockSpec((1, D), lambda t, e: (0, 0)),
                  pl.BlockSpec((N_EXPERTS, D), lambda t, e: (0, 0)),
                  pl.BlockSpec((N_EXPERTS, 1), lambda t, e: (0, 0)),
                  pl.BlockSpec((1, es, D, D_EXPERT), lambda t, e: (layer, e, 0, 0)),
                  pl.BlockSpec((1, es, D, D_EXPERT), lambda t, e: (layer, e, 0, 0)),
                  pl.BlockSpec((1, es, D_EXPERT, D), lambda t, e: (layer, e, 0, 0)),
                  pl.BlockSpec((1, D), lambda t, e: (0, 0))],
        out_specs=out_specs,
        out_shape=out_shape,
        scratch_shapes=[pltpu.VMEM((MOE_TM, D), BF16), pltpu.VMEM((MOE_TM, COMB_LANES), F32),
                        pltpu.VMEM((MOE_TM, D), F32)],
        compiler_params=_cp(("arbitrary", "arbitrary")),
        name="moe",
    )(x, mods, g.reshape(1, D), router_w.T, router_b.reshape(N_EXPERTS, 1), w1, w3, w2, gf.reshape(1, D))


def _rw_in_kernel(x_ref, xp_ref, xn_ref, mod_ref, g_ref, mu_ref, wr_ref, wk_ref, wv_ref, g1_ref, g2_ref,
                  w1_ref, w2_ref, w0_ref, a1_ref, a2_ref, a0_ref, ka_ref, rk_ref, hsel_ref,
                  r_ref, k_ref, v_ref, gg_ref, lw_ref, a_ref, coef_ref, wrkv_sc):
    i = pl.program_id(0)

    @pl.when(i == 0)
    def _():
        for c, w_ref in enumerate((wr_ref, wk_ref, wv_ref)):
            wrkv_sc[c] = w_ref[0].astype(BF16)

    m = mod_ref[0]
    g = g_ref[...]
    h = _rms_mod(x_ref[...], g, m[1:2], m[0:1])
    hp = jnp.where(_seg_is_first(i), 0.0, _rms_mod(xp_ref[...], g, m[1:2], m[0:1]))
    hn = jnp.where(_seg_is_last(i), 0.0, _rms_mod(xn_ref[...], g, m[1:2], m[0:1]))
    ext = jnp.concatenate([hp, h, hn], axis=0)
    n = ext.shape[0]
    h_prev = pltpu.roll(ext, 1, 0)[8:8 + SEG]
    h_next = pltpu.roll(ext, n - 1, 0)[8:8 + SEG]
    xx = 0.5 * (h_prev + h_next) - h
    mu = mu_ref[...]
    xr, xw, xk, xv, xa, xg = [h + xx * mu[c:c + 1] for c in range(6)]
    r = _dot(xr, wrkv_sc[0])
    k = _dot(xk, wrkv_sc[1])
    r_ref[...] = r.astype(BF16)
    k_ref[...] = k.astype(BF16)
    v_ref[...] = _dot(xv, wrkv_sc[2]).astype(BF16)
    gg_ref[...] = _dot(_sigmoid(_dot(xg, g1_ref[...])), g2_ref[...]).astype(BF16)
    half_w_in = w0_ref[...] + _dot(jnp.tanh(_dot(xw, w1_ref[...])), w2_ref[...])
    lw_ref[...] = (-0.5 * W_DECAY_SCALE * LOG2E) * jnp.tanh(half_w_in) - 0.5 * W_DECAY_SCALE * LOG2E
    a = 0.5 * jnp.tanh(a0_ref[...] + _dot(_dot(xa, a1_ref[...]), a2_ref[...])) + 0.5
    a_ref[...] = a
    kd_sum = k * (2.0 + (a[:, 0:D] + a[:, D:2 * D] - 2.0) * ka_ref[...])
    coef_ref[...] = _dot_sel_rhs(r * kd_sum * rk_ref[...], hsel_ref[...])


def _rw_inproj(x, mods, g, mu, wr, wk, wv, g1, g2, w1c, w2bd, w0c, a1c, a2bd, a0c, ka, rk):
    full = lambda a: pl.BlockSpec(a.shape, lambda i: (0,) * a.ndim)
    once = lambda a: pl.BlockSpec((1,) + a.shape[1:], lambda i: (0,) * a.ndim, pipeline_mode=pl.Buffered(1))
    seg = lambda width: pl.BlockSpec((SEG, width), lambda i: (i, 0))
    outs = ([jax.ShapeDtypeStruct((M_TOK, D), BF16)] * 4 + [jax.ShapeDtypeStruct((M_TOK, 2 * D), F32)] * 2
            + [jax.ShapeDtypeStruct((M_TOK, COEF_LANES), F32)])
    hsel = jnp.asarray(np.arange(D)[:, None] // C_HEAD == np.arange(COEF_LANES)[None, :], BF16)
    return pl.pallas_call(
        _rw_in_kernel,
        grid=(NSEG,),
        in_specs=[seg(D),
                  pl.BlockSpec((8, D), lambda i: (_prev8(i), 0)),
                  pl.BlockSpec((8, D), lambda i: (_next8(i), 0)),
                  pl.BlockSpec((1, 6, D), lambda i: (i, 0, 0)),
                  full(g), full(mu), once(wr), once(wk), once(wv), full(g1), full(g2),
                  full(w1c), full(w2bd), full(w0c), full(a1c), full(a2bd), full(a0c),
                  full(ka), full(rk), full(hsel)],
        out_specs=[seg(D)] * 4 + [seg(2 * D)] * 2 + [seg(COEF_LANES)],
        out_shape=outs,
        scratch_shapes=[pltpu.VMEM((3, D, D), BF16)],
        compiler_params=_cp(("arbitrary",)),
        name="rwkv_inproj",
    )(x, x, x, mods, g, mu, wr, wk, wv, g1, g2, w1c, w2bd, w0c, a1c, a2bd, a0c, ka, rk, hsel)


def _rw_consts():
    t = np.arange(SEG)
    same = (t[:, None] // RW_CHUNK) == (t[None, :] // RW_CHUNK)
    tri_f = np.logical_and(same, t[None, :] <= t[:, None]).astype(np.float32)
    tri_b = np.logical_and(same, t[None, :] >= t[:, None]).astype(np.float32)
    return [jnp.asarray(a, BF16) for a in (tri_f, tri_b)]


def _rw_dir(r, k, v, lw, a, kk_g, ka_g, tri, rev):
    c = RW_CHUNK
    lane = lax.broadcasted_iota(jnp.int32, (SEG, 2 * C_HEAD), 1)
    head0 = lane < C_HEAD
    kx = k * kk_g
    ss = kx * kx
    n0 = jnp.sum(jnp.where(head0, ss, 0.0), axis=-1, keepdims=True)
    n1 = jnp.sum(jnp.where(head0, 0.0, ss), axis=-1, keepdims=True)
    kk = kx / jnp.maximum(jnp.sqrt(jnp.where(head0, n0, n1)), 1e-12)
    kd = k * (1.0 + (a - 1.0) * ka_g)
    bhat = kk * a
    cum = _dot_sel(tri, lw)
    e_incl = jnp.exp2(cum)
    e_inv = jnp.exp2(-cum)
    ae = -kk * jnp.exp2(cum - lw)
    re = r * e_incl
    bi = bhat * e_inv
    ki = kd * e_inv

    chunks = []
    for ci in range(SEG // c):
        sl = slice(ci * c, (ci + 1) * c)
        ctot = cum[ci * c:ci * c + 1] if rev else cum[(ci + 1) * c - 1:(ci + 1) * c]
        dec = jnp.exp2(ctot - cum[sl])
        chunks.append(dict(ae=ae[sl], re=re[sl], bi=bi[sl], ki=ki[sl], v=v[sl], bdec=bhat[sl] * dec,
                           kdec=kd[sl] * dec, gam=jnp.exp2(ctot), rev=rev))
    return chunks


def _rw_transitions(chunks):
    c = RW_CHUNK
    w2 = 2 * C_HEAD
    h0c = lax.broadcasted_iota(jnp.int32, (c, w2), 1) < C_HEAD
    rowc = lax.broadcasted_iota(jnp.int32, (c, w2), 0)
    colc = jnp.bitwise_and(lax.broadcasted_iota(jnp.int32, (c, w2), 1), C_HEAD - 1)
    eye = (colc == rowc).astype(F32)
    bdmask = (lax.broadcasted_iota(jnp.int32, (w2, w2), 0) < C_HEAD) == (
        lax.broadcasted_iota(jnp.int32, (w2, w2), 1) < C_HEAD)
    keep0 = jnp.where(h0c, 1.0, 0.0).astype(BF16)
    keep1 = jnp.where(h0c, 0.0, 1.0).astype(BF16)

    def bd16(yb):
        return jnp.concatenate([yb * keep0, yb * keep1], axis=0)

    def dot_bd(x, pairs):
        blocks = [bd16(y.astype(BF16)) for y in pairs]
        rhs = blocks[0] if len(blocks) == 1 else jnp.concatenate(blocks, axis=1)
        return jnp.dot(x.astype(BF16), rhs, preferred_element_type=F32)

    mm_inv = lambda x, y: dot_bd(x, [y])

    n_ab, a_ak, a_rb, a_rk = [], [], [], []
    for ch in chunks:
        strict = (colc > rowc) if ch["rev"] else (colc < rowc)
        incl = (colc >= rowc) if ch["rev"] else (colc <= rowc)
        left = jnp.concatenate([ch["ae"], ch["re"]], axis=0)
        right = jnp.concatenate([jnp.where(h0c, ch["bi"], 0.0), jnp.where(h0c, 0.0, ch["bi"]),
                                 jnp.where(h0c, ch["ki"], 0.0), jnp.where(h0c, 0.0, ch["ki"])], axis=0)
        gm = _dot_nt(left, right)
        n_ab.append(jnp.where(strict, gm[0:c, 0:2 * c], 0.0))
        a_ak.append(jnp.where(strict, gm[0:c, 2 * c:4 * c], 0.0))
        a_rb.append(jnp.where(incl, gm[c:2 * c, 0:2 * c], 0.0))
        a_rk.append(jnp.where(incl, gm[c:2 * c, 2 * c:4 * c], 0.0))
    xorc = jnp.bitwise_xor(rowc, colc)
    tm = [eye + jnp.where(xorc < 2, n, 0.0) for n in n_ab]
    blk = 2
    while blk < c:
        couple = jnp.logical_and(xorc >= blk, xorc < 2 * blk)
        xs = [mm_inv(jnp.where(couple, n, 0.0), t) for n, t in zip(n_ab, tm)]
        tm = [t + mm_inv(t, x) for t, x in zip(tm, xs)]
        blk *= 2
    akv = [dot_bd(jnp.concatenate([x, y], axis=0), [ch["v"]])
           for x, y, ch in zip(a_ak, a_rk, chunks)]
    tav = [dot_bd(t, [ch["ae"], x[0:c]]) for t, ch, x in zip(tm, chunks, akv)]
    out = []
    for i, ch in enumerate(chunks):
        ta, tv = tav[i][:, 0:w2], tav[i][:, w2:2 * w2]
        qy = dot_bd(a_rb[i], [ta, tv])
        q = ch["re"] + qy[:, 0:w2]
        y = qy[:, w2:2 * w2] + akv[i][c:2 * c]
        wz = _dot_tn(tav[i], ch["bdec"])
        w = jnp.where(bdmask, wz[0:w2], 0.0)
        z = jnp.where(bdmask, wz[w2:2 * w2] + _dot_tn(ch["v"], ch["kdec"]), 0.0)
        out.append((q, y, w, z, ch["gam"]))
    return out


def _rw_scan_kernel(rf, kf, vf, lwf, af, rb, kb, vb, lwb, ab, kkg_ref, kag_ref, s0f, s0b, trif, trib,
                    of_ref, ob_ref, sf_out, sb_out, st_f, st_b):
    j = pl.program_id(1)
    w2 = 2 * C_HEAD
    dirs = ((False, (rf, kf, vf, lwf, af), s0f, trif, of_ref, sf_out, st_f),
            (True, (rb, kb, vb, lwb, ab), s0b, trib, ob_ref, sb_out, st_b))
    for rev, refs, s0, tri, o_ref, s_out, st in dirs:
        i = (NSEG - 1 - j) if rev else j
        starts = _seg_is_last(i) if rev else _seg_is_first(i)

        @pl.when(jnp.logical_and(starts, i < N_CTX_SEG))
        def _():
            st[...] = jnp.zeros_like(st)

        @pl.when(jnp.logical_and(starts, i >= N_CTX_SEG))
        def _():
            st[...] = s0[0, 0]

    chunks = []
    for rev, refs, s0, tri, o_ref, s_out, st in dirs:
        for pp in range(RW_PAIRS_STEP):
            lanes = slice(pp * w2, (pp + 1) * w2)
            r_, k_, v_, lw_, a_ = [z[:, lanes].astype(F32) for z in refs]
            chunks.extend(_rw_dir(r_, k_, v_, lw_, a_, kkg_ref[:, lanes], kag_ref[:, lanes], tri[...], rev))
    trans = _rw_transitions(chunks)
    n_c = SEG // RW_CHUNK
    states = [[st[pp] for pp in range(RW_PAIRS_STEP)] for (_, _, _, _, _, _, st) in dirs]
    for step in range(n_c):
        for d, (rev, refs, s0, tri, o_ref, s_out, st) in enumerate(dirs):
            ci = (n_c - 1 - step) if rev else step
            for pp in range(RW_PAIRS_STEP):
                q, y, w, z, gam = trans[(d * RW_PAIRS_STEP + pp) * n_c + ci]
                s = states[d][pp]
                o_ref[ci * RW_CHUNK:(ci + 1) * RW_CHUNK, pp * w2:(pp + 1) * w2] = (_dot_nt(q, s) + y).astype(BF16)
                states[d][pp] = s * gam + _dot(s, w) + z
    for d, (rev, refs, s0, tri, o_ref, s_out, st) in enumerate(dirs):
        for pp in range(RW_PAIRS_STEP):
            st[pp] = states[d][pp]
    for d, (rev, refs, s0, tri, o_ref, s_out, st) in enumerate(dirs):
        i = (NSEG - 1 - j) if rev else j

        @pl.when(i < N_CTX_SEG)
        def _():
            for pp in range(RW_PAIRS_STEP):
                s = states[d][pp]
                s_out[0, 2 * pp] = s[0:C_HEAD, 0:C_HEAD]
                s_out[0, 2 * pp + 1] = s[C_HEAD:w2, C_HEAD:w2]


def _rw_scan(r, k, v, lw, a, kk_g, ka_g, s0bd):
    consts = _rw_consts()
    w = 2 * C_HEAD
    pps = RW_PAIRS_STEP
    wb = pps * w
    n_steps = C_PAIRS // pps
    blk = lambda col0, rev: pl.BlockSpec(
        (SEG, wb), (lambda p, j: (NSEG - 1 - j, col0 + p)) if rev else (lambda p, j: (j, col0 + p)))
    cspec = lambda arr: pl.BlockSpec(arr.shape, lambda p, j: (0, 0))
    hps = 2 * pps
    fin_f = pl.BlockSpec((1, hps, C_HEAD, C_HEAD), lambda p, j: (jnp.minimum(j, N_CTX_SEG - 1), p, 0, 0))
    fin_b = pl.BlockSpec((1, hps, C_HEAD, C_HEAD),
                         lambda p, j: (jnp.minimum(NSEG - 1 - j, N_CTX_SEG - 1), p, 0, 0))
    return pl.pallas_call(
        _rw_scan_kernel,
        grid=(n_steps, NSEG),
        in_specs=[blk(0, False), blk(0, False), blk(0, False), blk(0, False), blk(0, False),
                  blk(0, True), blk(0, True), blk(0, True), blk(n_steps, True), blk(n_steps, True),
                  pl.BlockSpec((1, wb), lambda p, j: (0, p)), pl.BlockSpec((1, wb), lambda p, j: (0, p)),
                  pl.BlockSpec((1, 1, pps, w, w), lambda p, j: (_sample_of(j), 0, p, 0, 0)),
                  pl.BlockSpec((1, 1, pps, w, w), lambda p, j: (_sample_of(NSEG - 1 - j), 1, p, 0, 0)),
                  cspec(consts[0]), cspec(consts[1])],
        out_specs=[blk(0, False), blk(0, True), fin_f, fin_b],
        out_shape=[jax.ShapeDtypeStruct((M_TOK, D), BF16), jax.ShapeDtypeStruct((M_TOK, D), BF16),
                   jax.ShapeDtypeStruct((N_CTX_SEG, 2 * C_PAIRS, C_HEAD, C_HEAD), F32),
                   jax.ShapeDtypeStruct((N_CTX_SEG, 2 * C_PAIRS, C_HEAD, C_HEAD), F32)],
        scratch_shapes=[pltpu.VMEM((pps, w, w), F32), pltpu.VMEM((pps, w, w), F32)],
        compiler_params=_cp(("arbitrary", "arbitrary")),
        name="rwkv7_scan",
    )(r, k, v, lw, a, r, k, v, lw, a, kk_g, ka_g, s0bd, s0bd, *consts)


def _rw_out_kernel(x_ref, of_ref, ob_ref, v_ref, gg_ref, coef_ref, mod_ref, lnw_ref, lnb_ref, ones_ref,
                   hexp_ref, wo_ref, o_ref, wo_sc):
    @pl.when(pl.program_id(0) == 0)
    def _():
        wo_sc[...] = wo_ref[0].astype(BF16)

    m = mod_ref[0]
    ones_bd = ones_ref[...]
    w = 2 * C_HEAD
    inv_n = 1.0 / C_HEAD
    coef = _dot_sel_rhs(coef_ref[...], hexp_ref[...])
    parts = []
    for p in range(C_PAIRS):
        cs = slice(p * w, (p + 1) * w)
        osum = of_ref[:, cs].astype(F32) + ob_ref[:, cs].astype(F32)
        mu = _dot_sel_rhs(osum, ones_bd) * inv_n
        cen = osum - mu
        var = _dot_sel_rhs(cen * cen, ones_bd) * inv_n
        o = cen * lax.rsqrt(var + GN_EPS) * lnw_ref[:, cs] + lnb_ref[:, cs]
        bonus = coef[:, cs] * v_ref[:, cs].astype(F32)
        parts.append((o + bonus) * gg_ref[:, cs].astype(F32))
    y = _dot(jnp.concatenate(parts, axis=-1), wo_sc[...])
    o_ref[...] = x_ref[...] + m[2:3] * y


def _dot_sel_rhs(x, mat):
    h = x.astype(BF16)
    l = (x - h.astype(F32)).astype(BF16)
    return jnp.dot(h, mat, preferred_element_type=F32) + jnp.dot(l, mat, preferred_element_type=F32)


def _rw_out(x, o_f, o_b, v, gg, coef, mods, lnw, lnb, wo):
    seg = lambda width: pl.BlockSpec((SEG, width), lambda i: (i, 0))
    row = pl.BlockSpec((1, D), lambda i: (0, 0))
    hh = np.arange(2 * C_HEAD) // C_HEAD
    ones_bd = jnp.asarray((hh[:, None] == hh[None, :]).astype(np.float32), BF16)
    hexp = jnp.asarray(np.arange(COEF_LANES)[:, None] == np.arange(D)[None, :] // C_HEAD, BF16)
    return pl.pallas_call(
        _rw_out_kernel,
        grid=(NSEG,),
        in_specs=[seg(D), seg(D), seg(D), seg(D), seg(D), seg(COEF_LANES),
                  pl.BlockSpec((1, 6, D), lambda i: (i, 0, 0)),
                  row, row,
                  pl.BlockSpec((2 * C_HEAD, 2 * C_HEAD), lambda i: (0, 0)),
                  pl.BlockSpec((COEF_LANES, D), lambda i: (0, 0)),
                  pl.BlockSpec((1, D, D), lambda i: (0, 0, 0), pipeline_mode=pl.Buffered(1))],
        out_specs=seg(D),
        out_shape=jax.ShapeDtypeStruct((M_TOK, D), F32),
        scratch_shapes=[pltpu.VMEM((D, D), BF16)],
        compiler_params=_cp(("arbitrary",)),
        name="rwkv_out",
    )(x, o_f, o_b, v, gg, coef, mods, lnw.reshape(1, D), lnb.reshape(1, D), ones_bd, hexp, wo)


def _grid_pos_table(n_tok):
    rows = n_tok // GRID_W
    r, cl = np.meshgrid(np.arange(rows, dtype=np.float32), np.arange(GRID_W, dtype=np.float32), indexing='ij')
    quarter = D // 4
    omega = (1.0 / (np.float32(POS_BASE) ** (np.arange(quarter, dtype=np.float32) / np.float32(quarter))))
    ang_r = (r.reshape(-1, 1) * omega).astype(np.float32)
    ang_c = (cl.reshape(-1, 1) * omega).astype(np.float32)
    table = np.concatenate([np.sin(ang_r), np.cos(ang_r), np.sin(ang_c), np.cos(ang_c)], axis=-1)
    return jnp.asarray(table.astype(np.float32))


def _block_diag(blocks):
    g, n, _ = blocks.shape
    eye = jnp.eye(g, dtype=blocks.dtype)
    return (eye[:, None, :, None] * blocks[:, :, None, :]).reshape(g * n, g * n)


def kernel(x_prompt, x_sample, state_hgrn, state_rglru, state_rwkv, c, c_ctx, norm_mix_g, norm_ffn_g, w_mod, b_mod, ab_w_in, ab_w_out, hgrn_lb, hgrn_norm_g, rg_conv_w, rg_conv_b, rg_wa, rg_ba, rg_wx, rg_bx, rg_lambda, rw_mu, rw_wr, rw_wk, rw_wv, rw_wo, rw_w0, rw_w1, rw_w2, rw_a0, rw_a1, rw_a2, rw_g1, rw_g2, rw_kk, rw_ka, rw_rk, rw_lnw, rw_lnb, moe_router, moe_router_bias, moe_w1, moe_w3, moe_w2, norm_f_g):
    bf = lambda z: z.astype(BF16)
    xp = x_prompt.reshape(-1, D)
    xs = x_sample.reshape(-1, D)
    pos = _grid_pos_table(x_sample.shape[1])
    mods = _modulations(c, c_ctx, w_mod, b_mod)

    lower_bounds = jnp.cumsum(jax.nn.softmax(hgrn_lb.astype(F32), axis=1), axis=1)
    lb = lower_bounds[:, 0].reshape(2, A_HEADS, 1, A_DK)
    proj_f, proj_h = _ab_inproj(xp, xs, pos, mods[0], norm_mix_g[0], ab_w_in)
    s0t = jnp.swapaxes(state_hgrn[:, 0], -1, -2)
    o_f, o_b, new_hgrn = _gla(proj_f, proj_h, lb, s0t)
    wa_bd = bf(jnp.stack([_block_diag(rg_wa[0, d]) for d in range(2)]))
    wx_bd = bf(jnp.stack([_block_diag(rg_wx[0, d]) for d in range(2)]))
    h_f, h_b, lru_f, lru_b = _rglru(
        proj_f, rg_conv_w[0], rg_conv_b[0].reshape(1, D_B), wa_bd, rg_ba[0].reshape(2, 1, D_B), wx_bd,
        rg_bx[0].reshape(2, 1, D_B), rg_lambda[0].reshape(2, 1, D_B), state_rglru[:, 0].reshape(-1, 2, 1, D_B))
    x = _ab_out(xp, xs, pos, proj_h, o_f, o_b, h_f, h_b, mods[0], hgrn_norm_g[0], ab_w_out)
    x, = _moe(x, mods[0], norm_ffn_g[0], moe_router, moe_router_bias, moe_w1, moe_w3, moe_w2, 0, norm_f_g, False)

    w1c = bf(jnp.concatenate([rw_w1[0, 0], rw_w1[0, 1]], axis=-1))
    a1c = bf(jnp.concatenate([rw_a1[0, 0], rw_a1[0, 1]], axis=-1))
    half_w2, half_a2 = 0.5 * rw_w2[0], 0.5 * rw_a2[0]
    w2bd = bf(jnp.concatenate([jnp.concatenate([half_w2[0], jnp.zeros_like(half_w2[0])], axis=-1),
                               jnp.concatenate([jnp.zeros_like(half_w2[1]), half_w2[1]], axis=-1)], axis=0))
    a2bd = bf(jnp.concatenate([jnp.concatenate([half_a2[0], jnp.zeros_like(half_a2[0])], axis=-1),
                               jnp.concatenate([jnp.zeros_like(half_a2[1]), half_a2[1]], axis=-1)], axis=0))
    r, k, v, gg, lw, a, coef = _rw_inproj(
        x, mods[1], norm_mix_g[1].reshape(1, D), rw_mu[0], rw_wr, rw_wk, rw_wv,
        bf(rw_g1[0]), bf(rw_g2[0]), w1c, w2bd, 0.5 * rw_w0[0].reshape(1, 2 * D), a1c, a2bd,
        0.5 * rw_a0[0].reshape(1, 2 * D), rw_ka[0].reshape(1, D), rw_rk[0].reshape(1, D))
    s0 = state_rwkv[:, 0].reshape(N_SAMPLE, 2, C_PAIRS, 2, C_HEAD, C_HEAD)
    zeros = jnp.zeros_like(s0[:, :, :, 0])
    s0bd = jnp.concatenate([jnp.concatenate([s0[:, :, :, 0], zeros], axis=-1),
                            jnp.concatenate([zeros, s0[:, :, :, 1]], axis=-1)], axis=-2)
    ow_f, ow_b, rs_f, rs_b = _rw_scan(r, k, v, lw, a, rw_kk[0].reshape(1, D), rw_ka[0].reshape(1, D), s0bd)
    x = _rw_out(x, ow_f, ow_b, v, gg, coef, mods[1], rw_lnw[0], rw_lnb[0], rw_wo)
    y_p, y_s = _moe(x, mods[1], norm_ffn_g[1], moe_router, moe_router_bias, moe_w1, moe_w3, moe_w2, 1, norm_f_g, True)

    new_rglru = jnp.stack([lru_f[:, 0], lru_b[:, 0]], axis=1)[:, None]
    new_rwkv = jnp.stack([rs_f, rs_b], axis=1)[:, None]
    return (y_p.reshape(x_prompt.shape), y_s.reshape(x_sample.shape), new_hgrn, new_rglru, new_rwkv)
```

```python
import functools
import math

import numpy as np
import jax
import jax.numpy as jnp
from jax import lax
from jax.experimental import pallas as pl
from jax.experimental.pallas import tpu as pltpu

F32 = jnp.float32
BF16 = jnp.bfloat16

D = 1024
SEG = 256
N_CTX_SEG = 16
SEG_PER_SAMPLE = 4
N_SAMPLE = 4
NSEG = N_CTX_SEG + N_SAMPLE * SEG_PER_SAMPLE
M_TOK = NSEG * SEG
SUBLANES = 8
ROWS8_PER_SEG = SEG // SUBLANES

A_HEADS = 4
A_DK = 128
D_A = 512
D_B = 512
B_BLOCKS = 8
B_BLOCK = 64
LRU_C = 8.0
D_IN_AB = 5 * D_A + 2 * D_B
AB_F32_COLS = 2 * D_A + D_B
C_HEAD = 64
C_PAIRS = 8
RW_CHUNK = 64
COEF_LANES = 128
RW_PAIRS_STEP = 4
W_DECAY_SCALE = math.exp(-0.5)
LOG2E = math.log2(math.e)
N_EXPERTS = 16
N_GROUPS = 4
GROUP = 4
D_EXPERT = 256
RMS_EPS = 1e-6
GN_EPS = 64e-5
POS_BASE = 10000.0
GRID_W = 64
MOE_TM = 1024
MOE_CTX_TILES = N_CTX_SEG * SEG // MOE_TM
MOE_EXPERTS_STEP = 4
COMB_LANES = 128
GLA_LEVELS = (1, 2, 4, 8, 16, 32, 64, 128)
GLA_HALF = 128
GLA_HEADS_STEP = 4

VMEM_LIMIT = 56 * 1024 * 1024


def _cp(sem):
    return pltpu.CompilerParams(dimension_semantics=sem, vmem_limit_bytes=VMEM_LIMIT)


def _sigmoid(x):
    return 0.5 * jnp.tanh(0.5 * x) + 0.5


def _silu(x):
    return x * _sigmoid(x)


def _gelu_tanh(x):
    return 0.5 * x * (1.0 + jnp.tanh(math.sqrt(2.0 / math.pi) * (x + 0.044715 * (x * x * x))))


def _rms_mod(x, g, scale, shift):
    ms = jnp.mean(x * x, axis=-1, keepdims=True)
    return x * lax.rsqrt(ms + RMS_EPS) * (g * (1.0 + scale)) + shift


def _dot(a, b):
    return jnp.dot(a.astype(BF16), b.astype(BF16), preferred_element_type=F32)


def _dot_nt(a, b):
    return lax.dot_general(a.astype(BF16), b.astype(BF16), (((1,), (1,)), ((), ())),
                           preferred_element_type=F32)


def _dot_tn(a, b):
    return lax.dot_general(a.astype(BF16), b.astype(BF16), (((0,), (0,)), ((), ())),
                           preferred_element_type=F32)


def _split3(x):
    h = x.astype(BF16)
    r1 = x - h.astype(F32)
    m = r1.astype(BF16)
    r2 = r1 - m.astype(F32)
    return h, m, r2.astype(BF16)


def _dot_sel(mat, x):
    h, m, l = _split3(x)
    return (jnp.dot(mat, h, preferred_element_type=F32) + jnp.dot(mat, m, preferred_element_type=F32)
            + jnp.dot(mat, l, preferred_element_type=F32))


def _dot_x3_nt(a, b):
    dn = (((1,), (1,)), ((), ()))
    ah = a.astype(BF16)
    al = (a - ah.astype(F32)).astype(BF16)
    bh = b.astype(BF16)
    bl = (b - bh.astype(F32)).astype(BF16)
    return (lax.dot_general(ah, bh, dn, preferred_element_type=F32)
            + lax.dot_general(ah, bl, dn, preferred_element_type=F32)
            + lax.dot_general(al, bh, dn, preferred_element_type=F32))


def _seg_is_first(i):
    return jnp.logical_or(i < N_CTX_SEG, lax.rem(i - N_CTX_SEG, SEG_PER_SAMPLE) == 0)


def _seg_is_last(i):
    return jnp.logical_or(i < N_CTX_SEG, lax.rem(i - N_CTX_SEG, SEG_PER_SAMPLE) == SEG_PER_SAMPLE - 1)


def _sample_of(i):
    return jnp.maximum(i - N_CTX_SEG, 0) // SEG_PER_SAMPLE


def _prev8(i):
    return jnp.maximum(i * ROWS8_PER_SEG - 1, 0)


def _next8(i):
    return jnp.minimum((i + 1) * ROWS8_PER_SEG, M_TOK // 8 - 1)


def _mod_kernel(cv_ref, w_ref, b_ref, o_ref):
    cv = cv_ref[...]
    o_ref[0] = _dot(_silu(cv), w_ref[0]) + b_ref[0]


def _modulations(c, c_ctx, w_mod, b_mod):
    depth = w_mod.shape[0]
    cv = jnp.concatenate([c_ctx[None, :], c, jnp.zeros((3, D), F32)], axis=0)
    wt = 2 * D
    mod = pl.pallas_call(
        _mod_kernel,
        grid=(depth, 6 * D // wt),
        in_specs=[pl.BlockSpec((8, D), lambda l, n: (0, 0)),
                  pl.BlockSpec((1, D, wt), lambda l, n: (l, 0, n)),
                  pl.BlockSpec((1, 1, wt), lambda l, n: (l, 0, n))],
        out_specs=pl.BlockSpec((1, 8, wt), lambda l, n: (l, 0, n)),
        out_shape=jax.ShapeDtypeStruct((depth, 8, 6 * D), F32),
        compiler_params=_cp(("arbitrary", "arbitrary")),
        name="adaln_mod",
    )(cv, w_mod, b_mod.reshape(depth, 1, 6 * D))
    row_of_seg = np.array([0] * N_CTX_SEG + [1 + s // SEG_PER_SAMPLE for s in range(N_SAMPLE * SEG_PER_SAMPLE)])
    return mod[:, row_of_seg].reshape(depth, NSEG, 6, D)


def _x0_specs():
    return [pl.BlockSpec((SEG, D), lambda i: (jnp.minimum(i, N_CTX_SEG - 1), 0)),
            pl.BlockSpec((SEG, D), lambda i: (jnp.maximum(i - N_CTX_SEG, 0), 0)),
            pl.BlockSpec((SEG, D), lambda i: (lax.rem(jnp.maximum(i - N_CTX_SEG, 0), SEG_PER_SAMPLE), 0))]


def _x0(i, xp_ref, xs_ref, pos_ref):
    return jnp.where(i < N_CTX_SEG, xp_ref[...], xs_ref[...] + pos_ref[...])


def _ab_in_kernel(xp_ref, xs_ref, pos_ref, mod_ref, g_ref, w_ref, of_ref, oh_ref, w_sc):
    @pl.when(pl.program_id(0) == 0)
    def _():
        w_sc[...] = w_ref[0].astype(BF16)

    m = mod_ref[0]
    x = _x0(pl.program_id(0), xp_ref, xs_ref, pos_ref)
    h = _rms_mod(x, g_ref[...], m[1:2], m[0:1])
    res = jnp.dot(h.astype(BF16), w_sc[...], preferred_element_type=F32)
    of_ref[:, 0:2 * D_A] = res[:, D_A:3 * D_A]
    of_ref[:, 2 * D_A:AB_F32_COLS] = res[:, 5 * D_A:5 * D_A + D_B]
    oh_ref[:, 0:D_A] = res[:, 0:D_A].astype(BF16)
    oh_ref[:, D_A:3 * D_A] = res[:, 3 * D_A:5 * D_A].astype(BF16)
    oh_ref[:, 3 * D_A:3 * D_A + D_B] = res[:, 5 * D_A + D_B:D_IN_AB].astype(BF16)


def _ab_inproj(xp, xs, pos, mods, g, w_in):
    return pl.pallas_call(
        _ab_in_kernel,
        grid=(NSEG,),
        in_specs=_x0_specs() + [pl.BlockSpec((1, 6, D), lambda i: (i, 0, 0)),
                                pl.BlockSpec((1, D), lambda i: (0, 0)),
                                pl.BlockSpec((1, D, D_IN_AB), lambda i: (0, 0, 0),
                                             pipeline_mode=pl.Buffered(1))],
        out_specs=[pl.BlockSpec((SEG, AB_F32_COLS), lambda i: (i, 0)),
                   pl.BlockSpec((SEG, D_IN_AB - AB_F32_COLS), lambda i: (i, 0))],
        out_shape=[jax.ShapeDtypeStruct((M_TOK, AB_F32_COLS), F32),
                   jax.ShapeDtypeStruct((M_TOK, D_IN_AB - AB_F32_COLS), BF16)],
        scratch_shapes=[pltpu.VMEM((D, D_IN_AB), BF16)],
        compiler_params=_cp(("arbitrary",)),
        name="ab_inproj",
    )(xp, xs, pos, mods, g.reshape(1, D), w_in)


def _gla_consts():
    t = np.arange(SEG)
    tri_f = (t[None, :] <= t[:, None]).astype(np.float32)
    tri_b = (t[None, :] >= t[:, None]).astype(np.float32)
    th = np.arange(GLA_HALF)
    xor = th[:, None] ^ th[None, :]
    hb = np.where(xor > 0, 1 << np.floor(np.log2(np.maximum(xor, 1))).astype(np.int64), 0)
    code_f = np.where(th[None, :] < th[:, None], hb, 0).astype(np.int32)
    code_b = np.where(th[None, :] > th[:, None], hb, 0).astype(np.int32)
    return [jnp.asarray(tri_f, BF16), jnp.asarray(tri_b, BF16), jnp.asarray(code_f), jnp.asarray(code_b)]


def _gla_level_operand(q, k, b, g, rowi, w, rev):
    upper = jnp.bitwise_and(rowi, w) != 0
    qside = jnp.logical_not(upper) if rev else upper
    if w == 1:
        z = jnp.where(qside, g, 0.0)
    elif w >= SUBLANES:
        nv = 2 * w // SUBLANES
        b4 = b.reshape(SEG // (2 * w), nv, SUBLANES, A_DK)
        ref = (b4[:, nv // 2:nv // 2 + 1, 0:1, :] if rev
               else b4[:, nv // 2 - 1:nv // 2, SUBLANES - 1:SUBLANES, :])
        x = (b4 - ref).reshape(SEG, A_DK)
        z = jnp.where(qside, x, -x)
    else:
        b3 = b.reshape(SEG // SUBLANES, SUBLANES, A_DK)
        sub = lax.broadcasted_iota(jnp.int32, b3.shape, 1)
        beta = None
        for jb in range(SUBLANES // (2 * w)):
            r = jb * 2 * w + (w if rev else w - 1)
            cand = jnp.broadcast_to(b3[:, r:r + 1, :], b3.shape)
            beta = cand if beta is None else jnp.where(sub >= jb * 2 * w, cand, beta)
        x = (b3 - beta).reshape(SEG, A_DK)
        z = jnp.where(qside, x, -x)
    return jnp.where(qside, q, k) * jnp.exp2(z)


def _gla_dir(qraw, fraw, v, lb, st, tri, code, rev):
    hh = GLA_HALF
    q = _silu(qraw)
    f = lb + (1.0 - lb) * _sigmoid(fraw)
    g = jnp.log2(f)
    k = 1.0 - f
    b = _dot_sel(tri, g)
    rowi = lax.broadcasted_iota(jnp.int32, (SEG, A_DK), 0)
    att = [jnp.zeros((hh, hh), F32), jnp.zeros((hh, hh), F32)]
    cross = None
    for w in GLA_LEVELS:
        m = _gla_level_operand(q, k, b, g, rowi, w, rev).astype(BF16)
        if w == hh:
            cross = _dot_nt(m[0:hh], m[hh:SEG]) if rev else _dot_nt(m[hh:SEG], m[0:hh])
        else:
            for half in range(2):
                mh = m[half * hh:(half + 1) * hh]
                att[half] = jnp.where(code == w, _dot_nt(mh, mh), att[half])
    if rev:
        o_lo = _dot(jnp.concatenate([att[0], cross], axis=1), v)
        o_hi = _dot(att[1], v[hh:SEG])
    else:
        o_lo = _dot(att[0], v[0:hh])
        o_hi = _dot(jnp.concatenate([cross, att[1]], axis=1), v)
    diag = jnp.sum(q * k, axis=-1, keepdims=True)
    o = jnp.concatenate([o_lo, o_hi], axis=0) + diag * v + _dot_nt(q * jnp.exp2(b), st)
    btot = b[0:1] if rev else b[SEG - 1:SEG]
    st_new = st * jnp.exp2(btot) + _dot_tn(v, k * jnp.exp2(btot - b))
    return o, st_new


def _gla_kernel(qf, ff, vf, qb, fb, vb, lb_ref, s0f, s0b, trif, trib, codef, codeb,
                of_ref, ob_ref, fin_ref, st_f, st_b):
    j = pl.program_id(1)
    dirs = ((False, qf, ff, vf, s0f, trif, codef, of_ref, st_f),
            (True, qb, fb, vb, s0b, trib, codeb, ob_ref, st_b))
    for rev, qr, fr, vr, s0, tri, code, o_ref, st in dirs:
        i = (NSEG - 1 - j) if rev else j
        starts = _seg_is_last(i) if rev else _seg_is_first(i)

        @pl.when(jnp.logical_and(starts, i < N_CTX_SEG))
        def _():
            st[...] = jnp.zeros_like(st)

        @pl.when(jnp.logical_and(starts, i >= N_CTX_SEG))
        def _():
            st[...] = s0[0, 0]

    finals = []
    for d, (rev, qr, fr, vr, s0, tri, code, o_ref, st) in enumerate(dirs):
        for hh in range(GLA_HEADS_STEP):
            lanes = slice(hh * A_DK, (hh + 1) * A_DK)
            o, st_new = _gla_dir(qr[:, lanes].astype(F32), fr[:, lanes], vr[:, lanes].astype(F32),
                                 lb_ref[d, hh], st[hh], tri[...], code[...], rev)
            o_ref[:, lanes] = o.astype(BF16)
            finals.append(st_new)
    for d, (rev, qr, fr, vr, s0, tri, code, o_ref, st) in enumerate(dirs):
        for hh in range(GLA_HEADS_STEP):
            st[hh] = finals[d * GLA_HEADS_STEP + hh]
    for d, (rev, qr, fr, vr, s0, tri, code, o_ref, st) in enumerate(dirs):
        i = (NSEG - 1 - j) if rev else j

        @pl.when(i < N_CTX_SEG)
        def _():
            for hh in range(GLA_HEADS_STEP):
                fin_ref[i, 0, d, pl.program_id(0) * GLA_HEADS_STEP + hh] = finals[d * GLA_HEADS_STEP + hh].T


def _gla(proj_f, proj_h, lb, s0t):
    consts = _gla_consts()
    hs = GLA_HEADS_STEP
    wb = hs * A_DK
    n_col = D_A // wb
    blk = lambda col0, rev: pl.BlockSpec(
        (SEG, wb), (lambda h, j: (NSEG - 1 - j, col0 + h)) if rev else (lambda h, j: (j, col0 + h)))
    cspec = lambda a: pl.BlockSpec(a.shape, lambda h, j: (0, 0))
    fin_shape = (N_CTX_SEG, 1, 2, A_HEADS, A_DK, A_DK)
    fin = pl.BlockSpec(fin_shape, lambda h, j: (0,) * len(fin_shape))
    return pl.pallas_call(
        _gla_kernel,
        grid=(A_HEADS // hs, NSEG),
        in_specs=[blk(0, False), blk(0, False), blk(n_col, False),
                  blk(0, True), blk(n_col, True), blk(n_col, True),
                  pl.BlockSpec((2, hs, 1, A_DK), lambda h, j: (0, h, 0, 0)),
                  pl.BlockSpec((1, 1, hs, A_DK, A_DK), lambda h, j: (_sample_of(j), 0, h, 0, 0)),
                  pl.BlockSpec((1, 1, hs, A_DK, A_DK), lambda h, j: (_sample_of(NSEG - 1 - j), 1, h, 0, 0)),
                  cspec(consts[0]), cspec(consts[1]), cspec(consts[2]), cspec(consts[3])],
        out_specs=[blk(0, False), blk(0, True), fin],
        out_shape=[jax.ShapeDtypeStruct((M_TOK, D_A), BF16), jax.ShapeDtypeStruct((M_TOK, D_A), BF16),
                   jax.ShapeDtypeStruct(fin_shape, F32)],
        scratch_shapes=[pltpu.VMEM((hs, A_DK, A_DK), F32), pltpu.VMEM((hs, A_DK, A_DK), F32)],
        compiler_params=_cp(("arbitrary", "arbitrary")),
        name="hgrn2_gla",
    )(proj_h, proj_f, proj_h, proj_h, proj_f, proj_h, lb, s0t, s0t, *consts)


def _lin_scan(a, b, h_in, rev):
    t_len, c = a.shape
    ng = t_len // SUBLANES
    a3 = a.reshape(ng, SUBLANES, c)
    b3 = b.reshape(ng, SUBLANES, c)
    sub = lax.broadcasted_iota(jnp.int32, a3.shape, 1)
    s = 1
    while s < SUBLANES:
        shift = (SUBLANES - s) if rev else s
        valid = (sub < SUBLANES - s) if rev else (sub >= s)
        ap = jnp.where(valid, pltpu.roll(a3, shift, 1), 1.0)
        bp = jnp.where(valid, pltpu.roll(b3, shift, 1), 0.0)
        b3 = a3 * bp + b3
        a3 = a3 * ap
        s *= 2
    hs = [None] * ng
    carry = h_in
    for j in (range(ng - 1, -1, -1) if rev else range(ng)):
        hs[j] = a3[j] * carry + b3[j]
        carry = hs[j][0:1] if rev else hs[j][SUBLANES - 1:SUBLANES]
    return jnp.concatenate(hs, axis=0), carry


def _rglru_dir(x, xprev, xnext, first, last, cw, cb, wa, ba, wx, bx, lam, h_in, rev):
    zero = jnp.zeros_like(xprev)
    ext = jnp.concatenate([jnp.where(first, zero, xprev), x, jnp.where(last, zero, xnext)], axis=0)
    n = ext.shape[0]
    xm2 = pltpu.roll(ext, 2, 0)[8:8 + SEG]
    xm1 = pltpu.roll(ext, 1, 0)[8:8 + SEG]
    xp1 = pltpu.roll(ext, n - 1, 0)[8:8 + SEG]
    xc = cb + xm2 * cw[0:1] + xm1 * cw[1:2] + x * cw[2:3] + xp1 * cw[3:4]
    gate_r = _sigmoid(_dot(xc, wa) + ba)
    gate_i = _sigmoid(_dot(xc, wx) + bx)
    softplus_neg_lam = jnp.maximum(-lam, 0.0) + jnp.log(1.0 + jnp.exp(-jnp.abs(lam)))
    a = jnp.exp2((-LRU_C * LOG2E) * gate_r * softplus_neg_lam)
    b_in = jnp.sqrt(1.0 - a * a) * gate_i * xc
    return _lin_scan(a, b_in, h_in, rev)


def _rglru_kernel(xf, xf_p, xf_n, xb, xb_p, xb_n, cw_ref, cb_ref, wa_ref, ba_ref, wx_ref, bx_ref, lam_ref,
                  s0f, s0b, hf_ref, hb_ref, ff_out, fb_out, hc_f, hc_b):
    j = pl.program_id(0)
    dirs = ((False, xf, xf_p, xf_n, s0f, hf_ref, ff_out, hc_f),
            (True, xb, xb_p, xb_n, s0b, hb_ref, fb_out, hc_b))
    for rev, xr, xp, xn, s0, h_ref, f_out, hc in dirs:
        i = (NSEG - 1 - j) if rev else j
        starts = _seg_is_last(i) if rev else _seg_is_first(i)

        @pl.when(jnp.logical_and(starts, i < N_CTX_SEG))
        def _():
            hc[...] = jnp.zeros_like(hc)

        @pl.when(jnp.logical_and(starts, i >= N_CTX_SEG))
        def _():
            hc[...] = s0[0, 0]

    outs = []
    for d, (rev, xr, xp, xn, s0, h_ref, f_out, hc) in enumerate(dirs):
        i = (NSEG - 1 - j) if rev else j
        h, h_out = _rglru_dir(xr[...], xp[...], xn[...], _seg_is_first(i), _seg_is_last(i), cw_ref[...],
                              cb_ref[...], wa_ref[d], ba_ref[d], wx_ref[d], bx_ref[d], lam_ref[d], hc[...], rev)
        h_ref[...] = h.astype(BF16)
        outs.append(h_out)
    for d, (rev, xr, xp, xn, s0, h_ref, f_out, hc) in enumerate(dirs):
        hc[...] = outs[d]
    for d, (rev, xr, xp, xn, s0, h_ref, f_out, hc) in enumerate(dirs):
        i = (NSEG - 1 - j) if rev else j

        @pl.when(i < N_CTX_SEG)
        def _():
            f_out[0] = outs[d]


def _rglru(proj, conv_w, conv_b, wa_bd, ba, wx_bd, bx, lam, s0):
    xcol = 2 * D_A // D_B
    fwd = lambda f: (lambda j: f(j))
    bwd = lambda f: (lambda j: f(NSEG - 1 - j))
    seg_blk = lambda m: pl.BlockSpec((SEG, D_B), m(lambda i: (i, xcol)))
    prev_blk = lambda m: pl.BlockSpec((8, D_B), m(lambda i: (_prev8(i), xcol)))
    next_blk = lambda m: pl.BlockSpec((8, D_B), m(lambda i: (_next8(i), xcol)))
    full = lambda a: pl.BlockSpec(a.shape, lambda j: (0,) * a.ndim)
    return pl.pallas_call(
        _rglru_kernel,
        grid=(NSEG,),
        in_specs=[seg_blk(fwd), prev_blk(fwd), next_blk(fwd), seg_blk(bwd), prev_blk(bwd), next_blk(bwd),
                  full(conv_w), full(conv_b), full(wa_bd), full(ba), full(wx_bd), full(bx), full(lam),
                  pl.BlockSpec((1, 1, 1, D_B), lambda j: (_sample_of(j), 0, 0, 0)),
                  pl.BlockSpec((1, 1, 1, D_B), lambda j: (_sample_of(NSEG - 1 - j), 1, 0, 0))],
        out_specs=[pl.BlockSpec((SEG, D_B), lambda j: (j, 0)),
                   pl.BlockSpec((SEG, D_B), lambda j: (NSEG - 1 - j, 0)),
                   pl.BlockSpec((1, 1, D_B), lambda j: (jnp.minimum(j, N_CTX_SEG - 1), 0, 0)),
                   pl.BlockSpec((1, 1, D_B), lambda j: (jnp.minimum(NSEG - 1 - j, N_CTX_SEG - 1), 0, 0))],
        out_shape=[jax.ShapeDtypeStruct((M_TOK, D_B), BF16), jax.ShapeDtypeStruct((M_TOK, D_B), BF16),
                   jax.ShapeDtypeStruct((N_CTX_SEG, 1, D_B), F32), jax.ShapeDtypeStruct((N_CTX_SEG, 1, D_B), F32)],
        scratch_shapes=[pltpu.VMEM((1, D_B), F32), pltpu.VMEM((1, D_B), F32)],
        compiler_params=_cp(("arbitrary",)),
        name="rglru",
    )(proj, proj, proj, proj, proj, proj, conv_w, conv_b, wa_bd, ba, wx_bd, bx, lam, s0, s0)


def _ab_out_kernel(xp_ref, xs_ref, pos_ref, of_ref, ob_ref, og_ref, hf_ref, hb_ref, yr_ref, mod_ref, hg_ref,
                   w_ref, o_ref, w_sc):
    @pl.when(pl.program_id(0) == 0)
    def _():
        w_sc[...] = w_ref[0].astype(BF16)

    m = mod_ref[0]
    f32 = lambda ref: ref[...].astype(F32)
    oa = f32(of_ref) + f32(ob_ref)
    hg = hg_ref[...]
    parts = []
    for h in range(A_HEADS):
        z = oa[:, h * A_DK:(h + 1) * A_DK]
        parts.append(z * lax.rsqrt(jnp.mean(z * z, axis=-1, keepdims=True) + RMS_EPS) * hg)
    o_a = jnp.concatenate(parts, axis=-1) * _silu(f32(og_ref))
    o_b = (f32(hf_ref) + f32(hb_ref)) * _gelu_tanh(f32(yr_ref))
    y = _dot(o_a, w_sc[0:D_A]) + _dot(o_b, w_sc[D_A:D_A + D_B])
    o_ref[...] = _x0(pl.program_id(0), xp_ref, xs_ref, pos_ref) + m[2:3] * y


def _ab_out(xp, xs, pos, proj_h, o_f, o_b, h_f, h_b, mods, hg, w_out):
    seg = lambda width, col: pl.BlockSpec((SEG, width), lambda i: (i, col))
    return pl.pallas_call(
        _ab_out_kernel,
        grid=(NSEG,),
        in_specs=_x0_specs() + [seg(D_A, 0), seg(D_A, 0), seg(D_A, 2), seg(D_B, 0), seg(D_B, 0), seg(D_B, 3),
                                pl.BlockSpec((1, 6, D), lambda i: (i, 0, 0)),
                                pl.BlockSpec((1, A_DK), lambda i: (0, 0)),
                                pl.BlockSpec((1, D_A + D_B, D), lambda i: (0, 0, 0),
                                             pipeline_mode=pl.Buffered(1))],
        out_specs=seg(D, 0),
        out_shape=jax.ShapeDtypeStruct((M_TOK, D), F32),
        scratch_shapes=[pltpu.VMEM((D_A + D_B, D), BF16)],
        compiler_params=_cp(("arbitrary",)),
        name="ab_out",
    )(xp, xs, pos, o_f, o_b, proj_h, h_f, h_b, proj_h, mods, hg.reshape(1, A_DK), w_out)


def _route(scores, sel):
    cols = [sel[e:e + 1, :] for e in range(N_EXPERTS)]

    def rank(vals):
        out = []
        for i, vi in enumerate(vals):
            r = None
            for jx, vj in enumerate(vals):
                if jx == i:
                    continue
                beats = (vj >= vi) if jx < i else (vj > vi)
                r = beats.astype(F32) if r is None else r + beats.astype(F32)
            out.append(r)
        return out

    grp_scores, in_top2 = [], []
    for gi in range(N_GROUPS):
        vals = cols[gi * GROUP:(gi + 1) * GROUP]
        best_pair = None
        for a in range(GROUP):
            for bx in range(a + 1, GROUP):
                s = vals[a] + vals[bx]
                best_pair = s if best_pair is None else jnp.maximum(best_pair, s)
        grp_scores.append(best_pair)
        in_top2.extend([r < 2.0 for r in rank(vals)])
    grp_best = [r < 1.0 for r in rank(grp_scores)]
    picked = [jnp.where(jnp.logical_and(grp_best[e // GROUP], in_top2[e]), scores[e:e + 1, :], 0.0)
              for e in range(N_EXPERTS)]
    total = picked[0]
    for pe in picked[1:]:
        total = total + pe
    row = lax.broadcasted_iota(jnp.int32, scores.shape, 0)
    comb = jnp.zeros(scores.shape, F32)
    for e in range(N_EXPERTS):
        comb = jnp.where(row == e, picked[e] / total, comb)
    return comb


def _moe_kernel(final_norm, x_ref, mod_ref, g_ref, rw_ref, rb_ref, w1_ref, w3_ref, w2_ref, gf_ref, *rest):
    o_refs, (h_sc, comb_sc, acc_sc) = rest[:-3], rest[-3:]
    e = pl.program_id(1)

    @pl.when(e == 0)
    def _():
        for s in range(MOE_TM // SEG):
            m = mod_ref[s]
            rows = slice(s * SEG, (s + 1) * SEG)
            h = _rms_mod(x_ref[rows, :], g_ref[...], m[4:5], m[3:4])
            h_sc[rows, :] = h.astype(BF16)
            scores = _sigmoid(_dot_x3_nt(rw_ref[...], h))
            comb_t = _route(scores, scores + rb_ref[...])
            comb_t = jnp.concatenate([comb_t, jnp.zeros((COMB_LANES - N_EXPERTS, SEG), F32)], axis=0)
            comb_sc[rows, :] = comb_t.T
        acc_sc[...] = jnp.zeros_like(acc_sc)

    lane = lax.broadcasted_iota(jnp.int32, (MOE_TM, COMB_LANES), 1)
    h = h_sc[...]
    es = w1_ref.shape[1]
    hids = []
    for ee in range(es):
        comb = jnp.sum(jnp.where(lane == e * es + ee, comb_sc[...], 0.0), axis=-1, keepdims=True)
        u1 = jnp.dot(h, w1_ref[0, ee].astype(BF16), preferred_element_type=F32)
        u3 = jnp.dot(h, w3_ref[0, ee].astype(BF16), preferred_element_type=F32)
        hids.append((_silu(u1) * u3 * comb).astype(BF16))
    w2 = w2_ref[0].reshape(es * D_EXPERT, D).astype(BF16)
    acc_sc[...] += jnp.dot(jnp.concatenate(hids, axis=1), w2, preferred_element_type=F32)

    def emit(dst_ref):
        for s in range(MOE_TM // SEG):
            rows = slice(s * SEG, (s + 1) * SEG)
            y = x_ref[rows, :] + mod_ref[s][5:6] * acc_sc[rows, :]
            if final_norm:
                y = y * lax.rsqrt(jnp.mean(y * y, axis=-1, keepdims=True) + RMS_EPS) * gf_ref[...]
            dst_ref[rows, :] = y

    last = e == N_EXPERTS // es - 1
    if final_norm:
        is_ctx = pl.program_id(0) < MOE_CTX_TILES
        pl.when(jnp.logical_and(last, is_ctx))(lambda: emit(o_refs[0]))
        pl.when(jnp.logical_and(last, jnp.logical_not(is_ctx)))(lambda: emit(o_refs[1]))
    else:
        pl.when(last)(lambda: emit(o_refs[0]))


def _moe(x, mods, g, router_w, router_b, w1, w3, w2, layer, gf, final_norm):
    spt = MOE_TM // SEG
    es = MOE_EXPERTS_STEP
    tile = lambda f, **kw: pl.BlockSpec((MOE_TM, D), lambda t, e: (f(t), 0), **kw)
    if final_norm:
        n_half = M_TOK // 2
        out_specs = [tile(lambda t: jnp.minimum(t, MOE_CTX_TILES - 1), pipeline_mode=pl.Buffered(1)),
                     tile(lambda t: jnp.maximum(t - MOE_CTX_TILES, 0), pipeline_mode=pl.Buffered(1))]
        out_shape = [jax.ShapeDtypeStruct((n_half, D), F32), jax.ShapeDtypeStruct((M_TOK - n_half, D), F32)]
    else:
        out_specs = [tile(lambda t: t)]
        out_shape = [jax.ShapeDtypeStruct((M_TOK, D), F32)]
    return pl.pallas_call(
        functools.partial(_moe_kernel, final_norm),
        grid=(M_TOK // MOE_TM, N_EXPERTS // es),
        in_specs=[pl.BlockSpec((MOE_TM, D), lambda t, e: (t, 0)),
                  pl.BlockSpec((spt, 6, D), lambda t, e: (t, 0, 0)),
                  pl.BlockSpec((1, D), lambda t, e: (0, 0)),
                  pl.BlockSpec((N_EXPERTS, D), lambda t, e: (0, 0)),
                  pl.BlockSpec((N_EXPERTS, 1), lambda t, e: (0, 0)),
                  pl.BlockSpec((1, es, D, D_EXPERT), lambda t, e: (layer, e, 0, 0)),
                  pl.BlockSpec((1, es, D, D_EXPERT), lambda t, e: (layer, e, 0, 0)),
                  pl.BlockSpec((1, es, D_EXPERT, D), lambda t, e: (layer, e, 0, 0)),
                  pl.BlockSpec((1, D), lambda t, e: (0, 0))],
        out_specs=out_specs,
        out_shape=out_shape,
        scratch_shapes=[pltpu.VMEM((MOE_TM, D), BF16), pltpu.VMEM((MOE_TM, COMB_LANES), F32),
                        pltpu.VMEM((MOE_TM, D), F32)],
        compiler_params=_cp(("arbitrary", "arbitrary")),
        name="moe",
    )(x, mods, g.reshape(1, D), router_w.T, router_b.reshape(N_EXPERTS, 1), w1, w3, w2, gf.reshape(1, D))


def _rw_in_kernel(x_ref, xp_ref, xn_ref, mod_ref, g_ref, mu_ref, wr_ref, wk_ref, wv_ref, g1_ref, g2_ref,
                  w1_ref, w2_ref, w0_ref, a1_ref, a2_ref, a0_ref, ka_ref, rk_ref, hsel_ref,
                  r_ref, k_ref, v_ref, gg_ref, lw_ref, a_ref, coef_ref, wrkv_sc):
    i = pl.program_id(0)

    @pl.when(i == 0)
    def _():
        for c, w_ref in enumerate((wr_ref, wk_ref, wv_ref)):
            wrkv_sc[c] = w_ref[0].astype(BF16)

    m = mod_ref[0]
    g = g_ref[...]
    h = _rms_mod(x_ref[...], g, m[1:2], m[0:1])
    hp = jnp.where(_seg_is_first(i), 0.0, _rms_mod(xp_ref[...], g, m[1:2], m[0:1]))
    hn = jnp.where(_seg_is_last(i), 0.0, _rms_mod(xn_ref[...], g, m[1:2], m[0:1]))
    ext = jnp.concatenate([hp, h, hn], axis=0)
    n = ext.shape[0]
    h_prev = pltpu.roll(ext, 1, 0)[8:8 + SEG]
    h_next = pltpu.roll(ext, n - 1, 0)[8:8 + SEG]
    xx = 0.5 * (h_prev + h_next) - h
    mu = mu_ref[...]
    xr, xw, xk, xv, xa, xg = [h + xx * mu[c:c + 1] for c in range(6)]
    r = _dot(xr, wrkv_sc[0])
    k = _dot(xk, wrkv_sc[1])
    r_ref[...] = r.astype(BF16)
    k_ref[...] = k.astype(BF16)
    v_ref[...] = _dot(xv, wrkv_sc[2]).astype(BF16)
    gg_ref[...] = _dot(_sigmoid(_dot(xg, g1_ref[...])), g2_ref[...]).astype(BF16)
    half_w_in = w0_ref[...] + _dot(jnp.tanh(_dot(xw, w1_ref[...])), w2_ref[...])
    lw_ref[...] = (-0.5 * W_DECAY_SCALE * LOG2E) * jnp.tanh(half_w_in) - 0.5 * W_DECAY_SCALE * LOG2E
    a = 0.5 * jnp.tanh(a0_ref[...] + _dot(_dot(xa, a1_ref[...]), a2_ref[...])) + 0.5
    a_ref[...] = a
    kd_sum = k * (2.0 + (a[:, 0:D] + a[:, D:2 * D] - 2.0) * ka_ref[...])
    coef_ref[...] = _dot_sel_rhs(r * kd_sum * rk_ref[...], hsel_ref[...])


def _rw_inproj(x, mods, g, mu, wr, wk, wv, g1, g2, w1c, w2bd, w0c, a1c, a2bd, a0c, ka, rk):
    full = lambda a: pl.BlockSpec(a.shape, lambda i: (0,) * a.ndim)
    once = lambda a: pl.BlockSpec((1,) + a.shape[1:], lambda i: (0,) * a.ndim, pipeline_mode=pl.Buffered(1))
    seg = lambda width: pl.BlockSpec((SEG, width), lambda i: (i, 0))
    outs = ([jax.ShapeDtypeStruct((M_TOK, D), BF16)] * 4 + [jax.ShapeDtypeStruct((M_TOK, 2 * D), F32)] * 2
            + [jax.ShapeDtypeStruct((M_TOK, COEF_LANES), F32)])
    hsel = jnp.asarray(np.arange(D)[:, None] // C_HEAD == np.arange(COEF_LANES)[None, :], BF16)
    return pl.pallas_call(
        _rw_in_kernel,
        grid=(NSEG,),
        in_specs=[seg(D),
                  pl.BlockSpec((8, D), lambda i: (_prev8(i), 0)),
                  pl.BlockSpec((8, D), lambda i: (_next8(i), 0)),
                  pl.BlockSpec((1, 6, D), lambda i: (i, 0, 0)),
                  full(g), full(mu), once(wr), once(wk), once(wv), full(g1), full(g2),
                  full(w1c), full(w2bd), full(w0c), full(a1c), full(a2bd), full(a0c),
                  full(ka), full(rk), full(hsel)],
        out_specs=[seg(D)] * 4 + [seg(2 * D)] * 2 + [seg(COEF_LANES)],
        out_shape=outs,
        scratch_shapes=[pltpu.VMEM((3, D, D), BF16)],
        compiler_params=_cp(("arbitrary",)),
        name="rwkv_inproj",
    )(x, x, x, mods, g, mu, wr, wk, wv, g1, g2, w1c, w2bd, w0c, a1c, a2bd, a0c, ka, rk, hsel)


def _rw_consts():
    t = np.arange(SEG)
    same = (t[:, None] // RW_CHUNK) == (t[None, :] // RW_CHUNK)
    tri_f = np.logical_and(same, t[None, :] <= t[:, None]).astype(np.float32)
    tri_b = np.logical_and(same, t[None, :] >= t[:, None]).astype(np.float32)
    return [jnp.asarray(a, BF16) for a in (tri_f, tri_b)]


def _rw_dir(r, k, v, lw, a, kk_g, ka_g, tri, rev):
    c = RW_CHUNK
    lane = lax.broadcasted_iota(jnp.int32, (SEG, 2 * C_HEAD), 1)
    head0 = lane < C_HEAD
    kx = k * kk_g
    ss = kx * kx
    n0 = jnp.sum(jnp.where(head0, ss, 0.0), axis=-1, keepdims=True)
    n1 = jnp.sum(jnp.where(head0, 0.0, ss), axis=-1, keepdims=True)
    kk = kx / jnp.maximum(jnp.sqrt(jnp.where(head0, n0, n1)), 1e-12)
    kd = k * (1.0 + (a - 1.0) * ka_g)
    bhat = kk * a
    cum = _dot_sel(tri, lw)
    e_incl = jnp.exp2(cum)
    e_inv = jnp.exp2(-cum)
    ae = -kk * jnp.exp2(cum - lw)
    re = r * e_incl
    bi = bhat * e_inv
    ki = kd * e_inv

    chunks = []
    for ci in range(SEG // c):
        sl = slice(ci * c, (ci + 1) * c)
        ctot = cum[ci * c:ci * c + 1] if rev else cum[(ci + 1) * c - 1:(ci + 1) * c]
        dec = jnp.exp2(ctot - cum[sl])
        chunks.append(dict(ae=ae[sl], re=re[sl], bi=bi[sl], ki=ki[sl], v=v[sl], bdec=bhat[sl] * dec,
                           kdec=kd[sl] * dec, gam=jnp.exp2(ctot), rev=rev))
    return chunks


def _rw_transitions(chunks):
    c = RW_CHUNK
    w2 = 2 * C_HEAD
    h0c = lax.broadcasted_iota(jnp.int32, (c, w2), 1) < C_HEAD
    rowc = lax.broadcasted_iota(jnp.int32, (c, w2), 0)
    colc = jnp.bitwise_and(lax.broadcasted_iota(jnp.int32, (c, w2), 1), C_HEAD - 1)
    eye = (colc == rowc).astype(F32)
    bdmask = (lax.broadcasted_iota(jnp.int32, (w2, w2), 0) < C_HEAD) == (
        lax.broadcasted_iota(jnp.int32, (w2, w2), 1) < C_HEAD)
    keep0 = jnp.where(h0c, 1.0, 0.0).astype(BF16)
    keep1 = jnp.where(h0c, 0.0, 1.0).astype(BF16)

    def bd16(yb):
        return jnp.concatenate([yb * keep0, yb * keep1], axis=0)

    def dot_bd(x, pairs):
        blocks = [bd16(y.astype(BF16)) for y in pairs]
        rhs = blocks[0] if len(blocks) == 1 else jnp.concatenate(blocks, axis=1)
        return jnp.dot(x.astype(BF16), rhs, preferred_element_type=F32)

    mm_inv = lambda x, y: dot_bd(x, [y])

    n_ab, a_ak, a_rb, a_rk = [], [], [], []
    for ch in chunks:
        strict = (colc > rowc) if ch["rev"] else (colc < rowc)
        incl = (colc >= rowc) if ch["rev"] else (colc <= rowc)
        left = jnp.concatenate([ch["ae"], ch["re"]], axis=0)
        right = jnp.concatenate([jnp.where(h0c, ch["bi"], 0.0), jnp.where(h0c, 0.0, ch["bi"]),
                                 jnp.where(h0c, ch["ki"], 0.0), jnp.where(h0c, 0.0, ch["ki"])], axis=0)
        gm = _dot_nt(left, right)
        n_ab.append(jnp.where(strict, gm[0:c, 0:2 * c], 0.0))
        a_ak.append(jnp.where(strict, gm[0:c, 2 * c:4 * c], 0.0))
        a_rb.append(jnp.where(incl, gm[c:2 * c, 0:2 * c], 0.0))
        a_rk.append(jnp.where(incl, gm[c:2 * c, 2 * c:4 * c], 0.0))
    xorc = jnp.bitwise_xor(rowc, colc)
    tm = [eye + jnp.where(xorc < 2, n, 0.0) for n in n_ab]
    blk = 2
    while blk < c:
        couple = jnp.logical_and(xorc >= blk, xorc < 2 * blk)
        xs = [mm_inv(jnp.where(couple, n, 0.0), t) for n, t in zip(n_ab, tm)]
        tm = [t + mm_inv(t, x) for t, x in zip(tm, xs)]
        blk *= 2
    akv = [dot_bd(jnp.concatenate([x, y], axis=0), [ch["v"]])
           for x, y, ch in zip(a_ak, a_rk, chunks)]
    tav = [dot_bd(t, [ch["ae"], x[0:c]]) for t, ch, x in zip(tm, chunks, akv)]
    out = []
    for i, ch in enumerate(chunks):
        ta, tv = tav[i][:, 0:w2], tav[i][:, w2:2 * w2]
        qy = dot_bd(a_rb[i], [ta, tv])
        q = ch["re"] + qy[:, 0:w2]
        y = qy[:, w2:2 * w2] + akv[i][c:2 * c]
        wz = _dot_tn(tav[i], ch["bdec"])
        w = jnp.where(bdmask, wz[0:w2], 0.0)
        z = jnp.where(bdmask, wz[w2:2 * w2] + _dot_tn(ch["v"], ch["kdec"]), 0.0)
        out.append((q, y, w, z, ch["gam"]))
    return out


def _rw_scan_kernel(rf, kf, vf, lwf, af, rb, kb, vb, lwb, ab, kkg_ref, kag_ref, s0f, s0b, trif, trib,
                    of_ref, ob_ref, fin_ref, st_f, st_b):
    j = pl.program_id(1)
    w2 = 2 * C_HEAD
    dirs = ((False, (rf, kf, vf, lwf, af), s0f, trif, of_ref, 0, st_f),
            (True, (rb, kb, vb, lwb, ab), s0b, trib, ob_ref, 1, st_b))
    for rev, refs, s0, tri, o_ref, s_out, st in dirs:
        i = (NSEG - 1 - j) if rev else j
        starts = _seg_is_last(i) if rev else _seg_is_first(i)

        @pl.when(jnp.logical_and(starts, i < N_CTX_SEG))
        def _():
            st[...] = jnp.zeros_like(st)

        @pl.when(jnp.logical_and(starts, i >= N_CTX_SEG))
        def _():
            st[...] = s0[0, 0]

    chunks = []
    for rev, refs, s0, tri, o_ref, s_out, st in dirs:
        for pp in range(RW_PAIRS_STEP):
            lanes = slice(pp * w2, (pp + 1) * w2)
            r_, k_, v_, lw_, a_ = [z[:, lanes].astype(F32) for z in refs]
            chunks.extend(_rw_dir(r_, k_, v_, lw_, a_, kkg_ref[:, lanes], kag_ref[:, lanes], tri[...], rev))
    trans = _rw_transitions(chunks)
    n_c = SEG // RW_CHUNK
    states = [[st[pp] for pp in range(RW_PAIRS_STEP)] for (_, _, _, _, _, _, st) in dirs]
    for step in range(n_c):
        for d, (rev, refs, s0, tri, o_ref, s_out, st) in enumerate(dirs):
            ci = (n_c - 1 - step) if rev else step
            for pp in range(RW_PAIRS_STEP):
                q, y, w, z, gam = trans[(d * RW_PAIRS_STEP + pp) * n_c + ci]
                s = states[d][pp]
                o_ref[ci * RW_CHUNK:(ci + 1) * RW_CHUNK, pp * w2:(pp + 1) * w2] = (_dot_nt(q, s) + y).astype(BF16)
                states[d][pp] = s * gam + _dot(s, w) + z
    for d, (rev, refs, s0, tri, o_ref, s_out, st) in enumerate(dirs):
        for pp in range(RW_PAIRS_STEP):
            st[pp] = states[d][pp]
    for d, (rev, refs, s0, tri, o_ref, s_out, st) in enumerate(dirs):
        i = (NSEG - 1 - j) if rev else j

        @pl.when(i < N_CTX_SEG)
        def _():
            for pp in range(RW_PAIRS_STEP):
                s = states[d][pp]
                head = 2 * (pl.program_id(0) * RW_PAIRS_STEP + pp)
                fin_ref[i, 0, d, head] = s[0:C_HEAD, 0:C_HEAD]
                fin_ref[i, 0, d, head + 1] = s[C_HEAD:w2, C_HEAD:w2]


def _rw_scan(r, k, v, lw, a, kk_g, ka_g, s0bd):
    consts = _rw_consts()
    w = 2 * C_HEAD
    pps = RW_PAIRS_STEP
    wb = pps * w
    n_steps = C_PAIRS // pps
    blk = lambda col0, rev: pl.BlockSpec(
        (SEG, wb), (lambda p, j: (NSEG - 1 - j, col0 + p)) if rev else (lambda p, j: (j, col0 + p)))
    cspec = lambda arr: pl.BlockSpec(arr.shape, lambda p, j: (0, 0))
    fin_shape = (N_CTX_SEG, 1, 2, 2 * C_PAIRS, C_HEAD, C_HEAD)
    fin = pl.BlockSpec(fin_shape, lambda p, j: (0,) * len(fin_shape), pipeline_mode=pl.Buffered(1))
    return pl.pallas_call(
        _rw_scan_kernel,
        grid=(n_steps, NSEG),
        in_specs=[blk(0, False), blk(0, False), blk(0, False), blk(0, False), blk(0, False),
                  blk(0, True), blk(0, True), blk(0, True), blk(n_steps, True), blk(n_steps, True),
                  pl.BlockSpec((1, wb), lambda p, j: (0, p)), pl.BlockSpec((1, wb), lambda p, j: (0, p)),
                  pl.BlockSpec((1, 1, pps, w, w), lambda p, j: (_sample_of(j), 0, p, 0, 0)),
                  pl.BlockSpec((1, 1, pps, w, w), lambda p, j: (_sample_of(NSEG - 1 - j), 1, p, 0, 0)),
                  cspec(consts[0]), cspec(consts[1])],
        out_specs=[blk(0, False), blk(0, True), fin],
        out_shape=[jax.ShapeDtypeStruct((M_TOK, D), BF16), jax.ShapeDtypeStruct((M_TOK, D), BF16),
                   jax.ShapeDtypeStruct(fin_shape, F32)],
        scratch_shapes=[pltpu.VMEM((pps, w, w), F32), pltpu.VMEM((pps, w, w), F32)],
        compiler_params=_cp(("arbitrary", "arbitrary")),
        name="rwkv7_scan",
    )(r, k, v, lw, a, r, k, v, lw, a, kk_g, ka_g, s0bd, s0bd, *consts)


def _rw_out_kernel(x_ref, of_ref, ob_ref, v_ref, gg_ref, coef_ref, mod_ref, lnw_ref, lnb_ref, ones_ref,
                   hexp_ref, wo_ref, o_ref, wo_sc):
    @pl.when(pl.program_id(0) == 0)
    def _():
        wo_sc[...] = wo_ref[0].astype(BF16)

    m = mod_ref[0]
    ones_bd = ones_ref[...]
    w = 2 * C_HEAD
    inv_n = 1.0 / C_HEAD
    coef = _dot_sel_rhs(coef_ref[...], hexp_ref[...])
    parts = []
    for p in range(C_PAIRS):
        cs = slice(p * w, (p + 1) * w)
        osum = of_ref[:, cs].astype(F32) + ob_ref[:, cs].astype(F32)
        mu = _dot_sel_rhs(osum, ones_bd) * inv_n
        cen = osum - mu
        var = _dot_sel_rhs(cen * cen, ones_bd) * inv_n
        o = cen * lax.rsqrt(var + GN_EPS) * lnw_ref[:, cs] + lnb_ref[:, cs]
        bonus = coef[:, cs] * v_ref[:, cs].astype(F32)
        parts.append((o + bonus) * gg_ref[:, cs].astype(F32))
    y = _dot(jnp.concatenate(parts, axis=-1), wo_sc[...])
    o_ref[...] = x_ref[...] + m[2:3] * y


def _dot_sel_rhs(x, mat):
    h = x.astype(BF16)
    l = (x - h.astype(F32)).astype(BF16)
    return jnp.dot(h, mat, preferred_element_type=F32) + jnp.dot(l, mat, preferred_element_type=F32)


def _rw_out(x, o_f, o_b, v, gg, coef, mods, lnw, lnb, wo):
    seg = lambda width: pl.BlockSpec((SEG, width), lambda i: (i, 0))
    row = pl.BlockSpec((1, D), lambda i: (0, 0))
    hh = np.arange(2 * C_HEAD) // C_HEAD
    ones_bd = jnp.asarray((hh[:, None] == hh[None, :]).astype(np.float32), BF16)
    hexp = jnp.asarray(np.arange(COEF_LANES)[:, None] == np.arange(D)[None, :] // C_HEAD, BF16)
    return pl.pallas_call(
        _rw_out_kernel,
        grid=(NSEG,),
        in_specs=[seg(D), seg(D), seg(D), seg(D), seg(D), seg(COEF_LANES),
                  pl.BlockSpec((1, 6, D), lambda i: (i, 0, 0)),
                  row, row,
                  pl.BlockSpec((2 * C_HEAD, 2 * C_HEAD), lambda i: (0, 0)),
                  pl.BlockSpec((COEF_LANES, D), lambda i: (0, 0)),
                  pl.BlockSpec((1, D, D), lambda i: (0, 0, 0), pipeline_mode=pl.Buffered(1))],
        out_specs=seg(D),
        out_shape=jax.ShapeDtypeStruct((M_TOK, D), F32),
        scratch_shapes=[pltpu.VMEM((D, D), BF16)],
        compiler_params=_cp(("arbitrary",)),
        name="rwkv_out",
    )(x, o_f, o_b, v, gg, coef, mods, lnw.reshape(1, D), lnb.reshape(1, D), ones_bd, hexp, wo)


def _grid_pos_table(n_tok):
    rows = n_tok // GRID_W
    r, cl = np.meshgrid(np.arange(rows, dtype=np.float32), np.arange(GRID_W, dtype=np.float32), indexing='ij')
    quarter = D // 4
    omega = (1.0 / (np.float32(POS_BASE) ** (np.arange(quarter, dtype=np.float32) / np.float32(quarter))))
    ang_r = (r.reshape(-1, 1) * omega).astype(np.float32)
    ang_c = (cl.reshape(-1, 1) * omega).astype(np.float32)
    table = np.concatenate([np.sin(ang_r), np.cos(ang_r), np.sin(ang_c), np.cos(ang_c)], axis=-1)
    return jnp.asarray(table.astype(np.float32))


def _block_diag(blocks):
    g, n, _ = blocks.shape
    eye = jnp.eye(g, dtype=blocks.dtype)
    return (eye[:, None, :, None] * blocks[:, :, None, :]).reshape(g * n, g * n)


def kernel(x_prompt, x_sample, state_hgrn, state_rglru, state_rwkv, c, c_ctx, norm_mix_g, norm_ffn_g, w_mod, b_mod, ab_w_in, ab_w_out, hgrn_lb, hgrn_norm_g, rg_conv_w, rg_conv_b, rg_wa, rg_ba, rg_wx, rg_bx, rg_lambda, rw_mu, rw_wr, rw_wk, rw_wv, rw_wo, rw_w0, rw_w1, rw_w2, rw_a0, rw_a1, rw_a2, rw_g1, rw_g2, rw_kk, rw_ka, rw_rk, rw_lnw, rw_lnb, moe_router, moe_router_bias, moe_w1, moe_w3, moe_w2, norm_f_g):
    bf = lambda z: z.astype(BF16)
    xp = x_prompt.reshape(-1, D)
    xs = x_sample.reshape(-1, D)
    pos = _grid_pos_table(x_sample.shape[1])
    mods = _modulations(c, c_ctx, w_mod, b_mod)

    lower_bounds = jnp.cumsum(jax.nn.softmax(hgrn_lb.astype(F32), axis=1), axis=1)
    lb = lower_bounds[:, 0].reshape(2, A_HEADS, 1, A_DK)
    proj_f, proj_h = _ab_inproj(xp, xs, pos, mods[0], norm_mix_g[0], ab_w_in)
    s0t = jnp.swapaxes(state_hgrn[:, 0], -1, -2)
    o_f, o_b, new_hgrn = _gla(proj_f, proj_h, lb, s0t)
    wa_bd = bf(jnp.stack([_block_diag(rg_wa[0, d]) for d in range(2)]))
    wx_bd = bf(jnp.stack([_block_diag(rg_wx[0, d]) for d in range(2)]))
    h_f, h_b, lru_f, lru_b = _rglru(
        proj_f, rg_conv_w[0], rg_conv_b[0].reshape(1, D_B), wa_bd, rg_ba[0].reshape(2, 1, D_B), wx_bd,
        rg_bx[0].reshape(2, 1, D_B), rg_lambda[0].reshape(2, 1, D_B), state_rglru[:, 0].reshape(-1, 2, 1, D_B))
    x = _ab_out(xp, xs, pos, proj_h, o_f, o_b, h_f, h_b, mods[0], hgrn_norm_g[0], ab_w_out)
    x, = _moe(x, mods[0], norm_ffn_g[0], moe_router, moe_router_bias, moe_w1, moe_w3, moe_w2, 0, norm_f_g, False)

    w1c = bf(jnp.concatenate([rw_w1[0, 0], rw_w1[0, 1]], axis=-1))
    a1c = bf(jnp.concatenate([rw_a1[0, 0], rw_a1[0, 1]], axis=-1))
    half_w2, half_a2 = 0.5 * rw_w2[0], 0.5 * rw_a2[0]
    w2bd = bf(jnp.concatenate([jnp.concatenate([half_w2[0], jnp.zeros_like(half_w2[0])], axis=-1),
                               jnp.concatenate([jnp.zeros_like(half_w2[1]), half_w2[1]], axis=-1)], axis=0))
    a2bd = bf(jnp.concatenate([jnp.concatenate([half_a2[0], jnp.zeros_like(half_a2[0])], axis=-1),
                               jnp.concatenate([jnp.zeros_like(half_a2[1]), half_a2[1]], axis=-1)], axis=0))
    r, k, v, gg, lw, a, coef = _rw_inproj(
        x, mods[1], norm_mix_g[1].reshape(1, D), rw_mu[0], rw_wr, rw_wk, rw_wv,
        bf(rw_g1[0]), bf(rw_g2[0]), w1c, w2bd, 0.5 * rw_w0[0].reshape(1, 2 * D), a1c, a2bd,
        0.5 * rw_a0[0].reshape(1, 2 * D), rw_ka[0].reshape(1, D), rw_rk[0].reshape(1, D))
    s0 = state_rwkv[:, 0].reshape(N_SAMPLE, 2, C_PAIRS, 2, C_HEAD, C_HEAD)
    zeros = jnp.zeros_like(s0[:, :, :, 0])
    s0bd = jnp.concatenate([jnp.concatenate([s0[:, :, :, 0], zeros], axis=-1),
                            jnp.concatenate([zeros, s0[:, :, :, 1]], axis=-1)], axis=-2)
    ow_f, ow_b, new_rwkv = _rw_scan(r, k, v, lw, a, rw_kk[0].reshape(1, D), rw_ka[0].reshape(1, D), s0bd)
    x = _rw_out(x, ow_f, ow_b, v, gg, coef, mods[1], rw_lnw[0], rw_lnb[0], rw_wo)
    y_p, y_s = _moe(x, mods[1], norm_ffn_g[1], moe_router, moe_router_bias, moe_w1, moe_w3, moe_w2, 1, norm_f_g, True)

    new_rglru = jnp.stack([lru_f[:, 0], lru_b[:, 0]], axis=1)[:, None]
    return (y_p.reshape(x_prompt.shape), y_s.reshape(x_sample.shape), new_hgrn, new_rglru, new_rwkv)
```

```python
import functools
import math

import numpy as np
import jax
import jax.numpy as jnp
from jax import lax
from jax.experimental import pallas as pl
from jax.experimental.pallas import tpu as pltpu

F32 = jnp.float32
BF16 = jnp.bfloat16

D = 1024
SEG = 256
N_CTX_SEG = 16
SEG_PER_SAMPLE = 4
N_SAMPLE = 4
NSEG = N_CTX_SEG + N_SAMPLE * SEG_PER_SAMPLE
M_TOK = NSEG * SEG
SUBLANES = 8
ROWS8_PER_SEG = SEG // SUBLANES

A_HEADS = 4
A_DK = 128
D_A = 512
D_B = 512
B_BLOCKS = 8
B_BLOCK = 64
LRU_C = 8.0
D_IN_AB = 5 * D_A + 2 * D_B
AB_F32_COLS = 2 * D_A + D_B
C_HEAD = 64
C_PAIRS = 8
RW_CHUNK = 64
COEF_LANES = 128
RW_PAIRS_STEP = 4
W_DECAY_SCALE = math.exp(-0.5)
LOG2E = math.log2(math.e)
N_EXPERTS = 16
N_GROUPS = 4
GROUP = 4
D_EXPERT = 256
RMS_EPS = 1e-6
GN_EPS = 64e-5
POS_BASE = 10000.0
GRID_W = 64
MOE_TM = 1024
MOE_CTX_TILES = N_CTX_SEG * SEG // MOE_TM
MOE_EXPERTS_STEP = 4
COMB_LANES = 128
GLA_LEVELS = (1, 2, 4, 8, 16, 32, 64, 128)
GLA_HALF = 128
GLA_HEADS_STEP = 4

VMEM_LIMIT = 56 * 1024 * 1024


def _cp(sem):
    return pltpu.CompilerParams(dimension_semantics=sem, vmem_limit_bytes=VMEM_LIMIT)


def _sigmoid(x):
    return 0.5 * jnp.tanh(0.5 * x) + 0.5


def _silu(x):
    return x * _sigmoid(x)


def _gelu_tanh(x):
    return 0.5 * x * (1.0 + jnp.tanh(math.sqrt(2.0 / math.pi) * (x + 0.044715 * (x * x * x))))


def _rms_mod(x, g, scale, shift):
    ms = jnp.mean(x * x, axis=-1, keepdims=True)
    return x * lax.rsqrt(ms + RMS_EPS) * (g * (1.0 + scale)) + shift


def _dot(a, b):
    return jnp.dot(a.astype(BF16), b.astype(BF16), preferred_element_type=F32)


def _dot_nt(a, b):
    return lax.dot_general(a.astype(BF16), b.astype(BF16), (((1,), (1,)), ((), ())),
                           preferred_element_type=F32)


def _dot_tn(a, b):
    return lax.dot_general(a.astype(BF16), b.astype(BF16), (((0,), (0,)), ((), ())),
                           preferred_element_type=F32)


def _split3(x):
    h = x.astype(BF16)
    r1 = x - h.astype(F32)
    m = r1.astype(BF16)
    r2 = r1 - m.astype(F32)
    return h, m, r2.astype(BF16)


def _dot_sel(mat, x):
    h, m, l = _split3(x)
    return (jnp.dot(mat, h, preferred_element_type=F32) + jnp.dot(mat, m, preferred_element_type=F32)
            + jnp.dot(mat, l, preferred_element_type=F32))


def _dot_x3_nt(a, b):
    dn = (((1,), (1,)), ((), ()))
    ah = a.astype(BF16)
    al = (a - ah.astype(F32)).astype(BF16)
    bh = b.astype(BF16)
    bl = (b - bh.astype(F32)).astype(BF16)
    return (lax.dot_general(ah, bh, dn, preferred_element_type=F32)
            + lax.dot_general(ah, bl, dn, preferred_element_type=F32)
            + lax.dot_general(al, bh, dn, preferred_element_type=F32))


def _seg_is_first(i):
    return jnp.logical_or(i < N_CTX_SEG, lax.rem(i - N_CTX_SEG, SEG_PER_SAMPLE) == 0)


def _seg_is_last(i):
    return jnp.logical_or(i < N_CTX_SEG, lax.rem(i - N_CTX_SEG, SEG_PER_SAMPLE) == SEG_PER_SAMPLE - 1)


def _sample_of(i):
    return jnp.maximum(i - N_CTX_SEG, 0) // SEG_PER_SAMPLE


def _prev8(i):
    return jnp.maximum(i * ROWS8_PER_SEG - 1, 0)


def _next8(i):
    return jnp.minimum((i + 1) * ROWS8_PER_SEG, M_TOK // 8 - 1)


def _mod_kernel(cv_ref, w_ref, b_ref, o_ref):
    cv = cv_ref[...]
    o_ref[0] = _dot(_silu(cv), w_ref[0]) + b_ref[0]


def _modulations(c, c_ctx, w_mod, b_mod):
    depth = w_mod.shape[0]
    cv = jnp.concatenate([c_ctx[None, :], c, jnp.zeros((3, D), F32)], axis=0)
    wt = 2 * D
    mod = pl.pallas_call(
        _mod_kernel,
        grid=(depth, 6 * D // wt),
        in_specs=[pl.BlockSpec((8, D), lambda l, n: (0, 0)),
                  pl.BlockSpec((1, D, wt), lambda l, n: (l, 0, n)),
                  pl.BlockSpec((1, 1, wt), lambda l, n: (l, 0, n))],
        out_specs=pl.BlockSpec((1, 8, wt), lambda l, n: (l, 0, n)),
        out_shape=jax.ShapeDtypeStruct((depth, 8, 6 * D), F32),
        compiler_params=_cp(("arbitrary", "arbitrary")),
        name="adaln_mod",
    )(cv, w_mod, b_mod.reshape(depth, 1, 6 * D))
    row_of_seg = np.array([0] * N_CTX_SEG + [1 + s // SEG_PER_SAMPLE for s in range(N_SAMPLE * SEG_PER_SAMPLE)])
    return mod[:, row_of_seg].reshape(depth, NSEG, 6, D)


def _x0_specs():
    return [pl.BlockSpec((SEG, D), lambda i: (jnp.minimum(i, N_CTX_SEG - 1), 0)),
            pl.BlockSpec((SEG, D), lambda i: (jnp.maximum(i - N_CTX_SEG, 0), 0)),
            pl.BlockSpec((SEG, D), lambda i: (lax.rem(jnp.maximum(i - N_CTX_SEG, 0), SEG_PER_SAMPLE), 0))]


def _x0(i, xp_ref, xs_ref, pos_ref):
    return jnp.where(i < N_CTX_SEG, xp_ref[...], xs_ref[...] + pos_ref[...])


def _ab_in_kernel(xp_ref, xs_ref, pos_ref, mod_ref, g_ref, w_ref, of_ref, oh_ref, w_sc):
    @pl.when(pl.program_id(0) == 0)
    def _():
        w_sc[...] = w_ref[0].astype(BF16)

    m = mod_ref[0]
    x = _x0(pl.program_id(0), xp_ref, xs_ref, pos_ref)
    h = _rms_mod(x, g_ref[...], m[1:2], m[0:1])
    res = jnp.dot(h.astype(BF16), w_sc[...], preferred_element_type=F32)
    of_ref[:, 0:2 * D_A] = res[:, D_A:3 * D_A]
    of_ref[:, 2 * D_A:AB_F32_COLS] = res[:, 5 * D_A:5 * D_A + D_B]
    oh_ref[:, 0:D_A] = res[:, 0:D_A].astype(BF16)
    oh_ref[:, D_A:3 * D_A] = res[:, 3 * D_A:5 * D_A].astype(BF16)
    oh_ref[:, 3 * D_A:3 * D_A + D_B] = res[:, 5 * D_A + D_B:D_IN_AB].astype(BF16)


def _ab_inproj(xp, xs, pos, mods, g, w_in):
    return pl.pallas_call(
        _ab_in_kernel,
        grid=(NSEG,),
        in_specs=_x0_specs() + [pl.BlockSpec((1, 6, D), lambda i: (i, 0, 0)),
                                pl.BlockSpec((1, D), lambda i: (0, 0)),
                                pl.BlockSpec((1, D, D_IN_AB), lambda i: (0, 0, 0),
                                             pipeline_mode=pl.Buffered(1))],
        out_specs=[pl.BlockSpec((SEG, AB_F32_COLS), lambda i: (i, 0)),
                   pl.BlockSpec((SEG, D_IN_AB - AB_F32_COLS), lambda i: (i, 0))],
        out_shape=[jax.ShapeDtypeStruct((M_TOK, AB_F32_COLS), F32),
                   jax.ShapeDtypeStruct((M_TOK, D_IN_AB - AB_F32_COLS), BF16)],
        scratch_shapes=[pltpu.VMEM((D, D_IN_AB), BF16)],
        compiler_params=_cp(("arbitrary",)),
        name="ab_inproj",
    )(xp, xs, pos, mods, g.reshape(1, D), w_in)


def _gla_consts():
    t = np.arange(SEG)
    tri_f = (t[None, :] <= t[:, None]).astype(np.float32)
    tri_b = (t[None, :] >= t[:, None]).astype(np.float32)
    th = np.arange(GLA_HALF)
    xor = th[:, None] ^ th[None, :]
    hb = np.where(xor > 0, 1 << np.floor(np.log2(np.maximum(xor, 1))).astype(np.int64), 0)
    code_f = np.where(th[None, :] < th[:, None], hb, 0).astype(np.int32)
    code_b = np.where(th[None, :] > th[:, None], hb, 0).astype(np.int32)
    return [jnp.asarray(tri_f, BF16), jnp.asarray(tri_b, BF16), jnp.asarray(code_f), jnp.asarray(code_b)]


def _gla_level_operand(q, k, b, g, rowi, w, rev):
    upper = jnp.bitwise_and(rowi, w) != 0
    qside = jnp.logical_not(upper) if rev else upper
    if w == 1:
        z = jnp.where(qside, g, 0.0)
    elif w >= SUBLANES:
        nv = 2 * w // SUBLANES
        b4 = b.reshape(SEG // (2 * w), nv, SUBLANES, A_DK)
        ref = (b4[:, nv // 2:nv // 2 + 1, 0:1, :] if rev
               else b4[:, nv // 2 - 1:nv // 2, SUBLANES - 1:SUBLANES, :])
        x = (b4 - ref).reshape(SEG, A_DK)
        z = jnp.where(qside, x, -x)
    else:
        b3 = b.reshape(SEG // SUBLANES, SUBLANES, A_DK)
        sub = lax.broadcasted_iota(jnp.int32, b3.shape, 1)
        beta = None
        for jb in range(SUBLANES // (2 * w)):
            r = jb * 2 * w + (w if rev else w - 1)
            cand = jnp.broadcast_to(b3[:, r:r + 1, :], b3.shape)
            beta = cand if beta is None else jnp.where(sub >= jb * 2 * w, cand, beta)
        x = (b3 - beta).reshape(SEG, A_DK)
        z = jnp.where(qside, x, -x)
    return jnp.where(qside, q, k) * jnp.exp2(z)


def _gla_dir(qraw, fraw, v, lb, st, tri, code, rev):
    hh = GLA_HALF
    q = _silu(qraw)
    f = lb + (1.0 - lb) * _sigmoid(fraw)
    g = jnp.log2(f)
    k = 1.0 - f
    b = _dot_sel(tri, g)
    rowi = lax.broadcasted_iota(jnp.int32, (SEG, A_DK), 0)
    att = [jnp.zeros((hh, hh), F32), jnp.zeros((hh, hh), F32)]
    cross = None
    for w in GLA_LEVELS:
        m = _gla_level_operand(q, k, b, g, rowi, w, rev).astype(BF16)
        if w == hh:
            cross = _dot_nt(m[0:hh], m[hh:SEG]) if rev else _dot_nt(m[hh:SEG], m[0:hh])
        else:
            for half in range(2):
                mh = m[half * hh:(half + 1) * hh]
                att[half] = jnp.where(code == w, _dot_nt(mh, mh), att[half])
    if rev:
        o_lo = _dot(jnp.concatenate([att[0], cross], axis=1), v)
        o_hi = _dot(att[1], v[hh:SEG])
    else:
        o_lo = _dot(att[0], v[0:hh])
        o_hi = _dot(jnp.concatenate([cross, att[1]], axis=1), v)
    diag = jnp.sum(q * k, axis=-1, keepdims=True)
    o = jnp.concatenate([o_lo, o_hi], axis=0) + diag * v + _dot_nt(q * jnp.exp2(b), st)
    btot = b[0:1] if rev else b[SEG - 1:SEG]
    st_new = st * jnp.exp2(btot) + _dot_tn(v, k * jnp.exp2(btot - b))
    return o, st_new


def _gla_kernel(qf, ff, vf, qb, fb, vb, lb_ref, s0f, s0b, trif, trib, codef, codeb,
                of_ref, ob_ref, fin_ref, st_f, st_b):
    j = pl.program_id(1)
    dirs = ((False, qf, ff, vf, s0f, trif, codef, of_ref, st_f),
            (True, qb, fb, vb, s0b, trib, codeb, ob_ref, st_b))
    for rev, qr, fr, vr, s0, tri, code, o_ref, st in dirs:
        i = (NSEG - 1 - j) if rev else j
        starts = _seg_is_last(i) if rev else _seg_is_first(i)

        @pl.when(jnp.logical_and(starts, i < N_CTX_SEG))
        def _():
            st[...] = jnp.zeros_like(st)

        @pl.when(jnp.logical_and(starts, i >= N_CTX_SEG))
        def _():
            st[...] = s0[0, 0]

    finals = []
    for d, (rev, qr, fr, vr, s0, tri, code, o_ref, st) in enumerate(dirs):
        for hh in range(GLA_HEADS_STEP):
            lanes = slice(hh * A_DK, (hh + 1) * A_DK)
            o, st_new = _gla_dir(qr[:, lanes].astype(F32), fr[:, lanes], vr[:, lanes].astype(F32),
                                 lb_ref[d, hh], st[hh], tri[...], code[...], rev)
            o_ref[:, lanes] = o.astype(BF16)
            finals.append(st_new)
    for d, (rev, qr, fr, vr, s0, tri, code, o_ref, st) in enumerate(dirs):
        for hh in range(GLA_HEADS_STEP):
            st[hh] = finals[d * GLA_HEADS_STEP + hh]
    for d, (rev, qr, fr, vr, s0, tri, code, o_ref, st) in enumerate(dirs):
        i = (NSEG - 1 - j) if rev else j

        @pl.when(i < N_CTX_SEG)
        def _():
            for hh in range(GLA_HEADS_STEP):
                fin_ref[i, 0, d, pl.program_id(0) * GLA_HEADS_STEP + hh] = finals[d * GLA_HEADS_STEP + hh].T


def _gla(proj_f, proj_h, lb, s0t):
    consts = _gla_consts()
    hs = GLA_HEADS_STEP
    wb = hs * A_DK
    n_col = D_A // wb
    blk = lambda col0, rev: pl.BlockSpec(
        (SEG, wb), (lambda h, j: (NSEG - 1 - j, col0 + h)) if rev else (lambda h, j: (j, col0 + h)))
    cspec = lambda a: pl.BlockSpec(a.shape, lambda h, j: (0, 0))
    fin_shape = (N_CTX_SEG, 1, 2, A_HEADS, A_DK, A_DK)
    fin = pl.BlockSpec(fin_shape, lambda h, j: (0,) * len(fin_shape))
    return pl.pallas_call(
        _gla_kernel,
        grid=(A_HEADS // hs, NSEG),
        in_specs=[blk(0, False), blk(0, False), blk(n_col, False),
                  blk(0, True), blk(n_col, True), blk(n_col, True),
                  pl.BlockSpec((2, hs, 1, A_DK), lambda h, j: (0, h, 0, 0)),
                  pl.BlockSpec((1, 1, hs, A_DK, A_DK), lambda h, j: (_sample_of(j), 0, h, 0, 0)),
                  pl.BlockSpec((1, 1, hs, A_DK, A_DK), lambda h, j: (_sample_of(NSEG - 1 - j), 1, h, 0, 0)),
                  cspec(consts[0]), cspec(consts[1]), cspec(consts[2]), cspec(consts[3])],
        out_specs=[blk(0, False), blk(0, True), fin],
        out_shape=[jax.ShapeDtypeStruct((M_TOK, D_A), BF16), jax.ShapeDtypeStruct((M_TOK, D_A), BF16),
                   jax.ShapeDtypeStruct(fin_shape, F32)],
        scratch_shapes=[pltpu.VMEM((hs, A_DK, A_DK), F32), pltpu.VMEM((hs, A_DK, A_DK), F32)],
        compiler_params=_cp(("arbitrary", "arbitrary")),
        name="hgrn2_gla",
    )(proj_h, proj_f, proj_h, proj_h, proj_f, proj_h, lb, s0t, s0t, *consts)


def _lin_scan(a, b, h_in, rev):
    t_len, c = a.shape
    ng = t_len // SUBLANES
    a3 = a.reshape(ng, SUBLANES, c)
    b3 = b.reshape(ng, SUBLANES, c)
    sub = lax.broadcasted_iota(jnp.int32, a3.shape, 1)
    s = 1
    while s < SUBLANES:
        shift = (SUBLANES - s) if rev else s
        valid = (sub < SUBLANES - s) if rev else (sub >= s)
        ap = jnp.where(valid, pltpu.roll(a3, shift, 1), 1.0)
        bp = jnp.where(valid, pltpu.roll(b3, shift, 1), 0.0)
        b3 = a3 * bp + b3
        a3 = a3 * ap
        s *= 2
    hs = [None] * ng
    carry = h_in
    for j in (range(ng - 1, -1, -1) if rev else range(ng)):
        hs[j] = a3[j] * carry + b3[j]
        carry = hs[j][0:1] if rev else hs[j][SUBLANES - 1:SUBLANES]
    return jnp.concatenate(hs, axis=0), carry


def _rglru_dir(x, xprev, xnext, first, last, cw, cb, wa, ba, wx, bx, lam, h_in, rev):
    zero = jnp.zeros_like(xprev)
    ext = jnp.concatenate([jnp.where(first, zero, xprev), x, jnp.where(last, zero, xnext)], axis=0)
    n = ext.shape[0]
    xm2 = pltpu.roll(ext, 2, 0)[8:8 + SEG]
    xm1 = pltpu.roll(ext, 1, 0)[8:8 + SEG]
    xp1 = pltpu.roll(ext, n - 1, 0)[8:8 + SEG]
    xc = cb + xm2 * cw[0:1] + xm1 * cw[1:2] + x * cw[2:3] + xp1 * cw[3:4]
    gate_r = _sigmoid(_dot(xc, wa) + ba)
    gate_i = _sigmoid(_dot(xc, wx) + bx)
    softplus_neg_lam = jnp.maximum(-lam, 0.0) + jnp.log1p(jnp.exp(-jnp.abs(lam)))
    log_a = -LRU_C * gate_r * softplus_neg_lam
    a = jnp.exp(log_a)
    t = jnp.tanh(log_a)
    b_in = jnp.sqrt(-2.0 * t / (1.0 - t)) * gate_i * xc
    return _lin_scan(a, b_in, h_in, rev)


def _rglru_kernel(xf, xf_p, xf_n, xb, xb_p, xb_n, cw_ref, cb_ref, wa_ref, ba_ref, wx_ref, bx_ref, lam_ref,
                  s0f, s0b, hf_ref, hb_ref, ff_out, fb_out, hc_f, hc_b):
    j = pl.program_id(0)
    dirs = ((False, xf, xf_p, xf_n, s0f, hf_ref, ff_out, hc_f),
            (True, xb, xb_p, xb_n, s0b, hb_ref, fb_out, hc_b))
    for rev, xr, xp, xn, s0, h_ref, f_out, hc in dirs:
        i = (NSEG - 1 - j) if rev else j
        starts = _seg_is_last(i) if rev else _seg_is_first(i)

        @pl.when(jnp.logical_and(starts, i < N_CTX_SEG))
        def _():
            hc[...] = jnp.zeros_like(hc)

        @pl.when(jnp.logical_and(starts, i >= N_CTX_SEG))
        def _():
            hc[...] = s0[0, 0]

    outs = []
    for d, (rev, xr, xp, xn, s0, h_ref, f_out, hc) in enumerate(dirs):
        i = (NSEG - 1 - j) if rev else j
        h, h_out = _rglru_dir(xr[...], xp[...], xn[...], _seg_is_first(i), _seg_is_last(i), cw_ref[...],
                              cb_ref[...], wa_ref[d], ba_ref[d], wx_ref[d], bx_ref[d], lam_ref[d], hc[...], rev)
        h_ref[...] = h.astype(BF16)
        outs.append(h_out)
    for d, (rev, xr, xp, xn, s0, h_ref, f_out, hc) in enumerate(dirs):
        hc[...] = outs[d]
    for d, (rev, xr, xp, xn, s0, h_ref, f_out, hc) in enumerate(dirs):
        i = (NSEG - 1 - j) if rev else j

        @pl.when(i < N_CTX_SEG)
        def _():
            f_out[0] = outs[d]


def _rglru(proj, conv_w, conv_b, wa_bd, ba, wx_bd, bx, lam, s0):
    xcol = 2 * D_A // D_B
    fwd = lambda f: (lambda j: f(j))
    bwd = lambda f: (lambda j: f(NSEG - 1 - j))
    seg_blk = lambda m: pl.BlockSpec((SEG, D_B), m(lambda i: (i, xcol)))
    prev_blk = lambda m: pl.BlockSpec((8, D_B), m(lambda i: (_prev8(i), xcol)))
    next_blk = lambda m: pl.BlockSpec((8, D_B), m(lambda i: (_next8(i), xcol)))
    full = lambda a: pl.BlockSpec(a.shape, lambda j: (0,) * a.ndim)
    return pl.pallas_call(
        _rglru_kernel,
        grid=(NSEG,),
        in_specs=[seg_blk(fwd), prev_blk(fwd), next_blk(fwd), seg_blk(bwd), prev_blk(bwd), next_blk(bwd),
                  full(conv_w), full(conv_b), full(wa_bd), full(ba), full(wx_bd), full(bx), full(lam),
                  pl.BlockSpec((1, 1, 1, D_B), lambda j: (_sample_of(j), 0, 0, 0)),
                  pl.BlockSpec((1, 1, 1, D_B), lambda j: (_sample_of(NSEG - 1 - j), 1, 0, 0))],
        out_specs=[pl.BlockSpec((SEG, D_B), lambda j: (j, 0)),
                   pl.BlockSpec((SEG, D_B), lambda j: (NSEG - 1 - j, 0)),
                   pl.BlockSpec((1, 1, D_B), lambda j: (jnp.minimum(j, N_CTX_SEG - 1), 0, 0)),
                   pl.BlockSpec((1, 1, D_B), lambda j: (jnp.minimum(NSEG - 1 - j, N_CTX_SEG - 1), 0, 0))],
        out_shape=[jax.ShapeDtypeStruct((M_TOK, D_B), BF16), jax.ShapeDtypeStruct((M_TOK, D_B), BF16),
                   jax.ShapeDtypeStruct((N_CTX_SEG, 1, D_B), F32), jax.ShapeDtypeStruct((N_CTX_SEG, 1, D_B), F32)],
        scratch_shapes=[pltpu.VMEM((1, D_B), F32), pltpu.VMEM((1, D_B), F32)],
        compiler_params=_cp(("arbitrary",)),
        name="rglru",
    )(proj, proj, proj, proj, proj, proj, conv_w, conv_b, wa_bd, ba, wx_bd, bx, lam, s0, s0)


def _ab_out_kernel(xp_ref, xs_ref, pos_ref, of_ref, ob_ref, og_ref, hf_ref, hb_ref, yr_ref, mod_ref, hg_ref,
                   w_ref, o_ref, w_sc):
    @pl.when(pl.program_id(0) == 0)
    def _():
        w_sc[...] = w_ref[0].astype(BF16)

    m = mod_ref[0]
    f32 = lambda ref: ref[...].astype(F32)
    oa = f32(of_ref) + f32(ob_ref)
    hg = hg_ref[...]
    parts = []
    for h in range(A_HEADS):
        z = oa[:, h * A_DK:(h + 1) * A_DK]
        parts.append(z * lax.rsqrt(jnp.mean(z * z, axis=-1, keepdims=True) + RMS_EPS) * hg)
    o_a = jnp.concatenate(parts, axis=-1) * _silu(f32(og_ref))
    o_b = (f32(hf_ref) + f32(hb_ref)) * _gelu_tanh(f32(yr_ref))
    y = _dot(o_a, w_sc[0:D_A]) + _dot(o_b, w_sc[D_A:D_A + D_B])
    o_ref[...] = _x0(pl.program_id(0), xp_ref, xs_ref, pos_ref) + m[2:3] * y


def _ab_out(xp, xs, pos, proj_h, o_f, o_b, h_f, h_b, mods, hg, w_out):
    seg = lambda width, col: pl.BlockSpec((SEG, width), lambda i: (i, col))
    return pl.pallas_call(
        _ab_out_kernel,
        grid=(NSEG,),
        in_specs=_x0_specs() + [seg(D_A, 0), seg(D_A, 0), seg(D_A, 2), seg(D_B, 0), seg(D_B, 0), seg(D_B, 3),
                                pl.BlockSpec((1, 6, D), lambda i: (i, 0, 0)),
                                pl.BlockSpec((1, A_DK), lambda i: (0, 0)),
                                pl.BlockSpec((1, D_A + D_B, D), lambda i: (0, 0, 0),
                                             pipeline_mode=pl.Buffered(1))],
        out_specs=seg(D, 0),
        out_shape=jax.ShapeDtypeStruct((M_TOK, D), F32),
        scratch_shapes=[pltpu.VMEM((D_A + D_B, D), BF16)],
        compiler_params=_cp(("arbitrary",)),
        name="ab_out",
    )(xp, xs, pos, o_f, o_b, proj_h, h_f, h_b, proj_h, mods, hg.reshape(1, A_DK), w_out)


def _route(scores, sel):
    cols = [sel[e:e + 1, :] for e in range(N_EXPERTS)]

    def rank(vals):
        out = []
        for i, vi in enumerate(vals):
            r = None
            for jx, vj in enumerate(vals):
                if jx == i:
                    continue
                beats = (vj >= vi) if jx < i else (vj > vi)
                r = beats.astype(F32) if r is None else r + beats.astype(F32)
            out.append(r)
        return out

    grp_scores, in_top2 = [], []
    for gi in range(N_GROUPS):
        vals = cols[gi * GROUP:(gi + 1) * GROUP]
        best_pair = None
        for a in range(GROUP):
            for bx in range(a + 1, GROUP):
                s = vals[a] + vals[bx]
                best_pair = s if best_pair is None else jnp.maximum(best_pair, s)
        grp_scores.append(best_pair)
        in_top2.extend([r < 2.0 for r in rank(vals)])
    grp_best = [r < 1.0 for r in rank(grp_scores)]
    picked = [jnp.where(jnp.logical_and(grp_best[e // GROUP], in_top2[e]), scores[e:e + 1, :], 0.0)
              for e in range(N_EXPERTS)]
    total = picked[0]
    for pe in picked[1:]:
        total = total + pe
    row = lax.broadcasted_iota(jnp.int32, scores.shape, 0)
    comb = jnp.zeros(scores.shape, F32)
    for e in range(N_EXPERTS):
        comb = jnp.where(row == e, picked[e] / total, comb)
    return comb


def _moe_kernel(final_norm, x_ref, mod_ref, g_ref, rw_ref, rb_ref, w1_ref, w3_ref, w2_ref, gf_ref, *rest):
    o_refs, (h_sc, comb_sc, acc_sc) = rest[:-3], rest[-3:]
    e = pl.program_id(1)

    @pl.when(e == 0)
    def _():
        for s in range(MOE_TM // SEG):
            m = mod_ref[s]
            rows = slice(s * SEG, (s + 1) * SEG)
            h = _rms_mod(x_ref[rows, :], g_ref[...], m[4:5], m[3:4])
            h_sc[rows, :] = h.astype(BF16)
            scores = 1.0 / (1.0 + jnp.exp(-_dot_x3_nt(rw_ref[...], h)))
            comb_t = _route(scores, scores + rb_ref[...])
            comb_t = jnp.concatenate([comb_t, jnp.zeros((COMB_LANES - N_EXPERTS, SEG), F32)], axis=0)
            comb_sc[rows, :] = comb_t.T
        acc_sc[...] = jnp.zeros_like(acc_sc)

    lane = lax.broadcasted_iota(jnp.int32, (MOE_TM, COMB_LANES), 1)
    h = h_sc[...]
    es = w1_ref.shape[1]
    hids = []
    for ee in range(es):
        comb = jnp.sum(jnp.where(lane == e * es + ee, comb_sc[...], 0.0), axis=-1, keepdims=True)
        u1 = jnp.dot(h, w1_ref[0, ee].astype(BF16), preferred_element_type=F32)
        u3 = jnp.dot(h, w3_ref[0, ee].astype(BF16), preferred_element_type=F32)
        hids.append((_silu(u1) * u3 * comb).astype(BF16))
    w2 = w2_ref[0].reshape(es * D_EXPERT, D).astype(BF16)
    acc_sc[...] += jnp.dot(jnp.concatenate(hids, axis=1), w2, preferred_element_type=F32)

    def emit(dst_ref):
        for s in range(MOE_TM // SEG):
            rows = slice(s * SEG, (s + 1) * SEG)
            y = x_ref[rows, :] + mod_ref[s][5:6] * acc_sc[rows, :]
            if final_norm:
                y = y * lax.rsqrt(jnp.mean(y * y, axis=-1, keepdims=True) + RMS_EPS) * gf_ref[...]
            dst_ref[rows, :] = y

    last = e == N_EXPERTS // es - 1
    if final_norm:
        is_ctx = pl.program_id(0) < MOE_CTX_TILES
        pl.when(jnp.logical_and(last, is_ctx))(lambda: emit(o_refs[0]))
        pl.when(jnp.logical_and(last, jnp.logical_not(is_ctx)))(lambda: emit(o_refs[1]))
    else:
        pl.when(last)(lambda: emit(o_refs[0]))


def _moe(x, mods, g, router_w, router_b, w1, w3, w2, layer, gf, final_norm):
    spt = MOE_TM // SEG
    es = MOE_EXPERTS_STEP
    tile = lambda f, **kw: pl.BlockSpec((MOE_TM, D), lambda t, e: (f(t), 0), **kw)
    if final_norm:
        n_half = M_TOK // 2
        out_specs = [tile(lambda t: jnp.minimum(t, MOE_CTX_TILES - 1), pipeline_mode=pl.Buffered(1)),
                     tile(lambda t: jnp.maximum(t - MOE_CTX_TILES, 0), pipeline_mode=pl.Buffered(1))]
        out_shape = [jax.ShapeDtypeStruct((n_half, D), F32), jax.ShapeDtypeStruct((M_TOK - n_half, D), F32)]
    else:
        out_specs = [tile(lambda t: t)]
        out_shape = [jax.ShapeDtypeStruct((M_TOK, D), F32)]
    return pl.pallas_call(
        functools.partial(_moe_kernel, final_norm),
        grid=(M_TOK // MOE_TM, N_EXPERTS // es),
        in_specs=[pl.BlockSpec((MOE_TM, D), lambda t, e: (t, 0)),
                  pl.BlockSpec((spt, 6, D), lambda t, e: (t, 0, 0)),
                  pl.BlockSpec((1, D), lambda t, e: (0, 0)),
                  pl.BlockSpec((N_EXPERTS, D), lambda t, e: (0, 0)),
                  pl.BlockSpec((N_EXPERTS, 1), lambda t, e: (0, 0)),
                  pl.BlockSpec((1, es, D, D_EXPERT), lambda t, e: (layer, e, 0, 0)),
                  pl.BlockSpec((1, es, D, D_EXPERT), lambda t, e: (layer, e, 0, 0)),
                  pl.BlockSpec((1, es, D_EXPERT, D), lambda t, e: (layer, e, 0, 0)),
                  pl.BlockSpec((1, D), lambda t, e: (0, 0))],
        out_specs=out_specs,
        out_shape=out_shape,
        scratch_shapes=[pltpu.VMEM((MOE_TM, D), BF16), pltpu.VMEM((MOE_TM, COMB_LANES), F32),
                        pltpu.VMEM((MOE_TM, D), F32)],
        compiler_params=_cp(("arbitrary", "arbitrary")),
        name="moe",
    )(x, mods, g.reshape(1, D), router_w.T, router_b.reshape(N_EXPERTS, 1), w1, w3, w2, gf.reshape(1, D))


def _rw_in_kernel(x_ref, xp_ref, xn_ref, mod_ref, g_ref, mu_ref, wr_ref, wk_ref, wv_ref, g1_ref, g2_ref,
                  w1_ref, w2_ref, w0_ref, a1_ref, a2_ref, a0_ref, ka_ref, rk_ref, hsel_ref,
                  r_ref, k_ref, v_ref, gg_ref, lw_ref, a_ref, coef_ref, wrkv_sc):
    i = pl.program_id(0)

    @pl.when(i == 0)
    def _():
        for c, w_ref in enumerate((wr_ref, wk_ref, wv_ref)):
            wrkv_sc[c] = w_ref[0].astype(BF16)

    m = mod_ref[0]
    g = g_ref[...]
    h = _rms_mod(x_ref[...], g, m[1:2], m[0:1])
    hp = jnp.where(_seg_is_first(i), 0.0, _rms_mod(xp_ref[...], g, m[1:2], m[0:1]))
    hn = jnp.where(_seg_is_last(i), 0.0, _rms_mod(xn_ref[...], g, m[1:2], m[0:1]))
    ext = jnp.concatenate([hp, h, hn], axis=0)
    n = ext.shape[0]
    h_prev = pltpu.roll(ext, 1, 0)[8:8 + SEG]
    h_next = pltpu.roll(ext, n - 1, 0)[8:8 + SEG]
    xx = 0.5 * (h_prev + h_next) - h
    mu = mu_ref[...]
    xr, xw, xk, xv, xa, xg = [h + xx * mu[c:c + 1] for c in range(6)]
    r = _dot(xr, wrkv_sc[0])
    k = _dot(xk, wrkv_sc[1])
    r_ref[...] = r.astype(BF16)
    k_ref[...] = k.astype(BF16)
    v_ref[...] = _dot(xv, wrkv_sc[2]).astype(BF16)
    gg_ref[...] = _dot(_sigmoid(_dot(xg, g1_ref[...])), g2_ref[...]).astype(BF16)
    half_w_in = w0_ref[...] + _dot(jnp.tanh(_dot(xw, w1_ref[...])), w2_ref[...])
    lw_ref[...] = (-0.5 * W_DECAY_SCALE * LOG2E) * jnp.tanh(half_w_in) - 0.5 * W_DECAY_SCALE * LOG2E
    a = 0.5 * jnp.tanh(a0_ref[...] + _dot(_dot(xa, a1_ref[...]), a2_ref[...])) + 0.5
    a_ref[...] = a
    kd_sum = k * (2.0 + (a[:, 0:D] + a[:, D:2 * D] - 2.0) * ka_ref[...])
    coef_ref[...] = _dot_sel_rhs(r * kd_sum * rk_ref[...], hsel_ref[...])


def _rw_inproj(x, mods, g, mu, wr, wk, wv, g1, g2, w1c, w2bd, w0c, a1c, a2bd, a0c, ka, rk):
    full = lambda a: pl.BlockSpec(a.shape, lambda i: (0,) * a.ndim)
    once = lambda a: pl.BlockSpec((1,) + a.shape[1:], lambda i: (0,) * a.ndim, pipeline_mode=pl.Buffered(1))
    seg = lambda width: pl.BlockSpec((SEG, width), lambda i: (i, 0))
    outs = ([jax.ShapeDtypeStruct((M_TOK, D), BF16)] * 4 + [jax.ShapeDtypeStruct((M_TOK, 2 * D), F32)] * 2
            + [jax.ShapeDtypeStruct((M_TOK, COEF_LANES), F32)])
    hsel = jnp.asarray(np.arange(D)[:, None] // C_HEAD == np.arange(COEF_LANES)[None, :], BF16)
    return pl.pallas_call(
        _rw_in_kernel,
        grid=(NSEG,),
        in_specs=[seg(D),
                  pl.BlockSpec((8, D), lambda i: (_prev8(i), 0)),
                  pl.BlockSpec((8, D), lambda i: (_next8(i), 0)),
                  pl.BlockSpec((1, 6, D), lambda i: (i, 0, 0)),
                  full(g), full(mu), once(wr), once(wk), once(wv), full(g1), full(g2),
                  full(w1c), full(w2bd), full(w0c), full(a1c), full(a2bd), full(a0c),
                  full(ka), full(rk), full(hsel)],
        out_specs=[seg(D)] * 4 + [seg(2 * D)] * 2 + [seg(COEF_LANES)],
        out_shape=outs,
        scratch_shapes=[pltpu.VMEM((3, D, D), BF16)],
        compiler_params=_cp(("arbitrary",)),
        name="rwkv_inproj",
    )(x, x, x, mods, g, mu, wr, wk, wv, g1, g2, w1c, w2bd, w0c, a1c, a2bd, a0c, ka, rk, hsel)


def _rw_consts():
    t = np.arange(SEG)
    same = (t[:, None] // RW_CHUNK) == (t[None, :] // RW_CHUNK)
    tri_f = np.logical_and(same, t[None, :] <= t[:, None]).astype(np.float32)
    tri_b = np.logical_and(same, t[None, :] >= t[:, None]).astype(np.float32)
    return [jnp.asarray(a, BF16) for a in (tri_f, tri_b)]


def _rw_dir(r, k, v, lw, a, kk_g, ka_g, tri, rev):
    c = RW_CHUNK
    lane = lax.broadcasted_iota(jnp.int32, (SEG, 2 * C_HEAD), 1)
    head0 = lane < C_HEAD
    kx = k * kk_g
    ss = kx * kx
    n0 = jnp.sum(jnp.where(head0, ss, 0.0), axis=-1, keepdims=True)
    n1 = jnp.sum(jnp.where(head0, 0.0, ss), axis=-1, keepdims=True)
    kk = kx / jnp.maximum(jnp.sqrt(jnp.where(head0, n0, n1)), 1e-12)
    kd = k * (1.0 + (a - 1.0) * ka_g)
    bhat = kk * a
    cum = _dot_sel(tri, lw)
    e_incl = jnp.exp2(cum)
    e_inv = jnp.exp2(-cum)
    ae = -kk * jnp.exp2(cum - lw)
    re = r * e_incl
    bi = bhat * e_inv
    ki = kd * e_inv

    chunks = []
    for ci in range(SEG // c):
        sl = slice(ci * c, (ci + 1) * c)
        ctot = cum[ci * c:ci * c + 1] if rev else cum[(ci + 1) * c - 1:(ci + 1) * c]
        dec = jnp.exp2(ctot - cum[sl])
        chunks.append(dict(ae=ae[sl], re=re[sl], bi=bi[sl], ki=ki[sl], v=v[sl], bdec=bhat[sl] * dec,
                           kdec=kd[sl] * dec, gam=jnp.exp2(ctot), rev=rev))
    return chunks


def _rw_transitions(chunks):
    c = RW_CHUNK
    w2 = 2 * C_HEAD
    h0c = lax.broadcasted_iota(jnp.int32, (c, w2), 1) < C_HEAD
    rowc = lax.broadcasted_iota(jnp.int32, (c, w2), 0)
    colc = jnp.bitwise_and(lax.broadcasted_iota(jnp.int32, (c, w2), 1), C_HEAD - 1)
    eye = (colc == rowc).astype(F32)
    bdmask = (lax.broadcasted_iota(jnp.int32, (w2, w2), 0) < C_HEAD) == (
        lax.broadcasted_iota(jnp.int32, (w2, w2), 1) < C_HEAD)
    keep0 = jnp.where(h0c, 1.0, 0.0).astype(BF16)
    keep1 = jnp.where(h0c, 0.0, 1.0).astype(BF16)

    def bd16(yb):
        return jnp.concatenate([yb * keep0, yb * keep1], axis=0)

    def dot_bd(x, pairs):
        blocks = [bd16(y.astype(BF16)) for y in pairs]
        rhs = blocks[0] if len(blocks) == 1 else jnp.concatenate(blocks, axis=1)
        return jnp.dot(x.astype(BF16), rhs, preferred_element_type=F32)

    mm_inv = lambda x, y: dot_bd(x, [y])

    n_ab, a_ak, a_rb, a_rk = [], [], [], []
    for ch in chunks:
        strict = (colc > rowc) if ch["rev"] else (colc < rowc)
        incl = (colc >= rowc) if ch["rev"] else (colc <= rowc)
        left = jnp.concatenate([ch["ae"], ch["re"]], axis=0).astype(BF16)
        right = jnp.concatenate([bd16(ch["bi"].astype(BF16)), bd16(ch["ki"].astype(BF16))], axis=0)
        gm = lax.dot_general(left, right, (((1,), (1,)), ((), ())), preferred_element_type=F32)
        n_ab.append(jnp.where(strict, gm[0:c, 0:2 * c], 0.0))
        a_ak.append(jnp.where(strict, gm[0:c, 2 * c:4 * c], 0.0))
        a_rb.append(jnp.where(incl, gm[c:2 * c, 0:2 * c], 0.0))
        a_rk.append(jnp.where(incl, gm[c:2 * c, 2 * c:4 * c], 0.0))
    xorc = jnp.bitwise_xor(rowc, colc)
    tm = [eye + jnp.where(xorc < 2, n, 0.0) for n in n_ab]
    blk = 2
    while blk < c:
        couple = jnp.logical_and(xorc >= blk, xorc < 2 * blk)
        xs = [mm_inv(jnp.where(couple, n, 0.0), t) for n, t in zip(n_ab, tm)]
        tm = [t + mm_inv(t, x) for t, x in zip(tm, xs)]
        blk *= 2
    akv = [dot_bd(jnp.concatenate([x, y], axis=0), [ch["v"]])
           for x, y, ch in zip(a_ak, a_rk, chunks)]
    tav = [dot_bd(t, [ch["ae"], x[0:c]]) for t, ch, x in zip(tm, chunks, akv)]
    out = []
    for i, ch in enumerate(chunks):
        ta, tv = tav[i][:, 0:w2], tav[i][:, w2:2 * w2]
        qy = dot_bd(a_rb[i], [ta, tv])
        q = ch["re"] + qy[:, 0:w2]
        y = qy[:, w2:2 * w2] + akv[i][c:2 * c]
        wz = _dot_tn(tav[i], ch["bdec"])
        w = jnp.where(bdmask, wz[0:w2], 0.0)
        z = jnp.where(bdmask, wz[w2:2 * w2] + _dot_tn(ch["v"], ch["kdec"]), 0.0)
        out.append((q, y, w, z, ch["gam"]))
    return out


def _rw_scan_kernel(rf, kf, vf, lwf, af, rb, kb, vb, lwb, ab, kkg_ref, kag_ref, s0f, s0b, trif, trib,
                    of_ref, ob_ref, fin_ref, st_f, st_b):
    j = pl.program_id(1)
    w2 = 2 * C_HEAD
    dirs = ((False, (rf, kf, vf, lwf, af), s0f, trif, of_ref, 0, st_f),
            (True, (rb, kb, vb, lwb, ab), s0b, trib, ob_ref, 1, st_b))
    for rev, refs, s0, tri, o_ref, s_out, st in dirs:
        i = (NSEG - 1 - j) if rev else j
        starts = _seg_is_last(i) if rev else _seg_is_first(i)

        @pl.when(jnp.logical_and(starts, i < N_CTX_SEG))
        def _():
            st[...] = jnp.zeros_like(st)

        @pl.when(jnp.logical_and(starts, i >= N_CTX_SEG))
        def _():
            st[...] = s0[0, 0]

    chunks = []
    for rev, refs, s0, tri, o_ref, s_out, st in dirs:
        for pp in range(RW_PAIRS_STEP):
            lanes = slice(pp * w2, (pp + 1) * w2)
            r_, k_, v_, lw_, a_ = [z[:, lanes].astype(F32) for z in refs]
            chunks.extend(_rw_dir(r_, k_, v_, lw_, a_, kkg_ref[:, lanes], kag_ref[:, lanes], tri[...], rev))
    trans = _rw_transitions(chunks)
    n_c = SEG // RW_CHUNK
    states = [[st[pp] for pp in range(RW_PAIRS_STEP)] for (_, _, _, _, _, _, st) in dirs]
    for step in range(n_c):
        for d, (rev, refs, s0, tri, o_ref, s_out, st) in enumerate(dirs):
            ci = (n_c - 1 - step) if rev else step
            for pp in range(RW_PAIRS_STEP):
                q, y, w, z, gam = trans[(d * RW_PAIRS_STEP + pp) * n_c + ci]
                s = states[d][pp]
                o_ref[ci * RW_CHUNK:(ci + 1) * RW_CHUNK, pp * w2:(pp + 1) * w2] = (_dot_nt(q, s) + y).astype(BF16)
                states[d][pp] = s * gam + _dot(s, w) + z
    for d, (rev, refs, s0, tri, o_ref, s_out, st) in enumerate(dirs):
        for pp in range(RW_PAIRS_STEP):
            st[pp] = states[d][pp]
    for d, (rev, refs, s0, tri, o_ref, s_out, st) in enumerate(dirs):
        i = (NSEG - 1 - j) if rev else j

        @pl.when(i < N_CTX_SEG)
        def _():
            for pp in range(RW_PAIRS_STEP):
                s = states[d][pp]
                head = 2 * (pl.program_id(0) * RW_PAIRS_STEP + pp)
                fin_ref[i, 0, d, head] = s[0:C_HEAD, 0:C_HEAD]
                fin_ref[i, 0, d, head + 1] = s[C_HEAD:w2, C_HEAD:w2]


def _rw_scan(r, k, v, lw, a, kk_g, ka_g, s0bd):
    consts = _rw_consts()
    w = 2 * C_HEAD
    pps = RW_PAIRS_STEP
    wb = pps * w
    n_steps = C_PAIRS // pps
    blk = lambda col0, rev: pl.BlockSpec(
        (SEG, wb), (lambda p, j: (NSEG - 1 - j, col0 + p)) if rev else (lambda p, j: (j, col0 + p)))
    cspec = lambda arr: pl.BlockSpec(arr.shape, lambda p, j: (0, 0))
    fin_shape = (N_CTX_SEG, 1, 2, 2 * C_PAIRS, C_HEAD, C_HEAD)
    fin = pl.BlockSpec(fin_shape, lambda p, j: (0,) * len(fin_shape), pipeline_mode=pl.Buffered(1))
    return pl.pallas_call(
        _rw_scan_kernel,
        grid=(n_steps, NSEG),
        in_specs=[blk(0, False), blk(0, False), blk(0, False), blk(0, False), blk(0, False),
                  blk(0, True), blk(0, True), blk(0, True), blk(n_steps, True), blk(n_steps, True),
                  pl.BlockSpec((1, wb), lambda p, j: (0, p)), pl.BlockSpec((1, wb), lambda p, j: (0, p)),
                  pl.BlockSpec((1, 1, pps, w, w), lambda p, j: (_sample_of(j), 0, p, 0, 0)),
                  pl.BlockSpec((1, 1, pps, w, w), lambda p, j: (_sample_of(NSEG - 1 - j), 1, p, 0, 0)),
                  cspec(consts[0]), cspec(consts[1])],
        out_specs=[blk(0, False), blk(0, True), fin],
        out_shape=[jax.ShapeDtypeStruct((M_TOK, D), BF16), jax.ShapeDtypeStruct((M_TOK, D), BF16),
                   jax.ShapeDtypeStruct(fin_shape, F32)],
        scratch_shapes=[pltpu.VMEM((pps, w, w), F32), pltpu.VMEM((pps, w, w), F32)],
        compiler_params=_cp(("arbitrary", "arbitrary")),
        name="rwkv7_scan",
    )(r, k, v, lw, a, r, k, v, lw, a, kk_g, ka_g, s0bd, s0bd, *consts)


def _rw_out_kernel(x_ref, of_ref, ob_ref, v_ref, gg_ref, coef_ref, mod_ref, lnw_ref, lnb_ref, ones_ref,
                   hexp_ref, wo_ref, o_ref, wo_sc):
    @pl.when(pl.program_id(0) == 0)
    def _():
        wo_sc[...] = wo_ref[0].astype(BF16)

    m = mod_ref[0]
    ones_bd = ones_ref[...]
    w = 2 * C_HEAD
    inv_n = 1.0 / C_HEAD
    coef = _dot_sel_rhs(coef_ref[...], hexp_ref[...])
    parts = []
    for p in range(C_PAIRS):
        cs = slice(p * w, (p + 1) * w)
        osum = of_ref[:, cs].astype(F32) + ob_ref[:, cs].astype(F32)
        mu = _dot_sel_rhs(osum, ones_bd) * inv_n
        cen = osum - mu
        var = _dot_sel_rhs(cen * cen, ones_bd) * inv_n
        o = cen * lax.rsqrt(var + GN_EPS) * lnw_ref[:, cs] + lnb_ref[:, cs]
        bonus = coef[:, cs] * v_ref[:, cs].astype(F32)
        parts.append((o + bonus) * gg_ref[:, cs].astype(F32))
    y = _dot(jnp.concatenate(parts, axis=-1), wo_sc[...])
    o_ref[...] = x_ref[...] + m[2:3] * y


def _dot_sel_rhs(x, mat):
    h = x.astype(BF16)
    l = (x - h.astype(F32)).astype(BF16)
    return jnp.dot(h, mat, preferred_element_type=F32) + jnp.dot(l, mat, preferred_element_type=F32)


def _rw_out(x, o_f, o_b, v, gg, coef, mods, lnw, lnb, wo):
    seg = lambda width: pl.BlockSpec((SEG, width), lambda i: (i, 0))
    row = pl.BlockSpec((1, D), lambda i: (0, 0))
    hh = np.arange(2 * C_HEAD) // C_HEAD
    ones_bd = jnp.asarray((hh[:, None] == hh[None, :]).astype(np.float32), BF16)
    hexp = jnp.asarray(np.arange(COEF_LANES)[:, None] == np.arange(D)[None, :] // C_HEAD, BF16)
    return pl.pallas_call(
        _rw_out_kernel,
        grid=(NSEG,),
        in_specs=[seg(D), seg(D), seg(D), seg(D), seg(D), seg(COEF_LANES),
                  pl.BlockSpec((1, 6, D), lambda i: (i, 0, 0)),
                  row, row,
                  pl.BlockSpec((2 * C_HEAD, 2 * C_HEAD), lambda i: (0, 0)),
                  pl.BlockSpec((COEF_LANES, D), lambda i: (0, 0)),
                  pl.BlockSpec((1, D, D), lambda i: (0, 0, 0), pipeline_mode=pl.Buffered(1))],
        out_specs=seg(D),
        out_shape=jax.ShapeDtypeStruct((M_TOK, D), F32),
        scratch_shapes=[pltpu.VMEM((D, D), BF16)],
        compiler_params=_cp(("arbitrary",)),
        name="rwkv_out",
    )(x, o_f, o_b, v, gg, coef, mods, lnw.reshape(1, D), lnb.reshape(1, D), ones_bd, hexp, wo)


def _grid_pos_table(n_tok):
    rows = n_tok // GRID_W
    r, cl = np.meshgrid(np.arange(rows, dtype=np.float32), np.arange(GRID_W, dtype=np.float32), indexing='ij')
    quarter = D // 4
    omega = (1.0 / (np.float32(POS_BASE) ** (np.arange(quarter, dtype=np.float32) / np.float32(quarter))))
    ang_r = (r.reshape(-1, 1) * omega).astype(np.float32)
    ang_c = (cl.reshape(-1, 1) * omega).astype(np.float32)
    table = np.concatenate([np.sin(ang_r), np.cos(ang_r), np.sin(ang_c), np.cos(ang_c)], axis=-1)
    return jnp.asarray(table.astype(np.float32))


def _block_diag(blocks):
    g, n, _ = blocks.shape
    eye = jnp.eye(g, dtype=blocks.dtype)
    return (eye[:, None, :, None] * blocks[:, :, None, :]).reshape(g * n, g * n)


def kernel(x_prompt, x_sample, state_hgrn, state_rglru, state_rwkv, c, c_ctx, norm_mix_g, norm_ffn_g, w_mod, b_mod, ab_w_in, ab_w_out, hgrn_lb, hgrn_norm_g, rg_conv_w, rg_conv_b, rg_wa, rg_ba, rg_wx, rg_bx, rg_lambda, rw_mu, rw_wr, rw_wk, rw_wv, rw_wo, rw_w0, rw_w1, rw_w2, rw_a0, rw_a1, rw_a2, rw_g1, rw_g2, rw_kk, rw_ka, rw_rk, rw_lnw, rw_lnb, moe_router, moe_router_bias, moe_w1, moe_w3, moe_w2, norm_f_g):
    bf = lambda z: z.astype(BF16)
    xp = x_prompt.reshape(-1, D)
    xs = x_sample.reshape(-1, D)
    pos = _grid_pos_table(x_sample.shape[1])
    mods = _modulations(c, c_ctx, w_mod, b_mod)

    lower_bounds = jnp.cumsum(jax.nn.softmax(hgrn_lb.astype(F32), axis=1), axis=1)
    lb = lower_bounds[:, 0].reshape(2, A_HEADS, 1, A_DK)
    proj_f, proj_h = _ab_inproj(xp, xs, pos, mods[0], norm_mix_g[0], ab_w_in)
    s0t = jnp.swapaxes(state_hgrn[:, 0], -1, -2)
    o_f, o_b, new_hgrn = _gla(proj_f, proj_h, lb, s0t)
    wa_bd = bf(jnp.stack([_block_diag(rg_wa[0, d]) for d in range(2)]))
    wx_bd = bf(jnp.stack([_block_diag(rg_wx[0, d]) for d in range(2)]))
    h_f, h_b, lru_f, lru_b = _rglru(
        proj_f, rg_conv_w[0], rg_conv_b[0].reshape(1, D_B), wa_bd, rg_ba[0].reshape(2, 1, D_B), wx_bd,
        rg_bx[0].reshape(2, 1, D_B), rg_lambda[0].reshape(2, 1, D_B), state_rglru[:, 0].reshape(-1, 2, 1, D_B))
    x = _ab_out(xp, xs, pos, proj_h, o_f, o_b, h_f, h_b, mods[0], hgrn_norm_g[0], ab_w_out)
    x, = _moe(x, mods[0], norm_ffn_g[0], moe_router, moe_router_bias, moe_w1, moe_w3, moe_w2, 0, norm_f_g, False)

    w1c = bf(jnp.concatenate([rw_w1[0, 0], rw_w1[0, 1]], axis=-1))
    a1c = bf(jnp.concatenate([rw_a1[0, 0], rw_a1[0, 1]], axis=-1))
    half_w2, half_a2 = 0.5 * rw_w2[0], 0.5 * rw_a2[0]
    w2bd = bf(jnp.concatenate([jnp.concatenate([half_w2[0], jnp.zeros_like(half_w2[0])], axis=-1),
                               jnp.concatenate([jnp.zeros_like(half_w2[1]), half_w2[1]], axis=-1)], axis=0))
    a2bd = bf(jnp.concatenate([jnp.concatenate([half_a2[0], jnp.zeros_like(half_a2[0])], axis=-1),
                               jnp.concatenate([jnp.zeros_like(half_a2[1]), half_a2[1]], axis=-1)], axis=0))
    r, k, v, gg, lw, a, coef = _rw_inproj(
        x, mods[1], norm_mix_g[1].reshape(1, D), rw_mu[0], rw_wr, rw_wk, rw_wv,
        bf(rw_g1[0]), bf(rw_g2[0]), w1c, w2bd, 0.5 * rw_w0[0].reshape(1, 2 * D), a1c, a2bd,
        0.5 * rw_a0[0].reshape(1, 2 * D), rw_ka[0].reshape(1, D), rw_rk[0].reshape(1, D))
    s0 = state_rwkv[:, 0].reshape(N_SAMPLE, 2, C_PAIRS, 2, C_HEAD, C_HEAD)
    zeros = jnp.zeros_like(s0[:, :, :, 0])
    s0bd = jnp.concatenate([jnp.concatenate([s0[:, :, :, 0], zeros], axis=-1),
                            jnp.concatenate([zeros, s0[:, :, :, 1]], axis=-1)], axis=-2)
    ow_f, ow_b, new_rwkv = _rw_scan(r, k, v, lw, a, rw_kk[0].reshape(1, D), rw_ka[0].reshape(1, D), s0bd)
    x = _rw_out(x, ow_f, ow_b, v, gg, coef, mods[1], rw_lnw[0], rw_lnb[0], rw_wo)
    y_p, y_s = _moe(x, mods[1], norm_ffn_g[1], moe_router, moe_router_bias, moe_w1, moe_w3, moe_w2, 1, norm_f_g, True)

    new_rglru = jnp.stack([lru_f[:, 0], lru_b[:, 0]], axis=1)[:, None]
    return (y_p.reshape(x_prompt.shape), y_s.reshape(x_sample.shape), new_hgrn, new_rglru, new_rwkv)
```

```python
import functools
import math

import numpy as np
import jax
import jax.numpy as jnp
from jax import lax
from jax.experimental import pallas as pl
from jax.experimental.pallas import tpu as pltpu

F32 = jnp.float32
BF16 = jnp.bfloat16

D = 1024
SEG = 256
N_CTX_SEG = 16
SEG_PER_SAMPLE = 4
N_SAMPLE = 4
NSEG = N_CTX_SEG + N_SAMPLE * SEG_PER_SAMPLE
M_TOK = NSEG * SEG
SUBLANES = 8
ROWS8_PER_SEG = SEG // SUBLANES

A_HEADS = 4
A_DK = 128
D_A = 512
D_B = 512
B_BLOCKS = 8
B_BLOCK = 64
LRU_C = 8.0
D_IN_AB = 5 * D_A + 2 * D_B
AB_F32_COLS = 2 * D_A + D_B
C_HEAD = 64
C_PAIRS = 8
RW_CHUNK = 64
COEF_LANES = 128
RW_PAIRS_STEP = 8
W_DECAY_SCALE = math.exp(-0.5)
LOG2E = math.log2(math.e)
N_EXPERTS = 16
N_GROUPS = 4
GROUP = 4
D_EXPERT = 256
RMS_EPS = 1e-6
GN_EPS = 64e-5
POS_BASE = 10000.0
GRID_W = 64
MOE_TM = 1024
MOE_CTX_TILES = N_CTX_SEG * SEG // MOE_TM
MOE_EXPERTS_STEP = 4
COMB_LANES = 128
GLA_LEVELS = (1, 2, 4, 8, 16, 32, 64, 128)
GLA_HALF = 128
GLA_HEADS_STEP = 4

VMEM_LIMIT = 56 * 1024 * 1024


def _cp(sem):
    return pltpu.CompilerParams(dimension_semantics=sem, vmem_limit_bytes=VMEM_LIMIT)


def _sigmoid(x):
    return 0.5 * jnp.tanh(0.5 * x) + 0.5


def _silu(x):
    return x * _sigmoid(x)


def _gelu_tanh(x):
    return 0.5 * x * (1.0 + jnp.tanh(math.sqrt(2.0 / math.pi) * (x + 0.044715 * (x * x * x))))


def _rms_mod(x, g, scale, shift):
    ms = jnp.mean(x * x, axis=-1, keepdims=True)
    return x * lax.rsqrt(ms + RMS_EPS) * (g * (1.0 + scale)) + shift


def _dot(a, b):
    return jnp.dot(a.astype(BF16), b.astype(BF16), preferred_element_type=F32)


def _dot_nt(a, b):
    return lax.dot_general(a.astype(BF16), b.astype(BF16), (((1,), (1,)), ((), ())),
                           preferred_element_type=F32)


def _dot_tn(a, b):
    return lax.dot_general(a.astype(BF16), b.astype(BF16), (((0,), (0,)), ((), ())),
                           preferred_element_type=F32)


def _split3(x):
    h = x.astype(BF16)
    r1 = x - h.astype(F32)
    m = r1.astype(BF16)
    r2 = r1 - m.astype(F32)
    return h, m, r2.astype(BF16)


def _dot_sel(mat, x):
    h, m, l = _split3(x)
    return (jnp.dot(mat, h, preferred_element_type=F32) + jnp.dot(mat, m, preferred_element_type=F32)
            + jnp.dot(mat, l, preferred_element_type=F32))


def _dot_x3_nt(a, b):
    dn = (((1,), (1,)), ((), ()))
    ah = a.astype(BF16)
    al = (a - ah.astype(F32)).astype(BF16)
    bh = b.astype(BF16)
    bl = (b - bh.astype(F32)).astype(BF16)
    return (lax.dot_general(ah, bh, dn, preferred_element_type=F32)
            + lax.dot_general(ah, bl, dn, preferred_element_type=F32)
            + lax.dot_general(al, bh, dn, preferred_element_type=F32))


def _seg_is_first(i):
    return jnp.logical_or(i < N_CTX_SEG, lax.rem(i - N_CTX_SEG, SEG_PER_SAMPLE) == 0)


def _seg_is_last(i):
    return jnp.logical_or(i < N_CTX_SEG, lax.rem(i - N_CTX_SEG, SEG_PER_SAMPLE) == SEG_PER_SAMPLE - 1)


def _sample_of(i):
    return jnp.maximum(i - N_CTX_SEG, 0) // SEG_PER_SAMPLE


def _prev8(i):
    return jnp.maximum(i * ROWS8_PER_SEG - 1, 0)


def _next8(i):
    return jnp.minimum((i + 1) * ROWS8_PER_SEG, M_TOK // 8 - 1)


def _mod_kernel(cv_ref, w_ref, b_ref, o_ref):
    cv = cv_ref[...]
    o_ref[0] = _dot(_silu(cv), w_ref[0]) + b_ref[0]


def _modulations(c, c_ctx, w_mod, b_mod):
    depth = w_mod.shape[0]
    cv = jnp.concatenate([c_ctx[None, :], c, jnp.zeros((3, D), F32)], axis=0)
    wt = 2 * D
    mod = pl.pallas_call(
        _mod_kernel,
        grid=(depth, 6 * D // wt),
        in_specs=[pl.BlockSpec((8, D), lambda l, n: (0, 0)),
                  pl.BlockSpec((1, D, wt), lambda l, n: (l, 0, n)),
                  pl.BlockSpec((1, 1, wt), lambda l, n: (l, 0, n))],
        out_specs=pl.BlockSpec((1, 8, wt), lambda l, n: (l, 0, n)),
        out_shape=jax.ShapeDtypeStruct((depth, 8, 6 * D), F32),
        compiler_params=_cp(("arbitrary", "arbitrary")),
        name="adaln_mod",
    )(cv, w_mod, b_mod.reshape(depth, 1, 6 * D))
    row_of_seg = np.array([0] * N_CTX_SEG + [1 + s // SEG_PER_SAMPLE for s in range(N_SAMPLE * SEG_PER_SAMPLE)])
    return mod[:, row_of_seg].reshape(depth, NSEG, 6, D)


def _x0_specs():
    return [pl.BlockSpec((SEG, D), lambda i: (jnp.minimum(i, N_CTX_SEG - 1), 0)),
            pl.BlockSpec((SEG, D), lambda i: (jnp.maximum(i - N_CTX_SEG, 0), 0)),
            pl.BlockSpec((SEG, D), lambda i: (lax.rem(jnp.maximum(i - N_CTX_SEG, 0), SEG_PER_SAMPLE), 0))]


def _x0(i, xp_ref, xs_ref, pos_ref):
    return jnp.where(i < N_CTX_SEG, xp_ref[...], xs_ref[...] + pos_ref[...])


def _ab_in_kernel(xp_ref, xs_ref, pos_ref, mod_ref, g_ref, w_ref, of_ref, oh_ref, w_sc):
    @pl.when(pl.program_id(0) == 0)
    def _():
        w_sc[...] = w_ref[0].astype(BF16)

    m = mod_ref[0]
    x = _x0(pl.program_id(0), xp_ref, xs_ref, pos_ref)
    h = _rms_mod(x, g_ref[...], m[1:2], m[0:1])
    res = jnp.dot(h.astype(BF16), w_sc[...], preferred_element_type=F32)
    of_ref[:, 0:2 * D_A] = res[:, D_A:3 * D_A]
    of_ref[:, 2 * D_A:AB_F32_COLS] = res[:, 5 * D_A:5 * D_A + D_B]
    oh_ref[:, 0:D_A] = res[:, 0:D_A].astype(BF16)
    oh_ref[:, D_A:3 * D_A] = res[:, 3 * D_A:5 * D_A].astype(BF16)
    oh_ref[:, 3 * D_A:3 * D_A + D_B] = res[:, 5 * D_A + D_B:D_IN_AB].astype(BF16)


def _ab_inproj(xp, xs, pos, mods, g, w_in):
    return pl.pallas_call(
        _ab_in_kernel,
        grid=(NSEG,),
        in_specs=_x0_specs() + [pl.BlockSpec((1, 6, D), lambda i: (i, 0, 0)),
                                pl.BlockSpec((1, D), lambda i: (0, 0)),
                                pl.BlockSpec((1, D, D_IN_AB), lambda i: (0, 0, 0),
                                             pipeline_mode=pl.Buffered(1))],
        out_specs=[pl.BlockSpec((SEG, AB_F32_COLS), lambda i: (i, 0)),
                   pl.BlockSpec((SEG, D_IN_AB - AB_F32_COLS), lambda i: (i, 0))],
        out_shape=[jax.ShapeDtypeStruct((M_TOK, AB_F32_COLS), F32),
                   jax.ShapeDtypeStruct((M_TOK, D_IN_AB - AB_F32_COLS), BF16)],
        scratch_shapes=[pltpu.VMEM((D, D_IN_AB), BF16)],
        compiler_params=_cp(("arbitrary",)),
        name="ab_inproj",
    )(xp, xs, pos, mods, g.reshape(1, D), w_in)


def _gla_consts():
    t = np.arange(SEG)
    tri_f = (t[None, :] <= t[:, None]).astype(np.float32)
    tri_b = (t[None, :] >= t[:, None]).astype(np.float32)
    th = np.arange(GLA_HALF)
    xor = th[:, None] ^ th[None, :]
    hb = np.where(xor > 0, 1 << np.floor(np.log2(np.maximum(xor, 1))).astype(np.int64), 0)
    code_f = np.where(th[None, :] < th[:, None], hb, 0).astype(np.int32)
    code_b = np.where(th[None, :] > th[:, None], hb, 0).astype(np.int32)
    return [jnp.asarray(tri_f, BF16), jnp.asarray(tri_b, BF16), jnp.asarray(code_f), jnp.asarray(code_b)]


def _gla_level_operand(q, k, b, g, rowi, w, rev):
    upper = jnp.bitwise_and(rowi, w) != 0
    qside = jnp.logical_not(upper) if rev else upper
    if w == 1:
        z = jnp.where(qside, g, 0.0)
    elif w >= SUBLANES:
        nv = 2 * w // SUBLANES
        b4 = b.reshape(SEG // (2 * w), nv, SUBLANES, A_DK)
        ref = (b4[:, nv // 2:nv // 2 + 1, 0:1, :] if rev
               else b4[:, nv // 2 - 1:nv // 2, SUBLANES - 1:SUBLANES, :])
        x = (b4 - ref).reshape(SEG, A_DK)
        z = jnp.where(qside, x, -x)
    else:
        b3 = b.reshape(SEG // SUBLANES, SUBLANES, A_DK)
        sub = lax.broadcasted_iota(jnp.int32, b3.shape, 1)
        beta = None
        for jb in range(SUBLANES // (2 * w)):
            r = jb * 2 * w + (w if rev else w - 1)
            cand = jnp.broadcast_to(b3[:, r:r + 1, :], b3.shape)
            beta = cand if beta is None else jnp.where(sub >= jb * 2 * w, cand, beta)
        x = (b3 - beta).reshape(SEG, A_DK)
        z = jnp.where(qside, x, -x)
    return jnp.where(qside, q, k) * jnp.exp2(z)


def _gla_dir(qraw, fraw, v, lb, st, tri, code, rev):
    hh = GLA_HALF
    q = _silu(qraw)
    f = lb + (1.0 - lb) * _sigmoid(fraw)
    g = jnp.log2(f)
    k = 1.0 - f
    b = _dot_sel(tri, g)
    rowi = lax.broadcasted_iota(jnp.int32, (SEG, A_DK), 0)
    att = [jnp.zeros((hh, hh), F32), jnp.zeros((hh, hh), F32)]
    cross = None
    for w in GLA_LEVELS:
        m = _gla_level_operand(q, k, b, g, rowi, w, rev).astype(BF16)
        if w == hh:
            cross = _dot_nt(m[0:hh], m[hh:SEG]) if rev else _dot_nt(m[hh:SEG], m[0:hh])
        else:
            for half in range(2):
                mh = m[half * hh:(half + 1) * hh]
                att[half] = jnp.where(code == w, _dot_nt(mh, mh), att[half])
    if rev:
        o_lo = _dot(jnp.concatenate([att[0], cross], axis=1), v)
        o_hi = _dot(att[1], v[hh:SEG])
    else:
        o_lo = _dot(att[0], v[0:hh])
        o_hi = _dot(jnp.concatenate([cross, att[1]], axis=1), v)
    diag = jnp.sum(q * k, axis=-1, keepdims=True)
    o = jnp.concatenate([o_lo, o_hi], axis=0) + diag * v + _dot_nt(q * jnp.exp2(b), st)
    btot = b[0:1] if rev else b[SEG - 1:SEG]
    st_new = st * jnp.exp2(btot) + _dot_tn(v, k * jnp.exp2(btot - b))
    return o, st_new


def _gla_kernel(qf, ff, vf, qb, fb, vb, lb_ref, s0f, s0b, trif, trib, codef, codeb,
                of_ref, ob_ref, fin_ref, st_f, st_b):
    j = pl.program_id(1)
    dirs = ((False, qf, ff, vf, s0f, trif, codef, of_ref, st_f),
            (True, qb, fb, vb, s0b, trib, codeb, ob_ref, st_b))
    for rev, qr, fr, vr, s0, tri, code, o_ref, st in dirs:
        i = (NSEG - 1 - j) if rev else j
        starts = _seg_is_last(i) if rev else _seg_is_first(i)

        @pl.when(jnp.logical_and(starts, i < N_CTX_SEG))
        def _():
            st[...] = jnp.zeros_like(st)

        @pl.when(jnp.logical_and(starts, i >= N_CTX_SEG))
        def _():
            st[...] = s0[0, 0]

    finals = []
    for d, (rev, qr, fr, vr, s0, tri, code, o_ref, st) in enumerate(dirs):
        for hh in range(GLA_HEADS_STEP):
            lanes = slice(hh * A_DK, (hh + 1) * A_DK)
            o, st_new = _gla_dir(qr[:, lanes].astype(F32), fr[:, lanes], vr[:, lanes].astype(F32),
                                 lb_ref[d, hh], st[hh], tri[...], code[...], rev)
            o_ref[:, lanes] = o.astype(BF16)
            finals.append(st_new)
    for d, (rev, qr, fr, vr, s0, tri, code, o_ref, st) in enumerate(dirs):
        for hh in range(GLA_HEADS_STEP):
            st[hh] = finals[d * GLA_HEADS_STEP + hh]
    for d, (rev, qr, fr, vr, s0, tri, code, o_ref, st) in enumerate(dirs):
        i = (NSEG - 1 - j) if rev else j

        @pl.when(i < N_CTX_SEG)
        def _():
            for hh in range(GLA_HEADS_STEP):
                fin_ref[i, 0, d, pl.program_id(0) * GLA_HEADS_STEP + hh] = finals[d * GLA_HEADS_STEP + hh].T


def _gla(proj_f, proj_h, lb, s0t):
    consts = _gla_consts()
    hs = GLA_HEADS_STEP
    wb = hs * A_DK
    n_col = D_A // wb
    blk = lambda col0, rev: pl.BlockSpec(
        (SEG, wb), (lambda h, j: (NSEG - 1 - j, col0 + h)) if rev else (lambda h, j: (j, col0 + h)))
    cspec = lambda a: pl.BlockSpec(a.shape, lambda h, j: (0, 0))
    fin_shape = (N_CTX_SEG, 1, 2, A_HEADS, A_DK, A_DK)
    fin = pl.BlockSpec(fin_shape, lambda h, j: (0,) * len(fin_shape))
    return pl.pallas_call(
        _gla_kernel,
        grid=(A_HEADS // hs, NSEG),
        in_specs=[blk(0, False), blk(0, False), blk(n_col, False),
                  blk(0, True), blk(n_col, True), blk(n_col, True),
                  pl.BlockSpec((2, hs, 1, A_DK), lambda h, j: (0, h, 0, 0)),
                  pl.BlockSpec((1, 1, hs, A_DK, A_DK), lambda h, j: (_sample_of(j), 0, h, 0, 0)),
                  pl.BlockSpec((1, 1, hs, A_DK, A_DK), lambda h, j: (_sample_of(NSEG - 1 - j), 1, h, 0, 0)),
                  cspec(consts[0]), cspec(consts[1]), cspec(consts[2]), cspec(consts[3])],
        out_specs=[blk(0, False), blk(0, True), fin],
        out_shape=[jax.ShapeDtypeStruct((M_TOK, D_A), BF16), jax.ShapeDtypeStruct((M_TOK, D_A), BF16),
                   jax.ShapeDtypeStruct(fin_shape, F32)],
        scratch_shapes=[pltpu.VMEM((hs, A_DK, A_DK), F32), pltpu.VMEM((hs, A_DK, A_DK), F32)],
        compiler_params=_cp(("arbitrary", "arbitrary")),
        name="hgrn2_gla",
    )(proj_h, proj_f, proj_h, proj_h, proj_f, proj_h, lb, s0t, s0t, *consts)


def _lin_scan(a, b, h_in, rev):
    t_len, c = a.shape
    ng = t_len // SUBLANES
    a3 = a.reshape(ng, SUBLANES, c)
    b3 = b.reshape(ng, SUBLANES, c)
    sub = lax.broadcasted_iota(jnp.int32, a3.shape, 1)
    s = 1
    while s < SUBLANES:
        shift = (SUBLANES - s) if rev else s
        valid = (sub < SUBLANES - s) if rev else (sub >= s)
        ap = jnp.where(valid, pltpu.roll(a3, shift, 1), 1.0)
        bp = jnp.where(valid, pltpu.roll(b3, shift, 1), 0.0)
        b3 = a3 * bp + b3
        a3 = a3 * ap
        s *= 2
    hs = [None] * ng
    carry = h_in
    for j in (range(ng - 1, -1, -1) if rev else range(ng)):
        hs[j] = a3[j] * carry + b3[j]
        carry = hs[j][0:1] if rev else hs[j][SUBLANES - 1:SUBLANES]
    return jnp.concatenate(hs, axis=0), carry


def _rglru_dir(x, xprev, xnext, first, last, cw, cb, wa, ba, wx, bx, lam, h_in, rev):
    zero = jnp.zeros_like(xprev)
    ext = jnp.concatenate([jnp.where(first, zero, xprev), x, jnp.where(last, zero, xnext)], axis=0)
    n = ext.shape[0]
    xm2 = pltpu.roll(ext, 2, 0)[8:8 + SEG]
    xm1 = pltpu.roll(ext, 1, 0)[8:8 + SEG]
    xp1 = pltpu.roll(ext, n - 1, 0)[8:8 + SEG]
    xc = cb + xm2 * cw[0:1] + xm1 * cw[1:2] + x * cw[2:3] + xp1 * cw[3:4]
    gate_r = _sigmoid(_dot(xc, wa) + ba)
    gate_i = _sigmoid(_dot(xc, wx) + bx)
    softplus_neg_lam = jnp.maximum(-lam, 0.0) + jnp.log1p(jnp.exp(-jnp.abs(lam)))
    log_a = -LRU_C * gate_r * softplus_neg_lam
    a = jnp.exp(log_a)
    t = jnp.tanh(log_a)
    b_in = jnp.sqrt(-2.0 * t / (1.0 - t)) * gate_i * xc
    return _lin_scan(a, b_in, h_in, rev)


def _rglru_kernel(xf, xf_p, xf_n, xb, xb_p, xb_n, cw_ref, cb_ref, wa_ref, ba_ref, wx_ref, bx_ref, lam_ref,
                  s0f, s0b, hf_ref, hb_ref, ff_out, fb_out, hc_f, hc_b):
    j = pl.program_id(0)
    dirs = ((False, xf, xf_p, xf_n, s0f, hf_ref, ff_out, hc_f),
            (True, xb, xb_p, xb_n, s0b, hb_ref, fb_out, hc_b))
    for rev, xr, xp, xn, s0, h_ref, f_out, hc in dirs:
        i = (NSEG - 1 - j) if rev else j
        starts = _seg_is_last(i) if rev else _seg_is_first(i)

        @pl.when(jnp.logical_and(starts, i < N_CTX_SEG))
        def _():
            hc[...] = jnp.zeros_like(hc)

        @pl.when(jnp.logical_and(starts, i >= N_CTX_SEG))
        def _():
            hc[...] = s0[0, 0]

    outs = []
    for d, (rev, xr, xp, xn, s0, h_ref, f_out, hc) in enumerate(dirs):
        i = (NSEG - 1 - j) if rev else j
        h, h_out = _rglru_dir(xr[...], xp[...], xn[...], _seg_is_first(i), _seg_is_last(i), cw_ref[...],
                              cb_ref[...], wa_ref[d], ba_ref[d], wx_ref[d], bx_ref[d], lam_ref[d], hc[...], rev)
        h_ref[...] = h.astype(BF16)
        outs.append(h_out)
    for d, (rev, xr, xp, xn, s0, h_ref, f_out, hc) in enumerate(dirs):
        hc[...] = outs[d]
    for d, (rev, xr, xp, xn, s0, h_ref, f_out, hc) in enumerate(dirs):
        i = (NSEG - 1 - j) if rev else j

        @pl.when(i < N_CTX_SEG)
        def _():
            f_out[0] = outs[d]


def _rglru(proj, conv_w, conv_b, wa_bd, ba, wx_bd, bx, lam, s0):
    xcol = 2 * D_A // D_B
    fwd = lambda f: (lambda j: f(j))
    bwd = lambda f: (lambda j: f(NSEG - 1 - j))
    seg_blk = lambda m: pl.BlockSpec((SEG, D_B), m(lambda i: (i, xcol)))
    prev_blk = lambda m: pl.BlockSpec((8, D_B), m(lambda i: (_prev8(i), xcol)))
    next_blk = lambda m: pl.BlockSpec((8, D_B), m(lambda i: (_next8(i), xcol)))
    full = lambda a: pl.BlockSpec(a.shape, lambda j: (0,) * a.ndim)
    return pl.pallas_call(
        _rglru_kernel,
        grid=(NSEG,),
        in_specs=[seg_blk(fwd), prev_blk(fwd), next_blk(fwd), seg_blk(bwd), prev_blk(bwd), next_blk(bwd),
                  full(conv_w), full(conv_b), full(wa_bd), full(ba), full(wx_bd), full(bx), full(lam),
                  pl.BlockSpec((1, 1, 1, D_B), lambda j: (_sample_of(j), 0, 0, 0)),
                  pl.BlockSpec((1, 1, 1, D_B), lambda j: (_sample_of(NSEG - 1 - j), 1, 0, 0))],
        out_specs=[pl.BlockSpec((SEG, D_B), lambda j: (j, 0)),
                   pl.BlockSpec((SEG, D_B), lambda j: (NSEG - 1 - j, 0)),
                   pl.BlockSpec((1, 1, D_B), lambda j: (jnp.minimum(j, N_CTX_SEG - 1), 0, 0)),
                   pl.BlockSpec((1, 1, D_B), lambda j: (jnp.minimum(NSEG - 1 - j, N_CTX_SEG - 1), 0, 0))],
        out_shape=[jax.ShapeDtypeStruct((M_TOK, D_B), BF16), jax.ShapeDtypeStruct((M_TOK, D_B), BF16),
                   jax.ShapeDtypeStruct((N_CTX_SEG, 1, D_B), F32), jax.ShapeDtypeStruct((N_CTX_SEG, 1, D_B), F32)],
        scratch_shapes=[pltpu.VMEM((1, D_B), F32), pltpu.VMEM((1, D_B), F32)],
        compiler_params=_cp(("arbitrary",)),
        name="rglru",
    )(proj, proj, proj, proj, proj, proj, conv_w, conv_b, wa_bd, ba, wx_bd, bx, lam, s0, s0)


def _ab_out_kernel(xp_ref, xs_ref, pos_ref, of_ref, ob_ref, og_ref, hf_ref, hb_ref, yr_ref, mod_ref, hg_ref,
                   w_ref, o_ref, w_sc):
    @pl.when(pl.program_id(0) == 0)
    def _():
        w_sc[...] = w_ref[0].astype(BF16)

    m = mod_ref[0]
    f32 = lambda ref: ref[...].astype(F32)
    oa = f32(of_ref) + f32(ob_ref)
    hg = hg_ref[...]
    parts = []
    for h in range(A_HEADS):
        z = oa[:, h * A_DK:(h + 1) * A_DK]
        parts.append(z * lax.rsqrt(jnp.mean(z * z, axis=-1, keepdims=True) + RMS_EPS) * hg)
    o_a = jnp.concatenate(parts, axis=-1) * _silu(f32(og_ref))
    o_b = (f32(hf_ref) + f32(hb_ref)) * _gelu_tanh(f32(yr_ref))
    y = _dot(o_a, w_sc[0:D_A]) + _dot(o_b, w_sc[D_A:D_A + D_B])
    o_ref[...] = _x0(pl.program_id(0), xp_ref, xs_ref, pos_ref) + m[2:3] * y


def _ab_out(xp, xs, pos, proj_h, o_f, o_b, h_f, h_b, mods, hg, w_out):
    seg = lambda width, col: pl.BlockSpec((SEG, width), lambda i: (i, col))
    return pl.pallas_call(
        _ab_out_kernel,
        grid=(NSEG,),
        in_specs=_x0_specs() + [seg(D_A, 0), seg(D_A, 0), seg(D_A, 2), seg(D_B, 0), seg(D_B, 0), seg(D_B, 3),
                                pl.BlockSpec((1, 6, D), lambda i: (i, 0, 0)),
                                pl.BlockSpec((1, A_DK), lambda i: (0, 0)),
                                pl.BlockSpec((1, D_A + D_B, D), lambda i: (0, 0, 0),
                                             pipeline_mode=pl.Buffered(1))],
        out_specs=seg(D, 0),
        out_shape=jax.ShapeDtypeStruct((M_TOK, D), F32),
        scratch_shapes=[pltpu.VMEM((D_A + D_B, D), BF16)],
        compiler_params=_cp(("arbitrary",)),
        name="ab_out",
    )(xp, xs, pos, o_f, o_b, proj_h, h_f, h_b, proj_h, mods, hg.reshape(1, A_DK), w_out)


def _route(scores, sel):
    cols = [sel[e:e + 1, :] for e in range(N_EXPERTS)]

    def rank(vals):
        out = []
        for i, vi in enumerate(vals):
            r = None
            for jx, vj in enumerate(vals):
                if jx == i:
                    continue
                beats = (vj >= vi) if jx < i else (vj > vi)
                r = beats.astype(F32) if r is None else r + beats.astype(F32)
            out.append(r)
        return out

    grp_scores, in_top2 = [], []
    for gi in range(N_GROUPS):
        vals = cols[gi * GROUP:(gi + 1) * GROUP]
        best_pair = None
        for a in range(GROUP):
            for bx in range(a + 1, GROUP):
                s = vals[a] + vals[bx]
                best_pair = s if best_pair is None else jnp.maximum(best_pair, s)
        grp_scores.append(best_pair)
        in_top2.extend([r < 2.0 for r in rank(vals)])
    grp_best = [r < 1.0 for r in rank(grp_scores)]
    picked = [jnp.where(jnp.logical_and(grp_best[e // GROUP], in_top2[e]), scores[e:e + 1, :], 0.0)
              for e in range(N_EXPERTS)]
    total = picked[0]
    for pe in picked[1:]:
        total = total + pe
    row = lax.broadcasted_iota(jnp.int32, scores.shape, 0)
    comb = jnp.zeros(scores.shape, F32)
    for e in range(N_EXPERTS):
        comb = jnp.where(row == e, picked[e] / total, comb)
    return comb


def _moe_kernel(final_norm, x_ref, mod_ref, g_ref, rw_ref, rb_ref, w1_ref, w3_ref, w2_ref, gf_ref, *rest):
    o_refs, (h_sc, comb_sc, acc_sc) = rest[:-3], rest[-3:]
    e = pl.program_id(1)

    @pl.when(e == 0)
    def _():
        for s in range(MOE_TM // SEG):
            m = mod_ref[s]
            rows = slice(s * SEG, (s + 1) * SEG)
            h = _rms_mod(x_ref[rows, :], g_ref[...], m[4:5], m[3:4])
            h_sc[rows, :] = h.astype(BF16)
            scores = 1.0 / (1.0 + jnp.exp(-_dot_x3_nt(rw_ref[...], h)))
            comb_t = _route(scores, scores + rb_ref[...])
            comb_t = jnp.concatenate([comb_t, jnp.zeros((COMB_LANES - N_EXPERTS, SEG), F32)], axis=0)
            comb_sc[rows, :] = comb_t.T
        acc_sc[...] = jnp.zeros_like(acc_sc)

    lane = lax.broadcasted_iota(jnp.int32, (MOE_TM, COMB_LANES), 1)
    h = h_sc[...]
    es = w1_ref.shape[1]
    hids = []
    for ee in range(es):
        comb = jnp.sum(jnp.where(lane == e * es + ee, comb_sc[...], 0.0), axis=-1, keepdims=True)
        u1 = jnp.dot(h, w1_ref[0, ee].astype(BF16), preferred_element_type=F32)
        u3 = jnp.dot(h, w3_ref[0, ee].astype(BF16), preferred_element_type=F32)
        hids.append((_silu(u1) * u3 * comb).astype(BF16))
    w2 = w2_ref[0].reshape(es * D_EXPERT, D).astype(BF16)
    acc_sc[...] += jnp.dot(jnp.concatenate(hids, axis=1), w2, preferred_element_type=F32)

    def emit(dst_ref):
        for s in range(MOE_TM // SEG):
            rows = slice(s * SEG, (s + 1) * SEG)
            y = x_ref[rows, :] + mod_ref[s][5:6] * acc_sc[rows, :]
            if final_norm:
                y = y * lax.rsqrt(jnp.mean(y * y, axis=-1, keepdims=True) + RMS_EPS) * gf_ref[...]
            dst_ref[rows, :] = y

    last = e == N_EXPERTS // es - 1
    if final_norm:
        is_ctx = pl.program_id(0) < MOE_CTX_TILES
        pl.when(jnp.logical_and(last, is_ctx))(lambda: emit(o_refs[0]))
        pl.when(jnp.logical_and(last, jnp.logical_not(is_ctx)))(lambda: emit(o_refs[1]))
    else:
        pl.when(last)(lambda: emit(o_refs[0]))


def _moe(x, mods, g, router_w, router_b, w1, w3, w2, layer, gf, final_norm):
    spt = MOE_TM // SEG
    es = MOE_EXPERTS_STEP
    tile = lambda f, **kw: pl.BlockSpec((MOE_TM, D), lambda t, e: (f(t), 0), **kw)
    if final_norm:
        n_half = M_TOK // 2
        out_specs = [tile(lambda t: jnp.minimum(t, MOE_CTX_TILES - 1), pipeline_mode=pl.Buffered(1)),
                     tile(lambda t: jnp.maximum(t - MOE_CTX_TILES, 0), pipeline_mode=pl.Buffered(1))]
        out_shape = [jax.ShapeDtypeStruct((n_half, D), F32), jax.ShapeDtypeStruct((M_TOK - n_half, D), F32)]
    else:
        out_specs = [tile(lambda t: t)]
        out_shape = [jax.ShapeDtypeStruct((M_TOK, D), F32)]
    return pl.pallas_call(
        functools.partial(_moe_kernel, final_norm),
        grid=(M_TOK // MOE_TM, N_EXPERTS // es),
        in_specs=[pl.BlockSpec((MOE_TM, D), lambda t, e: (t, 0)),
                  pl.BlockSpec((spt, 6, D), lambda t, e: (t, 0, 0)),
                  pl.BlockSpec((1, D), lambda t, e: (0, 0)),
                  pl.BlockSpec((N_EXPERTS, D), lambda t, e: (0, 0)),
                  pl.BlockSpec((N_EXPERTS, 1), lambda t, e: (0, 0)),
                  pl.BlockSpec((1, es, D, D_EXPERT), lambda t, e: (layer, e, 0, 0)),
                  pl.BlockSpec((1, es, D, D_EXPERT), lambda t, e: (layer, e, 0, 0)),
                  pl.BlockSpec((1, es, D_EXPERT, D), lambda t, e: (layer, e, 0, 0)),
                  pl.BlockSpec((1, D), lambda t, e: (0, 0))],
        out_specs=out_specs,
        out_shape=out_shape,
        scratch_shapes=[pltpu.VMEM((MOE_TM, D), BF16), pltpu.VMEM((MOE_TM, COMB_LANES), F32),
                        pltpu.VMEM((MOE_TM, D), F32)],
        compiler_params=_cp(("arbitrary", "arbitrary")),
        name="moe",
    )(x, mods, g.reshape(1, D), router_w.T, router_b.reshape(N_EXPERTS, 1), w1, w3, w2, gf.reshape(1, D))


def _rw_in_kernel(x_ref, xp_ref, xn_ref, mod_ref, g_ref, mu_ref, wr_ref, wk_ref, wv_ref, g1_ref, g2_ref,
                  w1_ref, w2_ref, w0_ref, a1_ref, a2_ref, a0_ref, ka_ref, rk_ref, hsel_ref,
                  r_ref, k_ref, v_ref, gg_ref, lw_ref, a_ref, coef_ref, wrkv_sc):
    i = pl.program_id(0)

    @pl.when(i == 0)
    def _():
        for c, w_ref in enumerate((wr_ref, wk_ref, wv_ref)):
            wrkv_sc[c] = w_ref[0].astype(BF16)

    m = mod_ref[0]
    g = g_ref[...]
    h = _rms_mod(x_ref[...], g, m[1:2], m[0:1])
    hp = jnp.where(_seg_is_first(i), 0.0, _rms_mod(xp_ref[...], g, m[1:2], m[0:1]))
    hn = jnp.where(_seg_is_last(i), 0.0, _rms_mod(xn_ref[...], g, m[1:2], m[0:1]))
    ext = jnp.concatenate([hp, h, hn], axis=0)
    n = ext.shape[0]
    h_prev = pltpu.roll(ext, 1, 0)[8:8 + SEG]
    h_next = pltpu.roll(ext, n - 1, 0)[8:8 + SEG]
    xx = 0.5 * (h_prev + h_next) - h
    mu = mu_ref[...]
    xr, xw, xk, xv, xa, xg = [h + xx * mu[c:c + 1] for c in range(6)]
    r = _dot(xr, wrkv_sc[0])
    k = _dot(xk, wrkv_sc[1])
    r_ref[...] = r.astype(BF16)
    k_ref[...] = k.astype(BF16)
    v_ref[...] = _dot(xv, wrkv_sc[2]).astype(BF16)
    gg_ref[...] = _dot(_sigmoid(_dot(xg, g1_ref[...])), g2_ref[...]).astype(BF16)
    half_w_in = w0_ref[...] + _dot(jnp.tanh(_dot(xw, w1_ref[...])), w2_ref[...])
    lw_ref[...] = (-0.5 * W_DECAY_SCALE * LOG2E) * jnp.tanh(half_w_in) - 0.5 * W_DECAY_SCALE * LOG2E
    a = 0.5 * jnp.tanh(a0_ref[...] + _dot(_dot(xa, a1_ref[...]), a2_ref[...])) + 0.5
    a_ref[...] = a
    kd_sum = k * (2.0 + (a[:, 0:D] + a[:, D:2 * D] - 2.0) * ka_ref[...])
    coef_ref[...] = _dot_sel_rhs(r * kd_sum * rk_ref[...], hsel_ref[...])


def _rw_inproj(x, mods, g, mu, wr, wk, wv, g1, g2, w1c, w2bd, w0c, a1c, a2bd, a0c, ka, rk):
    full = lambda a: pl.BlockSpec(a.shape, lambda i: (0,) * a.ndim)
    once = lambda a: pl.BlockSpec((1,) + a.shape[1:], lambda i: (0,) * a.ndim, pipeline_mode=pl.Buffered(1))
    seg = lambda width: pl.BlockSpec((SEG, width), lambda i: (i, 0))
    outs = ([jax.ShapeDtypeStruct((M_TOK, D), BF16)] * 4 + [jax.ShapeDtypeStruct((M_TOK, 2 * D), F32)] * 2
            + [jax.ShapeDtypeStruct((M_TOK, COEF_LANES), F32)])
    hsel = jnp.asarray(np.arange(D)[:, None] // C_HEAD == np.arange(COEF_LANES)[None, :], BF16)
    return pl.pallas_call(
        _rw_in_kernel,
        grid=(NSEG,),
        in_specs=[seg(D),
                  pl.BlockSpec((8, D), lambda i: (_prev8(i), 0)),
                  pl.BlockSpec((8, D), lambda i: (_next8(i), 0)),
                  pl.BlockSpec((1, 6, D), lambda i: (i, 0, 0)),
                  full(g), full(mu), once(wr), once(wk), once(wv), full(g1), full(g2),
                  full(w1c), full(w2bd), full(w0c), full(a1c), full(a2bd), full(a0c),
                  full(ka), full(rk), full(hsel)],
        out_specs=[seg(D)] * 4 + [seg(2 * D)] * 2 + [seg(COEF_LANES)],
        out_shape=outs,
        scratch_shapes=[pltpu.VMEM((3, D, D), BF16)],
        compiler_params=_cp(("arbitrary",)),
        name="rwkv_inproj",
    )(x, x, x, mods, g, mu, wr, wk, wv, g1, g2, w1c, w2bd, w0c, a1c, a2bd, a0c, ka, rk, hsel)


def _rw_consts():
    t = np.arange(SEG)
    same = (t[:, None] // RW_CHUNK) == (t[None, :] // RW_CHUNK)
    tri_f = np.logical_and(same, t[None, :] <= t[:, None]).astype(np.float32)
    tri_b = np.logical_and(same, t[None, :] >= t[:, None]).astype(np.float32)
    return [jnp.asarray(a, BF16) for a in (tri_f, tri_b)]


def _rw_dir(r, k, v, lw, a, kk_g, ka_g, tri, rev):
    c = RW_CHUNK
    lane = lax.broadcasted_iota(jnp.int32, (SEG, 2 * C_HEAD), 1)
    head0 = lane < C_HEAD
    kx = k * kk_g
    ss = kx * kx
    n0 = jnp.sum(jnp.where(head0, ss, 0.0), axis=-1, keepdims=True)
    n1 = jnp.sum(jnp.where(head0, 0.0, ss), axis=-1, keepdims=True)
    kk = kx / jnp.maximum(jnp.sqrt(jnp.where(head0, n0, n1)), 1e-12)
    kd = k * (1.0 + (a - 1.0) * ka_g)
    bhat = kk * a
    cum = _dot_sel(tri, lw)
    e_incl = jnp.exp2(cum)
    e_inv = jnp.exp2(-cum)
    ae = -kk * jnp.exp2(cum - lw)
    re = r * e_incl
    bi = bhat * e_inv
    ki = kd * e_inv

    chunks = []
    for ci in range(SEG // c):
        sl = slice(ci * c, (ci + 1) * c)
        ctot = cum[ci * c:ci * c + 1] if rev else cum[(ci + 1) * c - 1:(ci + 1) * c]
        dec = jnp.exp2(ctot - cum[sl])
        chunks.append(dict(ae=ae[sl], re=re[sl], bi=bi[sl], ki=ki[sl], v=v[sl], bdec=bhat[sl] * dec,
                           kdec=kd[sl] * dec, gam=jnp.exp2(ctot), rev=rev))
    return chunks


def _rw_transitions(chunks):
    c = RW_CHUNK
    w2 = 2 * C_HEAD
    h0c = lax.broadcasted_iota(jnp.int32, (c, w2), 1) < C_HEAD
    rowc = lax.broadcasted_iota(jnp.int32, (c, w2), 0)
    colc = jnp.bitwise_and(lax.broadcasted_iota(jnp.int32, (c, w2), 1), C_HEAD - 1)
    eye = (colc == rowc).astype(F32)
    bdmask = (lax.broadcasted_iota(jnp.int32, (w2, w2), 0) < C_HEAD) == (
        lax.broadcasted_iota(jnp.int32, (w2, w2), 1) < C_HEAD)
    keep0 = jnp.where(h0c, 1.0, 0.0).astype(BF16)
    keep1 = jnp.where(h0c, 0.0, 1.0).astype(BF16)

    def bd16(yb):
        return jnp.concatenate([yb * keep0, yb * keep1], axis=0)

    def dot_bd(x, pairs):
        blocks = [bd16(y.astype(BF16)) for y in pairs]
        rhs = blocks[0] if len(blocks) == 1 else jnp.concatenate(blocks, axis=1)
        return jnp.dot(x.astype(BF16), rhs, preferred_element_type=F32)

    mm_inv = lambda x, y: dot_bd(x, [y])

    n_ab, a_ak, a_rb, a_rk = [], [], [], []
    for ch in chunks:
        strict = (colc > rowc) if ch["rev"] else (colc < rowc)
        incl = (colc >= rowc) if ch["rev"] else (colc <= rowc)
        left = jnp.concatenate([ch["ae"], ch["re"]], axis=0).astype(BF16)
        right = jnp.concatenate([bd16(ch["bi"].astype(BF16)), bd16(ch["ki"].astype(BF16))], axis=0)
        gm = lax.dot_general(left, right, (((1,), (1,)), ((), ())), preferred_element_type=F32)
        n_ab.append(jnp.where(strict, gm[0:c, 0:2 * c], 0.0))
        a_ak.append(jnp.where(strict, gm[0:c, 2 * c:4 * c], 0.0))
        a_rb.append(jnp.where(incl, gm[c:2 * c, 0:2 * c], 0.0))
        a_rk.append(jnp.where(incl, gm[c:2 * c, 2 * c:4 * c], 0.0))
    xorc = jnp.bitwise_xor(rowc, colc)
    tm = [eye + jnp.where(xorc < 2, n, 0.0) for n in n_ab]
    blk = 2
    while blk < c:
        couple = jnp.logical_and(xorc >= blk, xorc < 2 * blk)
        xs = [mm_inv(jnp.where(couple, n, 0.0), t) for n, t in zip(n_ab, tm)]
        tm = [t + mm_inv(t, x) for t, x in zip(tm, xs)]
        blk *= 2
    akv = [dot_bd(jnp.concatenate([x, y], axis=0), [ch["v"]])
           for x, y, ch in zip(a_ak, a_rk, chunks)]
    tav = [dot_bd(t, [ch["ae"], x[0:c]]) for t, ch, x in zip(tm, chunks, akv)]
    out = []
    for i, ch in enumerate(chunks):
        ta, tv = tav[i][:, 0:w2], tav[i][:, w2:2 * w2]
        qy = dot_bd(a_rb[i], [ta, tv])
        q = ch["re"] + qy[:, 0:w2]
        y = qy[:, w2:2 * w2] + akv[i][c:2 * c]
        wz = _dot_tn(tav[i], ch["bdec"])
        w = jnp.where(bdmask, wz[0:w2], 0.0)
        z = jnp.where(bdmask, wz[w2:2 * w2] + _dot_tn(ch["v"], ch["kdec"]), 0.0)
        out.append((q, y, w, z, ch["gam"]))
    return out


def _rw_scan_kernel(rf, kf, vf, lwf, af, rb, kb, vb, lwb, ab, kkg_ref, kag_ref, s0f, s0b, trif, trib,
                    of_ref, ob_ref, fin_ref, st_f, st_b):
    j = pl.program_id(1)
    w2 = 2 * C_HEAD
    dirs = ((False, (rf, kf, vf, lwf, af), s0f, trif, of_ref, 0, st_f),
            (True, (rb, kb, vb, lwb, ab), s0b, trib, ob_ref, 1, st_b))
    for rev, refs, s0, tri, o_ref, s_out, st in dirs:
        i = (NSEG - 1 - j) if rev else j
        starts = _seg_is_last(i) if rev else _seg_is_first(i)

        @pl.when(jnp.logical_and(starts, i < N_CTX_SEG))
        def _():
            st[...] = jnp.zeros_like(st)

        @pl.when(jnp.logical_and(starts, i >= N_CTX_SEG))
        def _():
            st[...] = s0[0, 0]

    chunks = []
    for rev, refs, s0, tri, o_ref, s_out, st in dirs:
        for pp in range(RW_PAIRS_STEP):
            lanes = slice(pp * w2, (pp + 1) * w2)
            r_, k_, v_, lw_, a_ = [z[:, lanes].astype(F32) for z in refs]
            chunks.extend(_rw_dir(r_, k_, v_, lw_, a_, kkg_ref[:, lanes], kag_ref[:, lanes], tri[...], rev))
    trans = _rw_transitions(chunks)
    n_c = SEG // RW_CHUNK
    states = [[st[pp] for pp in range(RW_PAIRS_STEP)] for (_, _, _, _, _, _, st) in dirs]
    for step in range(n_c):
        for d, (rev, refs, s0, tri, o_ref, s_out, st) in enumerate(dirs):
            ci = (n_c - 1 - step) if rev else step
            for pp in range(RW_PAIRS_STEP):
                q, y, w, z, gam = trans[(d * RW_PAIRS_STEP + pp) * n_c + ci]
                s = states[d][pp]
                o_ref[ci * RW_CHUNK:(ci + 1) * RW_CHUNK, pp * w2:(pp + 1) * w2] = (_dot_nt(q, s) + y).astype(BF16)
                states[d][pp] = s * gam + _dot(s, w) + z
    for d, (rev, refs, s0, tri, o_ref, s_out, st) in enumerate(dirs):
        for pp in range(RW_PAIRS_STEP):
            st[pp] = states[d][pp]
    for d, (rev, refs, s0, tri, o_ref, s_out, st) in enumerate(dirs):
        i = (NSEG - 1 - j) if rev else j

        @pl.when(i < N_CTX_SEG)
        def _():
            for pp in range(RW_PAIRS_STEP):
                s = states[d][pp]
                head = 2 * (pl.program_id(0) * RW_PAIRS_STEP + pp)
                fin_ref[i, 0, d, head] = s[0:C_HEAD, 0:C_HEAD]
                fin_ref[i, 0, d, head + 1] = s[C_HEAD:w2, C_HEAD:w2]


def _rw_scan(r, k, v, lw, a, kk_g, ka_g, s0bd):
    consts = _rw_consts()
    w = 2 * C_HEAD
    pps = RW_PAIRS_STEP
    wb = pps * w
    n_steps = C_PAIRS // pps
    blk = lambda col0, rev: pl.BlockSpec(
        (SEG, wb), (lambda p, j: (NSEG - 1 - j, col0 + p)) if rev else (lambda p, j: (j, col0 + p)))
    cspec = lambda arr: pl.BlockSpec(arr.shape, lambda p, j: (0, 0))
    fin_shape = (N_CTX_SEG, 1, 2, 2 * C_PAIRS, C_HEAD, C_HEAD)
    fin = pl.BlockSpec(fin_shape, lambda p, j: (0,) * len(fin_shape), pipeline_mode=pl.Buffered(1))
    return pl.pallas_call(
        _rw_scan_kernel,
        grid=(n_steps, NSEG),
        in_specs=[blk(0, False), blk(0, False), blk(0, False), blk(0, False), blk(0, False),
                  blk(0, True), blk(0, True), blk(0, True), blk(n_steps, True), blk(n_steps, True),
                  pl.BlockSpec((1, wb), lambda p, j: (0, p)), pl.BlockSpec((1, wb), lambda p, j: (0, p)),
                  pl.BlockSpec((1, 1, pps, w, w), lambda p, j: (_sample_of(j), 0, p, 0, 0)),
                  pl.BlockSpec((1, 1, pps, w, w), lambda p, j: (_sample_of(NSEG - 1 - j), 1, p, 0, 0)),
                  cspec(consts[0]), cspec(consts[1])],
        out_specs=[blk(0, False), blk(0, True), fin],
        out_shape=[jax.ShapeDtypeStruct((M_TOK, D), BF16), jax.ShapeDtypeStruct((M_TOK, D), BF16),
                   jax.ShapeDtypeStruct(fin_shape, F32)],
        scratch_shapes=[pltpu.VMEM((pps, w, w), F32), pltpu.VMEM((pps, w, w), F32)],
        compiler_params=_cp(("arbitrary", "arbitrary")),
        name="rwkv7_scan",
    )(r, k, v, lw, a, r, k, v, lw, a, kk_g, ka_g, s0bd, s0bd, *consts)


def _rw_out_kernel(x_ref, of_ref, ob_ref, v_ref, gg_ref, coef_ref, mod_ref, lnw_ref, lnb_ref, ones_ref,
                   hexp_ref, wo_ref, o_ref, wo_sc):
    @pl.when(pl.program_id(0) == 0)
    def _():
        wo_sc[...] = wo_ref[0].astype(BF16)

    m = mod_ref[0]
    ones_bd = ones_ref[...]
    w = 2 * C_HEAD
    inv_n = 1.0 / C_HEAD
    coef = _dot_sel_rhs(coef_ref[...], hexp_ref[...])
    parts = []
    for p in range(C_PAIRS):
        cs = slice(p * w, (p + 1) * w)
        osum = of_ref[:, cs].astype(F32) + ob_ref[:, cs].astype(F32)
        mu = _dot_sel_rhs(osum, ones_bd) * inv_n
        cen = osum - mu
        var = _dot_sel_rhs(cen * cen, ones_bd) * inv_n
        o = cen * lax.rsqrt(var + GN_EPS) * lnw_ref[:, cs] + lnb_ref[:, cs]
        bonus = coef[:, cs] * v_ref[:, cs].astype(F32)
        parts.append((o + bonus) * gg_ref[:, cs].astype(F32))
    y = _dot(jnp.concatenate(parts, axis=-1), wo_sc[...])
    o_ref[...] = x_ref[...] + m[2:3] * y


def _dot_sel_rhs(x, mat):
    h = x.astype(BF16)
    l = (x - h.astype(F32)).astype(BF16)
    return jnp.dot(h, mat, preferred_element_type=F32) + jnp.dot(l, mat, preferred_element_type=F32)


def _rw_out(x, o_f, o_b, v, gg, coef, mods, lnw, lnb, wo):
    seg = lambda width: pl.BlockSpec((SEG, width), lambda i: (i, 0))
    row = pl.BlockSpec((1, D), lambda i: (0, 0))
    hh = np.arange(2 * C_HEAD) // C_HEAD
    ones_bd = jnp.asarray((hh[:, None] == hh[None, :]).astype(np.float32), BF16)
    hexp = jnp.asarray(np.arange(COEF_LANES)[:, None] == np.arange(D)[None, :] // C_HEAD, BF16)
    return pl.pallas_call(
        _rw_out_kernel,
        grid=(NSEG,),
        in_specs=[seg(D), seg(D), seg(D), seg(D), seg(D), seg(COEF_LANES),
                  pl.BlockSpec((1, 6, D), lambda i: (i, 0, 0)),
                  row, row,
                  pl.BlockSpec((2 * C_HEAD, 2 * C_HEAD), lambda i: (0, 0)),
                  pl.BlockSpec((COEF_LANES, D), lambda i: (0, 0)),
                  pl.BlockSpec((1, D, D), lambda i: (0, 0, 0), pipeline_mode=pl.Buffered(1))],
        out_specs=seg(D),
        out_shape=jax.ShapeDtypeStruct((M_TOK, D), F32),
        scratch_shapes=[pltpu.VMEM((D, D), BF16)],
        compiler_params=_cp(("arbitrary",)),
        name="rwkv_out",
    )(x, o_f, o_b, v, gg, coef, mods, lnw.reshape(1, D), lnb.reshape(1, D), ones_bd, hexp, wo)


def _grid_pos_table(n_tok):
    rows = n_tok // GRID_W
    r, cl = np.meshgrid(np.arange(rows, dtype=np.float32), np.arange(GRID_W, dtype=np.float32), indexing='ij')
    quarter = D // 4
    omega = (1.0 / (np.float32(POS_BASE) ** (np.arange(quarter, dtype=np.float32) / np.float32(quarter))))
    ang_r = (r.reshape(-1, 1) * omega).astype(np.float32)
    ang_c = (cl.reshape(-1, 1) * omega).astype(np.float32)
    table = np.concatenate([np.sin(ang_r), np.cos(ang_r), np.sin(ang_c), np.cos(ang_c)], axis=-1)
    return jnp.asarray(table.astype(np.float32))


def _block_diag(blocks):
    g, n, _ = blocks.shape
    eye = jnp.eye(g, dtype=blocks.dtype)
    return (eye[:, None, :, None] * blocks[:, :, None, :]).reshape(g * n, g * n)


def kernel(x_prompt, x_sample, state_hgrn, state_rglru, state_rwkv, c, c_ctx, norm_mix_g, norm_ffn_g, w_mod, b_mod, ab_w_in, ab_w_out, hgrn_lb, hgrn_norm_g, rg_conv_w, rg_conv_b, rg_wa, rg_ba, rg_wx, rg_bx, rg_lambda, rw_mu, rw_wr, rw_wk, rw_wv, rw_wo, rw_w0, rw_w1, rw_w2, rw_a0, rw_a1, rw_a2, rw_g1, rw_g2, rw_kk, rw_ka, rw_rk, rw_lnw, rw_lnb, moe_router, moe_router_bias, moe_w1, moe_w3, moe_w2, norm_f_g):
    bf = lambda z: z.astype(BF16)
    xp = x_prompt.reshape(-1, D)
    xs = x_sample.reshape(-1, D)
    pos = _grid_pos_table(x_sample.shape[1])
    mods = _modulations(c, c_ctx, w_mod, b_mod)

    lower_bounds = jnp.cumsum(jax.nn.softmax(hgrn_lb.astype(F32), axis=1), axis=1)
    lb = lower_bounds[:, 0].reshape(2, A_HEADS, 1, A_DK)
    proj_f, proj_h = _ab_inproj(xp, xs, pos, mods[0], norm_mix_g[0], ab_w_in)
    s0t = jnp.swapaxes(state_hgrn[:, 0], -1, -2)
    o_f, o_b, new_hgrn = _gla(proj_f, proj_h, lb, s0t)
    wa_bd = bf(jnp.stack([_block_diag(rg_wa[0, d]) for d in range(2)]))
    wx_bd = bf(jnp.stack([_block_diag(rg_wx[0, d]) for d in range(2)]))
    h_f, h_b, lru_f, lru_b = _rglru(
        proj_f, rg_conv_w[0], rg_conv_b[0].reshape(1, D_B), wa_bd, rg_ba[0].reshape(2, 1, D_B), wx_bd,
        rg_bx[0].reshape(2, 1, D_B), rg_lambda[0].reshape(2, 1, D_B), state_rglru[:, 0].reshape(-1, 2, 1, D_B))
    x = _ab_out(xp, xs, pos, proj_h, o_f, o_b, h_f, h_b, mods[0], hgrn_norm_g[0], ab_w_out)
    x, = _moe(x, mods[0], norm_ffn_g[0], moe_router, moe_router_bias, moe_w1, moe_w3, moe_w2, 0, norm_f_g, False)

    w1c = bf(jnp.concatenate([rw_w1[0, 0], rw_w1[0, 1]], axis=-1))
    a1c = bf(jnp.concatenate([rw_a1[0, 0], rw_a1[0, 1]], axis=-1))
    half_w2, half_a2 = 0.5 * rw_w2[0], 0.5 * rw_a2[0]
    w2bd = bf(jnp.concatenate([jnp.concatenate([half_w2[0], jnp.zeros_like(half_w2[0])], axis=-1),
                               jnp.concatenate([jnp.zeros_like(half_w2[1]), half_w2[1]], axis=-1)], axis=0))
    a2bd = bf(jnp.concatenate([jnp.concatenate([half_a2[0], jnp.zeros_like(half_a2[0])], axis=-1),
                               jnp.concatenate([jnp.zeros_like(half_a2[1]), half_a2[1]], axis=-1)], axis=0))
    r, k, v, gg, lw, a, coef = _rw_inproj(
        x, mods[1], norm_mix_g[1].reshape(1, D), rw_mu[0], rw_wr, rw_wk, rw_wv,
        bf(rw_g1[0]), bf(rw_g2[0]), w1c, w2bd, 0.5 * rw_w0[0].reshape(1, 2 * D), a1c, a2bd,
        0.5 * rw_a0[0].reshape(1, 2 * D), rw_ka[0].reshape(1, D), rw_rk[0].reshape(1, D))
    s0 = state_rwkv[:, 0].reshape(N_SAMPLE, 2, C_PAIRS, 2, C_HEAD, C_HEAD)
    zeros = jnp.zeros_like(s0[:, :, :, 0])
    s0bd = jnp.concatenate([jnp.concatenate([s0[:, :, :, 0], zeros], axis=-1),
                            jnp.concatenate([zeros, s0[:, :, :, 1]], axis=-1)], axis=-2)
    ow_f, ow_b, new_rwkv = _rw_scan(r, k, v, lw, a, rw_kk[0].reshape(1, D), rw_ka[0].reshape(1, D), s0bd)
    x = _rw_out(x, ow_f, ow_b, v, gg, coef, mods[1], rw_lnw[0], rw_lnb[0], rw_wo)
    y_p, y_s = _moe(x, mods[1], norm_ffn_g[1], moe_router, moe_router_bias, moe_w1, moe_w3, moe_w2, 1, norm_f_g, True)

    new_rglru = jnp.stack([lru_f[:, 0], lru_b[:, 0]], axis=1)[:, None]
    return (y_p.reshape(x_prompt.shape), y_s.reshape(x_sample.shape), new_hgrn, new_rglru, new_rwkv)
```

```python
import functools
import math

import numpy as np
import jax
import jax.numpy as jnp
from jax import lax
from jax.experimental import pallas as pl
from jax.experimental.pallas import tpu as pltpu

F32 = jnp.float32
BF16 = jnp.bfloat16

D = 1024
SEG = 256
N_CTX_SEG = 16
SEG_PER_SAMPLE = 4
N_SAMPLE = 4
NSEG = N_CTX_SEG + N_SAMPLE * SEG_PER_SAMPLE
M_TOK = NSEG * SEG
SUBLANES = 8
ROWS8_PER_SEG = SEG // SUBLANES

A_HEADS = 4
A_DK = 128
D_A = 512
D_B = 512
B_BLOCKS = 8
B_BLOCK = 64
LRU_C = 8.0
D_IN_AB = 5 * D_A + 2 * D_B
AB_F32_COLS = 2 * D_A + D_B
C_HEAD = 64
C_PAIRS = 8
RW_CHUNK = 64
COEF_LANES = 128
RW_PAIRS_STEP = 8
W_DECAY_SCALE = math.exp(-0.5)
LOG2E = math.log2(math.e)
N_EXPERTS = 16
N_GROUPS = 4
GROUP = 4
D_EXPERT = 256
RMS_EPS = 1e-6
GN_EPS = 64e-5
POS_BASE = 10000.0
GRID_W = 64
MOE_TM = 1024
MOE_CTX_TILES = N_CTX_SEG * SEG // MOE_TM
MOE_EXPERTS_STEP = 4
COMB_LANES = 128
GLA_LEVELS = (1, 2, 4, 8, 16, 32, 64, 128)
GLA_HALF = 128
GLA_HEADS_STEP = 4

VMEM_LIMIT = 56 * 1024 * 1024


def _cp(sem):
    return pltpu.CompilerParams(dimension_semantics=sem, vmem_limit_bytes=VMEM_LIMIT)


def _sigmoid(x):
    return 0.5 * jnp.tanh(0.5 * x) + 0.5


def _silu(x):
    return x * _sigmoid(x)


def _gelu_tanh(x):
    return 0.5 * x * (1.0 + jnp.tanh(math.sqrt(2.0 / math.pi) * (x + 0.044715 * (x * x * x))))


def _rms_mod(x, g, scale, shift):
    ms = jnp.mean(x * x, axis=-1, keepdims=True)
    return x * lax.rsqrt(ms + RMS_EPS) * (g * (1.0 + scale)) + shift


def _dot(a, b):
    return jnp.dot(a.astype(BF16), b.astype(BF16), preferred_element_type=F32)


def _dot_nt(a, b):
    return lax.dot_general(a.astype(BF16), b.astype(BF16), (((1,), (1,)), ((), ())),
                           preferred_element_type=F32)


def _dot_tn(a, b):
    return lax.dot_general(a.astype(BF16), b.astype(BF16), (((0,), (0,)), ((), ())),
                           preferred_element_type=F32)


def _dot_sel(mat, x):
    h = x.astype(BF16)
    l = (x - h.astype(F32)).astype(BF16)
    return jnp.dot(mat, h, preferred_element_type=F32) + jnp.dot(mat, l, preferred_element_type=F32)


def _dot_x3_nt(a, b):
    dn = (((1,), (1,)), ((), ()))
    ah = a.astype(BF16)
    al = (a - ah.astype(F32)).astype(BF16)
    bh = b.astype(BF16)
    bl = (b - bh.astype(F32)).astype(BF16)
    return (lax.dot_general(ah, bh, dn, preferred_element_type=F32)
            + lax.dot_general(ah, bl, dn, preferred_element_type=F32)
            + lax.dot_general(al, bh, dn, preferred_element_type=F32))


def _seg_is_first(i):
    return jnp.logical_or(i < N_CTX_SEG, lax.rem(i - N_CTX_SEG, SEG_PER_SAMPLE) == 0)


def _seg_is_last(i):
    return jnp.logical_or(i < N_CTX_SEG, lax.rem(i - N_CTX_SEG, SEG_PER_SAMPLE) == SEG_PER_SAMPLE - 1)


def _sample_of(i):
    return jnp.maximum(i - N_CTX_SEG, 0) // SEG_PER_SAMPLE


def _prev8(i):
    return jnp.maximum(i * ROWS8_PER_SEG - 1, 0)


def _next8(i):
    return jnp.minimum((i + 1) * ROWS8_PER_SEG, M_TOK // 8 - 1)


def _mod_kernel(cv_ref, w_ref, b_ref, o_ref):
    cv = cv_ref[...]
    o_ref[0] = _dot(_silu(cv), w_ref[0]) + b_ref[0]


def _modulations(c, c_ctx, w_mod, b_mod):
    depth = w_mod.shape[0]
    cv = jnp.concatenate([c_ctx[None, :], c, jnp.zeros((3, D), F32)], axis=0)
    wt = 2 * D
    mod = pl.pallas_call(
        _mod_kernel,
        grid=(depth, 6 * D // wt),
        in_specs=[pl.BlockSpec((8, D), lambda l, n: (0, 0)),
                  pl.BlockSpec((1, D, wt), lambda l, n: (l, 0, n)),
                  pl.BlockSpec((1, 1, wt), lambda l, n: (l, 0, n))],
        out_specs=pl.BlockSpec((1, 8, wt), lambda l, n: (l, 0, n)),
        out_shape=jax.ShapeDtypeStruct((depth, 8, 6 * D), F32),
        compiler_params=_cp(("arbitrary", "arbitrary")),
        name="adaln_mod",
    )(cv, w_mod, b_mod.reshape(depth, 1, 6 * D))
    row_of_seg = np.array([0] * N_CTX_SEG + [1 + s // SEG_PER_SAMPLE for s in range(N_SAMPLE * SEG_PER_SAMPLE)])
    return mod[:, row_of_seg].reshape(depth, NSEG, 6, D)


def _x0_specs():
    return [pl.BlockSpec((SEG, D), lambda i: (jnp.minimum(i, N_CTX_SEG - 1), 0)),
            pl.BlockSpec((SEG, D), lambda i: (jnp.maximum(i - N_CTX_SEG, 0), 0)),
            pl.BlockSpec((SEG, D), lambda i: (lax.rem(jnp.maximum(i - N_CTX_SEG, 0), SEG_PER_SAMPLE), 0))]


def _x0(i, xp_ref, xs_ref, pos_ref):
    return jnp.where(i < N_CTX_SEG, xp_ref[...], xs_ref[...] + pos_ref[...])


def _ab_in_kernel(xp_ref, xs_ref, pos_ref, mod_ref, g_ref, w_ref, of_ref, oh_ref, w_sc):
    @pl.when(pl.program_id(0) == 0)
    def _():
        w_sc[...] = w_ref[0].astype(BF16)

    m = mod_ref[0]
    x = _x0(pl.program_id(0), xp_ref, xs_ref, pos_ref)
    h = _rms_mod(x, g_ref[...], m[1:2], m[0:1])
    res = jnp.dot(h.astype(BF16), w_sc[...], preferred_element_type=F32)
    of_ref[:, 0:2 * D_A] = res[:, D_A:3 * D_A]
    of_ref[:, 2 * D_A:AB_F32_COLS] = res[:, 5 * D_A:5 * D_A + D_B]
    oh_ref[:, 0:D_A] = res[:, 0:D_A].astype(BF16)
    oh_ref[:, D_A:3 * D_A] = res[:, 3 * D_A:5 * D_A].astype(BF16)
    oh_ref[:, 3 * D_A:3 * D_A + D_B] = res[:, 5 * D_A + D_B:D_IN_AB].astype(BF16)


def _ab_inproj(xp, xs, pos, mods, g, w_in):
    return pl.pallas_call(
        _ab_in_kernel,
        grid=(NSEG,),
        in_specs=_x0_specs() + [pl.BlockSpec((1, 6, D), lambda i: (i, 0, 0)),
                                pl.BlockSpec((1, D), lambda i: (0, 0)),
                                pl.BlockSpec((1, D, D_IN_AB), lambda i: (0, 0, 0),
                                             pipeline_mode=pl.Buffered(1))],
        out_specs=[pl.BlockSpec((SEG, AB_F32_COLS), lambda i: (i, 0)),
                   pl.BlockSpec((SEG, D_IN_AB - AB_F32_COLS), lambda i: (i, 0))],
        out_shape=[jax.ShapeDtypeStruct((M_TOK, AB_F32_COLS), F32),
                   jax.ShapeDtypeStruct((M_TOK, D_IN_AB - AB_F32_COLS), BF16)],
        scratch_shapes=[pltpu.VMEM((D, D_IN_AB), BF16)],
        compiler_params=_cp(("arbitrary",)),
        name="ab_inproj",
    )(xp, xs, pos, mods, g.reshape(1, D), w_in)


def _gla_consts():
    t = np.arange(SEG)
    tri_f = (t[None, :] <= t[:, None]).astype(np.float32)
    tri_b = (t[None, :] >= t[:, None]).astype(np.float32)
    th = np.arange(GLA_HALF)
    xor = th[:, None] ^ th[None, :]
    hb = np.where(xor > 0, 1 << np.floor(np.log2(np.maximum(xor, 1))).astype(np.int64), 0)
    code_f = np.where(th[None, :] < th[:, None], hb, 0).astype(np.int32)
    code_b = np.where(th[None, :] > th[:, None], hb, 0).astype(np.int32)
    return [jnp.asarray(tri_f, BF16), jnp.asarray(tri_b, BF16), jnp.asarray(code_f), jnp.asarray(code_b)]


def _gla_level_operand(q, k, b, g, rowi, w, rev):
    upper = jnp.bitwise_and(rowi, w) != 0
    qside = jnp.logical_not(upper) if rev else upper
    if w == 1:
        z = jnp.where(qside, g, 0.0)
    elif w >= SUBLANES:
        nv = 2 * w // SUBLANES
        b4 = b.reshape(SEG // (2 * w), nv, SUBLANES, A_DK)
        ref = (b4[:, nv // 2:nv // 2 + 1, 0:1, :] if rev
               else b4[:, nv // 2 - 1:nv // 2, SUBLANES - 1:SUBLANES, :])
        x = (b4 - ref).reshape(SEG, A_DK)
        z = jnp.where(qside, x, -x)
    else:
        b3 = b.reshape(SEG // SUBLANES, SUBLANES, A_DK)
        sub = lax.broadcasted_iota(jnp.int32, b3.shape, 1)
        beta = None
        for jb in range(SUBLANES // (2 * w)):
            r = jb * 2 * w + (w if rev else w - 1)
            cand = jnp.broadcast_to(b3[:, r:r + 1, :], b3.shape)
            beta = cand if beta is None else jnp.where(sub >= jb * 2 * w, cand, beta)
        x = (b3 - beta).reshape(SEG, A_DK)
        z = jnp.where(qside, x, -x)
    return jnp.where(qside, q, k) * jnp.exp2(z)


def _gla_dir(qraw, fraw, v, lb, st, tri, code, rev):
    hh = GLA_HALF
    q = _silu(qraw)
    f = lb + (1.0 - lb) * _sigmoid(fraw)
    g = jnp.log2(f)
    k = 1.0 - f
    b = _dot_sel(tri, g)
    rowi = lax.broadcasted_iota(jnp.int32, (SEG, A_DK), 0)
    att = [jnp.zeros((hh, hh), F32), jnp.zeros((hh, hh), F32)]
    cross = None
    for w in GLA_LEVELS:
        m = _gla_level_operand(q, k, b, g, rowi, w, rev).astype(BF16)
        if w == hh:
            cross = _dot_nt(m[0:hh], m[hh:SEG]) if rev else _dot_nt(m[hh:SEG], m[0:hh])
        else:
            for half in range(2):
                mh = m[half * hh:(half + 1) * hh]
                att[half] = jnp.where(code == w, _dot_nt(mh, mh), att[half])
    if rev:
        o_lo = _dot(jnp.concatenate([att[0], cross], axis=1), v)
        o_hi = _dot(att[1], v[hh:SEG])
    else:
        o_lo = _dot(att[0], v[0:hh])
        o_hi = _dot(jnp.concatenate([cross, att[1]], axis=1), v)
    diag = jnp.sum(q * k, axis=-1, keepdims=True)
    o = jnp.concatenate([o_lo, o_hi], axis=0) + diag * v + _dot_nt(q * jnp.exp2(b), st)
    btot = b[0:1] if rev else b[SEG - 1:SEG]
    st_new = st * jnp.exp2(btot) + _dot_tn(v, k * jnp.exp2(btot - b))
    return o, st_new


def _gla_kernel(qf, ff, vf, qb, fb, vb, lb_ref, s0f, s0b, trif, trib, codef, codeb,
                of_ref, ob_ref, fin_ref, st_f, st_b):
    j = pl.program_id(1)
    dirs = ((False, qf, ff, vf, s0f, trif, codef, of_ref, st_f),
            (True, qb, fb, vb, s0b, trib, codeb, ob_ref, st_b))
    for rev, qr, fr, vr, s0, tri, code, o_ref, st in dirs:
        i = (NSEG - 1 - j) if rev else j
        starts = _seg_is_last(i) if rev else _seg_is_first(i)

        @pl.when(jnp.logical_and(starts, i < N_CTX_SEG))
        def _():
            st[...] = jnp.zeros_like(st)

        @pl.when(jnp.logical_and(starts, i >= N_CTX_SEG))
        def _():
            st[...] = s0[0, 0]

    finals = []
    for d, (rev, qr, fr, vr, s0, tri, code, o_ref, st) in enumerate(dirs):
        for hh in range(GLA_HEADS_STEP):
            lanes = slice(hh * A_DK, (hh + 1) * A_DK)
            o, st_new = _gla_dir(qr[:, lanes].astype(F32), fr[:, lanes], vr[:, lanes].astype(F32),
                                 lb_ref[d, hh], st[hh], tri[...], code[...], rev)
            o_ref[:, lanes] = o.astype(BF16)
            finals.append(st_new)
    for d, (rev, qr, fr, vr, s0, tri, code, o_ref, st) in enumerate(dirs):
        for hh in range(GLA_HEADS_STEP):
            st[hh] = finals[d * GLA_HEADS_STEP + hh]
    for d, (rev, qr, fr, vr, s0, tri, code, o_ref, st) in enumerate(dirs):
        i = (NSEG - 1 - j) if rev else j

        @pl.when(i < N_CTX_SEG)
        def _():
            for hh in range(GLA_HEADS_STEP):
                fin_ref[i, 0, d, pl.program_id(0) * GLA_HEADS_STEP + hh] = finals[d * GLA_HEADS_STEP + hh].T


def _gla(proj_f, proj_h, lb, s0t):
    consts = _gla_consts()
    hs = GLA_HEADS_STEP
    wb = hs * A_DK
    n_col = D_A // wb
    blk = lambda col0, rev: pl.BlockSpec(
        (SEG, wb), (lambda h, j: (NSEG - 1 - j, col0 + h)) if rev else (lambda h, j: (j, col0 + h)))
    cspec = lambda a: pl.BlockSpec(a.shape, lambda h, j: (0, 0))
    fin_shape = (N_CTX_SEG, 1, 2, A_HEADS, A_DK, A_DK)
    fin = pl.BlockSpec(fin_shape, lambda h, j: (0,) * len(fin_shape))
    return pl.pallas_call(
        _gla_kernel,
        grid=(A_HEADS // hs, NSEG),
        in_specs=[blk(0, False), blk(0, False), blk(n_col, False),
                  blk(0, True), blk(n_col, True), blk(n_col, True),
                  pl.BlockSpec((2, hs, 1, A_DK), lambda h, j: (0, h, 0, 0)),
                  pl.BlockSpec((1, 1, hs, A_DK, A_DK), lambda h, j: (_sample_of(j), 0, h, 0, 0)),
                  pl.BlockSpec((1, 1, hs, A_DK, A_DK), lambda h, j: (_sample_of(NSEG - 1 - j), 1, h, 0, 0)),
                  cspec(consts[0]), cspec(consts[1]), cspec(consts[2]), cspec(consts[3])],
        out_specs=[blk(0, False), blk(0, True), fin],
        out_shape=[jax.ShapeDtypeStruct((M_TOK, D_A), BF16), jax.ShapeDtypeStruct((M_TOK, D_A), BF16),
                   jax.ShapeDtypeStruct(fin_shape, F32)],
        scratch_shapes=[pltpu.VMEM((hs, A_DK, A_DK), F32), pltpu.VMEM((hs, A_DK, A_DK), F32)],
        compiler_params=_cp(("arbitrary", "arbitrary")),
        name="hgrn2_gla",
    )(proj_h, proj_f, proj_h, proj_h, proj_f, proj_h, lb, s0t, s0t, *consts)


def _lin_scan(a, b, h_in, rev):
    t_len, c = a.shape
    ng = t_len // SUBLANES
    a3 = a.reshape(ng, SUBLANES, c)
    b3 = b.reshape(ng, SUBLANES, c)
    sub = lax.broadcasted_iota(jnp.int32, a3.shape, 1)
    s = 1
    while s < SUBLANES:
        shift = (SUBLANES - s) if rev else s
        valid = (sub < SUBLANES - s) if rev else (sub >= s)
        ap = jnp.where(valid, pltpu.roll(a3, shift, 1), 1.0)
        bp = jnp.where(valid, pltpu.roll(b3, shift, 1), 0.0)
        b3 = a3 * bp + b3
        a3 = a3 * ap
        s *= 2
    hs = [None] * ng
    carry = h_in
    for j in (range(ng - 1, -1, -1) if rev else range(ng)):
        hs[j] = a3[j] * carry + b3[j]
        carry = hs[j][0:1] if rev else hs[j][SUBLANES - 1:SUBLANES]
    return jnp.concatenate(hs, axis=0), carry


def _rglru_dir(x, xprev, xnext, first, last, cw, cb, wa, ba, wx, bx, lam, h_in, rev):
    zero = jnp.zeros_like(xprev)
    ext = jnp.concatenate([jnp.where(first, zero, xprev), x, jnp.where(last, zero, xnext)], axis=0)
    n = ext.shape[0]
    xm2 = pltpu.roll(ext, 2, 0)[8:8 + SEG]
    xm1 = pltpu.roll(ext, 1, 0)[8:8 + SEG]
    xp1 = pltpu.roll(ext, n - 1, 0)[8:8 + SEG]
    xc = cb + xm2 * cw[0:1] + xm1 * cw[1:2] + x * cw[2:3] + xp1 * cw[3:4]
    gate_r = _sigmoid(_dot(xc, wa) + ba)
    gate_i = _sigmoid(_dot(xc, wx) + bx)
    softplus_neg_lam = jnp.maximum(-lam, 0.0) + jnp.log1p(jnp.exp(-jnp.abs(lam)))
    log_a = -LRU_C * gate_r * softplus_neg_lam
    a = jnp.exp(log_a)
    t = jnp.tanh(log_a)
    b_in = jnp.sqrt(-2.0 * t / (1.0 - t)) * gate_i * xc
    return _lin_scan(a, b_in, h_in, rev)


def _rglru_kernel(xf, xf_p, xf_n, xb, xb_p, xb_n, cw_ref, cb_ref, wa_ref, ba_ref, wx_ref, bx_ref, lam_ref,
                  s0f, s0b, hf_ref, hb_ref, ff_out, fb_out, hc_f, hc_b):
    j = pl.program_id(0)
    dirs = ((False, xf, xf_p, xf_n, s0f, hf_ref, ff_out, hc_f),
            (True, xb, xb_p, xb_n, s0b, hb_ref, fb_out, hc_b))
    for rev, xr, xp, xn, s0, h_ref, f_out, hc in dirs:
        i = (NSEG - 1 - j) if rev else j
        starts = _seg_is_last(i) if rev else _seg_is_first(i)

        @pl.when(jnp.logical_and(starts, i < N_CTX_SEG))
        def _():
            hc[...] = jnp.zeros_like(hc)

        @pl.when(jnp.logical_and(starts, i >= N_CTX_SEG))
        def _():
            hc[...] = s0[0, 0]

    outs = []
    for d, (rev, xr, xp, xn, s0, h_ref, f_out, hc) in enumerate(dirs):
        i = (NSEG - 1 - j) if rev else j
        h, h_out = _rglru_dir(xr[...], xp[...], xn[...], _seg_is_first(i), _seg_is_last(i), cw_ref[...],
                              cb_ref[...], wa_ref[d], ba_ref[d], wx_ref[d], bx_ref[d], lam_ref[d], hc[...], rev)
        h_ref[...] = h.astype(BF16)
        outs.append(h_out)
    for d, (rev, xr, xp, xn, s0, h_ref, f_out, hc) in enumerate(dirs):
        hc[...] = outs[d]
    for d, (rev, xr, xp, xn, s0, h_ref, f_out, hc) in enumerate(dirs):
        i = (NSEG - 1 - j) if rev else j

        @pl.when(i < N_CTX_SEG)
        def _():
            f_out[0] = outs[d]


def _rglru(proj, conv_w, conv_b, wa_bd, ba, wx_bd, bx, lam, s0):
    xcol = 2 * D_A // D_B
    fwd = lambda f: (lambda j: f(j))
    bwd = lambda f: (lambda j: f(NSEG - 1 - j))
    seg_blk = lambda m: pl.BlockSpec((SEG, D_B), m(lambda i: (i, xcol)))
    prev_blk = lambda m: pl.BlockSpec((8, D_B), m(lambda i: (_prev8(i), xcol)))
    next_blk = lambda m: pl.BlockSpec((8, D_B), m(lambda i: (_next8(i), xcol)))
    full = lambda a: pl.BlockSpec(a.shape, lambda j: (0,) * a.ndim)
    return pl.pallas_call(
        _rglru_kernel,
        grid=(NSEG,),
        in_specs=[seg_blk(fwd), prev_blk(fwd), next_blk(fwd), seg_blk(bwd), prev_blk(bwd), next_blk(bwd),
                  full(conv_w), full(conv_b), full(wa_bd), full(ba), full(wx_bd), full(bx), full(lam),
                  pl.BlockSpec((1, 1, 1, D_B), lambda j: (_sample_of(j), 0, 0, 0)),
                  pl.BlockSpec((1, 1, 1, D_B), lambda j: (_sample_of(NSEG - 1 - j), 1, 0, 0))],
        out_specs=[pl.BlockSpec((SEG, D_B), lambda j: (j, 0)),
                   pl.BlockSpec((SEG, D_B), lambda j: (NSEG - 1 - j, 0)),
                   pl.BlockSpec((1, 1, D_B), lambda j: (jnp.minimum(j, N_CTX_SEG - 1), 0, 0)),
                   pl.BlockSpec((1, 1, D_B), lambda j: (jnp.minimum(NSEG - 1 - j, N_CTX_SEG - 1), 0, 0))],
        out_shape=[jax.ShapeDtypeStruct((M_TOK, D_B), BF16), jax.ShapeDtypeStruct((M_TOK, D_B), BF16),
                   jax.ShapeDtypeStruct((N_CTX_SEG, 1, D_B), F32), jax.ShapeDtypeStruct((N_CTX_SEG, 1, D_B), F32)],
        scratch_shapes=[pltpu.VMEM((1, D_B), F32), pltpu.VMEM((1, D_B), F32)],
        compiler_params=_cp(("arbitrary",)),
        name="rglru",
    )(proj, proj, proj, proj, proj, proj, conv_w, conv_b, wa_bd, ba, wx_bd, bx, lam, s0, s0)


def _ab_out_kernel(xp_ref, xs_ref, pos_ref, of_ref, ob_ref, og_ref, hf_ref, hb_ref, yr_ref, mod_ref, hg_ref,
                   w_ref, o_ref, w_sc):
    @pl.when(pl.program_id(0) == 0)
    def _():
        w_sc[...] = w_ref[0].astype(BF16)

    m = mod_ref[0]
    f32 = lambda ref: ref[...].astype(F32)
    oa = f32(of_ref) + f32(ob_ref)
    hg = hg_ref[...]
    parts = []
    for h in range(A_HEADS):
        z = oa[:, h * A_DK:(h + 1) * A_DK]
        parts.append(z * lax.rsqrt(jnp.mean(z * z, axis=-1, keepdims=True) + RMS_EPS) * hg)
    o_a = jnp.concatenate(parts, axis=-1) * _silu(f32(og_ref))
    o_b = (f32(hf_ref) + f32(hb_ref)) * _gelu_tanh(f32(yr_ref))
    y = _dot(o_a, w_sc[0:D_A]) + _dot(o_b, w_sc[D_A:D_A + D_B])
    o_ref[...] = _x0(pl.program_id(0), xp_ref, xs_ref, pos_ref) + m[2:3] * y


def _ab_out(xp, xs, pos, proj_h, o_f, o_b, h_f, h_b, mods, hg, w_out):
    seg = lambda width, col: pl.BlockSpec((SEG, width), lambda i: (i, col))
    return pl.pallas_call(
        _ab_out_kernel,
        grid=(NSEG,),
        in_specs=_x0_specs() + [seg(D_A, 0), seg(D_A, 0), seg(D_A, 2), seg(D_B, 0), seg(D_B, 0), seg(D_B, 3),
                                pl.BlockSpec((1, 6, D), lambda i: (i, 0, 0)),
                                pl.BlockSpec((1, A_DK), lambda i: (0, 0)),
                                pl.BlockSpec((1, D_A + D_B, D), lambda i: (0, 0, 0),
                                             pipeline_mode=pl.Buffered(1))],
        out_specs=seg(D, 0),
        out_shape=jax.ShapeDtypeStruct((M_TOK, D), F32),
        scratch_shapes=[pltpu.VMEM((D_A + D_B, D), BF16)],
        compiler_params=_cp(("arbitrary",)),
        name="ab_out",
    )(xp, xs, pos, o_f, o_b, proj_h, h_f, h_b, proj_h, mods, hg.reshape(1, A_DK), w_out)


def _route(scores, sel):
    cols = [sel[e:e + 1, :] for e in range(N_EXPERTS)]

    def rank(vals):
        out = []
        for i, vi in enumerate(vals):
            r = None
            for jx, vj in enumerate(vals):
                if jx == i:
                    continue
                beats = (vj >= vi) if jx < i else (vj > vi)
                r = beats.astype(F32) if r is None else r + beats.astype(F32)
            out.append(r)
        return out

    grp_scores, in_top2 = [], []
    for gi in range(N_GROUPS):
        vals = cols[gi * GROUP:(gi + 1) * GROUP]
        best_pair = None
        for a in range(GROUP):
            for bx in range(a + 1, GROUP):
                s = vals[a] + vals[bx]
                best_pair = s if best_pair is None else jnp.maximum(best_pair, s)
        grp_scores.append(best_pair)
        in_top2.extend([r < 2.0 for r in rank(vals)])
    grp_best = [r < 1.0 for r in rank(grp_scores)]
    picked = [jnp.where(jnp.logical_and(grp_best[e // GROUP], in_top2[e]), scores[e:e + 1, :], 0.0)
              for e in range(N_EXPERTS)]
    total = picked[0]
    for pe in picked[1:]:
        total = total + pe
    row = lax.broadcasted_iota(jnp.int32, scores.shape, 0)
    comb = jnp.zeros(scores.shape, F32)
    for e in range(N_EXPERTS):
        comb = jnp.where(row == e, picked[e] / total, comb)
    return comb


def _moe_kernel(final_norm, x_ref, mod_ref, g_ref, rw_ref, rb_ref, w1_ref, w3_ref, w2_ref, gf_ref, *rest):
    o_refs, (h_sc, comb_sc, acc_sc) = rest[:-3], rest[-3:]
    e = pl.program_id(1)

    @pl.when(e == 0)
    def _():
        for s in range(MOE_TM // SEG):
            m = mod_ref[s]
            rows = slice(s * SEG, (s + 1) * SEG)
            h = _rms_mod(x_ref[rows, :], g_ref[...], m[4:5], m[3:4])
            h_sc[rows, :] = h.astype(BF16)
            scores = 1.0 / (1.0 + jnp.exp(-_dot_x3_nt(rw_ref[...], h)))
            comb_t = _route(scores, scores + rb_ref[...])
            comb_t = jnp.concatenate([comb_t, jnp.zeros((COMB_LANES - N_EXPERTS, SEG), F32)], axis=0)
            comb_sc[rows, :] = comb_t.T
        acc_sc[...] = jnp.zeros_like(acc_sc)

    lane = lax.broadcasted_iota(jnp.int32, (MOE_TM, COMB_LANES), 1)
    h = h_sc[...]
    es = w1_ref.shape[1]
    hids = []
    for ee in range(es):
        comb = jnp.sum(jnp.where(lane == e * es + ee, comb_sc[...], 0.0), axis=-1, keepdims=True)
        u1 = jnp.dot(h, w1_ref[0, ee].astype(BF16), preferred_element_type=F32)
        u3 = jnp.dot(h, w3_ref[0, ee].astype(BF16), preferred_element_type=F32)
        hids.append((_silu(u1) * u3 * comb).astype(BF16))
    w2 = w2_ref[0].reshape(es * D_EXPERT, D).astype(BF16)
    acc_sc[...] += jnp.dot(jnp.concatenate(hids, axis=1), w2, preferred_element_type=F32)

    def emit(dst_ref):
        for s in range(MOE_TM // SEG):
            rows = slice(s * SEG, (s + 1) * SEG)
            y = x_ref[rows, :] + mod_ref[s][5:6] * acc_sc[rows, :]
            if final_norm:
                y = y * lax.rsqrt(jnp.mean(y * y, axis=-1, keepdims=True) + RMS_EPS) * gf_ref[...]
            dst_ref[rows, :] = y

    last = e == N_EXPERTS // es - 1
    if final_norm:
        is_ctx = pl.program_id(0) < MOE_CTX_TILES
        pl.when(jnp.logical_and(last, is_ctx))(lambda: emit(o_refs[0]))
        pl.when(jnp.logical_and(last, jnp.logical_not(is_ctx)))(lambda: emit(o_refs[1]))
    else:
        pl.when(last)(lambda: emit(o_refs[0]))


def _moe(x, mods, g, router_w, router_b, w1, w3, w2, layer, gf, final_norm):
    spt = MOE_TM // SEG
    es = MOE_EXPERTS_STEP
    tile = lambda f, **kw: pl.BlockSpec((MOE_TM, D), lambda t, e: (f(t), 0), **kw)
    if final_norm:
        n_half = M_TOK // 2
        out_specs = [tile(lambda t: jnp.minimum(t, MOE_CTX_TILES - 1), pipeline_mode=pl.Buffered(1)),
                     tile(lambda t: jnp.maximum(t - MOE_CTX_TILES, 0), pipeline_mode=pl.Buffered(1))]
        out_shape = [jax.ShapeDtypeStruct((n_half, D), F32), jax.ShapeDtypeStruct((M_TOK - n_half, D), F32)]
    else:
        out_specs = [tile(lambda t: t)]
        out_shape = [jax.ShapeDtypeStruct((M_TOK, D), F32)]
    return pl.pallas_call(
        functools.partial(_moe_kernel, final_norm),
        grid=(M_TOK // MOE_TM, N_EXPERTS // es),
        in_specs=[pl.BlockSpec((MOE_TM, D), lambda t, e: (t, 0)),
                  pl.BlockSpec((spt, 6, D), lambda t, e: (t, 0, 0)),
                  pl.BlockSpec((1, D), lambda t, e: (0, 0)),
                  pl.BlockSpec((N_EXPERTS, D), lambda t, e: (0, 0)),
                  pl.BlockSpec((N_EXPERTS, 1), lambda t, e: (0, 0)),
                  pl.BlockSpec((1, es, D, D_EXPERT), lambda t, e: (layer, e, 0, 0)),
                  pl.BlockSpec((1, es, D, D_EXPERT), lambda t, e: (layer, e, 0, 0)),
                  pl.BlockSpec((1, es, D_EXPERT, D), lambda t, e: (layer, e, 0, 0)),
                  pl.BlockSpec((1, D), lambda t, e: (0, 0))],
        out_specs=out_specs,
        out_shape=out_shape,
        scratch_shapes=[pltpu.VMEM((MOE_TM, D), BF16), pltpu.VMEM((MOE_TM, COMB_LANES), F32),
                        pltpu.VMEM((MOE_TM, D), F32)],
        compiler_params=_cp(("arbitrary", "arbitrary")),
        name="moe",
    )(x, mods, g.reshape(1, D), router_w.T, router_b.reshape(N_EXPERTS, 1), w1, w3, w2, gf.reshape(1, D))


def _rw_in_kernel(x_ref, xp_ref, xn_ref, mod_ref, g_ref, mu_ref, wr_ref, wk_ref, wv_ref, g1_ref, g2_ref,
                  w1_ref, w2_ref, w0_ref, a1_ref, a2_ref, a0_ref, ka_ref, rk_ref, hsel_ref,
                  r_ref, k_ref, v_ref, gg_ref, lw_ref, a_ref, coef_ref, wrkv_sc):
    i = pl.program_id(0)

    @pl.when(i == 0)
    def _():
        for c, w_ref in enumerate((wr_ref, wk_ref, wv_ref)):
            wrkv_sc[c] = w_ref[0].astype(BF16)

    m = mod_ref[0]
    g = g_ref[...]
    h = _rms_mod(x_ref[...], g, m[1:2], m[0:1])
    hp = jnp.where(_seg_is_first(i), 0.0, _rms_mod(xp_ref[...], g, m[1:2], m[0:1]))
    hn = jnp.where(_seg_is_last(i), 0.0, _rms_mod(xn_ref[...], g, m[1:2], m[0:1]))
    ext = jnp.concatenate([hp, h, hn], axis=0)
    n = ext.shape[0]
    h_prev = pltpu.roll(ext, 1, 0)[8:8 + SEG]
    h_next = pltpu.roll(ext, n - 1, 0)[8:8 + SEG]
    xx = 0.5 * (h_prev + h_next) - h
    mu = mu_ref[...]
    xr, xw, xk, xv, xa, xg = [h + xx * mu[c:c + 1] for c in range(6)]
    r = _dot(xr, wrkv_sc[0])
    k = _dot(xk, wrkv_sc[1])
    r_ref[...] = r.astype(BF16)
    k_ref[...] = k.astype(BF16)
    v_ref[...] = _dot(xv, wrkv_sc[2]).astype(BF16)
    gg_ref[...] = _dot(_sigmoid(_dot(xg, g1_ref[...])), g2_ref[...]).astype(BF16)
    half_w_in = w0_ref[...] + _dot(jnp.tanh(_dot(xw, w1_ref[...])), w2_ref[...])
    lw_ref[...] = (-0.5 * W_DECAY_SCALE * LOG2E) * jnp.tanh(half_w_in) - 0.5 * W_DECAY_SCALE * LOG2E
    a = 0.5 * jnp.tanh(a0_ref[...] + _dot(_dot(xa, a1_ref[...]), a2_ref[...])) + 0.5
    a_ref[...] = a
    kd_sum = k * (2.0 + (a[:, 0:D] + a[:, D:2 * D] - 2.0) * ka_ref[...])
    coef_ref[...] = _dot_sel_rhs(r * kd_sum * rk_ref[...], hsel_ref[...])


def _rw_inproj(x, mods, g, mu, wr, wk, wv, g1, g2, w1c, w2bd, w0c, a1c, a2bd, a0c, ka, rk):
    full = lambda a: pl.BlockSpec(a.shape, lambda i: (0,) * a.ndim)
    once = lambda a: pl.BlockSpec((1,) + a.shape[1:], lambda i: (0,) * a.ndim, pipeline_mode=pl.Buffered(1))
    seg = lambda width: pl.BlockSpec((SEG, width), lambda i: (i, 0))
    outs = ([jax.ShapeDtypeStruct((M_TOK, D), BF16)] * 4 + [jax.ShapeDtypeStruct((M_TOK, 2 * D), F32)] * 2
            + [jax.ShapeDtypeStruct((M_TOK, COEF_LANES), F32)])
    hsel = jnp.asarray(np.arange(D)[:, None] // C_HEAD == np.arange(COEF_LANES)[None, :], BF16)
    return pl.pallas_call(
        _rw_in_kernel,
        grid=(NSEG,),
        in_specs=[seg(D),
                  pl.BlockSpec((8, D), lambda i: (_prev8(i), 0)),
                  pl.BlockSpec((8, D), lambda i: (_next8(i), 0)),
                  pl.BlockSpec((1, 6, D), lambda i: (i, 0, 0)),
                  full(g), full(mu), once(wr), once(wk), once(wv), full(g1), full(g2),
                  full(w1c), full(w2bd), full(w0c), full(a1c), full(a2bd), full(a0c),
                  full(ka), full(rk), full(hsel)],
        out_specs=[seg(D)] * 4 + [seg(2 * D)] * 2 + [seg(COEF_LANES)],
        out_shape=outs,
        scratch_shapes=[pltpu.VMEM((3, D, D), BF16)],
        compiler_params=_cp(("arbitrary",)),
        name="rwkv_inproj",
    )(x, x, x, mods, g, mu, wr, wk, wv, g1, g2, w1c, w2bd, w0c, a1c, a2bd, a0c, ka, rk, hsel)


def _rw_consts():
    t = np.arange(SEG)
    same = (t[:, None] // RW_CHUNK) == (t[None, :] // RW_CHUNK)
    tri_f = np.logical_and(same, t[None, :] <= t[:, None]).astype(np.float32)
    tri_b = np.logical_and(same, t[None, :] >= t[:, None]).astype(np.float32)
    return [jnp.asarray(a, BF16) for a in (tri_f, tri_b)]


def _rw_dir(r, k, v, lw, a, kk_g, ka_g, tri, rev):
    c = RW_CHUNK
    lane = lax.broadcasted_iota(jnp.int32, (SEG, 2 * C_HEAD), 1)
    head0 = lane < C_HEAD
    kx = k * kk_g
    ss = kx * kx
    n0 = jnp.sum(jnp.where(head0, ss, 0.0), axis=-1, keepdims=True)
    n1 = jnp.sum(jnp.where(head0, 0.0, ss), axis=-1, keepdims=True)
    kk = kx / jnp.maximum(jnp.sqrt(jnp.where(head0, n0, n1)), 1e-12)
    kd = k * (1.0 + (a - 1.0) * ka_g)
    bhat = kk * a
    cum = _dot_sel(tri, lw)
    e_incl = jnp.exp2(cum)
    e_inv = jnp.exp2(-cum)
    ae = -kk * jnp.exp2(cum - lw)
    re = r * e_incl
    bi = bhat * e_inv
    ki = kd * e_inv

    chunks = []
    for ci in range(SEG // c):
        sl = slice(ci * c, (ci + 1) * c)
        ctot = cum[ci * c:ci * c + 1] if rev else cum[(ci + 1) * c - 1:(ci + 1) * c]
        dec = jnp.exp2(ctot - cum[sl])
        chunks.append(dict(ae=ae[sl], re=re[sl], bi=bi[sl], ki=ki[sl], v=v[sl], bdec=bhat[sl] * dec,
                           kdec=kd[sl] * dec, gam=jnp.exp2(ctot), rev=rev))
    return chunks


def _rw_transitions(chunks):
    c = RW_CHUNK
    w2 = 2 * C_HEAD
    h0c = lax.broadcasted_iota(jnp.int32, (c, w2), 1) < C_HEAD
    rowc = lax.broadcasted_iota(jnp.int32, (c, w2), 0)
    colc = jnp.bitwise_and(lax.broadcasted_iota(jnp.int32, (c, w2), 1), C_HEAD - 1)
    eye = (colc == rowc).astype(F32)
    bdmask = (lax.broadcasted_iota(jnp.int32, (w2, w2), 0) < C_HEAD) == (
        lax.broadcasted_iota(jnp.int32, (w2, w2), 1) < C_HEAD)
    keep0 = jnp.where(h0c, 1.0, 0.0).astype(BF16)
    keep1 = jnp.where(h0c, 0.0, 1.0).astype(BF16)

    def bd16(yb):
        return jnp.concatenate([yb * keep0, yb * keep1], axis=0)

    def dot_bd(x, pairs):
        blocks = [bd16(y.astype(BF16)) for y in pairs]
        rhs = blocks[0] if len(blocks) == 1 else jnp.concatenate(blocks, axis=1)
        return jnp.dot(x.astype(BF16), rhs, preferred_element_type=F32)

    mm_inv = lambda x, y: dot_bd(x, [y])

    n_ab, a_ak, a_rb, a_rk = [], [], [], []
    for ch in chunks:
        strict = (colc > rowc) if ch["rev"] else (colc < rowc)
        incl = (colc >= rowc) if ch["rev"] else (colc <= rowc)
        left = jnp.concatenate([ch["ae"], ch["re"]], axis=0).astype(BF16)
        right = jnp.concatenate([bd16(ch["bi"].astype(BF16)), bd16(ch["ki"].astype(BF16))], axis=0)
        gm = lax.dot_general(left, right, (((1,), (1,)), ((), ())), preferred_element_type=F32)
        n_ab.append(jnp.where(strict, gm[0:c, 0:2 * c], 0.0))
        a_ak.append(jnp.where(strict, gm[0:c, 2 * c:4 * c], 0.0))
        a_rb.append(jnp.where(incl, gm[c:2 * c, 0:2 * c], 0.0))
        a_rk.append(jnp.where(incl, gm[c:2 * c, 2 * c:4 * c], 0.0))
    xorc = jnp.bitwise_xor(rowc, colc)
    tm = [eye + jnp.where(xorc < 2, n, 0.0) for n in n_ab]
    blk = 2
    while blk < c:
        couple = jnp.logical_and(xorc >= blk, xorc < 2 * blk)
        xs = [mm_inv(jnp.where(couple, n, 0.0), t) for n, t in zip(n_ab, tm)]
        tm = [t + mm_inv(t, x) for t, x in zip(tm, xs)]
        blk *= 2
    akv = [dot_bd(jnp.concatenate([x, y], axis=0), [ch["v"]])
           for x, y, ch in zip(a_ak, a_rk, chunks)]
    tav = [dot_bd(t, [ch["ae"], x[0:c]]) for t, ch, x in zip(tm, chunks, akv)]
    out = []
    for i, ch in enumerate(chunks):
        ta, tv = tav[i][:, 0:w2], tav[i][:, w2:2 * w2]
        qy = dot_bd(a_rb[i], [ta, tv])
        q = ch["re"] + qy[:, 0:w2]
        y = qy[:, w2:2 * w2] + akv[i][c:2 * c]
        wz = _dot_tn(tav[i], ch["bdec"])
        w = jnp.where(bdmask, wz[0:w2], 0.0)
        z = jnp.where(bdmask, wz[w2:2 * w2] + _dot_tn(ch["v"], ch["kdec"]), 0.0)
        out.append((q, y, w, z, ch["gam"]))
    return out


def _rw_scan_kernel(rf, kf, vf, lwf, af, rb, kb, vb, lwb, ab, kkg_ref, kag_ref, s0f, s0b, trif, trib,
                    of_ref, ob_ref, fin_ref, st_f, st_b):
    j = pl.program_id(1)
    w2 = 2 * C_HEAD
    dirs = ((False, (rf, kf, vf, lwf, af), s0f, trif, of_ref, 0, st_f),
            (True, (rb, kb, vb, lwb, ab), s0b, trib, ob_ref, 1, st_b))
    for rev, refs, s0, tri, o_ref, s_out, st in dirs:
        i = (NSEG - 1 - j) if rev else j
        starts = _seg_is_last(i) if rev else _seg_is_first(i)

        @pl.when(jnp.logical_and(starts, i < N_CTX_SEG))
        def _():
            st[...] = jnp.zeros_like(st)

        @pl.when(jnp.logical_and(starts, i >= N_CTX_SEG))
        def _():
            st[...] = s0[0, 0]

    chunks = []
    for rev, refs, s0, tri, o_ref, s_out, st in dirs:
        for pp in range(RW_PAIRS_STEP):
            lanes = slice(pp * w2, (pp + 1) * w2)
            r_, k_, v_, lw_, a_ = [z[:, lanes].astype(F32) for z in refs]
            chunks.extend(_rw_dir(r_, k_, v_, lw_, a_, kkg_ref[:, lanes], kag_ref[:, lanes], tri[...], rev))
    trans = _rw_transitions(chunks)
    n_c = SEG // RW_CHUNK
    states = [[st[pp] for pp in range(RW_PAIRS_STEP)] for (_, _, _, _, _, _, st) in dirs]
    for step in range(n_c):
        for d, (rev, refs, s0, tri, o_ref, s_out, st) in enumerate(dirs):
            ci = (n_c - 1 - step) if rev else step
            for pp in range(RW_PAIRS_STEP):
                q, y, w, z, gam = trans[(d * RW_PAIRS_STEP + pp) * n_c + ci]
                s = states[d][pp]
                o_ref[ci * RW_CHUNK:(ci + 1) * RW_CHUNK, pp * w2:(pp + 1) * w2] = (_dot_nt(q, s) + y).astype(BF16)
                states[d][pp] = s * gam + _dot(s, w) + z
    for d, (rev, refs, s0, tri, o_ref, s_out, st) in enumerate(dirs):
        for pp in range(RW_PAIRS_STEP):
            st[pp] = states[d][pp]
    for d, (rev, refs, s0, tri, o_ref, s_out, st) in enumerate(dirs):
        i = (NSEG - 1 - j) if rev else j

        @pl.when(i < N_CTX_SEG)
        def _():
            for pp in range(RW_PAIRS_STEP):
                s = states[d][pp]
                head = 2 * (pl.program_id(0) * RW_PAIRS_STEP + pp)
                fin_ref[i, 0, d, head] = s[0:C_HEAD, 0:C_HEAD]
                fin_ref[i, 0, d, head + 1] = s[C_HEAD:w2, C_HEAD:w2]


def _rw_scan(r, k, v, lw, a, kk_g, ka_g, s0bd):
    consts = _rw_consts()
    w = 2 * C_HEAD
    pps = RW_PAIRS_STEP
    wb = pps * w
    n_steps = C_PAIRS // pps
    blk = lambda col0, rev: pl.BlockSpec(
        (SEG, wb), (lambda p, j: (NSEG - 1 - j, col0 + p)) if rev else (lambda p, j: (j, col0 + p)))
    cspec = lambda arr: pl.BlockSpec(arr.shape, lambda p, j: (0, 0))
    fin_shape = (N_CTX_SEG, 1, 2, 2 * C_PAIRS, C_HEAD, C_HEAD)
    fin = pl.BlockSpec(fin_shape, lambda p, j: (0,) * len(fin_shape), pipeline_mode=pl.Buffered(1))
    return pl.pallas_call(
        _rw_scan_kernel,
        grid=(n_steps, NSEG),
        in_specs=[blk(0, False), blk(0, False), blk(0, False), blk(0, False), blk(0, False),
                  blk(0, True), blk(0, True), blk(0, True), blk(n_steps, True), blk(n_steps, True),
                  pl.BlockSpec((1, wb), lambda p, j: (0, p)), pl.BlockSpec((1, wb), lambda p, j: (0, p)),
                  pl.BlockSpec((1, 1, pps, w, w), lambda p, j: (_sample_of(j), 0, p, 0, 0)),
                  pl.BlockSpec((1, 1, pps, w, w), lambda p, j: (_sample_of(NSEG - 1 - j), 1, p, 0, 0)),
                  cspec(consts[0]), cspec(consts[1])],
        out_specs=[blk(0, False), blk(0, True), fin],
        out_shape=[jax.ShapeDtypeStruct((M_TOK, D), BF16), jax.ShapeDtypeStruct((M_TOK, D), BF16),
                   jax.ShapeDtypeStruct(fin_shape, F32)],
        scratch_shapes=[pltpu.VMEM((pps, w, w), F32), pltpu.VMEM((pps, w, w), F32)],
        compiler_params=_cp(("arbitrary", "arbitrary")),
        name="rwkv7_scan",
    )(r, k, v, lw, a, r, k, v, lw, a, kk_g, ka_g, s0bd, s0bd, *consts)


def _rw_out_kernel(x_ref, of_ref, ob_ref, v_ref, gg_ref, coef_ref, mod_ref, lnw_ref, lnb_ref, ones_ref,
                   hexp_ref, wo_ref, o_ref, wo_sc):
    @pl.when(pl.program_id(0) == 0)
    def _():
        wo_sc[...] = wo_ref[0].astype(BF16)

    m = mod_ref[0]
    ones_bd = ones_ref[...]
    w = 2 * C_HEAD
    inv_n = 1.0 / C_HEAD
    coef = _dot_sel_rhs(coef_ref[...], hexp_ref[...])
    parts = []
    for p in range(C_PAIRS):
        cs = slice(p * w, (p + 1) * w)
        osum = of_ref[:, cs].astype(F32) + ob_ref[:, cs].astype(F32)
        mu = _dot_sel_rhs(osum, ones_bd) * inv_n
        cen = osum - mu
        var = _dot_sel_rhs(cen * cen, ones_bd) * inv_n
        o = cen * lax.rsqrt(var + GN_EPS) * lnw_ref[:, cs] + lnb_ref[:, cs]
        bonus = coef[:, cs] * v_ref[:, cs].astype(F32)
        parts.append((o + bonus) * gg_ref[:, cs].astype(F32))
    y = _dot(jnp.concatenate(parts, axis=-1), wo_sc[...])
    o_ref[...] = x_ref[...] + m[2:3] * y


def _dot_sel_rhs(x, mat):
    h = x.astype(BF16)
    l = (x - h.astype(F32)).astype(BF16)
    return jnp.dot(h, mat, preferred_element_type=F32) + jnp.dot(l, mat, preferred_element_type=F32)


def _rw_out(x, o_f, o_b, v, gg, coef, mods, lnw, lnb, wo):
    seg = lambda width: pl.BlockSpec((SEG, width), lambda i: (i, 0))
    row = pl.BlockSpec((1, D), lambda i: (0, 0))
    hh = np.arange(2 * C_HEAD) // C_HEAD
    ones_bd = jnp.asarray((hh[:, None] == hh[None, :]).astype(np.float32), BF16)
    hexp = jnp.asarray(np.arange(COEF_LANES)[:, None] == np.arange(D)[None, :] // C_HEAD, BF16)
    return pl.pallas_call(
        _rw_out_kernel,
        grid=(NSEG,),
        in_specs=[seg(D), seg(D), seg(D), seg(D), seg(D), seg(COEF_LANES),
                  pl.BlockSpec((1, 6, D), lambda i: (i, 0, 0)),
                  row, row,
                  pl.BlockSpec((2 * C_HEAD, 2 * C_HEAD), lambda i: (0, 0)),
                  pl.BlockSpec((COEF_LANES, D), lambda i: (0, 0)),
                  pl.BlockSpec((1, D, D), lambda i: (0, 0, 0), pipeline_mode=pl.Buffered(1))],
        out_specs=seg(D),
        out_shape=jax.ShapeDtypeStruct((M_TOK, D), F32),
        scratch_shapes=[pltpu.VMEM((D, D), BF16)],
        compiler_params=_cp(("arbitrary",)),
        name="rwkv_out",
    )(x, o_f, o_b, v, gg, coef, mods, lnw.reshape(1, D), lnb.reshape(1, D), ones_bd, hexp, wo)


def _grid_pos_table(n_tok):
    rows = n_tok // GRID_W
    r, cl = np.meshgrid(np.arange(rows, dtype=np.float32), np.arange(GRID_W, dtype=np.float32), indexing='ij')
    quarter = D // 4
    omega = (1.0 / (np.float32(POS_BASE) ** (np.arange(quarter, dtype=np.float32) / np.float32(quarter))))
    ang_r = (r.reshape(-1, 1) * omega).astype(np.float32)
    ang_c = (cl.reshape(-1, 1) * omega).astype(np.float32)
    table = np.concatenate([np.sin(ang_r), np.cos(ang_r), np.sin(ang_c), np.cos(ang_c)], axis=-1)
    return jnp.asarray(table.astype(np.float32))


def _block_diag(blocks):
    g, n, _ = blocks.shape
    eye = jnp.eye(g, dtype=blocks.dtype)
    return (eye[:, None, :, None] * blocks[:, :, None, :]).reshape(g * n, g * n)


def kernel(x_prompt, x_sample, state_hgrn, state_rglru, state_rwkv, c, c_ctx, norm_mix_g, norm_ffn_g, w_mod, b_mod, ab_w_in, ab_w_out, hgrn_lb, hgrn_norm_g, rg_conv_w, rg_conv_b, rg_wa, rg_ba, rg_wx, rg_bx, rg_lambda, rw_mu, rw_wr, rw_wk, rw_wv, rw_wo, rw_w0, rw_w1, rw_w2, rw_a0, rw_a1, rw_a2, rw_g1, rw_g2, rw_kk, rw_ka, rw_rk, rw_lnw, rw_lnb, moe_router, moe_router_bias, moe_w1, moe_w3, moe_w2, norm_f_g):
    bf = lambda z: z.astype(BF16)
    xp = x_prompt.reshape(-1, D)
    xs = x_sample.reshape(-1, D)
    pos = _grid_pos_table(x_sample.shape[1])
    mods = _modulations(c, c_ctx, w_mod, b_mod)

    lower_bounds = jnp.cumsum(jax.nn.softmax(hgrn_lb.astype(F32), axis=1), axis=1)
    lb = lower_bounds[:, 0].reshape(2, A_HEADS, 1, A_DK)
    proj_f, proj_h = _ab_inproj(xp, xs, pos, mods[0], norm_mix_g[0], ab_w_in)
    s0t = jnp.swapaxes(state_hgrn[:, 0], -1, -2)
    o_f, o_b, new_hgrn = _gla(proj_f, proj_h, lb, s0t)
    wa_bd = bf(jnp.stack([_block_diag(rg_wa[0, d]) for d in range(2)]))
    wx_bd = bf(jnp.stack([_block_diag(rg_wx[0, d]) for d in range(2)]))
    h_f, h_b, lru_f, lru_b = _rglru(
        proj_f, rg_conv_w[0], rg_conv_b[0].reshape(1, D_B), wa_bd, rg_ba[0].reshape(2, 1, D_B), wx_bd,
        rg_bx[0].reshape(2, 1, D_B), rg_lambda[0].reshape(2, 1, D_B), state_rglru[:, 0].reshape(-1, 2, 1, D_B))
    x = _ab_out(xp, xs, pos, proj_h, o_f, o_b, h_f, h_b, mods[0], hgrn_norm_g[0], ab_w_out)
    x, = _moe(x, mods[0], norm_ffn_g[0], moe_router, moe_router_bias, moe_w1, moe_w3, moe_w2, 0, norm_f_g, False)

    w1c = bf(jnp.concatenate([rw_w1[0, 0], rw_w1[0, 1]], axis=-1))
    a1c = bf(jnp.concatenate([rw_a1[0, 0], rw_a1[0, 1]], axis=-1))
    half_w2, half_a2 = 0.5 * rw_w2[0], 0.5 * rw_a2[0]
    w2bd = bf(jnp.concatenate([jnp.concatenate([half_w2[0], jnp.zeros_like(half_w2[0])], axis=-1),
                               jnp.concatenate([jnp.zeros_like(half_w2[1]), half_w2[1]], axis=-1)], axis=0))
    a2bd = bf(jnp.concatenate([jnp.concatenate([half_a2[0], jnp.zeros_like(half_a2[0])], axis=-1),
                               jnp.concatenate([jnp.zeros_like(half_a2[1]), half_a2[1]], axis=-1)], axis=0))
    r, k, v, gg, lw, a, coef = _rw_inproj(
        x, mods[1], norm_mix_g[1].reshape(1, D), rw_mu[0], rw_wr, rw_wk, rw_wv,
        bf(rw_g1[0]), bf(rw_g2[0]), w1c, w2bd, 0.5 * rw_w0[0].reshape(1, 2 * D), a1c, a2bd,
        0.5 * rw_a0[0].reshape(1, 2 * D), rw_ka[0].reshape(1, D), rw_rk[0].reshape(1, D))
    s0 = state_rwkv[:, 0].reshape(N_SAMPLE, 2, C_PAIRS, 2, C_HEAD, C_HEAD)
    zeros = jnp.zeros_like(s0[:, :, :, 0])
    s0bd = jnp.concatenate([jnp.concatenate([s0[:, :, :, 0], zeros], axis=-1),
                            jnp.concatenate([zeros, s0[:, :, :, 1]], axis=-1)], axis=-2)
    ow_f, ow_b, new_rwkv = _rw_scan(r, k, v, lw, a, rw_kk[0].reshape(1, D), rw_ka[0].reshape(1, D), s0bd)
    x = _rw_out(x, ow_f, ow_b, v, gg, coef, mods[1], rw_lnw[0], rw_lnb[0], rw_wo)
    y_p, y_s = _moe(x, mods[1], norm_ffn_g[1], moe_router, moe_router_bias, moe_w1, moe_w3, moe_w2, 1, norm_f_g, True)

    new_rglru = jnp.stack([lru_f[:, 0], lru_b[:, 0]], axis=1)[:, None]
    return (y_p.reshape(x_prompt.shape), y_s.reshape(x_sample.shape), new_hgrn, new_rglru, new_rwkv)
```
